```python
import jax, jax.numpy as jnp
from jax import lax
import numpy as np

D_MODEL = 1024
BATCH = 2
SEQ = 8192
DEPTH = 2

N_META = 16
CONV_WIDTH = 3
N_HEADS = 16
HEAD_DIM = 64
N_KV_HEADS = 4
Q_PER_KV = N_HEADS // N_KV_HEADS
WINDOW = 128
BLOCK = 128
ROPE_THETA = 10000.0
N_GROUPS = 4
EXPERTS_PER_GROUP = 8
N_EXPERTS = N_GROUPS * EXPERTS_PER_GROUP
TOP_K = 2
D_EXPERT = 256
N_A_LAYERS = DEPTH // 2
N_B_LAYERS = DEPTH - N_A_LAYERS
NORM_EPS = 1e-5
NEG_INF = -1e30

kernel_name = "yoco_shortconv_swa_sink_hier_moe"


def rms_norm(x, g):
    xf = x.astype(jnp.float32)
    y = xf * lax.rsqrt(jnp.mean(xf * xf, axis=-1, keepdims=True) + NORM_EPS)
    return (y * g.astype(jnp.float32)).astype(x.dtype)


def rope(x, pos):
    half = HEAD_DIM // 2
    inv_freq = ROPE_THETA ** (-jnp.arange(half, dtype=jnp.float32) / half)
    ang = pos.astype(jnp.float32)[:, None] * inv_freq[None, :]
    cos = jnp.cos(ang)[None, :, None, :]
    sin = jnp.sin(ang)[None, :, None, :]
    xf = x.astype(jnp.float32)
    x1, x2 = xf[..., :half], xf[..., half:]
    out = jnp.concatenate([x1 * cos - x2 * sin, x2 * cos + x1 * sin], axis=-1)
    return out.astype(x.dtype)


def short_conv_mixer(xn, w_in, conv_w, w_out):
    gate_b, gate_c, v = jnp.split(xn @ w_in, 3, axis=-1)
    u = gate_c * v
    L = u.shape[1]
    up = jnp.pad(u, ((0, 0), (CONV_WIDTH - 1, 0), (0, 0)))
    conv = up[:, 0:L] * conv_w[0]
    for i in range(1, CONV_WIDTH):
        conv = conv + up[:, i:i + L] * conv_w[i]
    return (gate_b * conv) @ w_out


def _to_blocks(t, pad):
    widths = [(0, 0), (pad, 0)] + [(0, 0)] * (t.ndim - 2)
    t = jnp.pad(t, widths)
    return t.reshape((t.shape[0], t.shape[1] // BLOCK, BLOCK) + t.shape[2:])


def _band(tb):
    widths = [(0, 0), (1, 0)] + [(0, 0)] * (tb.ndim - 2)
    prev = jnp.pad(tb[:, :-1], widths)
    return jnp.concatenate([prev, tb], axis=2)


def shared_kv(h, kv_norm_g, w_kv, pos):
    B, L, _ = h.shape
    pad = (-L) % BLOCK
    k, v = jnp.split(rms_norm(h, kv_norm_g) @ w_kv, 2, axis=-1)
    k = rope(k.reshape(B, L, N_KV_HEADS, HEAD_DIM), pos)
    v = v.reshape(B, L, N_KV_HEADS, HEAD_DIM)
    return _band(_to_blocks(k, pad)), _band(_to_blocks(v, pad))


def sliding_window_attention(xn, w_q, w_o, sinks, k_band, v_band, pos):
    B, L, _ = xn.shape
    pad = (-L) % BLOCK
    n_blk = (L + pad) // BLOCK
    q = rope((xn @ w_q).reshape(B, L, N_HEADS, HEAD_DIM), pos)
    q = _to_blocks(q, pad).reshape(B, n_blk, BLOCK, N_KV_HEADS, Q_PER_KV, HEAD_DIM)
    s = jnp.einsum('bnqkgd,bnjkd->bnkgqj', q, k_band,
                   preferred_element_type=jnp.float32) * (HEAD_DIM ** -0.5)
    qi = jnp.arange(BLOCK)[:, None]
    kj = jnp.arange(2 * BLOCK)[None, :]
    diff = qi - kj + BLOCK
    band = (diff >= 0) & (diff < WINDOW)
    kpos = (jnp.arange(n_blk)[:, None] - 1) * BLOCK + kj - pad
    valid = band[None] & (kpos >= 0)[:, None, :]
    s = jnp.where(valid[None, :, None, None], s, NEG_INF)
    sink = jnp.broadcast_to(
        sinks.astype(jnp.float32).reshape(1, 1, N_KV_HEADS, Q_PER_KV, 1, 1),
        s.shape[:-1] + (1,))
    p = jax.nn.softmax(jnp.concatenate([s, sink], axis=-1), axis=-1)[..., :-1]
    o = jnp.einsum('bnkgqj,bnjkd->bnqkgd', p.astype(v_band.dtype), v_band)
    o = o.reshape(B, n_blk * BLOCK, N_HEADS * HEAD_DIM)[:, pad:]
    return o @ w_o


def hierarchical_moe(xn, rg_w, rg_b, re_w, re_b, w_gate, w_up, w_down):
    shape = xn.shape
    t = xn.reshape(-1, shape[-1])
    T = t.shape[0]
    g_logits = (t @ rg_w + rg_b).astype(jnp.float32)
    g_idx = jnp.argmax(g_logits, axis=-1)
    g_w = jnp.take_along_axis(jax.nn.softmax(g_logits, axis=-1), g_idx[:, None], axis=-1)
    e_logits = (t @ re_w + re_b).astype(jnp.float32).reshape(T, N_GROUPS, EXPERTS_PER_GROUP)
    e_sel = jnp.take_along_axis(e_logits, g_idx[:, None, None], axis=1)[:, 0]
    top_v, top_i = lax.top_k(e_sel, TOP_K)
    top_w = jax.nn.softmax(top_v, axis=-1) * g_w
    expert_id = g_idx[:, None] * EXPERTS_PER_GROUP + top_i
    combine = jnp.sum(jax.nn.one_hot(expert_id, N_EXPERTS, dtype=jnp.float32)
                      * top_w[..., None], axis=1).astype(t.dtype)
    out = jnp.zeros_like(t)
    for e in range(N_EXPERTS):
        hdn = jax.nn.silu(t @ w_gate[e]) * (t @ w_up[e])
        out = out + (hdn * combine[:, e:e + 1]) @ w_down[e]
    return out.reshape(shape)


def setup_inputs(seed: int = 0) -> dict:
    key = jax.random.key(seed)
    ks = jax.random.split(key, 21)
    D, F, E, G = D_MODEL, D_EXPERT, N_EXPERTS, N_GROUPS
    nrm = lambda k, s, sc: jax.random.normal(k, s, jnp.float32) * sc
    gain = lambda k, s: 1.0 + 0.02 * jax.random.normal(k, s, jnp.float32)
    return {
        "x": nrm(ks[0], (BATCH, SEQ, D), 1.0),
        "meta_tokens": nrm(ks[1], (N_META, D), 1.0),
        "conv_norm_g": gain(ks[2], (N_A_LAYERS, D)),
        "conv_w_in": nrm(ks[3], (N_A_LAYERS, D, 3 * D), D ** -0.5),
        "conv_w": nrm(ks[4], (N_A_LAYERS, CONV_WIDTH, D), CONV_WIDTH ** -0.5),
        "conv_w_out": nrm(ks[5], (N_A_LAYERS, D, D), D ** -0.5),
        "kv_norm_g": gain(ks[6], (D,)),
        "w_kv": nrm(ks[7], (D, 2 * N_KV_HEADS * HEAD_DIM), D ** -0.5),
        "attn_norm_g": gain(ks[8], (N_B_LAYERS, D)),
        "w_q": nrm(ks[9], (N_B_LAYERS, D, N_HEADS * HEAD_DIM), D ** -0.5),
        "w_o": nrm(ks[10], (N_B_LAYERS, N_HEADS * HEAD_DIM, D), (N_HEADS * HEAD_DIM) ** -0.5),
        "sinks": nrm(ks[11], (N_B_LAYERS, N_HEADS), 0.5),
        "ffn_norm_g": gain(ks[12], (DEPTH, D)),
        "router_group_w": nrm(ks[13], (DEPTH, D, G), D ** -0.5),
        "router_group_b": nrm(ks[14], (DEPTH, G), 0.01),
        "router_expert_w": nrm(ks[15], (DEPTH, D, E), D ** -0.5),
        "router_expert_b": nrm(ks[16], (DEPTH, E), 0.01),
        "w_gate": nrm(ks[17], (DEPTH, E, D, F), D ** -0.5),
        "w_up": nrm(ks[18], (DEPTH, E, D, F), D ** -0.5),
        "w_down": nrm(ks[19], (DEPTH, E, F, D), F ** -0.5),
        "final_norm_g": gain(ks[20], (D,)),
    }


def reference(x, meta_tokens, conv_norm_g, conv_w_in, conv_w, conv_w_out, kv_norm_g, w_kv,
              attn_norm_g, w_q, w_o, sinks, ffn_norm_g, router_group_w, router_group_b,
              router_expert_w, router_expert_b, w_gate, w_up, w_down, final_norm_g):
    B = x.shape[0]
    meta = jnp.broadcast_to(meta_tokens.astype(x.dtype)[None], (B, N_META, x.shape[-1]))
    h = jnp.concatenate([meta, x], axis=1)
    pos = jnp.arange(h.shape[1], dtype=jnp.int32)
    k_band, v_band = None, None
    for layer in range(DEPTH):
        if layer < N_A_LAYERS:
            i = layer
            h = h + short_conv_mixer(rms_norm(h, conv_norm_g[i]), conv_w_in[i], conv_w[i], conv_w_out[i])
        else:
            j = layer - N_A_LAYERS
            if j == 0:
                k_band, v_band = shared_kv(h, kv_norm_g, w_kv, pos)
            h = h + sliding_window_attention(rms_norm(h, attn_norm_g[j]), w_q[j], w_o[j], sinks[j],
                                             k_band, v_band, pos)
        h = h + hierarchical_moe(rms_norm(h, ffn_norm_g[layer]), router_group_w[layer],
                                 router_group_b[layer], router_expert_w[layer], router_expert_b[layer],
                                 w_gate[layer], w_up[layer], w_down[layer])
    h = rms_norm(h, final_norm_g)
    return h[:, N_META:]
```

```python
import functools

import jax
import jax.numpy as jnp
from jax import lax
from jax.experimental import pallas as pl
from jax.experimental.pallas import tpu as pltpu

D_MODEL = 1024
BATCH = 2
SEQ = 8192
N_META = 16
N_HEADS = 16
HEAD_DIM = 64
N_KV_HEADS = 4
WINDOW = 128
ROPE_THETA = 10000.0
N_GROUPS = 4
EXPERTS_PER_GROUP = 8
N_EXPERTS = N_GROUPS * EXPERTS_PER_GROUP
D_EXPERT = 256
NORM_EPS = 1e-5
NEG_INF = -1e30

TM = 512
N_XT = BATCH * SEQ // TM
T_X = BATCH * SEQ
T_PAD0 = (N_XT + 1) * TM
T_VALID0 = T_X + N_META
HALF_SPLIT = SEQ
N_KEYS = 2 * N_EXPERTS
TMG = 256
LANES = 128
SLAB = D_MODEL // 2 // LANES
VMEM_LIMIT = 60 * 1024 * 1024


def _cdiv(a, b):
    return (a + b - 1) // b


def _rms_hat(x):
    return x * lax.rsqrt(jnp.mean(x * x, axis=-1, keepdims=True) + NORM_EPS)


def _pack_rows(xn):
    bits = lax.bitcast_convert_type(xn.astype(jnp.bfloat16).astype(jnp.float32), jnp.int32)
    half = D_MODEL // 2
    lo = (bits[:, :half] >> 16) & 0xFFFF
    hi = bits[:, half:] & jnp.int32(-65536)
    return lo | hi


def _store_slabs(ref, words):
    m = words.shape[0]
    for k in range(SLAB):
        ref[pl.ds(k, m, stride=SLAB), :] = words[:, LANES * k:LANES * (k + 1)]


def _load_slabs(ref, m):
    return jnp.concatenate([ref[pl.ds(k, m, stride=SLAB), :] for k in range(SLAB)], axis=1)


def _unpack_words(words):
    lo = lax.bitcast_convert_type(words << 16, jnp.float32)
    hi = lax.bitcast_convert_type(words & jnp.int32(-65536), jnp.float32)
    return lo, hi


def _route(xn, wr_ref, br_ref, run_scr, tok_base, valid_limit):
    logits = lax.dot_general(wr_ref[...], xn, (((1,), (1,)), ((), ())),
                             precision=lax.Precision.HIGHEST,
                             preferred_element_type=jnp.float32) + br_ref[...]
    g = logits[0:N_GROUPS]
    gmax = jnp.max(g, axis=0, keepdims=True)
    rid_g = lax.broadcasted_iota(jnp.int32, g.shape, 0).astype(jnp.float32)
    g_idx = jnp.min(jnp.where(g == gmax, rid_g, float(N_GROUPS)), axis=0, keepdims=True).astype(jnp.int32)
    g_w = 1.0 / jnp.sum(jnp.exp(g - gmax), axis=0, keepdims=True)
    e_sel = logits[8:8 + EXPERTS_PER_GROUP]
    for gi in range(1, N_GROUPS):
        lo = 8 + EXPERTS_PER_GROUP * gi
        e_sel = jnp.where(g_idx == gi, logits[lo:lo + EXPERTS_PER_GROUP], e_sel)
    rid_e = lax.broadcasted_iota(jnp.int32, e_sel.shape, 0).astype(jnp.float32)
    none = float(EXPERTS_PER_GROUP)
    m1 = jnp.max(e_sel, axis=0, keepdims=True)
    i1f = jnp.min(jnp.where(e_sel == m1, rid_e, none), axis=0, keepdims=True)
    e_rest = jnp.where(rid_e == i1f, -jnp.inf, e_sel)
    m2 = jnp.max(e_rest, axis=0, keepdims=True)
    i2 = jnp.min(jnp.where(e_rest == m2, rid_e, none), axis=0, keepdims=True).astype(jnp.int32)
    i1 = i1f.astype(jnp.int32)
    ex = jnp.exp(m2 - m1)
    den = 1.0 / (1.0 + ex)
    w1 = den * g_w
    w2 = ex * den * g_w

    tok = tok_base + lax.broadcasted_iota(jnp.int32, (1, TM), 1)
    half = jnp.where(tok >= HALF_SPLIT, N_EXPERTS, 0)
    key1 = half + g_idx * EXPERTS_PER_GROUP + i1
    key2 = half + g_idx * EXPERTS_PER_GROUP + i2
    kid = lax.broadcasted_iota(jnp.int32, (N_KEYS, TM), 0)
    validf = jnp.where(tok < valid_limit, 1.0, 0.0)
    oh1 = jnp.where(kid == key1, validf, 0.0)
    oh2 = jnp.where(kid == key2, validf, 0.0)
    cnt = oh1 + oh2
    tri = jnp.where(lax.broadcasted_iota(jnp.int32, (TM, TM), 0)
                    <= lax.broadcasted_iota(jnp.int32, (TM, TM), 1), 1.0, 0.0).astype(jnp.bfloat16)
    cum = jnp.dot(cnt.astype(jnp.bfloat16), tri, preferred_element_type=jnp.float32)
    before = run_scr[...] + (cum - cnt)
    rank1 = jnp.sum(oh1 * before, axis=0, keepdims=True).astype(jnp.int32)
    rank2 = jnp.sum(oh2 * before, axis=0, keepdims=True).astype(jnp.int32)
    run_scr[...] = run_scr[...] + cum[:, TM - 1:TM]
    code = jnp.concatenate([key1 * 65536 + rank1, key2 * 65536 + rank2], axis=0)
    wts = jnp.concatenate([w1, w2], axis=0)
    return code, wts


def _ffn_prologue(h_new, g_ffn_ref, wr_ref, br_ref, run_scr, tok_base, valid_limit,
                  xp_ref, code_ref, wts_ref, cnt_ref):
    xn2 = _rms_hat(h_new) * g_ffn_ref[...]
    _store_slabs(xp_ref, _pack_rows(xn2))
    code, wts = _route(xn2, wr_ref, br_ref, run_scr, tok_base, valid_limit)
    code_ref[...] = code
    wts_ref[...] = wts
    cnt_ref[...] = jnp.broadcast_to(run_scr[...], cnt_ref.shape).astype(jnp.int32)


NC = 512


def _mixer0_kernel(x_ref, meta_ref, g_conv_ref, w_in_ref, cw_ref, w_out_ref,
                   g_ffn_ref, wr_ref, br_ref,
                   h1_ref, xp_ref, code_ref, wts_ref, cnt_ref,
                   h0_scr, acc_scr, carry_scr, meta_carry_scr, run_scr):
    s = pl.program_id(0)

    @pl.when(s == 0)
    def _():
        h0_scr[...] = jnp.zeros_like(h0_scr)
        h0_scr[0:N_META, :] = meta_ref[...]
        carry_scr[...] = jnp.zeros_like(carry_scr)
        run_scr[...] = jnp.zeros_like(run_scr)

    @pl.when(s > 0)
    def _():
        h0_scr[...] = x_ref[...]

    @pl.when(s == 1 + N_XT // BATCH)
    def _():
        carry_scr[...] = meta_carry_scr[...]

    h0 = h0_scr[...]
    xn = (_rms_hat(h0) * g_conv_ref[...]).astype(jnp.bfloat16)
    row = lax.broadcasted_iota(jnp.int32, (TM, NC), 0)
    for c in range(D_MODEL // NC):
        cols = slice(NC * c, NC * (c + 1))
        gate_c = jnp.dot(xn, w_in_ref[:, D_MODEL + NC * c:D_MODEL + NC * (c + 1)],
                         preferred_element_type=jnp.float32)
        val = jnp.dot(xn, w_in_ref[:, 2 * D_MODEL + NC * c:2 * D_MODEL + NC * (c + 1)],
                      preferred_element_type=jnp.float32)
        u = gate_c * val
        tail = carry_scr[:, cols]
        c1 = tail[7:8, :]
        c2 = tail[6:7, :]
        um1 = jnp.where(row == 0, c1, pltpu.roll(u, 1, 0))
        um2 = jnp.where(row == 0, c2, jnp.where(row == 1, c1, pltpu.roll(u, 2, 0)))
        conv = um2 * cw_ref[0:1, cols] + um1 * cw_ref[1:2, cols] + u * cw_ref[2:3, cols]

        @pl.when(s == 0)
        def _():
            carry_scr[:, cols] = u[N_META - 8:N_META, :]
            meta_carry_scr[:, cols] = u[N_META - 8:N_META, :]

        @pl.when(s > 0)
        def _():
            carry_scr[:, cols] = u[TM - 8:TM, :]

        gate_b = jnp.dot(xn, w_in_ref[:, cols], preferred_element_type=jnp.float32)
        gated = (gate_b * conv).astype(jnp.bfloat16)
        part = jnp.dot(gated, w_out_ref[cols, :], preferred_element_type=jnp.float32)
        if c == 0:
            acc_scr[...] = h0 + part
        else:
            acc_scr[...] = acc_scr[...] + part

    h1 = acc_scr[...]
    h1_ref[...] = h1
    tile = jnp.where(s == 0, N_XT, s - 1)
    _ffn_prologue(h1, g_ffn_ref, wr_ref, br_ref, run_scr, tile * TM, T_VALID0,
                  xp_ref, code_ref, wts_ref, cnt_ref)


def _tile_first_meta(s):
    return jnp.where(s == 0, N_XT, s - 1)


def _const_spec(shape):
    return pl.BlockSpec(shape, lambda s: (0,) * len(shape))


def _mixer0(x2d, meta, g_conv, w_in, cw, w_out, g_ffn, wr, br):
    out_shape = [
        jax.ShapeDtypeStruct((T_PAD0, D_MODEL), jnp.float32),
        jax.ShapeDtypeStruct((T_PAD0 * SLAB, LANES), jnp.int32),
        jax.ShapeDtypeStruct((2, T_PAD0), jnp.int32),
        jax.ShapeDtypeStruct((2, T_PAD0), jnp.float32),
        jax.ShapeDtypeStruct((N_KEYS, LANES), jnp.int32),
    ]
    return pl.pallas_call(
        _mixer0_kernel,
        grid=(N_XT + 1,),
        in_specs=[
            pl.BlockSpec((TM, D_MODEL), lambda s: (jnp.maximum(s - 1, 0), 0)),
            _const_spec((N_META, D_MODEL)),
            _const_spec((1, D_MODEL)),
            _const_spec((D_MODEL, 3 * D_MODEL)),
            _const_spec((3, D_MODEL)),
            _const_spec((D_MODEL, D_MODEL)),
            _const_spec((1, D_MODEL)),
            _const_spec((N_KEYS, D_MODEL)),
            _const_spec((N_KEYS, 1)),
        ],
        out_specs=[
            pl.BlockSpec((TM, D_MODEL), lambda s: (_tile_first_meta(s), 0)),
            pl.BlockSpec((TM * SLAB, LANES), lambda s: (_tile_first_meta(s), 0)),
            pl.BlockSpec((2, TM), lambda s: (0, _tile_first_meta(s))),
            pl.BlockSpec((2, TM), lambda s: (0, _tile_first_meta(s))),
            _const_spec((N_KEYS, LANES)),
        ],
        out_shape=out_shape,
        scratch_shapes=[
            pltpu.VMEM((TM, D_MODEL), jnp.float32),
            pltpu.VMEM((TM, D_MODEL), jnp.float32),
            pltpu.VMEM((8, D_MODEL), jnp.float32),
            pltpu.VMEM((8, D_MODEL), jnp.float32),
            pltpu.VMEM((N_KEYS, 1), jnp.float32),
        ],
        compiler_params=pltpu.CompilerParams(
            dimension_semantics=("arbitrary",), vmem_limit_bytes=VMEM_LIMIT),
        name="mixer0_route",
    )(x2d, meta, g_conv, w_in, cw, w_out, g_ffn, wr, br)


def _n_tiles(t_valid):
    return _cdiv(2 * t_valid, TMG)


def _n_visits(t_valid):
    return _n_tiles(t_valid) + N_KEYS - 1


def _positions_kernel(t_pad, t_valid, cnt_ref, code_ref, pos_ref, gstart_ref, vtile_ref, vgroup_ref,
                      nvis_ref):
    n_vis = _n_visits(t_valid)

    def offs(g, acc):
        gstart_ref[g] = acc
        return acc + cnt_ref[g]

    total = lax.fori_loop(0, N_KEYS, offs, jnp.int32(0))
    gstart_ref[N_KEYS] = total

    def per_group(g, carry):
        v, last_g = carry
        c = cnt_ref[g]
        start = gstart_ref[g]
        t0 = start // TMG
        t1 = jnp.where(c > 0, (start + c - 1) // TMG + 1, t0)

        def per_tile(tt, vv):
            vtile_ref[vv] = tt
            vgroup_ref[vv] = g
            return vv + 1

        v = lax.fori_loop(t0, t1, per_tile, v)
        return v, jnp.where(c > 0, g, last_g)

    nvis, last_g = lax.fori_loop(0, N_KEYS, per_group, (jnp.int32(0), jnp.int32(0)))
    nvis_ref[0] = nvis

    def pad(vv, c):
        vtile_ref[vv] = _n_tiles(t_valid) - 1
        vgroup_ref[vv] = last_g
        return c

    lax.fori_loop(nvis, n_vis, pad, 0)

    code = code_ref[...]
    key = code >> 16
    pos = code & 0xFFFF
    for g in range(N_KEYS):
        pos = pos + jnp.where(key == g, gstart_ref[g], 0)
    pos_ref[...] = pos


def _positions(cnt, code, t_pad, t_valid):
    n_vis = _n_visits(t_valid)
    smem = pl.BlockSpec(memory_space=pltpu.SMEM)
    return pl.pallas_call(
        functools.partial(_positions_kernel, t_pad, t_valid),
        in_specs=[smem, pl.BlockSpec(memory_space=pltpu.VMEM)],
        out_specs=[pl.BlockSpec(memory_space=pltpu.VMEM), smem, smem, smem, smem],
        out_shape=[
            jax.ShapeDtypeStruct((2, t_pad), jnp.int32),
            jax.ShapeDtypeStruct((N_KEYS + 1,), jnp.int32),
            jax.ShapeDtypeStruct((n_vis,), jnp.int32),
            jax.ShapeDtypeStruct((n_vis,), jnp.int32),
            jax.ShapeDtypeStruct((1,), jnp.int32),
        ],
        name="sort_positions",
    )(cnt, code)


SRC_UNROLL = 8


def _invert_kernel(t_valid, r_pad, pos1_ref, pos2_ref, src_ref):
    def tail(r, c):
        src_ref[r] = 0
        return c

    lax.fori_loop(2 * t_valid, r_pad, tail, 0)

    def body(i, c):
        for j in range(SRC_UNROLL):
            t = i * SRC_UNROLL + j
            src_ref[pos1_ref[t]] = t
            src_ref[pos2_ref[t]] = t
        return c

    lax.fori_loop(0, t_valid // SRC_UNROLL, body, 0)


def _invert(pos1, pos2, t_valid):
    r_pad = _n_tiles(t_valid) * TMG
    smem = pl.BlockSpec(memory_space=pltpu.SMEM)
    return pl.pallas_call(
        functools.partial(_invert_kernel, t_valid, r_pad),
        in_specs=[smem, smem],
        out_specs=smem,
        out_shape=jax.ShapeDtypeStruct((r_pad,), jnp.int32),
        name="sort_invert",
    )(pos1, pos2)


GATHER_UNROLL = 8


def _ffn_kernel(t_pad, vtile_ref, vgroup_ref, nvis_ref, gstart_ref, src_ref,
                xp_hbm, wg_ref, wu_ref, wd_ref, ys_ref,
                xp_scr, xs_scr, acc_scr, sem):
    v = pl.program_id(0)

    @pl.when(v == 0)
    def _():
        cp = pltpu.make_async_copy(xp_hbm, xp_scr, sem)
        cp.start()
        cp.wait()

    tile = vtile_ref[v]
    g = vgroup_ref[v]
    active = v < nvis_ref[0]
    first = jnp.logical_or(v == 0, tile != vtile_ref[jnp.maximum(v - 1, 0)])

    @pl.when(jnp.logical_and(active, first))
    def _():
        base = tile * TMG

        def gather(i, c):
            for j in range(GATHER_UNROLL):
                r = i * GATHER_UNROLL + j
                t = src_ref[base + r]
                xs_scr[pl.ds(pl.multiple_of(r * SLAB, SLAB), SLAB), :] = (
                    xp_scr[pl.ds(pl.multiple_of(t * SLAB, SLAB), SLAB), :])
            return c

        lax.fori_loop(0, TMG // GATHER_UNROLL, gather, 0)
        acc_scr[...] = jnp.zeros_like(acc_scr)

    @pl.when(active)
    def _():
        lo, hi = _unpack_words(_load_slabs(xs_scr, TMG))
        xs = jnp.concatenate([lo, hi], axis=1).astype(jnp.bfloat16)
        hg = jnp.dot(xs, wg_ref[0].astype(jnp.bfloat16), preferred_element_type=jnp.float32)
        hu = jnp.dot(xs, wu_ref[0].astype(jnp.bfloat16), preferred_element_type=jnp.float32)
        hdn = (hg * jax.nn.sigmoid(hg) * hu).astype(jnp.bfloat16)
        y = jnp.dot(hdn, wd_ref[0].astype(jnp.bfloat16), preferred_element_type=jnp.float32)
        rows = tile * TMG + lax.broadcasted_iota(jnp.int32, (TMG, 1), 0)
        mine = jnp.logical_and(rows >= gstart_ref[g], rows < gstart_ref[g + 1])
        acc = jnp.where(mine, y, acc_scr[...])
        acc_scr[...] = acc
        _store_slabs(ys_ref, _pack_rows(acc))


def _ffn(vtile, vgroup, nvis, gstart, src, xp, wg, wu, wd, t_pad, t_valid):
    n_vis = _n_visits(t_valid)
    r_pad = _n_tiles(t_valid) * TMG

    def w_map(v, vt, vg, nv, gs, sr):
        return (vg[v] % N_EXPERTS, 0, 0)

    return pl.pallas_call(
        functools.partial(_ffn_kernel, t_pad),
        grid_spec=pltpu.PrefetchScalarGridSpec(
            num_scalar_prefetch=5,
            grid=(n_vis,),
            in_specs=[
                pl.BlockSpec(memory_space=pl.ANY),
                pl.BlockSpec((1, D_MODEL, D_EXPERT), w_map),
                pl.BlockSpec((1, D_MODEL, D_EXPERT), w_map),
                pl.BlockSpec((1, D_EXPERT, D_MODEL), w_map),
            ],
            out_specs=pl.BlockSpec((TMG * SLAB, LANES), lambda v, vt, vg, nv, gs, sr: (vt[v], 0)),
            scratch_shapes=[
                pltpu.VMEM((t_pad * SLAB, LANES), jnp.int32),
                pltpu.VMEM((TMG * SLAB, LANES), jnp.int32),
                pltpu.VMEM((TMG, D_MODEL), jnp.float32),
                pltpu.SemaphoreType.DMA,
            ],
        ),
        out_shape=jax.ShapeDtypeStruct((r_pad * SLAB, LANES), jnp.int32),
        compiler_params=pltpu.CompilerParams(
            dimension_semantics=("arbitrary",), vmem_limit_bytes=VMEM_LIMIT),
        name="expert_ffn",
    )(vtile, vgroup, nvis, gstart, src, xp, wg, wu, wd)


def _combine_rows(pos1_ref, pos2_ref, ys_hbm, ys_scr, y1_scr, y2_scr, sem, tile, half_rows, rh):
    tiles_per_half = HALF_SPLIT // TM
    half = jnp.where(tile >= tiles_per_half, 1, 0)
    base = half * half_rows

    @pl.when(jnp.logical_or(tile == 0, tile == tiles_per_half))
    def _():
        cp = pltpu.make_async_copy(
            ys_hbm.at[pl.ds(pl.multiple_of(base * SLAB, TMG * SLAB), rh * SLAB), :], ys_scr, sem)
        cp.start()
        cp.wait()

    tok0 = tile * TM

    def gather(i, c):
        for j in range(GATHER_UNROLL):
            r = i * GATHER_UNROLL + j
            p1 = pos1_ref[tok0 + r] - base
            p2 = pos2_ref[tok0 + r] - base
            dst = pl.ds(pl.multiple_of(r * SLAB, SLAB), SLAB)
            y1_scr[dst, :] = ys_scr[pl.ds(pl.multiple_of(p1 * SLAB, SLAB), SLAB), :]
            y2_scr[dst, :] = ys_scr[pl.ds(pl.multiple_of(p2 * SLAB, SLAB), SLAB), :]
        return c

    lax.fori_loop(0, TM // GATHER_UNROLL, gather, 0)


def _weighted_sum(y1_scr, y2_scr, w_ref):
    lo1, hi1 = _unpack_words(_load_slabs(y1_scr, TM))
    lo2, hi2 = _unpack_words(_load_slabs(y2_scr, TM))
    w1 = w_ref[:, 0:1]
    w2 = w_ref[:, 1:2]
    return jnp.concatenate([w1 * lo1 + w2 * lo2, w1 * hi1 + w2 * hi2], axis=1)


def _combine_kernel(half_rows, rh, final_norm, pos1_ref, pos2_ref, h_ref, w_ref, g_ref, ys_hbm,
                    o_ref, ys_scr, y1_scr, y2_scr, sem):
    tile = pl.program_id(0)
    _combine_rows(pos1_ref, pos2_ref, ys_hbm, ys_scr, y1_scr, y2_scr, sem, tile, half_rows, rh)
    h = h_ref[...] + _weighted_sum(y1_scr, y2_scr, w_ref)
    if final_norm:
        h = _rms_hat(h) * g_ref[...]
    o_ref[...] = h


def _combine(pos1, pos2, h, wcols, g, ys, t_valid, final_norm):
    n_tiles = _cdiv(t_valid, TM)
    half_rows = 2 * HALF_SPLIT
    rh = _n_tiles(t_valid) * TMG - half_rows
    return pl.pallas_call(
        functools.partial(_combine_kernel, half_rows, rh, final_norm),
        grid_spec=pltpu.PrefetchScalarGridSpec(
            num_scalar_prefetch=2,
            grid=(n_tiles,),
            in_specs=[
                pl.BlockSpec((TM, D_MODEL), lambda i, p1, p2: (i, 0)),
                pl.BlockSpec((TM, 2), lambda i, p1, p2: (i, 0)),
                pl.BlockSpec((1, D_MODEL), lambda i, p1, p2: (0, 0)),
                pl.BlockSpec(memory_space=pl.ANY),
            ],
            out_specs=pl.BlockSpec((TM, D_MODEL), lambda i, p1, p2: (i, 0)),
            scratch_shapes=[
                pltpu.VMEM((rh * SLAB, LANES), jnp.int32),
                pltpu.VMEM((TM * SLAB, LANES), jnp.int32),
                pltpu.VMEM((TM * SLAB, LANES), jnp.int32),
                pltpu.SemaphoreType.DMA,
            ],
        ),
        out_shape=jax.ShapeDtypeStruct((n_tiles * TM, D_MODEL), jnp.float32),
        compiler_params=pltpu.CompilerParams(
            dimension_semantics=("arbitrary",), vmem_limit_bytes=VMEM_LIMIT),
        name="moe_combine_final" if final_norm else "moe_combine",
    )(pos1, pos2, h, wcols, g, ys)


QB = WINDOW
KV_W = N_KV_HEADS * HEAD_DIM
N_QB = TM // QB
META_ROW0 = QB - N_META


def _rope(x, cos, sin_signed):
    q = lax.broadcasted_iota(jnp.int32, x.shape, 1) // (HEAD_DIM // 2)
    swapped = jnp.where(q % 2 == 0, pltpu.roll(x, LANES - HEAD_DIM // 2, 1),
                        pltpu.roll(x, HEAD_DIM // 2, 1))
    return x * cos + swapped * sin_signed


def _dup_heads(blk):
    lane = lax.broadcasted_iota(jnp.int32, blk.shape, 1)
    rolled = pltpu.roll(blk, HEAD_DIM, 1)
    return jnp.where(lane < HEAD_DIM, blk, rolled), jnp.where(lane < HEAD_DIM, rolled, blk)


def _kv_rows(xhat, g_kv_ref, w_kv_ref, cos, sin_signed):
    xk = (xhat * g_kv_ref[...]).astype(jnp.bfloat16)
    kv = jnp.dot(xk, w_kv_ref[...], preferred_element_type=jnp.float32)
    ks, vs = [], []
    for b in range(KV_W // LANES):
        kb = _rope(kv[:, LANES * b:LANES * (b + 1)], cos, sin_signed)
        vb = kv[:, KV_W + LANES * b:KV_W + LANES * (b + 1)]
        ks.extend(_dup_heads(kb))
        vs.extend(_dup_heads(vb))
    return (jnp.concatenate(ks, axis=1).astype(jnp.bfloat16),
            jnp.concatenate(vs, axis=1).astype(jnp.bfloat16))


def _attn_kernel(h_ref, cos_ref, sin_ref, g_attn_ref, g_kv_ref, w_q_ref, w_kv_ref, w_o_ref, sink_ref,
                 g_ffn_ref, wr_ref, br_ref,
                 h3_ref, xp_ref, code_ref, wts_ref, cnt_ref,
                 k_scr, v_scr, mk_scr, mv_scr, o_scr, run_scr):
    s = pl.program_id(0)
    tiles_per_batch = N_XT // BATCH

    @pl.when(s == 0)
    def _():
        run_scr[...] = jnp.zeros_like(run_scr)
        xhat = _rms_hat(h_ref[0:N_META, :])
        k, v = _kv_rows(xhat, g_kv_ref, w_kv_ref, cos_ref[0:N_META, :], sin_ref[0:N_META, :])
        for scr, val in ((mk_scr, k), (mv_scr, v)):
            scr[...] = jnp.zeros_like(scr)
            scr[META_ROW0:QB, :] = val

    @pl.when(s > 0)
    def _():
        batch_first = jnp.logical_or(s == 1, s == 1 + tiles_per_batch)

        @pl.when(batch_first)
        def _():
            k_scr[0:QB, :] = mk_scr[...]
            v_scr[0:QB, :] = mv_scr[...]

        h = h_ref[...]
        xhat = _rms_hat(h)
        cos = cos_ref[...]
        sin = sin_ref[...]
        k, v = _kv_rows(xhat, g_kv_ref, w_kv_ref, cos, sin)
        k_scr[QB:QB + TM, :] = k
        v_scr[QB:QB + TM, :] = v

        xq = (xhat * g_attn_ref[...]).astype(jnp.bfloat16)
        q = jnp.dot(xq, w_q_ref[...], preferred_element_type=jnp.float32)
        lane = lax.broadcasted_iota(jnp.int32, (QB, LANES), 1)
        qi = lax.broadcasted_iota(jnp.int32, (QB, 2 * QB), 0)
        kj = lax.broadcasted_iota(jnp.int32, (QB, 2 * QB), 1)
        band = jnp.logical_and(kj > qi, kj <= qi + QB)
        for hb in range(N_HEADS // 2):
            qb_all = _rope(q[:, LANES * hb:LANES * (hb + 1)], cos, sin) * (HEAD_DIM ** -0.5)
            kvh = (2 * hb) // (N_HEADS // N_KV_HEADS)
            for b in range(N_QB):
                qb = qb_all[QB * b:QB * (b + 1), :]
                qs = jnp.concatenate([jnp.where(lane < HEAD_DIM, qb, 0.0),
                                      jnp.where(lane < HEAD_DIM, 0.0, qb)], axis=0).astype(jnp.bfloat16)
                kk = k_scr[QB * b:QB * (b + 2), LANES * kvh:LANES * (kvh + 1)]
                vv = v_scr[QB * b:QB * (b + 2), LANES * kvh:LANES * (kvh + 1)]
                sc = lax.dot_general(qs, kk, (((1,), (1,)), ((), ())),
                                     preferred_element_type=jnp.float32)
                ok = band
                if b == 0:
                    ok = jnp.logical_and(band, kj >= jnp.where(batch_first, META_ROW0, 0))
                outs = []
                for j in range(2):
                    sj = jnp.where(ok, sc[QB * j:QB * (j + 1), :], NEG_INF)
                    sink = sink_ref[2 * hb + j]
                    m = jnp.maximum(jnp.max(sj, axis=-1, keepdims=True), sink)
                    p = jnp.exp(sj - m)
                    den = jnp.sum(p, axis=-1, keepdims=True) + jnp.exp(sink - m)
                    pj = (p / den).astype(jnp.bfloat16)
                    outs.append(jnp.dot(pj, vv, preferred_element_type=jnp.float32))
                o_scr[QB * b:QB * (b + 1), LANES * hb:LANES * (hb + 1)] = jnp.where(
                    lane < HEAD_DIM, outs[0], outs[1]).astype(jnp.bfloat16)

        k_scr[0:QB, :] = k_scr[TM:TM + QB, :]
        v_scr[0:QB, :] = v_scr[TM:TM + QB, :]

        h3 = h + jnp.dot(o_scr[...], w_o_ref[...], preferred_element_type=jnp.float32)
        h3_ref[...] = h3
        _ffn_prologue(h3, g_ffn_ref, wr_ref, br_ref, run_scr, (s - 1) * TM, T_X,
                      xp_ref, code_ref, wts_ref, cnt_ref)


def _attn(h2, cos_t, sin_t, g_attn, g_kv, w_q, w_kv, w_o, sinks, g_ffn, wr, br):
    def tile_x(s):
        return jnp.maximum(s - 1, 0)

    def rope_tile(s):
        per_batch = N_XT // BATCH
        return (jnp.where(s == 0, per_batch, (s - 1) % per_batch), 0)

    out_shape = [
        jax.ShapeDtypeStruct((T_X, D_MODEL), jnp.float32),
        jax.ShapeDtypeStruct((T_X * SLAB, LANES), jnp.int32),
        jax.ShapeDtypeStruct((2, T_X), jnp.int32),
        jax.ShapeDtypeStruct((2, T_X), jnp.float32),
        jax.ShapeDtypeStruct((N_KEYS, LANES), jnp.int32),
    ]
    return pl.pallas_call(
        _attn_kernel,
        grid=(N_XT + 1,),
        in_specs=[
            pl.BlockSpec((TM, D_MODEL), lambda s: (_tile_first_meta(s), 0)),
            pl.BlockSpec((TM, LANES), rope_tile),
            pl.BlockSpec((TM, LANES), rope_tile),
            _const_spec((1, D_MODEL)),
            _const_spec((1, D_MODEL)),
            _const_spec((D_MODEL, D_MODEL)),
            _const_spec((D_MODEL, 2 * KV_W)),
            _const_spec((D_MODEL, D_MODEL)),
            pl.BlockSpec(memory_space=pltpu.SMEM),
            _const_spec((1, D_MODEL)),
            _const_spec((N_KEYS, D_MODEL)),
            _const_spec((N_KEYS, 1)),
        ],
        out_specs=[
            pl.BlockSpec((TM, D_MODEL), lambda s: (tile_x(s), 0)),
            pl.BlockSpec((TM * SLAB, LANES), lambda s: (tile_x(s), 0)),
            pl.BlockSpec((2, TM), lambda s: (0, tile_x(s))),
            pl.BlockSpec((2, TM), lambda s: (0, tile_x(s))),
            _const_spec((N_KEYS, LANES)),
        ],
        out_shape=out_shape,
        scratch_shapes=[
            pltpu.VMEM((QB + TM, N_KV_HEADS * LANES), jnp.bfloat16),
            pltpu.VMEM((QB + TM, N_KV_HEADS * LANES), jnp.bfloat16),
            pltpu.VMEM((QB, N_KV_HEADS * LANES), jnp.bfloat16),
            pltpu.VMEM((QB, N_KV_HEADS * LANES), jnp.bfloat16),
            pltpu.VMEM((TM, D_MODEL), jnp.bfloat16),
            pltpu.VMEM((N_KEYS, 1), jnp.float32),
        ],
        compiler_params=pltpu.CompilerParams(
            dimension_semantics=("arbitrary",), vmem_limit_bytes=VMEM_LIMIT),
        name="attn_route",
    )(h2, cos_t, sin_t, g_attn, g_kv, w_q, w_kv, w_o, sinks, g_ffn, wr, br)


def _router_rows(rg_w, rg_b, re_w, re_b):
    wr = jnp.zeros((N_KEYS, D_MODEL), jnp.float32)
    wr = wr.at[0:N_GROUPS].set(rg_w.T).at[8:8 + N_EXPERTS].set(re_w.T)
    br = jnp.zeros((N_KEYS, 1), jnp.float32)
    br = br.at[0:N_GROUPS, 0].set(rg_b).at[8:8 + N_EXPERTS, 0].set(re_b)
    return wr, br


def _rope_tables():
    half = HEAD_DIM // 2
    inv_freq = ROPE_THETA ** (-jnp.arange(half, dtype=jnp.float32) / half)
    pos = jnp.concatenate([N_META + jnp.arange(SEQ), jnp.arange(TM)]).astype(jnp.float32)
    ang = pos[:, None] * inv_freq[None, :]
    cos = jnp.cos(ang)
    sin = jnp.sin(ang)
    return jnp.tile(cos, (1, 4)), jnp.concatenate([-sin, sin, -sin, sin], axis=1)


def _moe(cnt, code, wts, xp, h, wg, wu, wd, g_final, t_pad, t_valid, final_norm):
    pos, gstart, vtile, vgroup, nvis = _positions(cnt[:, 0], code, t_pad, t_valid)
    src = _invert(pos[0], pos[1], t_valid)
    ys = _ffn(vtile, vgroup, nvis, gstart, src, xp, wg, wu, wd, t_pad, t_valid)
    return _combine(pos[0], pos[1], h, wts.T, g_final, ys, t_valid, final_norm)


def kernel(x, meta_tokens, conv_norm_g, conv_w_in, conv_w, conv_w_out, kv_norm_g, w_kv, attn_norm_g,
           w_q, w_o, sinks, ffn_norm_g, router_group_w, router_group_b, router_expert_w,
           router_expert_b, w_gate, w_up, w_down, final_norm_g):
    bf = jnp.bfloat16
    x2d = x.reshape(T_X, D_MODEL)
    wr0, br0 = _router_rows(router_group_w[0], router_group_b[0], router_expert_w[0], router_expert_b[0])
    wr1, br1 = _router_rows(router_group_w[1], router_group_b[1], router_expert_w[1], router_expert_b[1])
    g_final = final_norm_g.reshape(1, D_MODEL)

    h1, xp0, code0, wts0, cnt0 = _mixer0(
        x2d, meta_tokens, conv_norm_g[0].reshape(1, D_MODEL), conv_w_in[0].astype(bf), conv_w[0],
        conv_w_out[0].astype(bf), ffn_norm_g[0].reshape(1, D_MODEL), wr0, br0)
    h2 = _moe(cnt0, code0, wts0, xp0, h1, w_gate[0], w_up[0], w_down[0], g_final,
              T_PAD0, T_VALID0, False)

    cos_t, sin_t = _rope_tables()
    h3, xp1, code1, wts1, cnt1 = _attn(
        h2, cos_t, sin_t, attn_norm_g[0].reshape(1, D_MODEL), kv_norm_g.reshape(1, D_MODEL),
        w_q[0].astype(bf), w_kv.astype(bf), w_o[0].astype(bf), sinks[0], ffn_norm_g[1].reshape(1, D_MODEL),
        wr1, br1)
    out = _moe(cnt1, code1, wts1, xp1, h3, w_gate[1], w_up[1], w_down[1], g_final,
               T_X, T_X, True)
    return out.reshape(BATCH, SEQ, D_MODEL)
```

```python
import functools

import jax
import jax.numpy as jnp
from jax import lax
from jax.experimental import pallas as pl
from jax.experimental.pallas import tpu as pltpu

D_MODEL = 1024
BATCH = 2
SEQ = 8192
N_META = 16
N_HEADS = 16
HEAD_DIM = 64
N_KV_HEADS = 4
WINDOW = 128
ROPE_THETA = 10000.0
N_GROUPS = 4
EXPERTS_PER_GROUP = 8
N_EXPERTS = N_GROUPS * EXPERTS_PER_GROUP
D_EXPERT = 256
NORM_EPS = 1e-5
NEG_INF = -1e30

TM = 512
N_XT = BATCH * SEQ // TM
T_X = BATCH * SEQ
T_PAD0 = (N_XT + 1) * TM
T_VALID0 = T_X + N_META
HALF_SPLIT = SEQ
N_KEYS = 2 * N_EXPERTS
TMG = 256
LANES = 128
SLAB = D_MODEL // 2 // LANES
VMEM_LIMIT = 60 * 1024 * 1024


def _cdiv(a, b):
    return (a + b - 1) // b


def _rms_hat(x):
    return x * lax.rsqrt(jnp.mean(x * x, axis=-1, keepdims=True) + NORM_EPS)


def _pack_rows(xn):
    bits = lax.bitcast_convert_type(xn.astype(jnp.bfloat16).astype(jnp.float32), jnp.int32)
    half = D_MODEL // 2
    lo = (bits[:, :half] >> 16) & 0xFFFF
    hi = bits[:, half:] & jnp.int32(-65536)
    return lo | hi


def _store_slabs(ref, words):
    m = words.shape[0]
    for k in range(SLAB):
        ref[pl.ds(k, m, stride=SLAB), :] = words[:, LANES * k:LANES * (k + 1)]


def _load_slabs(ref, m):
    return jnp.concatenate([ref[pl.ds(k, m, stride=SLAB), :] for k in range(SLAB)], axis=1)


def _unpack_words(words):
    lo = lax.bitcast_convert_type(words << 16, jnp.float32)
    hi = lax.bitcast_convert_type(words & jnp.int32(-65536), jnp.float32)
    return lo, hi


def _route(xn, wr_ref, br_ref, run_scr, tok_base, valid_limit):
    logits = lax.dot_general(wr_ref[...], xn, (((1,), (1,)), ((), ())),
                             precision=lax.Precision.HIGHEST,
                             preferred_element_type=jnp.float32) + br_ref[...]
    g = logits[0:N_GROUPS]
    gmax = jnp.max(g, axis=0, keepdims=True)
    rid_g = lax.broadcasted_iota(jnp.int32, g.shape, 0).astype(jnp.float32)
    g_idx = jnp.min(jnp.where(g == gmax, rid_g, float(N_GROUPS)), axis=0, keepdims=True).astype(jnp.int32)
    g_w = 1.0 / jnp.sum(jnp.exp(g - gmax), axis=0, keepdims=True)
    e_sel = logits[8:8 + EXPERTS_PER_GROUP]
    for gi in range(1, N_GROUPS):
        lo = 8 + EXPERTS_PER_GROUP * gi
        e_sel = jnp.where(g_idx == gi, logits[lo:lo + EXPERTS_PER_GROUP], e_sel)
    rid_e = lax.broadcasted_iota(jnp.int32, e_sel.shape, 0).astype(jnp.float32)
    none = float(EXPERTS_PER_GROUP)
    m1 = jnp.max(e_sel, axis=0, keepdims=True)
    i1f = jnp.min(jnp.where(e_sel == m1, rid_e, none), axis=0, keepdims=True)
    e_rest = jnp.where(rid_e == i1f, -jnp.inf, e_sel)
    m2 = jnp.max(e_rest, axis=0, keepdims=True)
    i2 = jnp.min(jnp.where(e_rest == m2, rid_e, none), axis=0, keepdims=True).astype(jnp.int32)
    i1 = i1f.astype(jnp.int32)
    ex = jnp.exp(m2 - m1)
    den = 1.0 / (1.0 + ex)
    w1 = den * g_w
    w2 = ex * den * g_w

    tok = tok_base + lax.broadcasted_iota(jnp.int32, (1, TM), 1)
    half = jnp.where(tok >= HALF_SPLIT, N_EXPERTS, 0)
    key1 = half + g_idx * EXPERTS_PER_GROUP + i1
    key2 = half + g_idx * EXPERTS_PER_GROUP + i2
    kid = lax.broadcasted_iota(jnp.int32, (N_KEYS, TM), 0)
    validf = jnp.where(tok < valid_limit, 1.0, 0.0)
    oh1 = jnp.where(kid == key1, validf, 0.0)
    oh2 = jnp.where(kid == key2, validf, 0.0)
    cnt = oh1 + oh2
    tri = jnp.where(lax.broadcasted_iota(jnp.int32, (TM, TM), 0)
                    <= lax.broadcasted_iota(jnp.int32, (TM, TM), 1), 1.0, 0.0).astype(jnp.bfloat16)
    cum = jnp.dot(cnt.astype(jnp.bfloat16), tri, preferred_element_type=jnp.float32)
    before = run_scr[...] + (cum - cnt)
    rank1 = jnp.sum(oh1 * before, axis=0, keepdims=True).astype(jnp.int32)
    rank2 = jnp.sum(oh2 * before, axis=0, keepdims=True).astype(jnp.int32)
    run_scr[...] = run_scr[...] + cum[:, TM - 1:TM]
    code = jnp.concatenate([key1 * 65536 + rank1, key2 * 65536 + rank2], axis=0)
    wts = jnp.concatenate([w1, w2], axis=0)
    return code, wts


def _ffn_prologue(h_new, g_ffn_ref, wr_ref, br_ref, run_scr, tok_base, valid_limit,
                  xp_ref, code_ref, wts_ref, cnt_ref):
    xn2 = _rms_hat(h_new) * g_ffn_ref[...]
    _store_slabs(xp_ref, _pack_rows(xn2))
    code, wts = _route(xn2, wr_ref, br_ref, run_scr, tok_base, valid_limit)
    code_ref[...] = code
    wts_ref[...] = wts
    cnt_ref[...] = jnp.broadcast_to(run_scr[...], cnt_ref.shape).astype(jnp.int32)


NC = 512


def _mixer0_kernel(x_ref, meta_ref, g_conv_ref, w_in_ref, cw_ref, w_out_ref,
                   g_ffn_ref, wr_ref, br_ref,
                   h1_ref, xp_ref, code_ref, wts_ref, cnt_ref,
                   h0_scr, acc_scr, carry_scr, meta_carry_scr, run_scr):
    s = pl.program_id(0)

    @pl.when(s == 0)
    def _():
        h0_scr[...] = jnp.zeros_like(h0_scr)
        h0_scr[0:N_META, :] = meta_ref[...]
        carry_scr[...] = jnp.zeros_like(carry_scr)
        run_scr[...] = jnp.zeros_like(run_scr)

    @pl.when(s > 0)
    def _():
        h0_scr[...] = x_ref[...]

    @pl.when(s == 1 + N_XT // BATCH)
    def _():
        carry_scr[...] = meta_carry_scr[...]

    h0 = h0_scr[...]
    xn = (_rms_hat(h0) * g_conv_ref[...]).astype(jnp.bfloat16)
    row = lax.broadcasted_iota(jnp.int32, (TM, NC), 0)
    for c in range(D_MODEL // NC):
        cols = slice(NC * c, NC * (c + 1))
        gate_c = jnp.dot(xn, w_in_ref[:, D_MODEL + NC * c:D_MODEL + NC * (c + 1)],
                         preferred_element_type=jnp.float32)
        val = jnp.dot(xn, w_in_ref[:, 2 * D_MODEL + NC * c:2 * D_MODEL + NC * (c + 1)],
                      preferred_element_type=jnp.float32)
        u = gate_c * val
        tail = carry_scr[:, cols]
        c1 = tail[7:8, :]
        c2 = tail[6:7, :]
        um1 = jnp.where(row == 0, c1, pltpu.roll(u, 1, 0))
        um2 = jnp.where(row == 0, c2, jnp.where(row == 1, c1, pltpu.roll(u, 2, 0)))
        conv = um2 * cw_ref[0:1, cols] + um1 * cw_ref[1:2, cols] + u * cw_ref[2:3, cols]

        @pl.when(s == 0)
        def _():
            carry_scr[:, cols] = u[N_META - 8:N_META, :]
            meta_carry_scr[:, cols] = u[N_META - 8:N_META, :]

        @pl.when(s > 0)
        def _():
            carry_scr[:, cols] = u[TM - 8:TM, :]

        gate_b = jnp.dot(xn, w_in_ref[:, cols], preferred_element_type=jnp.float32)
        gated = (gate_b * conv).astype(jnp.bfloat16)
        part = jnp.dot(gated, w_out_ref[cols, :], preferred_element_type=jnp.float32)
        if c == 0:
            acc_scr[...] = h0 + part
        else:
            acc_scr[...] = acc_scr[...] + part

    h1 = acc_scr[...]
    h1_ref[...] = h1
    tile = jnp.where(s == 0, N_XT, s - 1)
    _ffn_prologue(h1, g_ffn_ref, wr_ref, br_ref, run_scr, tile * TM, T_VALID0,
                  xp_ref, code_ref, wts_ref, cnt_ref)


def _tile_first_meta(s):
    return jnp.where(s == 0, N_XT, s - 1)


def _const_spec(shape):
    return pl.BlockSpec(shape, lambda s: (0,) * len(shape))


def _mixer0(x2d, meta, g_conv, w_in, cw, w_out, g_ffn, wr, br):
    out_shape = [
        jax.ShapeDtypeStruct((T_PAD0, D_MODEL), jnp.float32),
        jax.ShapeDtypeStruct((T_PAD0 * SLAB, LANES), jnp.int32),
        jax.ShapeDtypeStruct((2, T_PAD0), jnp.int32),
        jax.ShapeDtypeStruct((2, T_PAD0), jnp.float32),
        jax.ShapeDtypeStruct((N_KEYS, LANES), jnp.int32),
    ]
    return pl.pallas_call(
        _mixer0_kernel,
        grid=(N_XT + 1,),
        in_specs=[
            pl.BlockSpec((TM, D_MODEL), lambda s: (jnp.maximum(s - 1, 0), 0)),
            _const_spec((N_META, D_MODEL)),
            _const_spec((1, D_MODEL)),
            _const_spec((D_MODEL, 3 * D_MODEL)),
            _const_spec((3, D_MODEL)),
            _const_spec((D_MODEL, D_MODEL)),
            _const_spec((1, D_MODEL)),
            _const_spec((N_KEYS, D_MODEL)),
            _const_spec((N_KEYS, 1)),
        ],
        out_specs=[
            pl.BlockSpec((TM, D_MODEL), lambda s: (_tile_first_meta(s), 0)),
            pl.BlockSpec((TM * SLAB, LANES), lambda s: (_tile_first_meta(s), 0)),
            pl.BlockSpec((2, TM), lambda s: (0, _tile_first_meta(s))),
            pl.BlockSpec((2, TM), lambda s: (0, _tile_first_meta(s))),
            _const_spec((N_KEYS, LANES)),
        ],
        out_shape=out_shape,
        scratch_shapes=[
            pltpu.VMEM((TM, D_MODEL), jnp.float32),
            pltpu.VMEM((TM, D_MODEL), jnp.float32),
            pltpu.VMEM((8, D_MODEL), jnp.float32),
            pltpu.VMEM((8, D_MODEL), jnp.float32),
            pltpu.VMEM((N_KEYS, 1), jnp.float32),
        ],
        compiler_params=pltpu.CompilerParams(
            dimension_semantics=("arbitrary",), vmem_limit_bytes=VMEM_LIMIT),
        name="mixer0_route",
    )(x2d, meta, g_conv, w_in, cw, w_out, g_ffn, wr, br)


def _n_tiles(t_valid):
    return _cdiv(2 * t_valid, TMG)


def _n_visits(t_valid):
    return _n_tiles(t_valid) + N_KEYS - 1


def _positions_kernel(t_pad, t_valid, cnt_ref, code_ref, pos_ref, gstart_ref, vtile_ref, vgroup_ref,
                      nvis_ref):
    n_vis = _n_visits(t_valid)

    def offs(g, acc):
        gstart_ref[g] = acc
        return acc + cnt_ref[g]

    total = lax.fori_loop(0, N_KEYS, offs, jnp.int32(0))
    gstart_ref[N_KEYS] = total

    def per_group(g, carry):
        v, last_g = carry
        c = cnt_ref[g]
        start = gstart_ref[g]
        t0 = start // TMG
        t1 = jnp.where(c > 0, (start + c - 1) // TMG + 1, t0)

        def per_tile(tt, vv):
            vtile_ref[vv] = tt
            vgroup_ref[vv] = g
            return vv + 1

        v = lax.fori_loop(t0, t1, per_tile, v)
        return v, jnp.where(c > 0, g, last_g)

    nvis, last_g = lax.fori_loop(0, N_KEYS, per_group, (jnp.int32(0), jnp.int32(0)))
    nvis_ref[0] = nvis

    def pad(vv, c):
        vtile_ref[vv] = _n_tiles(t_valid) - 1
        vgroup_ref[vv] = last_g
        return c

    lax.fori_loop(nvis, n_vis, pad, 0)

    code = code_ref[...]
    key = code >> 16
    pos = code & 0xFFFF
    for g in range(N_KEYS):
        pos = pos + jnp.where(key == g, gstart_ref[g], 0)
    pos_ref[...] = pos


def _positions(cnt, code, t_pad, t_valid):
    n_vis = _n_visits(t_valid)
    smem = pl.BlockSpec(memory_space=pltpu.SMEM)
    return pl.pallas_call(
        functools.partial(_positions_kernel, t_pad, t_valid),
        in_specs=[smem, pl.BlockSpec(memory_space=pltpu.VMEM)],
        out_specs=[pl.BlockSpec(memory_space=pltpu.VMEM), smem, smem, smem, smem],
        out_shape=[
            jax.ShapeDtypeStruct((2, t_pad), jnp.int32),
            jax.ShapeDtypeStruct((N_KEYS + 1,), jnp.int32),
            jax.ShapeDtypeStruct((n_vis,), jnp.int32),
            jax.ShapeDtypeStruct((n_vis,), jnp.int32),
            jax.ShapeDtypeStruct((1,), jnp.int32),
        ],
        name="sort_positions",
    )(cnt, code)


SRC_UNROLL = 8


def _invert_kernel(t_valid, r_pad, pos1_ref, pos2_ref, src_ref):
    def tail(r, c):
        src_ref[r] = 0
        return c

    lax.fori_loop(2 * t_valid, r_pad, tail, 0)

    def body(i, c):
        for j in range(SRC_UNROLL):
            t = i * SRC_UNROLL + j
            src_ref[pos1_ref[t]] = t
            src_ref[pos2_ref[t]] = t
        return c

    lax.fori_loop(0, t_valid // SRC_UNROLL, body, 0)


def _invert(pos1, pos2, t_valid):
    r_pad = _n_tiles(t_valid) * TMG
    smem = pl.BlockSpec(memory_space=pltpu.SMEM)
    return pl.pallas_call(
        functools.partial(_invert_kernel, t_valid, r_pad),
        in_specs=[smem, smem],
        out_specs=smem,
        out_shape=jax.ShapeDtypeStruct((r_pad,), jnp.int32),
        name="sort_invert",
    )(pos1, pos2)


GATHER_UNROLL = 8


def _ffn_kernel(t_pad, vtile_ref, vgroup_ref, nvis_ref, gstart_ref, src_ref,
                xp_hbm, wg_ref, wu_ref, wd_ref, ys_ref,
                xp_scr, xs_scr, acc_scr, sem):
    v = pl.program_id(0)

    @pl.when(v == 0)
    def _():
        cp = pltpu.make_async_copy(xp_hbm, xp_scr, sem)
        cp.start()
        cp.wait()

    tile = vtile_ref[v]
    g = vgroup_ref[v]
    active = v < nvis_ref[0]
    first = jnp.logical_or(v == 0, tile != vtile_ref[jnp.maximum(v - 1, 0)])

    @pl.when(jnp.logical_and(active, first))
    def _():
        base = tile * TMG

        def gather(i, c):
            for j in range(GATHER_UNROLL):
                r = i * GATHER_UNROLL + j
                t = src_ref[base + r]
                xs_scr[pl.ds(pl.multiple_of(r * SLAB, SLAB), SLAB), :] = (
                    xp_scr[pl.ds(pl.multiple_of(t * SLAB, SLAB), SLAB), :])
            return c

        lax.fori_loop(0, TMG // GATHER_UNROLL, gather, 0)
        acc_scr[...] = jnp.zeros_like(acc_scr)

    @pl.when(active)
    def _():
        lo, hi = _unpack_words(_load_slabs(xs_scr, TMG))
        xs = jnp.concatenate([lo, hi], axis=1).astype(jnp.bfloat16)
        hg = jnp.dot(xs, wg_ref[...].astype(jnp.bfloat16), preferred_element_type=jnp.float32)
        hu = jnp.dot(xs, wu_ref[...].astype(jnp.bfloat16), preferred_element_type=jnp.float32)
        hdn = (hg * jax.nn.sigmoid(hg) * hu).astype(jnp.bfloat16)
        y = jnp.dot(hdn, wd_ref[...].astype(jnp.bfloat16), preferred_element_type=jnp.float32)
        rows = tile * TMG + lax.broadcasted_iota(jnp.int32, (TMG, 1), 0)
        mine = jnp.logical_and(rows >= gstart_ref[g], rows < gstart_ref[g + 1])
        acc = jnp.where(mine, y, acc_scr[...])
        acc_scr[...] = acc
        _store_slabs(ys_ref, _pack_rows(acc))


def _ffn(vtile, vgroup, nvis, gstart, src, xp, wg, wu, wd, layer, t_pad, t_valid):
    n_vis = _n_visits(t_valid)
    r_pad = _n_tiles(t_valid) * TMG

    def w_map(v, vt, vg, nv, gs, sr):
        return (layer, vg[v] % N_EXPERTS, 0, 0)

    def w_spec(rows, cols):
        return pl.BlockSpec((None, None, rows, cols), w_map)

    return pl.pallas_call(
        functools.partial(_ffn_kernel, t_pad),
        grid_spec=pltpu.PrefetchScalarGridSpec(
            num_scalar_prefetch=5,
            grid=(n_vis,),
            in_specs=[
                pl.BlockSpec(memory_space=pl.ANY),
                w_spec(D_MODEL, D_EXPERT),
                w_spec(D_MODEL, D_EXPERT),
                w_spec(D_EXPERT, D_MODEL),
            ],
            out_specs=pl.BlockSpec((TMG * SLAB, LANES), lambda v, vt, vg, nv, gs, sr: (vt[v], 0)),
            scratch_shapes=[
                pltpu.VMEM((t_pad * SLAB, LANES), jnp.int32),
                pltpu.VMEM((TMG * SLAB, LANES), jnp.int32),
                pltpu.VMEM((TMG, D_MODEL), jnp.float32),
                pltpu.SemaphoreType.DMA,
            ],
        ),
        out_shape=jax.ShapeDtypeStruct((r_pad * SLAB, LANES), jnp.int32),
        compiler_params=pltpu.CompilerParams(
            dimension_semantics=("arbitrary",), vmem_limit_bytes=VMEM_LIMIT),
        name="expert_ffn",
    )(vtile, vgroup, nvis, gstart, src, xp, wg, wu, wd)


def _combine_rows(pos1_ref, pos2_ref, ys_hbm, ys_scr, y1_scr, y2_scr, sem, tile, half_rows, rh):
    tiles_per_half = HALF_SPLIT // TM
    half = jnp.where(tile >= tiles_per_half, 1, 0)
    base = half * half_rows

    @pl.when(jnp.logical_or(tile == 0, tile == tiles_per_half))
    def _():
        cp = pltpu.make_async_copy(
            ys_hbm.at[pl.ds(pl.multiple_of(base * SLAB, TMG * SLAB), rh * SLAB), :], ys_scr, sem)
        cp.start()
        cp.wait()

    tok0 = tile * TM

    def gather(i, c):
        for j in range(GATHER_UNROLL):
            r = i * GATHER_UNROLL + j
            p1 = pos1_ref[tok0 + r] - base
            p2 = pos2_ref[tok0 + r] - base
            dst = pl.ds(pl.multiple_of(r * SLAB, SLAB), SLAB)
            y1_scr[dst, :] = ys_scr[pl.ds(pl.multiple_of(p1 * SLAB, SLAB), SLAB), :]
            y2_scr[dst, :] = ys_scr[pl.ds(pl.multiple_of(p2 * SLAB, SLAB), SLAB), :]
        return c

    lax.fori_loop(0, TM // GATHER_UNROLL, gather, 0)


def _weighted_sum(y1_scr, y2_scr, w_ref):
    lo1, hi1 = _unpack_words(_load_slabs(y1_scr, TM))
    lo2, hi2 = _unpack_words(_load_slabs(y2_scr, TM))
    w1 = w_ref[:, 0:1]
    w2 = w_ref[:, 1:2]
    return jnp.concatenate([w1 * lo1 + w2 * lo2, w1 * hi1 + w2 * hi2], axis=1)


def _combine_kernel(half_rows, rh, final_norm, pos1_ref, pos2_ref, h_ref, w_ref, g_ref, ys_hbm,
                    o_ref, ys_scr, y1_scr, y2_scr, sem):
    tile = pl.program_id(0)
    _combine_rows(pos1_ref, pos2_ref, ys_hbm, ys_scr, y1_scr, y2_scr, sem, tile, half_rows, rh)
    h = h_ref[...] + _weighted_sum(y1_scr, y2_scr, w_ref)
    if final_norm:
        h = _rms_hat(h) * g_ref[...]
    o_ref[...] = h


def _combine(pos1, pos2, h, wcols, g, ys, t_valid, final_norm):
    n_tiles = _cdiv(t_valid, TM)
    half_rows = 2 * HALF_SPLIT
    rh = _n_tiles(t_valid) * TMG - half_rows
    return pl.pallas_call(
        functools.partial(_combine_kernel, half_rows, rh, final_norm),
        grid_spec=pltpu.PrefetchScalarGridSpec(
            num_scalar_prefetch=2,
            grid=(n_tiles,),
            in_specs=[
                pl.BlockSpec((TM, D_MODEL), lambda i, p1, p2: (i, 0)),
                pl.BlockSpec((TM, 2), lambda i, p1, p2: (i, 0)),
                pl.BlockSpec((1, D_MODEL), lambda i, p1, p2: (0, 0)),
                pl.BlockSpec(memory_space=pl.ANY),
            ],
            out_specs=pl.BlockSpec((TM, D_MODEL), lambda i, p1, p2: (i, 0)),
            scratch_shapes=[
                pltpu.VMEM((rh * SLAB, LANES), jnp.int32),
                pltpu.VMEM((TM * SLAB, LANES), jnp.int32),
                pltpu.VMEM((TM * SLAB, LANES), jnp.int32),
                pltpu.SemaphoreType.DMA,
            ],
        ),
        out_shape=jax.ShapeDtypeStruct((n_tiles * TM, D_MODEL), jnp.float32),
        compiler_params=pltpu.CompilerParams(
            dimension_semantics=("arbitrary",), vmem_limit_bytes=VMEM_LIMIT),
        name="moe_combine_final" if final_norm else "moe_combine",
    )(pos1, pos2, h, wcols, g, ys)


QB = WINDOW
KV_W = N_KV_HEADS * HEAD_DIM
N_QB = TM // QB
META_ROW0 = QB - N_META


def _rope(x, cos, sin_signed):
    q = lax.broadcasted_iota(jnp.int32, x.shape, 1) // (HEAD_DIM // 2)
    swapped = jnp.where(q % 2 == 0, pltpu.roll(x, LANES - HEAD_DIM // 2, 1),
                        pltpu.roll(x, HEAD_DIM // 2, 1))
    return x * cos + swapped * sin_signed


def _dup_heads(blk):
    lane = lax.broadcasted_iota(jnp.int32, blk.shape, 1)
    rolled = pltpu.roll(blk, HEAD_DIM, 1)
    return jnp.where(lane < HEAD_DIM, blk, rolled), jnp.where(lane < HEAD_DIM, rolled, blk)


def _kv_rows(xhat, g_kv_ref, w_kv_ref, cos, sin_signed):
    xk = (xhat * g_kv_ref[...]).astype(jnp.bfloat16)
    kv = jnp.dot(xk, w_kv_ref[...], preferred_element_type=jnp.float32)
    ks, vs = [], []
    for b in range(KV_W // LANES):
        kb = _rope(kv[:, LANES * b:LANES * (b + 1)], cos, sin_signed)
        vb = kv[:, KV_W + LANES * b:KV_W + LANES * (b + 1)]
        ks.extend(_dup_heads(kb))
        vs.extend(_dup_heads(vb))
    return [k.astype(jnp.bfloat16) for k in ks], [v.astype(jnp.bfloat16) for v in vs]


def _attn_kernel(h_ref, cos_ref, sin_ref, g_attn_ref, g_kv_ref, w_q_ref, w_kv_ref, w_o_ref, sink_ref,
                 g_ffn_ref, wr_ref, br_ref,
                 h3_ref, xp_ref, code_ref, wts_ref, cnt_ref,
                 k_scr, v_scr, mk_scr, mv_scr, q_scr, o_scr, bias_scr, run_scr):
    s = pl.program_id(0)
    tiles_per_batch = N_XT // BATCH

    @pl.when(s == 0)
    def _():
        run_scr[...] = jnp.zeros_like(run_scr)
        xhat = _rms_hat(h_ref[0:N_META, :])
        ks, vs = _kv_rows(xhat, g_kv_ref, w_kv_ref, cos_ref[0:N_META, :], sin_ref[0:N_META, :])
        mk_scr[...] = jnp.zeros_like(mk_scr)
        mv_scr[...] = jnp.zeros_like(mv_scr)
        for kvh in range(N_KV_HEADS):
            mk_scr[kvh, META_ROW0:QB, :] = ks[kvh]
            mv_scr[kvh, META_ROW0:QB, :] = vs[kvh]
        qi = lax.broadcasted_iota(jnp.int32, (QB, 2 * QB), 0)
        kj = lax.broadcasted_iota(jnp.int32, (QB, 2 * QB), 1)
        band = jnp.logical_and(kj > qi, kj <= qi + QB)
        bias_scr[0] = jnp.where(band, 0.0, NEG_INF)
        bias_scr[1] = jnp.where(jnp.logical_and(band, kj >= META_ROW0), 0.0, NEG_INF)

    @pl.when(s > 0)
    def _():
        batch_first = jnp.logical_or(s == 1, s == 1 + tiles_per_batch)

        @pl.when(batch_first)
        def _():
            k_scr[:, 0:QB, :] = mk_scr[...]
            v_scr[:, 0:QB, :] = mv_scr[...]

        h = h_ref[...]
        xhat = _rms_hat(h)
        ks, vs = _kv_rows(xhat, g_kv_ref, w_kv_ref, cos_ref[...], sin_ref[...])
        for kvh in range(N_KV_HEADS):
            k_scr[kvh, QB:QB + TM, :] = ks[kvh]
            v_scr[kvh, QB:QB + TM, :] = vs[kvh]

        xq = (xhat * g_attn_ref[...]).astype(jnp.bfloat16)
        q = jnp.dot(xq, w_q_ref[...], preferred_element_type=jnp.float32)
        for hb in range(N_HEADS // 2):
            q_scr[hb] = q[:, LANES * hb:LANES * (hb + 1)]
        first_bias = jnp.where(batch_first, 1, 0)

        def head_pair(hb, carry):
            kvh = hb // (N_HEADS // N_KV_HEADS // 2)
            lane = lax.broadcasted_iota(jnp.int32, (QB, LANES), 1)
            q_all = _rope(q_scr[hb], cos_ref[...], sin_ref[...]) * (HEAD_DIM ** -0.5)
            for b in range(N_QB):
                qb = q_all[QB * b:QB * (b + 1), :]
                qs = jnp.concatenate([jnp.where(lane < HEAD_DIM, qb, 0.0),
                                      jnp.where(lane < HEAD_DIM, 0.0, qb)], axis=0).astype(jnp.bfloat16)
                kk = k_scr[kvh, QB * b:QB * (b + 2), :]
                vv = v_scr[kvh, QB * b:QB * (b + 2), :]
                sc = lax.dot_general(qs, kk, (((1,), (1,)), ((), ())),
                                     preferred_element_type=jnp.float32)
                bias = bias_scr[first_bias] if b == 0 else bias_scr[0]
                outs = []
                for j in range(2):
                    sj = sc[QB * j:QB * (j + 1), :] + bias
                    sink = sink_ref[2 * hb + j]
                    m = jnp.maximum(jnp.max(sj, axis=-1, keepdims=True), sink)
                    p = jnp.exp(sj - m)
                    den = jnp.sum(p, axis=-1, keepdims=True) + jnp.exp(sink - m)
                    pv = jnp.dot(p.astype(jnp.bfloat16), vv, preferred_element_type=jnp.float32)
                    outs.append(pv * (1.0 / den))
                o_scr[hb, QB * b:QB * (b + 1), :] = jnp.where(
                    lane < HEAD_DIM, outs[0], outs[1]).astype(jnp.bfloat16)
            return carry

        lax.fori_loop(0, N_HEADS // 2, head_pair, 0)

        k_scr[:, 0:QB, :] = k_scr[:, TM:TM + QB, :]
        v_scr[:, 0:QB, :] = v_scr[:, TM:TM + QB, :]

        o = jnp.concatenate([o_scr[hb] for hb in range(N_HEADS // 2)], axis=1)
        h3 = h + jnp.dot(o, w_o_ref[...], preferred_element_type=jnp.float32)
        h3_ref[...] = h3
        _ffn_prologue(h3, g_ffn_ref, wr_ref, br_ref, run_scr, (s - 1) * TM, T_X,
                      xp_ref, code_ref, wts_ref, cnt_ref)


def _attn(h2, cos_t, sin_t, g_attn, g_kv, w_q, w_kv, w_o, sinks, g_ffn, wr, br):
    def tile_x(s):
        return jnp.maximum(s - 1, 0)

    def rope_tile(s):
        per_batch = N_XT // BATCH
        return (jnp.where(s == 0, per_batch, (s - 1) % per_batch), 0)

    out_shape = [
        jax.ShapeDtypeStruct((T_X, D_MODEL), jnp.float32),
        jax.ShapeDtypeStruct((T_X * SLAB, LANES), jnp.int32),
        jax.ShapeDtypeStruct((2, T_X), jnp.int32),
        jax.ShapeDtypeStruct((2, T_X), jnp.float32),
        jax.ShapeDtypeStruct((N_KEYS, LANES), jnp.int32),
    ]
    return pl.pallas_call(
        _attn_kernel,
        grid=(N_XT + 1,),
        in_specs=[
            pl.BlockSpec((TM, D_MODEL), lambda s: (_tile_first_meta(s), 0)),
            pl.BlockSpec((TM, LANES), rope_tile),
            pl.BlockSpec((TM, LANES), rope_tile),
            _const_spec((1, D_MODEL)),
            _const_spec((1, D_MODEL)),
            _const_spec((D_MODEL, D_MODEL)),
            _const_spec((D_MODEL, 2 * KV_W)),
            _const_spec((D_MODEL, D_MODEL)),
            pl.BlockSpec(memory_space=pltpu.SMEM),
            _const_spec((1, D_MODEL)),
            _const_spec((N_KEYS, D_MODEL)),
            _const_spec((N_KEYS, 1)),
        ],
        out_specs=[
            pl.BlockSpec((TM, D_MODEL), lambda s: (tile_x(s), 0)),
            pl.BlockSpec((TM * SLAB, LANES), lambda s: (tile_x(s), 0)),
            pl.BlockSpec((2, TM), lambda s: (0, tile_x(s))),
            pl.BlockSpec((2, TM), lambda s: (0, tile_x(s))),
            _const_spec((N_KEYS, LANES)),
        ],
        out_shape=out_shape,
        scratch_shapes=[
            pltpu.VMEM((N_KV_HEADS, QB + TM, LANES), jnp.bfloat16),
            pltpu.VMEM((N_KV_HEADS, QB + TM, LANES), jnp.bfloat16),
            pltpu.VMEM((N_KV_HEADS, QB, LANES), jnp.bfloat16),
            pltpu.VMEM((N_KV_HEADS, QB, LANES), jnp.bfloat16),
            pltpu.VMEM((N_HEADS // 2, TM, LANES), jnp.float32),
            pltpu.VMEM((N_HEADS // 2, TM, LANES), jnp.bfloat16),
            pltpu.VMEM((2, QB, 2 * QB), jnp.float32),
            pltpu.VMEM((N_KEYS, 1), jnp.float32),
        ],
        compiler_params=pltpu.CompilerParams(
            dimension_semantics=("arbitrary",), vmem_limit_bytes=VMEM_LIMIT),
        name="attn_route",
    )(h2, cos_t, sin_t, g_attn, g_kv, w_q, w_kv, w_o, sinks, g_ffn, wr, br)


def _router_rows(rg_w, rg_b, re_w, re_b):
    wr = jnp.zeros((N_KEYS, D_MODEL), jnp.float32)
    wr = wr.at[0:N_GROUPS].set(rg_w.T).at[8:8 + N_EXPERTS].set(re_w.T)
    br = jnp.zeros((N_KEYS, 1), jnp.float32)
    br = br.at[0:N_GROUPS, 0].set(rg_b).at[8:8 + N_EXPERTS, 0].set(re_b)
    return wr, br


def _rope_tables():
    half = HEAD_DIM // 2
    inv_freq = ROPE_THETA ** (-jnp.arange(half, dtype=jnp.float32) / half)
    pos = jnp.concatenate([N_META + jnp.arange(SEQ), jnp.arange(TM)]).astype(jnp.float32)
    ang = pos[:, None] * inv_freq[None, :]
    cos = jnp.cos(ang)
    sin = jnp.sin(ang)
    return jnp.tile(cos, (1, 4)), jnp.concatenate([-sin, sin, -sin, sin], axis=1)


def _moe(cnt, code, wts, xp, h, wg, wu, wd, layer, g_final, t_pad, t_valid, final_norm):
    pos, gstart, vtile, vgroup, nvis = _positions(cnt[:, 0], code, t_pad, t_valid)
    src = _invert(pos[0], pos[1], t_valid)
    ys = _ffn(vtile, vgroup, nvis, gstart, src, xp, wg, wu, wd, layer, t_pad, t_valid)
    return _combine(pos[0], pos[1], h, wts.T, g_final, ys, t_valid, final_norm)


def kernel(x, meta_tokens, conv_norm_g, conv_w_in, conv_w, conv_w_out, kv_norm_g, w_kv, attn_norm_g,
           w_q, w_o, sinks, ffn_norm_g, router_group_w, router_group_b, router_expert_w,
           router_expert_b, w_gate, w_up, w_down, final_norm_g):
    bf = jnp.bfloat16
    x2d = x.reshape(T_X, D_MODEL)
    wr0, br0 = _router_rows(router_group_w[0], router_group_b[0], router_expert_w[0], router_expert_b[0])
    wr1, br1 = _router_rows(router_group_w[1], router_group_b[1], router_expert_w[1], router_expert_b[1])
    g_final = final_norm_g.reshape(1, D_MODEL)

    h1, xp0, code0, wts0, cnt0 = _mixer0(
        x2d, meta_tokens, conv_norm_g[0].reshape(1, D_MODEL), conv_w_in[0].astype(bf), conv_w[0],
        conv_w_out[0].astype(bf), ffn_norm_g[0].reshape(1, D_MODEL), wr0, br0)
    h2 = _moe(cnt0, code0, wts0, xp0, h1, w_gate, w_up, w_down, 0, g_final,
              T_PAD0, T_VALID0, False)

    cos_t, sin_t = _rope_tables()
    h3, xp1, code1, wts1, cnt1 = _attn(
        h2, cos_t, sin_t, attn_norm_g[0].reshape(1, D_MODEL), kv_norm_g.reshape(1, D_MODEL),
        w_q[0].astype(bf), w_kv.astype(bf), w_o[0].astype(bf), sinks[0], ffn_norm_g[1].reshape(1, D_MODEL),
        wr1, br1)
    out = _moe(cnt1, code1, wts1, xp1, h3, w_gate, w_up, w_down, 1, g_final,
               T_X, T_X, True)
    return out.reshape(BATCH, SEQ, D_MODEL)
```

```python
import functools

import jax
import jax.numpy as jnp
from jax import lax
from jax.experimental import pallas as pl
from jax.experimental.pallas import tpu as pltpu

D_MODEL = 1024
BATCH = 2
SEQ = 8192
N_META = 16
N_HEADS = 16
HEAD_DIM = 64
N_KV_HEADS = 4
WINDOW = 128
ROPE_THETA = 10000.0
N_GROUPS = 4
EXPERTS_PER_GROUP = 8
N_EXPERTS = N_GROUPS * EXPERTS_PER_GROUP
D_EXPERT = 256
NORM_EPS = 1e-5
NEG_INF = -1e30

TM = 512
N_XT = BATCH * SEQ // TM
T_X = BATCH * SEQ
T_PAD0 = (N_XT + 1) * TM
T_VALID0 = T_X + N_META
HALF_SPLIT = SEQ
N_KEYS = 2 * N_EXPERTS
TMG = 256
HALF_TILE = 2 * HALF_SPLIT // TMG
LANES = 128
SLAB = D_MODEL // LANES
VMEM_LIMIT = 60 * 1024 * 1024


def _cdiv(a, b):
    return (a + b - 1) // b


def _rms_hat(x):
    return x * lax.rsqrt(jnp.mean(x * x, axis=-1, keepdims=True) + NORM_EPS)


def _store_slabs(ref, x):
    m = x.shape[0]
    for j in range(SLAB):
        ref[pl.ds(j, m, stride=SLAB), :] = x[:, LANES * j:LANES * (j + 1)]


def _load_slabs(ref, m):
    return jnp.concatenate([ref[pl.ds(j, m, stride=SLAB), :] for j in range(SLAB)], axis=1)


def _moe_out(y_ref, w_ref, m):
    parts = []
    for j in range(SLAB):
        y1 = y_ref[pl.ds(j, m, stride=2 * SLAB), :]
        y2 = y_ref[pl.ds(SLAB + j, m, stride=2 * SLAB), :]
        parts.append(w_ref[0:m, 0:1] * y1 + w_ref[0:m, 1:2] * y2)
    return jnp.concatenate(parts, axis=1)


def _route(xn, wr_ref, br_ref, run_scr, tok_base, valid_limit):
    logits = lax.dot_general(wr_ref[...], xn, (((1,), (1,)), ((), ())),
                             precision=lax.Precision.HIGHEST,
                             preferred_element_type=jnp.float32) + br_ref[...]
    g = logits[0:N_GROUPS]
    gmax = jnp.max(g, axis=0, keepdims=True)
    rid_g = lax.broadcasted_iota(jnp.int32, g.shape, 0).astype(jnp.float32)
    g_idx = jnp.min(jnp.where(g == gmax, rid_g, float(N_GROUPS)), axis=0, keepdims=True).astype(jnp.int32)
    g_w = 1.0 / jnp.sum(jnp.exp(g - gmax), axis=0, keepdims=True)
    e_sel = logits[8:8 + EXPERTS_PER_GROUP]
    for gi in range(1, N_GROUPS):
        lo = 8 + EXPERTS_PER_GROUP * gi
        e_sel = jnp.where(g_idx == gi, logits[lo:lo + EXPERTS_PER_GROUP], e_sel)
    rid_e = lax.broadcasted_iota(jnp.int32, e_sel.shape, 0).astype(jnp.float32)
    none = float(EXPERTS_PER_GROUP)
    m1 = jnp.max(e_sel, axis=0, keepdims=True)
    i1f = jnp.min(jnp.where(e_sel == m1, rid_e, none), axis=0, keepdims=True)
    e_rest = jnp.where(rid_e == i1f, -jnp.inf, e_sel)
    m2 = jnp.max(e_rest, axis=0, keepdims=True)
    i2 = jnp.min(jnp.where(e_rest == m2, rid_e, none), axis=0, keepdims=True).astype(jnp.int32)
    i1 = i1f.astype(jnp.int32)
    ex = jnp.exp(m2 - m1)
    den = 1.0 / (1.0 + ex)
    w1 = den * g_w
    w2 = ex * den * g_w

    tok = tok_base + lax.broadcasted_iota(jnp.int32, (1, TM), 1)
    half = jnp.where(tok >= HALF_SPLIT, N_EXPERTS, 0)
    key1 = half + g_idx * EXPERTS_PER_GROUP + i1
    key2 = half + g_idx * EXPERTS_PER_GROUP + i2
    kid = lax.broadcasted_iota(jnp.int32, (N_KEYS, TM), 0)
    validf = jnp.where(tok < valid_limit, 1.0, 0.0)
    oh1 = jnp.where(kid == key1, validf, 0.0)
    oh2 = jnp.where(kid == key2, validf, 0.0)
    cnt = oh1 + oh2
    tri = jnp.where(lax.broadcasted_iota(jnp.int32, (TM, TM), 0)
                    <= lax.broadcasted_iota(jnp.int32, (TM, TM), 1), 1.0, 0.0).astype(jnp.bfloat16)
    cum = jnp.dot(cnt.astype(jnp.bfloat16), tri, preferred_element_type=jnp.float32)
    before = run_scr[...] + (cum - cnt)
    rank1 = jnp.sum(oh1 * before, axis=0, keepdims=True).astype(jnp.int32)
    rank2 = jnp.sum(oh2 * before, axis=0, keepdims=True).astype(jnp.int32)
    run_scr[...] = run_scr[...] + cum[:, TM - 1:TM]
    code = jnp.concatenate([key1 * 65536 + rank1, key2 * 65536 + rank2], axis=0)
    wts = jnp.concatenate([w1, w2], axis=0)
    return code, wts


def _ffn_prologue(h_new, g_ffn_ref, wr_ref, br_ref, run_scr, tok_base, valid_limit,
                  xp_ref, code_ref, wts_ref, cnt_ref):
    xn2 = _rms_hat(h_new) * g_ffn_ref[...]
    _store_slabs(xp_ref, xn2)
    code, wts = _route(xn2, wr_ref, br_ref, run_scr, tok_base, valid_limit)
    code_ref[...] = code
    wts_ref[...] = wts
    cnt_ref[...] = jnp.broadcast_to(run_scr[...], cnt_ref.shape).astype(jnp.int32)


NC = 512


def _mixer0_kernel(x_ref, meta_ref, g_conv_ref, w_in_ref, cw_ref, w_out_ref,
                   g_ffn_ref, wr_ref, br_ref,
                   h1_ref, xp_ref, code_ref, wts_ref, cnt_ref,
                   h0_scr, acc_scr, carry_scr, meta_carry_scr, run_scr):
    s = pl.program_id(0)

    @pl.when(s == 0)
    def _():
        h0_scr[...] = jnp.zeros_like(h0_scr)
        h0_scr[0:N_META, :] = meta_ref[...]
        carry_scr[...] = jnp.zeros_like(carry_scr)
        run_scr[...] = jnp.zeros_like(run_scr)

    @pl.when(s > 0)
    def _():
        h0_scr[...] = x_ref[...]

    @pl.when(s == 1 + N_XT // BATCH)
    def _():
        carry_scr[...] = meta_carry_scr[...]

    h0 = h0_scr[...]
    xn = (_rms_hat(h0) * g_conv_ref[...]).astype(jnp.bfloat16)
    row = lax.broadcasted_iota(jnp.int32, (TM, NC), 0)
    for c in range(D_MODEL // NC):
        cols = slice(NC * c, NC * (c + 1))
        gate_c = jnp.dot(xn, w_in_ref[:, D_MODEL + NC * c:D_MODEL + NC * (c + 1)],
                         preferred_element_type=jnp.float32)
        val = jnp.dot(xn, w_in_ref[:, 2 * D_MODEL + NC * c:2 * D_MODEL + NC * (c + 1)],
                      preferred_element_type=jnp.float32)
        u = gate_c * val
        tail = carry_scr[:, cols]
        c1 = tail[7:8, :]
        c2 = tail[6:7, :]
        um1 = jnp.where(row == 0, c1, pltpu.roll(u, 1, 0))
        um2 = jnp.where(row == 0, c2, jnp.where(row == 1, c1, pltpu.roll(u, 2, 0)))
        conv = um2 * cw_ref[0:1, cols] + um1 * cw_ref[1:2, cols] + u * cw_ref[2:3, cols]

        @pl.when(s == 0)
        def _():
            carry_scr[:, cols] = u[N_META - 8:N_META, :]
            meta_carry_scr[:, cols] = u[N_META - 8:N_META, :]

        @pl.when(s > 0)
        def _():
            carry_scr[:, cols] = u[TM - 8:TM, :]

        gate_b = jnp.dot(xn, w_in_ref[:, cols], preferred_element_type=jnp.float32)
        gated = (gate_b * conv).astype(jnp.bfloat16)
        part = jnp.dot(gated, w_out_ref[cols, :], preferred_element_type=jnp.float32)
        if c == 0:
            acc_scr[...] = h0 + part
        else:
            acc_scr[...] = acc_scr[...] + part

    h1 = acc_scr[...]
    h1_ref[...] = h1
    tile = jnp.where(s == 0, N_XT, s - 1)
    _ffn_prologue(h1, g_ffn_ref, wr_ref, br_ref, run_scr, tile * TM, T_VALID0,
                  xp_ref, code_ref, wts_ref, cnt_ref)


def _tile_first_meta(s):
    return jnp.where(s == 0, N_XT, s - 1)


_RESIDENT = pl.BlockSpec(memory_space=pltpu.VMEM)


def _mixer0(x2d, meta, g_conv, w_in, cw, w_out, g_ffn, wr, br):
    out_shape = [
        jax.ShapeDtypeStruct((T_PAD0, D_MODEL), jnp.float32),
        jax.ShapeDtypeStruct((T_PAD0 * SLAB, LANES), jnp.float32),
        jax.ShapeDtypeStruct((2, T_PAD0), jnp.int32),
        jax.ShapeDtypeStruct((2, T_PAD0), jnp.float32),
        jax.ShapeDtypeStruct((N_KEYS, LANES), jnp.int32),
    ]
    return pl.pallas_call(
        _mixer0_kernel,
        grid=(N_XT + 1,),
        in_specs=[pl.BlockSpec((TM, D_MODEL), lambda s: (jnp.maximum(s - 1, 0), 0))] + [_RESIDENT] * 8,
        out_specs=[
            pl.BlockSpec((TM, D_MODEL), lambda s: (_tile_first_meta(s), 0)),
            pl.BlockSpec((TM * SLAB, LANES), lambda s: (_tile_first_meta(s), 0)),
            pl.BlockSpec((2, TM), lambda s: (0, _tile_first_meta(s))),
            pl.BlockSpec((2, TM), lambda s: (0, _tile_first_meta(s))),
            _RESIDENT,
        ],
        out_shape=out_shape,
        scratch_shapes=[
            pltpu.VMEM((TM, D_MODEL), jnp.float32),
            pltpu.VMEM((TM, D_MODEL), jnp.float32),
            pltpu.VMEM((8, D_MODEL), jnp.float32),
            pltpu.VMEM((8, D_MODEL), jnp.float32),
            pltpu.VMEM((N_KEYS, 1), jnp.float32),
        ],
        compiler_params=pltpu.CompilerParams(
            dimension_semantics=("arbitrary",), vmem_limit_bytes=VMEM_LIMIT),
        name="mixer0_route",
    )(x2d, meta, g_conv, w_in, cw, w_out, g_ffn, wr, br)


def _n_tiles(t_valid):
    return _cdiv(2 * t_valid, TMG)


def _n_visits(t_valid):
    return _n_tiles(t_valid) + N_KEYS - 1


def _positions_kernel(t_pad, t_valid, cnt_ref, code_ref, pos_ref, gstart_ref, vtile_ref, vgroup_ref,
                      vslot_ref, vnext_ref, nvis_ref, nextg_scr):
    n_vis = _n_visits(t_valid)

    def offs(g, acc):
        gstart_ref[g] = acc
        return acc + cnt_ref[g]

    total = lax.fori_loop(0, N_KEYS, offs, jnp.int32(0))
    gstart_ref[N_KEYS] = total

    def next_nonempty(i, nxt):
        g = N_KEYS - 1 - i
        nextg_scr[g] = nxt
        return jnp.where(cnt_ref[g] > 0, g, nxt)

    lax.fori_loop(0, N_KEYS, next_nonempty, jnp.int32(-1))

    def per_group(g, carry):
        v, last_g, rank = carry
        c = cnt_ref[g]
        start = gstart_ref[g]
        t0 = start // TMG
        t1 = jnp.where(c > 0, (start + c - 1) // TMG + 1, t0)

        def per_tile(tt, vv):
            vtile_ref[vv] = tt
            vgroup_ref[vv] = g
            vslot_ref[vv] = rank % 2
            vnext_ref[vv] = nextg_scr[g]
            return vv + 1

        v = lax.fori_loop(t0, t1, per_tile, v)
        return v, jnp.where(c > 0, g, last_g), jnp.where(c > 0, rank + 1, rank)

    nvis, last_g, _ = lax.fori_loop(0, N_KEYS, per_group, (jnp.int32(0), jnp.int32(0), jnp.int32(0)))
    nvis_ref[0] = nvis

    def pad(vv, c):
        vtile_ref[vv] = _n_tiles(t_valid) - 1
        vgroup_ref[vv] = last_g
        vslot_ref[vv] = 0
        vnext_ref[vv] = -1
        return c

    lax.fori_loop(nvis, n_vis, pad, 0)

    code = code_ref[...]
    key = code >> 16
    pos = code & 0xFFFF
    for g in range(N_KEYS):
        pos = pos + jnp.where(key == g, gstart_ref[g], 0)
    pos_ref[...] = pos


def _positions(cnt, code, t_pad, t_valid):
    n_vis = _n_visits(t_valid)
    smem = pl.BlockSpec(memory_space=pltpu.SMEM)
    return pl.pallas_call(
        functools.partial(_positions_kernel, t_pad, t_valid),
        in_specs=[smem, pl.BlockSpec(memory_space=pltpu.VMEM)],
        out_specs=[pl.BlockSpec(memory_space=pltpu.VMEM), smem, smem, smem, smem, smem, smem],
        out_shape=[
            jax.ShapeDtypeStruct((2, t_pad), jnp.int32),
            jax.ShapeDtypeStruct((N_KEYS + 1,), jnp.int32),
            jax.ShapeDtypeStruct((n_vis,), jnp.int32),
            jax.ShapeDtypeStruct((n_vis,), jnp.int32),
            jax.ShapeDtypeStruct((n_vis,), jnp.int32),
            jax.ShapeDtypeStruct((n_vis,), jnp.int32),
            jax.ShapeDtypeStruct((1,), jnp.int32),
        ],
        scratch_shapes=[pltpu.SMEM((N_KEYS,), jnp.int32)],
        name="sort_positions",
    )(cnt, code)


SRC_UNROLL = 8


def _invert_kernel(t_valid, r_pad, pos1_ref, pos2_ref, dst_ref):
    def tail(r, c):
        dst_ref[r] = r
        return c

    lax.fori_loop(2 * t_valid, r_pad, tail, 0)

    def body(i, c):
        for j in range(SRC_UNROLL):
            t = i * SRC_UNROLL + j
            dst_ref[pos1_ref[t]] = 2 * t
            dst_ref[pos2_ref[t]] = 2 * t + 1
        return c

    lax.fori_loop(0, t_valid // SRC_UNROLL, body, 0)


def _invert(pos1, pos2, t_valid):
    r_pad = _n_tiles(t_valid) * TMG
    smem = pl.BlockSpec(memory_space=pltpu.SMEM)
    return pl.pallas_call(
        functools.partial(_invert_kernel, t_valid, r_pad),
        in_specs=[smem, smem],
        out_specs=smem,
        out_shape=jax.ShapeDtypeStruct((r_pad,), jnp.int32),
        name="sort_invert",
    )(pos1, pos2)


GATHER_UNROLL = 8


def _ffn_kernel(layer, n_tiles, half_tokens, vtile_ref, vgroup_ref, vslot_ref, vnext_ref, nvis_ref,
                gstart_ref, dst_ref,
                xp_hbm, wg_hbm, wu_hbm, wd_hbm, y_hbm,
                xp_scr, wg_scr, wu_scr, wd_scr, xnext_scr, xs_scr, acc_scr, out_scr, xsem, wsem, osem):
    v = pl.program_id(0)
    nvis = nvis_ref[0]
    active = v < nvis
    tile = vtile_ref[v]
    g = vgroup_ref[v]
    slot = vslot_ref[v]
    prev = jnp.maximum(v - 1, 0)
    tile_first = jnp.logical_or(v == 0, tile != vtile_ref[prev])
    tile_last = jnp.logical_or(v == nvis - 1, tile != vtile_ref[jnp.minimum(v + 1, pl.num_programs(0) - 1)])
    group_first = jnp.logical_or(v == 0, g != vgroup_ref[prev])
    tile_rows = TMG * SLAB

    def weight_copies(group, sl):
        e = group % N_EXPERTS
        return (pltpu.make_async_copy(wg_hbm.at[layer, e], wg_scr.at[sl], wsem.at[sl, 0]),
                pltpu.make_async_copy(wu_hbm.at[layer, e], wu_scr.at[sl], wsem.at[sl, 1]),
                pltpu.make_async_copy(wd_hbm.at[layer, e], wd_scr.at[sl], wsem.at[sl, 2]))

    def half_copy(half):
        return pltpu.make_async_copy(
            xp_hbm.at[pl.ds(pl.multiple_of(half * (HALF_SPLIT * SLAB), SLAB), half_tokens * SLAB), :],
            xp_scr, xsem)

    def gather_rows(t, unrolled):
        base = t * TMG
        tok_base = jnp.where(t >= HALF_TILE, HALF_SPLIT * SLAB, 0)

        def one(r):
            off = (dst_ref[base + r] >> 1) * SLAB - tok_base
            xnext_scr[pl.ds(pl.multiple_of(r * SLAB, SLAB), SLAB), :] = (
                xp_scr[pl.ds(pl.multiple_of(off, SLAB), SLAB), :])

        if unrolled:
            for r in range(TMG):
                one(r)
        else:
            def chunk(i, c):
                for j in range(GATHER_UNROLL):
                    one(i * GATHER_UNROLL + j)
                return c

            lax.fori_loop(0, TMG // GATHER_UNROLL, chunk, 0)

    def row_copy(t, r):
        sl = t % 2
        d = dst_ref[t * TMG + r]
        return pltpu.make_async_copy(
            out_scr.at[pl.ds(pl.multiple_of(sl * tile_rows + r * SLAB, SLAB), SLAB), :],
            y_hbm.at[pl.ds(pl.multiple_of(d * SLAB, SLAB), SLAB), :], osem.at[sl])

    def scatter_rows(t, unrolled):
        if unrolled:
            for r in range(TMG):
                row_copy(t, r).start()
        else:
            def chunk(i, c):
                for j in range(GATHER_UNROLL):
                    row_copy(t, i * GATHER_UNROLL + j).start()
                return c

            lax.fori_loop(0, TMG // GATHER_UNROLL, chunk, 0)

    def wait_rows(sl):
        pltpu.make_async_copy(out_scr.at[pl.ds(pl.multiple_of(sl * tile_rows, SLAB), tile_rows), :],
                              y_hbm.at[pl.ds(0, tile_rows), :], osem.at[sl]).wait()

    @pl.when(v == 0)
    def _():
        half_copy(0).start()
        for c in weight_copies(g, slot):
            c.start()
        first_unwritten = n_tiles * tile_rows
        if first_unwritten < y_hbm.shape[0]:
            out_scr[...] = jnp.zeros_like(out_scr)
            for lo in range(first_unwritten, y_hbm.shape[0], out_scr.shape[0]):
                n = min(out_scr.shape[0], y_hbm.shape[0] - lo)
                fill = pltpu.make_async_copy(out_scr.at[pl.ds(0, n), :], y_hbm.at[pl.ds(lo, n), :], osem.at[0])
                fill.start()
                fill.wait()
        half_copy(0).wait()
        gather_rows(tile, False)

    @pl.when(jnp.logical_and(jnp.logical_and(active, tile_first), tile == HALF_TILE))
    def _():
        half_copy(1).start()
        half_copy(1).wait()
        gather_rows(tile, False)

    @pl.when(jnp.logical_and(active, group_first))
    def _():
        for c in weight_copies(g, slot):
            c.wait()
        nxt = vnext_ref[v]

        @pl.when(nxt >= 0)
        def _():
            for c in weight_copies(nxt, 1 - slot):
                c.start()

    def visit(xs, prev_acc):
        hg = jnp.dot(xs, wg_scr[slot].astype(jnp.bfloat16), preferred_element_type=jnp.float32)
        hu = jnp.dot(xs, wu_scr[slot].astype(jnp.bfloat16), preferred_element_type=jnp.float32)
        hdn = (hg * jax.nn.sigmoid(hg) * hu).astype(jnp.bfloat16)
        y = jnp.dot(hdn, wd_scr[slot].astype(jnp.bfloat16), preferred_element_type=jnp.float32)
        rows = tile * TMG + lax.broadcasted_iota(jnp.int32, (TMG, 1), 0)
        mine = jnp.logical_and(rows >= gstart_ref[g], rows < gstart_ref[g + 1])
        acc_scr[...] = jnp.where(mine, y, prev_acc)

    def start_tile(scatter_prev):
        xs = _load_slabs(xnext_scr, TMG).astype(jnp.bfloat16)
        xs_scr[...] = xs
        nxt = jnp.where(jnp.logical_or(tile + 1 == HALF_TILE, tile + 1 == n_tiles), tile, tile + 1)
        gather_rows(nxt, True)
        if scatter_prev:
            scatter_rows(tile - 1, True)
        visit(xs, 0.0)

    @pl.when(jnp.logical_and(jnp.logical_and(active, tile_first), v == 0))
    def _():
        start_tile(False)

    @pl.when(jnp.logical_and(jnp.logical_and(active, tile_first), v > 0))
    def _():
        start_tile(True)

    @pl.when(jnp.logical_and(active, jnp.logical_not(tile_first)))
    def _():
        visit(xs_scr[...], acc_scr[...])

    @pl.when(jnp.logical_and(active, tile_last))
    def _():
        sl = tile % 2

        @pl.when(tile >= 2)
        def _():
            wait_rows(sl)

        _store_slabs(out_scr.at[pl.ds(pl.multiple_of(sl * tile_rows, SLAB), tile_rows), :], acc_scr[...])

    @pl.when(v == nvis - 1)
    def _():
        scatter_rows(n_tiles - 1, False)
        wait_rows((n_tiles - 2) % 2)
        wait_rows((n_tiles - 1) % 2)


def _ffn(vtile, vgroup, vslot, vnext, nvis, gstart, dst, xp, wg, wu, wd, layer, t_pad, t_valid):
    n_vis = _n_visits(t_valid)
    n_tiles = _n_tiles(t_valid)
    half_tokens = t_pad - HALF_SPLIT
    any_spec = pl.BlockSpec(memory_space=pl.ANY)
    return pl.pallas_call(
        functools.partial(_ffn_kernel, layer, n_tiles, half_tokens),
        grid_spec=pltpu.PrefetchScalarGridSpec(
            num_scalar_prefetch=7,
            grid=(n_vis,),
            in_specs=[any_spec, any_spec, any_spec, any_spec],
            out_specs=any_spec,
            scratch_shapes=[
                pltpu.VMEM((half_tokens * SLAB, LANES), jnp.float32),
                pltpu.VMEM((2, D_MODEL, D_EXPERT), jnp.float32),
                pltpu.VMEM((2, D_MODEL, D_EXPERT), jnp.float32),
                pltpu.VMEM((2, D_EXPERT, D_MODEL), jnp.float32),
                pltpu.VMEM((TMG * SLAB, LANES), jnp.float32),
                pltpu.VMEM((TMG, D_MODEL), jnp.bfloat16),
                pltpu.VMEM((TMG, D_MODEL), jnp.float32),
                pltpu.VMEM((2 * TMG * SLAB, LANES), jnp.float32),
                pltpu.SemaphoreType.DMA,
                pltpu.SemaphoreType.DMA((2, 3)),
                pltpu.SemaphoreType.DMA((2,)),
            ],
        ),
        out_shape=jax.ShapeDtypeStruct((2 * t_pad * SLAB, LANES), jnp.float32),
        compiler_params=pltpu.CompilerParams(
            dimension_semantics=("arbitrary",), vmem_limit_bytes=VMEM_LIMIT),
        name="expert_ffn",
    )(vtile, vgroup, vslot, vnext, nvis, gstart, dst, xp, wg, wu, wd)


def _final_kernel(h_ref, y_ref, w_ref, g_ref, o_ref):
    h = h_ref[...] + _moe_out(y_ref, w_ref, TM)
    o_ref[...] = _rms_hat(h) * g_ref[...]


def _final(h, y, wcols, g):
    n_tiles = h.shape[0] // TM
    return pl.pallas_call(
        _final_kernel,
        grid=(n_tiles,),
        in_specs=[
            pl.BlockSpec((TM, D_MODEL), lambda i: (i, 0)),
            pl.BlockSpec((2 * TM * SLAB, LANES), lambda i: (i, 0)),
            pl.BlockSpec((TM, 2), lambda i: (i, 0)),
            _RESIDENT,
        ],
        out_specs=pl.BlockSpec((TM, D_MODEL), lambda i: (i, 0)),
        out_shape=jax.ShapeDtypeStruct(h.shape, jnp.float32),
        compiler_params=pltpu.CompilerParams(
            dimension_semantics=("arbitrary",), vmem_limit_bytes=VMEM_LIMIT),
        name="moe_combine_final",
    )(h, y, wcols, g)


QB = WINDOW
KV_W = N_KV_HEADS * HEAD_DIM
N_QB = TM // QB
META_ROW0 = QB - N_META


def _rope(x, cos, sin_signed):
    q = lax.broadcasted_iota(jnp.int32, x.shape, 1) // (HEAD_DIM // 2)
    swapped = jnp.where(q % 2 == 0, pltpu.roll(x, LANES - HEAD_DIM // 2, 1),
                        pltpu.roll(x, HEAD_DIM // 2, 1))
    return x * cos + swapped * sin_signed


def _dup_heads(blk):
    lane = lax.broadcasted_iota(jnp.int32, blk.shape, 1)
    rolled = pltpu.roll(blk, HEAD_DIM, 1)
    return jnp.where(lane < HEAD_DIM, blk, rolled), jnp.where(lane < HEAD_DIM, rolled, blk)


def _kv_rows(xhat, g_kv_ref, w_kv_ref, cos, sin_signed):
    xk = (xhat * g_kv_ref[...]).astype(jnp.bfloat16)
    kv = jnp.dot(xk, w_kv_ref[...], preferred_element_type=jnp.float32)
    ks, vs = [], []
    for b in range(KV_W // LANES):
        kb = _rope(kv[:, LANES * b:LANES * (b + 1)], cos, sin_signed)
        vb = kv[:, KV_W + LANES * b:KV_W + LANES * (b + 1)]
        ks.extend(_dup_heads(kb))
        vs.extend(_dup_heads(vb))
    return [k.astype(jnp.bfloat16) for k in ks], [v.astype(jnp.bfloat16) for v in vs]


def _attn_kernel(h_ref, y_ref, w_ref, cos_ref, sin_ref, g_attn_ref, g_kv_ref, w_q_ref, w_kv_ref, w_o_ref,
                 sink_ref, g_ffn_ref, wr_ref, br_ref,
                 h3_ref, xp_ref, code_ref, wts_ref, cnt_ref,
                 k_scr, v_scr, mk_scr, mv_scr, q_scr, o_scr, bias_scr, h_scr, run_scr):
    s = pl.program_id(0)
    tiles_per_batch = N_XT // BATCH

    @pl.when(s == 0)
    def _():
        run_scr[...] = jnp.zeros_like(run_scr)
        h_meta = h_ref[0:N_META, :] + _moe_out(y_ref, w_ref, N_META)
        xhat = _rms_hat(h_meta)
        ks, vs = _kv_rows(xhat, g_kv_ref, w_kv_ref, cos_ref[0:N_META, :], sin_ref[0:N_META, :])
        mk_scr[...] = jnp.zeros_like(mk_scr)
        mv_scr[...] = jnp.zeros_like(mv_scr)
        for kvh in range(N_KV_HEADS):
            mk_scr[kvh, META_ROW0:QB, :] = ks[kvh]
            mv_scr[kvh, META_ROW0:QB, :] = vs[kvh]
        qi = lax.broadcasted_iota(jnp.int32, (QB, 2 * QB), 0)
        kj = lax.broadcasted_iota(jnp.int32, (QB, 2 * QB), 1)
        band = jnp.logical_and(kj > qi, kj <= qi + QB)
        bias_scr[0] = jnp.where(band, 0.0, NEG_INF)
        bias_scr[1] = jnp.where(jnp.logical_and(band, kj >= META_ROW0), 0.0, NEG_INF)

    @pl.when(s > 0)
    def _():
        batch_first = jnp.logical_or(s == 1, s == 1 + tiles_per_batch)

        @pl.when(batch_first)
        def _():
            k_scr[:, 0:QB, :] = mk_scr[...]
            v_scr[:, 0:QB, :] = mv_scr[...]

        h = h_ref[...] + _moe_out(y_ref, w_ref, TM)
        h_scr[...] = h
        xhat = _rms_hat(h)
        ks, vs = _kv_rows(xhat, g_kv_ref, w_kv_ref, cos_ref[...], sin_ref[...])
        for kvh in range(N_KV_HEADS):
            k_scr[kvh, QB:QB + TM, :] = ks[kvh]
            v_scr[kvh, QB:QB + TM, :] = vs[kvh]

        xq = (xhat * g_attn_ref[...]).astype(jnp.bfloat16)
        q = jnp.dot(xq, w_q_ref[...], preferred_element_type=jnp.float32)
        for hb in range(N_HEADS // 2):
            q_scr[hb] = q[:, LANES * hb:LANES * (hb + 1)]
        first_bias = jnp.where(batch_first, 1, 0)

        def head_pair(hb, carry):
            kvh = hb // (N_HEADS // N_KV_HEADS // 2)
            lane = lax.broadcasted_iota(jnp.int32, (QB, LANES), 1)
            q_all = _rope(q_scr[hb], cos_ref[...], sin_ref[...]) * (HEAD_DIM ** -0.5)
            for b in range(N_QB):
                qb = q_all[QB * b:QB * (b + 1), :]
                qs = jnp.concatenate([jnp.where(lane < HEAD_DIM, qb, 0.0),
                                      jnp.where(lane < HEAD_DIM, 0.0, qb)], axis=0).astype(jnp.bfloat16)
                kk = k_scr[kvh, QB * b:QB * (b + 2), :]
                vv = v_scr[kvh, QB * b:QB * (b + 2), :]
                sc = lax.dot_general(qs, kk, (((1,), (1,)), ((), ())),
                                     preferred_element_type=jnp.float32)
                bias = bias_scr[first_bias] if b == 0 else bias_scr[0]
                outs = []
                for j in range(2):
                    sj = sc[QB * j:QB * (j + 1), :] + bias
                    sink = sink_ref[2 * hb + j]
                    m = jnp.maximum(jnp.max(sj, axis=-1, keepdims=True), sink)
                    p = jnp.exp(sj - m)
                    den = jnp.sum(p, axis=-1, keepdims=True) + jnp.exp(sink - m)
                    pv = jnp.dot(p.astype(jnp.bfloat16), vv, preferred_element_type=jnp.float32)
                    outs.append(pv * (1.0 / den))
                o_scr[hb, QB * b:QB * (b + 1), :] = jnp.where(
                    lane < HEAD_DIM, outs[0], outs[1]).astype(jnp.bfloat16)
            return carry

        lax.fori_loop(0, N_HEADS // 2, head_pair, 0)

        k_scr[:, 0:QB, :] = k_scr[:, TM:TM + QB, :]
        v_scr[:, 0:QB, :] = v_scr[:, TM:TM + QB, :]

        o = jnp.concatenate([o_scr[hb] for hb in range(N_HEADS // 2)], axis=1)
        h3 = h_scr[...] + jnp.dot(o, w_o_ref[...], preferred_element_type=jnp.float32)
        h3_ref[...] = h3
        _ffn_prologue(h3, g_ffn_ref, wr_ref, br_ref, run_scr, (s - 1) * TM, T_X,
                      xp_ref, code_ref, wts_ref, cnt_ref)


def _attn(h1, y0, wcols0, cos_t, sin_t, g_attn, g_kv, w_q, w_kv, w_o, sinks, g_ffn, wr, br):
    def tile_x(s):
        return jnp.maximum(s - 1, 0)

    def rope_tile(s):
        per_batch = N_XT // BATCH
        return (jnp.where(s == 0, per_batch, (s - 1) % per_batch), 0)

    out_shape = [
        jax.ShapeDtypeStruct((T_X, D_MODEL), jnp.float32),
        jax.ShapeDtypeStruct((T_X * SLAB, LANES), jnp.float32),
        jax.ShapeDtypeStruct((2, T_X), jnp.int32),
        jax.ShapeDtypeStruct((2, T_X), jnp.float32),
        jax.ShapeDtypeStruct((N_KEYS, LANES), jnp.int32),
    ]
    return pl.pallas_call(
        _attn_kernel,
        grid=(N_XT + 1,),
        in_specs=[
            pl.BlockSpec((TM, D_MODEL), lambda s: (_tile_first_meta(s), 0)),
            pl.BlockSpec((2 * TM * SLAB, LANES), lambda s: (_tile_first_meta(s), 0)),
            pl.BlockSpec((TM, 2), lambda s: (_tile_first_meta(s), 0)),
            pl.BlockSpec((TM, LANES), rope_tile),
            pl.BlockSpec((TM, LANES), rope_tile),
            _RESIDENT, _RESIDENT, _RESIDENT, _RESIDENT, _RESIDENT,
            pl.BlockSpec(memory_space=pltpu.SMEM),
            _RESIDENT, _RESIDENT, _RESIDENT,
        ],
        out_specs=[
            pl.BlockSpec((TM, D_MODEL), lambda s: (tile_x(s), 0)),
            pl.BlockSpec((TM * SLAB, LANES), lambda s: (tile_x(s), 0)),
            pl.BlockSpec((2, TM), lambda s: (0, tile_x(s))),
            pl.BlockSpec((2, TM), lambda s: (0, tile_x(s))),
            _RESIDENT,
        ],
        out_shape=out_shape,
        scratch_shapes=[
            pltpu.VMEM((N_KV_HEADS, QB + TM, LANES), jnp.bfloat16),
            pltpu.VMEM((N_KV_HEADS, QB + TM, LANES), jnp.bfloat16),
            pltpu.VMEM((N_KV_HEADS, QB, LANES), jnp.bfloat16),
            pltpu.VMEM((N_KV_HEADS, QB, LANES), jnp.bfloat16),
            pltpu.VMEM((N_HEADS // 2, TM, LANES), jnp.float32),
            pltpu.VMEM((N_HEADS // 2, TM, LANES), jnp.bfloat16),
            pltpu.VMEM((2, QB, 2 * QB), jnp.float32),
            pltpu.VMEM((TM, D_MODEL), jnp.float32),
            pltpu.VMEM((N_KEYS, 1), jnp.float32),
        ],
        compiler_params=pltpu.CompilerParams(
            dimension_semantics=("arbitrary",), vmem_limit_bytes=VMEM_LIMIT),
        name="attn_route",
    )(h1, y0, wcols0, cos_t, sin_t, g_attn, g_kv, w_q, w_kv, w_o, sinks, g_ffn, wr, br)


def _router_rows(rg_w, rg_b, re_w, re_b):
    wr = jnp.zeros((N_KEYS, D_MODEL), jnp.float32)
    wr = wr.at[0:N_GROUPS].set(rg_w.T).at[8:8 + N_EXPERTS].set(re_w.T)
    br = jnp.zeros((N_KEYS, 1), jnp.float32)
    br = br.at[0:N_GROUPS, 0].set(rg_b).at[8:8 + N_EXPERTS, 0].set(re_b)
    return wr, br


def _rope_tables():
    half = HEAD_DIM // 2
    inv_freq = ROPE_THETA ** (-jnp.arange(half, dtype=jnp.float32) / half)
    pos = jnp.concatenate([N_META + jnp.arange(SEQ), jnp.arange(TM)]).astype(jnp.float32)
    ang = pos[:, None] * inv_freq[None, :]
    cos = jnp.cos(ang)
    sin = jnp.sin(ang)
    return jnp.tile(cos, (1, 4)), jnp.concatenate([-sin, sin, -sin, sin], axis=1)


def _moe(cnt, code, xp, wg, wu, wd, layer, t_pad, t_valid):
    pos, gstart, vtile, vgroup, vslot, vnext, nvis = _positions(cnt[:, 0], code, t_pad, t_valid)
    dst = _invert(pos[0], pos[1], t_valid)
    return _ffn(vtile, vgroup, vslot, vnext, nvis, gstart, dst, xp, wg, wu, wd, layer, t_pad, t_valid)


def kernel(x, meta_tokens, conv_norm_g, conv_w_in, conv_w, conv_w_out, kv_norm_g, w_kv, attn_norm_g,
           w_q, w_o, sinks, ffn_norm_g, router_group_w, router_group_b, router_expert_w,
           router_expert_b, w_gate, w_up, w_down, final_norm_g):
    bf = jnp.bfloat16
    x2d = x.reshape(T_X, D_MODEL)
    wr0, br0 = _router_rows(router_group_w[0], router_group_b[0], router_expert_w[0], router_expert_b[0])
    wr1, br1 = _router_rows(router_group_w[1], router_group_b[1], router_expert_w[1], router_expert_b[1])

    h1, xp0, code0, wts0, cnt0 = _mixer0(
        x2d, meta_tokens, conv_norm_g[0].reshape(1, D_MODEL), conv_w_in[0].astype(bf), conv_w[0],
        conv_w_out[0].astype(bf), ffn_norm_g[0].reshape(1, D_MODEL), wr0, br0)
    y0 = _moe(cnt0, code0, xp0, w_gate, w_up, w_down, 0, T_PAD0, T_VALID0)

    cos_t, sin_t = _rope_tables()
    h3, xp1, code1, wts1, cnt1 = _attn(
        h1, y0, wts0.T, cos_t, sin_t, attn_norm_g[0].reshape(1, D_MODEL), kv_norm_g.reshape(1, D_MODEL),
        w_q[0].astype(bf), w_kv.astype(bf), w_o[0].astype(bf), sinks[0], ffn_norm_g[1].reshape(1, D_MODEL),
        wr1, br1)
    y1 = _moe(cnt1, code1, xp1, w_gate, w_up, w_down, 1, T_X, T_X)
    out = _final(h3, y1, wts1.T, final_norm_g.reshape(1, D_MODEL))
    return out.reshape(BATCH, SEQ, D_MODEL)
```

```python
import functools

import jax
import jax.numpy as jnp
from jax import lax
from jax.experimental import pallas as pl
from jax.experimental.pallas import tpu as pltpu

D_MODEL = 1024
BATCH = 2
SEQ = 8192
N_META = 16
N_HEADS = 16
HEAD_DIM = 64
N_KV_HEADS = 4
WINDOW = 128
ROPE_THETA = 10000.0
N_GROUPS = 4
EXPERTS_PER_GROUP = 8
N_EXPERTS = N_GROUPS * EXPERTS_PER_GROUP
D_EXPERT = 256
NORM_EPS = 1e-5
NEG_INF = -1e30

TM = 512
N_XT = BATCH * SEQ // TM
T_X = BATCH * SEQ
T_PAD0 = (N_XT + 1) * TM
T_VALID0 = T_X + N_META
HALF_SPLIT = SEQ
N_KEYS = 2 * N_EXPERTS
TMG = 256
LANES = 128
SLAB = D_MODEL // 2 // LANES
VMEM_LIMIT = 60 * 1024 * 1024


def _cdiv(a, b):
    return (a + b - 1) // b


def _rms_hat(x):
    return x * lax.rsqrt(jnp.mean(x * x, axis=-1, keepdims=True) + NORM_EPS)


def _pack_rows(xn):
    half = D_MODEL // 2
    return pltpu.pack_elementwise([xn[:, :half], xn[:, half:]], packed_dtype=jnp.bfloat16)


def _store_slabs(ref, words):
    m = words.shape[0]
    for k in range(SLAB):
        ref[pl.ds(k, m, stride=SLAB), :] = words[:, LANES * k:LANES * (k + 1)]


def _load_slabs(ref, m):
    return jnp.concatenate([ref[pl.ds(k, m, stride=SLAB), :] for k in range(SLAB)], axis=1)


def _unpack_words(words):
    lo = pltpu.unpack_elementwise(words, index=0, packed_dtype=jnp.bfloat16, unpacked_dtype=jnp.float32)
    hi = pltpu.unpack_elementwise(words, index=1, packed_dtype=jnp.bfloat16, unpacked_dtype=jnp.float32)
    return lo, hi


def _route(xn, wr_ref, br_ref, run_scr, tok_base, valid_limit):
    def split(a):
        hi = a.astype(jnp.bfloat16)
        return hi, (a - hi.astype(jnp.float32)).astype(jnp.bfloat16)

    def nt_dot(a, b):
        return lax.dot_general(a, b, (((1,), (1,)), ((), ())), preferred_element_type=jnp.float32)

    w_hi, w_lo = split(wr_ref[...])
    x_hi, x_lo = split(xn)
    logits = nt_dot(w_hi, x_hi) + nt_dot(w_hi, x_lo) + nt_dot(w_lo, x_hi) + br_ref[...]
    g = logits[0:N_GROUPS]
    gmax = jnp.max(g, axis=0, keepdims=True)
    rid_g = lax.broadcasted_iota(jnp.int32, g.shape, 0).astype(jnp.float32)
    g_idx = jnp.min(jnp.where(g == gmax, rid_g, float(N_GROUPS)), axis=0, keepdims=True).astype(jnp.int32)
    g_w = 1.0 / jnp.sum(jnp.exp(g - gmax), axis=0, keepdims=True)
    e_sel = logits[8:8 + EXPERTS_PER_GROUP]
    for gi in range(1, N_GROUPS):
        lo = 8 + EXPERTS_PER_GROUP * gi
        e_sel = jnp.where(g_idx == gi, logits[lo:lo + EXPERTS_PER_GROUP], e_sel)
    rid_e = lax.broadcasted_iota(jnp.int32, e_sel.shape, 0).astype(jnp.float32)
    none = float(EXPERTS_PER_GROUP)
    m1 = jnp.max(e_sel, axis=0, keepdims=True)
    i1f = jnp.min(jnp.where(e_sel == m1, rid_e, none), axis=0, keepdims=True)
    e_rest = jnp.where(rid_e == i1f, -jnp.inf, e_sel)
    m2 = jnp.max(e_rest, axis=0, keepdims=True)
    i2 = jnp.min(jnp.where(e_rest == m2, rid_e, none), axis=0, keepdims=True).astype(jnp.int32)
    i1 = i1f.astype(jnp.int32)
    ex = jnp.exp(m2 - m1)
    den = 1.0 / (1.0 + ex)
    w1 = den * g_w
    w2 = ex * den * g_w

    tok = tok_base + lax.broadcasted_iota(jnp.int32, (1, TM), 1)
    half = jnp.where(tok >= HALF_SPLIT, N_EXPERTS, 0)
    key1 = half + g_idx * EXPERTS_PER_GROUP + i1
    key2 = half + g_idx * EXPERTS_PER_GROUP + i2
    kid = lax.broadcasted_iota(jnp.int32, (N_KEYS, TM), 0)
    validf = jnp.where(tok < valid_limit, 1.0, 0.0)
    oh1 = jnp.where(kid == key1, validf, 0.0)
    oh2 = jnp.where(kid == key2, validf, 0.0)
    cnt = oh1 + oh2
    tri = jnp.where(lax.broadcasted_iota(jnp.int32, (TM, TM), 0)
                    <= lax.broadcasted_iota(jnp.int32, (TM, TM), 1), 1.0, 0.0).astype(jnp.bfloat16)
    cum = jnp.dot(cnt.astype(jnp.bfloat16), tri, preferred_element_type=jnp.float32)
    before = run_scr[...] + (cum - cnt)
    rank1 = jnp.sum(oh1 * before, axis=0, keepdims=True).astype(jnp.int32)
    rank2 = jnp.sum(oh2 * before, axis=0, keepdims=True).astype(jnp.int32)
    run_scr[...] = run_scr[...] + cum[:, TM - 1:TM]
    code = jnp.concatenate([key1 * 65536 + rank1, key2 * 65536 + rank2], axis=0)
    wts = jnp.concatenate([w1, w2], axis=0)
    return code, wts


def _ffn_prologue(h_new, g_ffn_ref, wr_ref, br_ref, run_scr, tok_base, valid_limit,
                  xp_ref, code_ref, wts_ref, cnt_ref):
    xn2 = _rms_hat(h_new) * g_ffn_ref[...]
    _store_slabs(xp_ref, _pack_rows(xn2))
    code, wts = _route(xn2, wr_ref, br_ref, run_scr, tok_base, valid_limit)
    code_ref[...] = code
    wts_ref[...] = wts
    cnt_ref[...] = jnp.broadcast_to(run_scr[...], cnt_ref.shape).astype(jnp.int32)


NC = 512


def _mixer0_kernel(x_ref, meta_ref, g_conv_ref, w_in_ref, cw_ref, w_out_ref,
                   g_ffn_ref, wr_ref, br_ref,
                   h1_ref, xp_ref, code_ref, wts_ref, cnt_ref,
                   h0_scr, acc_scr, carry_scr, meta_carry_scr, run_scr):
    s = pl.program_id(0)

    @pl.when(s == 0)
    def _():
        h0_scr[...] = jnp.zeros_like(h0_scr)
        h0_scr[0:N_META, :] = meta_ref[...]
        carry_scr[...] = jnp.zeros_like(carry_scr)
        run_scr[...] = jnp.zeros_like(run_scr)

    @pl.when(s > 0)
    def _():
        h0_scr[...] = x_ref[...]

    @pl.when(s == 1 + N_XT // BATCH)
    def _():
        carry_scr[...] = meta_carry_scr[...]

    h0 = h0_scr[...]
    xn = (_rms_hat(h0) * g_conv_ref[...]).astype(jnp.bfloat16)
    row = lax.broadcasted_iota(jnp.int32, (TM, NC), 0)
    for c in range(D_MODEL // NC):
        cols = slice(NC * c, NC * (c + 1))
        gate_c = jnp.dot(xn, w_in_ref[:, D_MODEL + NC * c:D_MODEL + NC * (c + 1)],
                         preferred_element_type=jnp.float32)
        val = jnp.dot(xn, w_in_ref[:, 2 * D_MODEL + NC * c:2 * D_MODEL + NC * (c + 1)],
                      preferred_element_type=jnp.float32)
        u = gate_c * val
        tail = carry_scr[:, cols]
        c1 = tail[7:8, :]
        c2 = tail[6:7, :]
        um1 = jnp.where(row == 0, c1, pltpu.roll(u, 1, 0))
        um2 = jnp.where(row == 0, c2, jnp.where(row == 1, c1, pltpu.roll(u, 2, 0)))
        conv = um2 * cw_ref[0:1, cols] + um1 * cw_ref[1:2, cols] + u * cw_ref[2:3, cols]

        @pl.when(s == 0)
        def _():
            carry_scr[:, cols] = u[N_META - 8:N_META, :]
            meta_carry_scr[:, cols] = u[N_META - 8:N_META, :]

        @pl.when(s > 0)
        def _():
            carry_scr[:, cols] = u[TM - 8:TM, :]

        gate_b = jnp.dot(xn, w_in_ref[:, cols], preferred_element_type=jnp.float32)
        gated = (gate_b * conv).astype(jnp.bfloat16)
        part = jnp.dot(gated, w_out_ref[cols, :], preferred_element_type=jnp.float32)
        if c == 0:
            acc_scr[...] = h0 + part
        else:
            acc_scr[...] = acc_scr[...] + part

    h1 = acc_scr[...]
    h1_ref[...] = h1
    tile = jnp.where(s == 0, N_XT, s - 1)
    _ffn_prologue(h1, g_ffn_ref, wr_ref, br_ref, run_scr, tile * TM, T_VALID0,
                  xp_ref, code_ref, wts_ref, cnt_ref)


def _tile_first_meta(s):
    return jnp.where(s == 0, N_XT, s - 1)


_RESIDENT = pl.BlockSpec(memory_space=pltpu.VMEM)


def _mixer0(x2d, meta, g_conv, w_in, cw, w_out, g_ffn, wr, br):
    out_shape = [
        jax.ShapeDtypeStruct((T_PAD0, D_MODEL), jnp.float32),
        jax.ShapeDtypeStruct((T_PAD0 * SLAB, LANES), jnp.uint32),
        jax.ShapeDtypeStruct((2, T_PAD0), jnp.int32),
        jax.ShapeDtypeStruct((2, T_PAD0), jnp.float32),
        jax.ShapeDtypeStruct((N_KEYS, LANES), jnp.int32),
    ]
    return pl.pallas_call(
        _mixer0_kernel,
        grid=(N_XT + 1,),
        in_specs=[pl.BlockSpec((TM, D_MODEL), lambda s: (jnp.maximum(s - 1, 0), 0))] + [_RESIDENT] * 8,
        out_specs=[
            pl.BlockSpec((TM, D_MODEL), lambda s: (_tile_first_meta(s), 0)),
            pl.BlockSpec((TM * SLAB, LANES), lambda s: (_tile_first_meta(s), 0)),
            pl.BlockSpec((2, TM), lambda s: (0, _tile_first_meta(s))),
            pl.BlockSpec((2, TM), lambda s: (0, _tile_first_meta(s))),
            _RESIDENT,
        ],
        out_shape=out_shape,
        scratch_shapes=[
            pltpu.VMEM((TM, D_MODEL), jnp.float32),
            pltpu.VMEM((TM, D_MODEL), jnp.float32),
            pltpu.VMEM((8, D_MODEL), jnp.float32),
            pltpu.VMEM((8, D_MODEL), jnp.float32),
            pltpu.VMEM((N_KEYS, 1), jnp.float32),
        ],
        compiler_params=pltpu.CompilerParams(
            dimension_semantics=("arbitrary",), vmem_limit_bytes=VMEM_LIMIT),
        name="mixer0_route",
    )(x2d, meta, g_conv, w_in, cw, w_out, g_ffn, wr, br)


def _n_tiles(t_valid):
    return _cdiv(2 * t_valid, TMG)


def _n_visits(t_valid):
    return _n_tiles(t_valid) + N_KEYS - 1


def _positions_kernel(t_pad, t_valid, cnt_ref, code_ref, pos_ref, gstart_ref, vtile_ref, vgroup_ref,
                      vslot_ref, vnext_ref, nvis_ref, nextg_scr):
    n_vis = _n_visits(t_valid)

    def offs(g, acc):
        gstart_ref[g] = acc
        return acc + cnt_ref[g]

    total = lax.fori_loop(0, N_KEYS, offs, jnp.int32(0))
    gstart_ref[N_KEYS] = total

    def next_nonempty(i, nxt):
        g = N_KEYS - 1 - i
        nextg_scr[g] = nxt
        return jnp.where(cnt_ref[g] > 0, g, nxt)

    lax.fori_loop(0, N_KEYS, next_nonempty, jnp.int32(-1))

    def per_group(g, carry):
        v, last_g, rank = carry
        c = cnt_ref[g]
        start = gstart_ref[g]
        t0 = start // TMG
        t1 = jnp.where(c > 0, (start + c - 1) // TMG + 1, t0)

        def per_tile(tt, vv):
            vtile_ref[vv] = tt
            vgroup_ref[vv] = g
            vslot_ref[vv] = rank % 2
            vnext_ref[vv] = nextg_scr[g]
            return vv + 1

        v = lax.fori_loop(t0, t1, per_tile, v)
        return v, jnp.where(c > 0, g, last_g), jnp.where(c > 0, rank + 1, rank)

    nvis, last_g, _ = lax.fori_loop(0, N_KEYS, per_group, (jnp.int32(0), jnp.int32(0), jnp.int32(0)))
    nvis_ref[0] = nvis

    def pad(vv, c):
        vtile_ref[vv] = _n_tiles(t_valid) - 1
        vgroup_ref[vv] = last_g
        vslot_ref[vv] = 0
        vnext_ref[vv] = -1
        return c

    lax.fori_loop(nvis, n_vis, pad, 0)

    code = code_ref[...]
    key = code >> 16
    pos = code & 0xFFFF
    for g in range(N_KEYS):
        pos = pos + jnp.where(key == g, gstart_ref[g], 0)
    pos_ref[...] = pos


def _positions(cnt, code, t_pad, t_valid):
    n_vis = _n_visits(t_valid)
    smem = pl.BlockSpec(memory_space=pltpu.SMEM)
    return pl.pallas_call(
        functools.partial(_positions_kernel, t_pad, t_valid),
        in_specs=[smem, pl.BlockSpec(memory_space=pltpu.VMEM)],
        out_specs=[pl.BlockSpec(memory_space=pltpu.VMEM), smem, smem, smem, smem, smem, smem],
        out_shape=[
            jax.ShapeDtypeStruct((2, t_pad), jnp.int32),
            jax.ShapeDtypeStruct((N_KEYS + 1,), jnp.int32),
            jax.ShapeDtypeStruct((n_vis,), jnp.int32),
            jax.ShapeDtypeStruct((n_vis,), jnp.int32),
            jax.ShapeDtypeStruct((n_vis,), jnp.int32),
            jax.ShapeDtypeStruct((n_vis,), jnp.int32),
            jax.ShapeDtypeStruct((1,), jnp.int32),
        ],
        scratch_shapes=[pltpu.SMEM((N_KEYS,), jnp.int32)],
        name="sort_positions",
    )(cnt, code)


SRC_UNROLL = 8


def _invert_kernel(t_valid, r_pad, pos1_ref, pos2_ref, src_ref):
    def tail(r, c):
        src_ref[r] = 0
        return c

    lax.fori_loop(2 * t_valid, r_pad, tail, 0)

    def body(i, c):
        for j in range(SRC_UNROLL):
            t = i * SRC_UNROLL + j
            src_ref[pos1_ref[t]] = t
            src_ref[pos2_ref[t]] = t
        return c

    lax.fori_loop(0, t_valid // SRC_UNROLL, body, 0)


def _invert(pos1, pos2, t_valid):
    r_pad = _n_tiles(t_valid) * TMG
    smem = pl.BlockSpec(memory_space=pltpu.SMEM)
    return pl.pallas_call(
        functools.partial(_invert_kernel, t_valid, r_pad),
        in_specs=[smem, smem],
        out_specs=smem,
        out_shape=jax.ShapeDtypeStruct((r_pad,), jnp.int32),
        name="sort_invert",
    )(pos1, pos2)


GATHER_UNROLL = 8


def _ffn_kernel(layer, n_tiles, vtile_ref, vgroup_ref, vslot_ref, vnext_ref, nvis_ref, gstart_ref, src_ref,
                xp_hbm, wg_hbm, wu_hbm, wd_hbm, ys_hbm,
                xp_scr, wg_scr, wu_scr, wd_scr, xnext_scr, xs_scr, acc_scr, out_scr, xsem, wsem, osem):
    v = pl.program_id(0)
    nvis = nvis_ref[0]
    active = v < nvis
    tile = vtile_ref[v]
    g = vgroup_ref[v]
    slot = vslot_ref[v]
    prev = jnp.maximum(v - 1, 0)
    tile_first = jnp.logical_or(v == 0, tile != vtile_ref[prev])
    tile_last = jnp.logical_or(v == nvis - 1, tile != vtile_ref[jnp.minimum(v + 1, pl.num_programs(0) - 1)])
    group_first = jnp.logical_or(v == 0, g != vgroup_ref[prev])

    def weight_copies(group, sl):
        e = group % N_EXPERTS
        return (pltpu.make_async_copy(wg_hbm.at[layer, e], wg_scr.at[sl], wsem.at[sl, 0]),
                pltpu.make_async_copy(wu_hbm.at[layer, e], wu_scr.at[sl], wsem.at[sl, 1]),
                pltpu.make_async_copy(wd_hbm.at[layer, e], wd_scr.at[sl], wsem.at[sl, 2]))

    def out_copy(t, sl):
        return pltpu.make_async_copy(
            out_scr.at[sl], ys_hbm.at[pl.ds(pl.multiple_of(t * (TMG * SLAB), TMG * SLAB), TMG * SLAB), :],
            osem.at[sl])

    def gather_rows(t, unrolled):
        base = jnp.minimum(t, n_tiles - 1) * TMG

        def one(r):
            tok = src_ref[base + r]
            xnext_scr[pl.ds(pl.multiple_of(r * SLAB, SLAB), SLAB), :] = (
                xp_scr[pl.ds(pl.multiple_of(tok * SLAB, SLAB), SLAB), :])

        if unrolled:
            for r in range(TMG):
                one(r)
        else:
            def chunk(i, c):
                for j in range(GATHER_UNROLL):
                    one(i * GATHER_UNROLL + j)
                return c

            lax.fori_loop(0, TMG // GATHER_UNROLL, chunk, 0)

    @pl.when(v == 0)
    def _():
        cp = pltpu.make_async_copy(xp_hbm, xp_scr, xsem)
        cp.start()
        for c in weight_copies(g, slot):
            c.start()
        cp.wait()
        gather_rows(tile, False)

    @pl.when(jnp.logical_and(active, group_first))
    def _():
        for c in weight_copies(g, slot):
            c.wait()
        nxt = vnext_ref[v]

        @pl.when(nxt >= 0)
        def _():
            for c in weight_copies(nxt, 1 - slot):
                c.start()

    def visit(xs, prev_acc):
        hg = jnp.dot(xs, wg_scr[slot].astype(jnp.bfloat16), preferred_element_type=jnp.float32)
        hu = jnp.dot(xs, wu_scr[slot].astype(jnp.bfloat16), preferred_element_type=jnp.float32)
        hdn = (hg * jax.nn.sigmoid(hg) * hu).astype(jnp.bfloat16)
        y = jnp.dot(hdn, wd_scr[slot].astype(jnp.bfloat16), preferred_element_type=jnp.float32)
        rows = tile * TMG + lax.broadcasted_iota(jnp.int32, (TMG, 1), 0)
        mine = jnp.logical_and(rows >= gstart_ref[g], rows < gstart_ref[g + 1])
        acc_scr[...] = jnp.where(mine, y, prev_acc)

    @pl.when(jnp.logical_and(active, tile_first))
    def _():
        lo, hi = _unpack_words(_load_slabs(xnext_scr, TMG))
        xs = jnp.concatenate([lo, hi], axis=1).astype(jnp.bfloat16)
        xs_scr[...] = xs
        gather_rows(tile + 1, True)
        visit(xs, 0.0)

    @pl.when(jnp.logical_and(active, jnp.logical_not(tile_first)))
    def _():
        visit(xs_scr[...], acc_scr[...])

    @pl.when(jnp.logical_and(active, tile_last))
    def _():
        sl = tile % 2

        @pl.when(tile >= 2)
        def _():
            out_copy(tile - 2, sl).wait()

        _store_slabs(out_scr.at[sl], _pack_rows(acc_scr[...]))
        out_copy(tile, sl).start()

    @pl.when(v == nvis - 1)
    def _():
        out_copy(n_tiles - 2, (n_tiles - 2) % 2).wait()
        out_copy(n_tiles - 1, (n_tiles - 1) % 2).wait()


def _ffn(vtile, vgroup, vslot, vnext, nvis, gstart, src, xp, wg, wu, wd, layer, t_pad, t_valid):
    n_vis = _n_visits(t_valid)
    n_tiles = _n_tiles(t_valid)
    any_spec = pl.BlockSpec(memory_space=pl.ANY)
    return pl.pallas_call(
        functools.partial(_ffn_kernel, layer, n_tiles),
        grid_spec=pltpu.PrefetchScalarGridSpec(
            num_scalar_prefetch=7,
            grid=(n_vis,),
            in_specs=[any_spec, any_spec, any_spec, any_spec],
            out_specs=any_spec,
            scratch_shapes=[
                pltpu.VMEM((t_pad * SLAB, LANES), jnp.uint32),
                pltpu.VMEM((2, D_MODEL, D_EXPERT), jnp.float32),
                pltpu.VMEM((2, D_MODEL, D_EXPERT), jnp.float32),
                pltpu.VMEM((2, D_EXPERT, D_MODEL), jnp.float32),
                pltpu.VMEM((TMG * SLAB, LANES), jnp.uint32),
                pltpu.VMEM((TMG, D_MODEL), jnp.bfloat16),
                pltpu.VMEM((TMG, D_MODEL), jnp.float32),
                pltpu.VMEM((2, TMG * SLAB, LANES), jnp.uint32),
                pltpu.SemaphoreType.DMA,
                pltpu.SemaphoreType.DMA((2, 3)),
                pltpu.SemaphoreType.DMA((2,)),
            ],
        ),
        out_shape=jax.ShapeDtypeStruct((n_tiles * TMG * SLAB, LANES), jnp.uint32),
        compiler_params=pltpu.CompilerParams(
            dimension_semantics=("arbitrary",), vmem_limit_bytes=VMEM_LIMIT),
        name="expert_ffn",
    )(vtile, vgroup, vslot, vnext, nvis, gstart, src, xp, wg, wu, wd)


def _combine_rows(pos1_ref, pos2_ref, ys_hbm, ys_scr, y1_scr, y2_scr, sem, tile, half_rows, rh):
    tiles_per_half = HALF_SPLIT // TM
    half = jnp.where(tile >= tiles_per_half, 1, 0)
    base = half * half_rows

    @pl.when(jnp.logical_or(tile == 0, tile == tiles_per_half))
    def _():
        cp = pltpu.make_async_copy(
            ys_hbm.at[pl.ds(pl.multiple_of(base * SLAB, TMG * SLAB), rh * SLAB), :], ys_scr, sem)
        cp.start()
        cp.wait()

    tok0 = tile * TM

    def gather(i, c):
        for j in range(GATHER_UNROLL):
            r = i * GATHER_UNROLL + j
            p1 = pos1_ref[tok0 + r] - base
            p2 = pos2_ref[tok0 + r] - base
            dst = pl.ds(pl.multiple_of(r * SLAB, SLAB), SLAB)
            y1_scr[dst, :] = ys_scr[pl.ds(pl.multiple_of(p1 * SLAB, SLAB), SLAB), :]
            y2_scr[dst, :] = ys_scr[pl.ds(pl.multiple_of(p2 * SLAB, SLAB), SLAB), :]
        return c

    lax.fori_loop(0, TM // GATHER_UNROLL, gather, 0)


def _weighted_sum(y1_scr, y2_scr, w_ref):
    lo1, hi1 = _unpack_words(_load_slabs(y1_scr, TM))
    lo2, hi2 = _unpack_words(_load_slabs(y2_scr, TM))
    w1 = w_ref[:, 0:1]
    w2 = w_ref[:, 1:2]
    return jnp.concatenate([w1 * lo1 + w2 * lo2, w1 * hi1 + w2 * hi2], axis=1)


def _combine_kernel(half_rows, rh, final_norm, pos1_ref, pos2_ref, h_ref, w_ref, g_ref, ys_hbm,
                    o_ref, ys_scr, y1_scr, y2_scr, sem):
    tile = pl.program_id(0)
    _combine_rows(pos1_ref, pos2_ref, ys_hbm, ys_scr, y1_scr, y2_scr, sem, tile, half_rows, rh)
    h = h_ref[...] + _weighted_sum(y1_scr, y2_scr, w_ref)
    if final_norm:
        h = _rms_hat(h) * g_ref[...]
    o_ref[...] = h


def _combine(pos1, pos2, h, wcols, g, ys, t_valid, final_norm):
    n_tiles = _cdiv(t_valid, TM)
    half_rows = 2 * HALF_SPLIT
    rh = _n_tiles(t_valid) * TMG - half_rows
    return pl.pallas_call(
        functools.partial(_combine_kernel, half_rows, rh, final_norm),
        grid_spec=pltpu.PrefetchScalarGridSpec(
            num_scalar_prefetch=2,
            grid=(n_tiles,),
            in_specs=[
                pl.BlockSpec((TM, D_MODEL), lambda i, p1, p2: (i, 0)),
                pl.BlockSpec((TM, 2), lambda i, p1, p2: (i, 0)),
                _RESIDENT,
                pl.BlockSpec(memory_space=pl.ANY),
            ],
            out_specs=pl.BlockSpec((TM, D_MODEL), lambda i, p1, p2: (i, 0)),
            scratch_shapes=[
                pltpu.VMEM((rh * SLAB, LANES), jnp.uint32),
                pltpu.VMEM((TM * SLAB, LANES), jnp.uint32),
                pltpu.VMEM((TM * SLAB, LANES), jnp.uint32),
                pltpu.SemaphoreType.DMA,
            ],
        ),
        out_shape=jax.ShapeDtypeStruct((n_tiles * TM, D_MODEL), jnp.float32),
        compiler_params=pltpu.CompilerParams(
            dimension_semantics=("arbitrary",), vmem_limit_bytes=VMEM_LIMIT),
        name="moe_combine_final" if final_norm else "moe_combine",
    )(pos1, pos2, h, wcols, g, ys)


QB = WINDOW
KV_W = N_KV_HEADS * HEAD_DIM
N_QB = TM // QB
META_ROW0 = QB - N_META


def _rope(x, cos, sin_signed):
    q = lax.broadcasted_iota(jnp.int32, x.shape, 1) // (HEAD_DIM // 2)
    swapped = jnp.where(q % 2 == 0, pltpu.roll(x, LANES - HEAD_DIM // 2, 1),
                        pltpu.roll(x, HEAD_DIM // 2, 1))
    return x * cos + swapped * sin_signed


def _dup_heads(blk):
    lane = lax.broadcasted_iota(jnp.int32, blk.shape, 1)
    rolled = pltpu.roll(blk, HEAD_DIM, 1)
    return jnp.where(lane < HEAD_DIM, blk, rolled), jnp.where(lane < HEAD_DIM, rolled, blk)


def _kv_rows(xhat, g_kv_ref, w_kv_ref, cos, sin_signed):
    xk = (xhat * g_kv_ref[...]).astype(jnp.bfloat16)
    kv = jnp.dot(xk, w_kv_ref[...], preferred_element_type=jnp.float32)
    ks, vs = [], []
    for b in range(KV_W // LANES):
        kb = _rope(kv[:, LANES * b:LANES * (b + 1)], cos, sin_signed)
        vb = kv[:, KV_W + LANES * b:KV_W + LANES * (b + 1)]
        ks.extend(_dup_heads(kb))
        vs.extend(_dup_heads(vb))
    return [k.astype(jnp.bfloat16) for k in ks], [v.astype(jnp.bfloat16) for v in vs]


def _attn_kernel(h_ref, cos_ref, sin_ref, g_attn_ref, g_kv_ref, w_q_ref, w_kv_ref, w_o_ref, sink_ref,
                 g_ffn_ref, wr_ref, br_ref,
                 h3_ref, xp_ref, code_ref, wts_ref, cnt_ref,
                 k_scr, v_scr, mk_scr, mv_scr, q_scr, o_scr, bias_scr, run_scr):
    s = pl.program_id(0)
    tiles_per_batch = N_XT // BATCH

    @pl.when(s == 0)
    def _():
        run_scr[...] = jnp.zeros_like(run_scr)
        xhat = _rms_hat(h_ref[0:N_META, :])
        ks, vs = _kv_rows(xhat, g_kv_ref, w_kv_ref, cos_ref[0:N_META, :], sin_ref[0:N_META, :])
        mk_scr[...] = jnp.zeros_like(mk_scr)
        mv_scr[...] = jnp.zeros_like(mv_scr)
        for kvh in range(N_KV_HEADS):
            mk_scr[kvh, META_ROW0:QB, :] = ks[kvh]
            mv_scr[kvh, META_ROW0:QB, :] = vs[kvh]
        qi = lax.broadcasted_iota(jnp.int32, (QB, 2 * QB), 0)
        kj = lax.broadcasted_iota(jnp.int32, (QB, 2 * QB), 1)
        band = jnp.logical_and(kj > qi, kj <= qi + QB)
        bias_scr[0] = jnp.where(band, 0.0, NEG_INF)
        bias_scr[1] = jnp.where(jnp.logical_and(band, kj >= META_ROW0), 0.0, NEG_INF)

    @pl.when(s > 0)
    def _():
        batch_first = jnp.logical_or(s == 1, s == 1 + tiles_per_batch)

        @pl.when(batch_first)
        def _():
            k_scr[:, 0:QB, :] = mk_scr[...]
            v_scr[:, 0:QB, :] = mv_scr[...]

        h = h_ref[...]
        xhat = _rms_hat(h)
        ks, vs = _kv_rows(xhat, g_kv_ref, w_kv_ref, cos_ref[...], sin_ref[...])
        for kvh in range(N_KV_HEADS):
            k_scr[kvh, QB:QB + TM, :] = ks[kvh]
            v_scr[kvh, QB:QB + TM, :] = vs[kvh]

        xq = (xhat * g_attn_ref[...]).astype(jnp.bfloat16)
        q = jnp.dot(xq, w_q_ref[...], preferred_element_type=jnp.float32)
        for hb in range(N_HEADS // 2):
            q_scr[hb] = q[:, LANES * hb:LANES * (hb + 1)]
        first_bias = jnp.where(batch_first, 1, 0)

        def head_pair(hb, carry):
            kvh = hb // (N_HEADS // N_KV_HEADS // 2)
            lane = lax.broadcasted_iota(jnp.int32, (QB, LANES), 1)
            q_all = _rope(q_scr[hb], cos_ref[...], sin_ref[...]) * (HEAD_DIM ** -0.5)
            for b in range(N_QB):
                qb = q_all[QB * b:QB * (b + 1), :]
                qs = jnp.concatenate([jnp.where(lane < HEAD_DIM, qb, 0.0),
                                      jnp.where(lane < HEAD_DIM, 0.0, qb)], axis=0).astype(jnp.bfloat16)
                kk = k_scr[kvh, QB * b:QB * (b + 2), :]
                vv = v_scr[kvh, QB * b:QB * (b + 2), :]
                sc = lax.dot_general(qs, kk, (((1,), (1,)), ((), ())),
                                     preferred_element_type=jnp.float32)
                bias = bias_scr[first_bias] if b == 0 else bias_scr[0]
                outs = []
                for j in range(2):
                    sj = sc[QB * j:QB * (j + 1), :] + bias
                    sink = sink_ref[2 * hb + j]
                    m = jnp.maximum(jnp.max(sj, axis=-1, keepdims=True), sink)
                    p = jnp.exp(sj - m)
                    den = jnp.sum(p, axis=-1, keepdims=True) + jnp.exp(sink - m)
                    pv = jnp.dot(p.astype(jnp.bfloat16), vv, preferred_element_type=jnp.float32)
                    outs.append(pv * (1.0 / den))
                o_scr[hb, QB * b:QB * (b + 1), :] = jnp.where(
                    lane < HEAD_DIM, outs[0], outs[1]).astype(jnp.bfloat16)
            return carry

        lax.fori_loop(0, N_HEADS // 2, head_pair, 0)

        k_scr[:, 0:QB, :] = k_scr[:, TM:TM + QB, :]
        v_scr[:, 0:QB, :] = v_scr[:, TM:TM + QB, :]

        o = jnp.concatenate([o_scr[hb] for hb in range(N_HEADS // 2)], axis=1)
        h3 = h + jnp.dot(o, w_o_ref[...], preferred_element_type=jnp.float32)
        h3_ref[...] = h3
        _ffn_prologue(h3, g_ffn_ref, wr_ref, br_ref, run_scr, (s - 1) * TM, T_X,
                      xp_ref, code_ref, wts_ref, cnt_ref)


def _attn(h2, cos_t, sin_t, g_attn, g_kv, w_q, w_kv, w_o, sinks, g_ffn, wr, br):
    def tile_x(s):
        return jnp.maximum(s - 1, 0)

    def rope_tile(s):
        per_batch = N_XT // BATCH
        return (jnp.where(s == 0, per_batch, (s - 1) % per_batch), 0)

    out_shape = [
        jax.ShapeDtypeStruct((T_X, D_MODEL), jnp.float32),
        jax.ShapeDtypeStruct((T_X * SLAB, LANES), jnp.uint32),
        jax.ShapeDtypeStruct((2, T_X), jnp.int32),
        jax.ShapeDtypeStruct((2, T_X), jnp.float32),
        jax.ShapeDtypeStruct((N_KEYS, LANES), jnp.int32),
    ]
    return pl.pallas_call(
        _attn_kernel,
        grid=(N_XT + 1,),
        in_specs=[
            pl.BlockSpec((TM, D_MODEL), lambda s: (_tile_first_meta(s), 0)),
            pl.BlockSpec((TM, LANES), rope_tile),
            pl.BlockSpec((TM, LANES), rope_tile),
            _RESIDENT, _RESIDENT, _RESIDENT, _RESIDENT, _RESIDENT,
            pl.BlockSpec(memory_space=pltpu.SMEM),
            _RESIDENT, _RESIDENT, _RESIDENT,
        ],
        out_specs=[
            pl.BlockSpec((TM, D_MODEL), lambda s: (tile_x(s), 0)),
            pl.BlockSpec((TM * SLAB, LANES), lambda s: (tile_x(s), 0)),
            pl.BlockSpec((2, TM), lambda s: (0, tile_x(s))),
            pl.BlockSpec((2, TM), lambda s: (0, tile_x(s))),
            _RESIDENT,
        ],
        out_shape=out_shape,
        scratch_shapes=[
            pltpu.VMEM((N_KV_HEADS, QB + TM, LANES), jnp.bfloat16),
            pltpu.VMEM((N_KV_HEADS, QB + TM, LANES), jnp.bfloat16),
            pltpu.VMEM((N_KV_HEADS, QB, LANES), jnp.bfloat16),
            pltpu.VMEM((N_KV_HEADS, QB, LANES), jnp.bfloat16),
            pltpu.VMEM((N_HEADS // 2, TM, LANES), jnp.float32),
            pltpu.VMEM((N_HEADS // 2, TM, LANES), jnp.bfloat16),
            pltpu.VMEM((2, QB, 2 * QB), jnp.float32),
            pltpu.VMEM((N_KEYS, 1), jnp.float32),
        ],
        compiler_params=pltpu.CompilerParams(
            dimension_semantics=("arbitrary",), vmem_limit_bytes=VMEM_LIMIT),
        name="attn_route",
    )(h2, cos_t, sin_t, g_attn, g_kv, w_q, w_kv, w_o, sinks, g_ffn, wr, br)


def _router_rows(rg_w, rg_b, re_w, re_b):
    wr = jnp.zeros((N_KEYS, D_MODEL), jnp.float32)
    wr = wr.at[0:N_GROUPS].set(rg_w.T).at[8:8 + N_EXPERTS].set(re_w.T)
    br = jnp.zeros((N_KEYS, 1), jnp.float32)
    br = br.at[0:N_GROUPS, 0].set(rg_b).at[8:8 + N_EXPERTS, 0].set(re_b)
    return wr, br


def _rope_tables():
    half = HEAD_DIM // 2
    inv_freq = ROPE_THETA ** (-jnp.arange(half, dtype=jnp.float32) / half)
    pos = jnp.concatenate([N_META + jnp.arange(SEQ), jnp.arange(TM)]).astype(jnp.float32)
    ang = pos[:, None] * inv_freq[None, :]
    cos = jnp.cos(ang)
    sin = jnp.sin(ang)
    return jnp.tile(cos, (1, 4)), jnp.concatenate([-sin, sin, -sin, sin], axis=1)


def _moe(cnt, code, wts, xp, h, wg, wu, wd, layer, g_final, t_pad, t_valid, final_norm):
    pos, gstart, vtile, vgroup, vslot, vnext, nvis = _positions(cnt[:, 0], code, t_pad, t_valid)
    src = _invert(pos[0], pos[1], t_valid)
    ys = _ffn(vtile, vgroup, vslot, vnext, nvis, gstart, src, xp, wg, wu, wd, layer, t_pad, t_valid)
    return _combine(pos[0], pos[1], h, wts.T, g_final, ys, t_valid, final_norm)


def kernel(x, meta_tokens, conv_norm_g, conv_w_in, conv_w, conv_w_out, kv_norm_g, w_kv, attn_norm_g,
           w_q, w_o, sinks, ffn_norm_g, router_group_w, router_group_b, router_expert_w,
           router_expert_b, w_gate, w_up, w_down, final_norm_g):
    bf = jnp.bfloat16
    x2d = x.reshape(T_X, D_MODEL)
    wr0, br0 = _router_rows(router_group_w[0], router_group_b[0], router_expert_w[0], router_expert_b[0])
    wr1, br1 = _router_rows(router_group_w[1], router_group_b[1], router_expert_w[1], router_expert_b[1])
    g_final = final_norm_g.reshape(1, D_MODEL)

    h1, xp0, code0, wts0, cnt0 = _mixer0(
        x2d, meta_tokens, conv_norm_g[0].reshape(1, D_MODEL), conv_w_in[0].astype(bf), conv_w[0],
        conv_w_out[0].astype(bf), ffn_norm_g[0].reshape(1, D_MODEL), wr0, br0)
    h2 = _moe(cnt0, code0, wts0, xp0, h1, w_gate, w_up, w_down, 0, g_final, T_PAD0, T_VALID0, False)

    cos_t, sin_t = _rope_tables()
    h3, xp1, code1, wts1, cnt1 = _attn(
        h2, cos_t, sin_t, attn_norm_g[0].reshape(1, D_MODEL), kv_norm_g.reshape(1, D_MODEL),
        w_q[0].astype(bf), w_kv.astype(bf), w_o[0].astype(bf), sinks[0], ffn_norm_g[1].reshape(1, D_MODEL),
        wr1, br1)
    out = _moe(cnt1, code1, wts1, xp1, h3, w_gate, w_up, w_down, 1, g_final, T_X, T_X, True)
    return out.reshape(BATCH, SEQ, D_MODEL)
```

```python
import functools

import jax
import jax.numpy as jnp
from jax import lax
from jax.experimental import pallas as pl
from jax.experimental.pallas import tpu as pltpu

D_MODEL = 1024
BATCH = 2
SEQ = 8192
N_META = 16
N_HEADS = 16
HEAD_DIM = 64
N_KV_HEADS = 4
WINDOW = 128
ROPE_THETA = 10000.0
N_GROUPS = 4
EXPERTS_PER_GROUP = 8
N_EXPERTS = N_GROUPS * EXPERTS_PER_GROUP
D_EXPERT = 256
NORM_EPS = 1e-5
NEG_INF = -1e30

TM = 512
N_XT = BATCH * SEQ // TM
T_X = BATCH * SEQ
T_PAD0 = (N_XT + 1) * TM
T_VALID0 = T_X + N_META
HALF_SPLIT = SEQ
N_KEYS = 2 * N_EXPERTS
TMG = 256
LANES = 128
SLAB = D_MODEL // 2 // LANES
VMEM_V7X = 64 * 1024 * 1024
VMEM_LIMIT = VMEM_V7X - 2 * 1024 * 1024


def _cdiv(a, b):
    return (a + b - 1) // b


def _rms_hat(x):
    return x * lax.rsqrt(jnp.mean(x * x, axis=-1, keepdims=True) + NORM_EPS)


def _pack_rows(xn):
    half = D_MODEL // 2
    return pltpu.pack_elementwise([xn[:, :half], xn[:, half:]], packed_dtype=jnp.bfloat16)


def _store_slabs(ref, words):
    m = words.shape[0]
    for k in range(SLAB):
        ref[pl.ds(k, m, stride=SLAB), :] = words[:, LANES * k:LANES * (k + 1)]


def _load_slabs(ref, m):
    return jnp.concatenate([ref[pl.ds(k, m, stride=SLAB), :] for k in range(SLAB)], axis=1)


def _unpack_words(words):
    lo = pltpu.unpack_elementwise(words, index=0, packed_dtype=jnp.bfloat16, unpacked_dtype=jnp.float32)
    hi = pltpu.unpack_elementwise(words, index=1, packed_dtype=jnp.bfloat16, unpacked_dtype=jnp.float32)
    return lo, hi


def _route(xn, wr_ref, br_ref, run_scr, tok_base, valid_limit):
    def split(a):
        hi = a.astype(jnp.bfloat16)
        return hi, (a - hi.astype(jnp.float32)).astype(jnp.bfloat16)

    def nt_dot(a, b):
        return lax.dot_general(a, b, (((1,), (1,)), ((), ())), preferred_element_type=jnp.float32)

    w_hi, w_lo = split(wr_ref[...])
    x_hi, x_lo = split(xn)
    logits = nt_dot(w_hi, x_hi) + nt_dot(w_hi, x_lo) + nt_dot(w_lo, x_hi) + br_ref[...]
    g = logits[0:N_GROUPS]
    gmax = jnp.max(g, axis=0, keepdims=True)
    rid_g = lax.broadcasted_iota(jnp.int32, g.shape, 0).astype(jnp.float32)
    g_idx = jnp.min(jnp.where(g == gmax, rid_g, float(N_GROUPS)), axis=0, keepdims=True).astype(jnp.int32)
    g_w = 1.0 / jnp.sum(jnp.exp(g - gmax), axis=0, keepdims=True)
    e_sel = logits[8:8 + EXPERTS_PER_GROUP]
    for gi in range(1, N_GROUPS):
        lo = 8 + EXPERTS_PER_GROUP * gi
        e_sel = jnp.where(g_idx == gi, logits[lo:lo + EXPERTS_PER_GROUP], e_sel)
    rid_e = lax.broadcasted_iota(jnp.int32, e_sel.shape, 0).astype(jnp.float32)
    none = float(EXPERTS_PER_GROUP)
    m1 = jnp.max(e_sel, axis=0, keepdims=True)
    i1f = jnp.min(jnp.where(e_sel == m1, rid_e, none), axis=0, keepdims=True)
    e_rest = jnp.where(rid_e == i1f, -jnp.inf, e_sel)
    m2 = jnp.max(e_rest, axis=0, keepdims=True)
    i2 = jnp.min(jnp.where(e_rest == m2, rid_e, none), axis=0, keepdims=True).astype(jnp.int32)
    i1 = i1f.astype(jnp.int32)
    ex = jnp.exp(m2 - m1)
    den = 1.0 / (1.0 + ex)
    w1 = den * g_w
    w2 = ex * den * g_w

    tok = tok_base + lax.broadcasted_iota(jnp.int32, (1, TM), 1)
    half = jnp.where(jnp.logical_and(tok >= HALF_SPLIT, tok < T_X), N_EXPERTS, 0)
    key1 = half + g_idx * EXPERTS_PER_GROUP + i1
    key2 = half + g_idx * EXPERTS_PER_GROUP + i2
    kid = lax.broadcasted_iota(jnp.int32, (N_KEYS, TM), 0)
    validf = jnp.where(tok < valid_limit, 1.0, 0.0)
    oh1 = jnp.where(kid == key1, validf, 0.0)
    oh2 = jnp.where(kid == key2, validf, 0.0)
    cnt = oh1 + oh2
    tri = jnp.where(lax.broadcasted_iota(jnp.int32, (TM, TM), 0)
                    <= lax.broadcasted_iota(jnp.int32, (TM, TM), 1), 1.0, 0.0).astype(jnp.bfloat16)
    cum = jnp.dot(cnt.astype(jnp.bfloat16), tri, preferred_element_type=jnp.float32)
    before = run_scr[...] + (cum - cnt)
    rank1 = jnp.sum(oh1 * before, axis=0, keepdims=True).astype(jnp.int32)
    rank2 = jnp.sum(oh2 * before, axis=0, keepdims=True).astype(jnp.int32)
    run_scr[...] = run_scr[...] + cum[:, TM - 1:TM]
    code = jnp.concatenate([key1 * 65536 + rank1, key2 * 65536 + rank2], axis=0)
    wts = jnp.concatenate([w1, w2], axis=0)
    return code, wts


def _ffn_prologue(h_new, g_ffn_ref, wr_ref, br_ref, run_scr, tok_base, valid_limit,
                  xp_ref, code_ref, wts_ref, cnt_ref):
    xn2 = _rms_hat(h_new) * g_ffn_ref[...]
    _store_slabs(xp_ref, _pack_rows(xn2))
    code, wts = _route(xn2, wr_ref, br_ref, run_scr, tok_base, valid_limit)
    code_ref[...] = code
    wts_ref[...] = wts
    cnt_ref[...] = jnp.broadcast_to(run_scr[...], cnt_ref.shape).astype(jnp.int32)


NC = 512


def _mixer0_kernel(x_ref, meta_ref, g_conv_ref, w_in_ref, cw_ref, w_out_ref,
                   g_ffn_ref, wr_ref, br_ref,
                   h1_ref, xp_ref, code_ref, wts_ref, cnt_ref,
                   h0_scr, acc_scr, carry_scr, meta_carry_scr, run_scr):
    s = pl.program_id(0)

    @pl.when(s == 0)
    def _():
        h0_scr[...] = jnp.zeros_like(h0_scr)
        h0_scr[0:N_META, :] = meta_ref[...]
        carry_scr[...] = jnp.zeros_like(carry_scr)
        run_scr[...] = jnp.zeros_like(run_scr)

    @pl.when(s > 0)
    def _():
        h0_scr[...] = x_ref[...]

    @pl.when(s == 1 + N_XT // BATCH)
    def _():
        carry_scr[...] = meta_carry_scr[...]

    h0 = h0_scr[...]
    xn = (_rms_hat(h0) * g_conv_ref[...]).astype(jnp.bfloat16)
    row = lax.broadcasted_iota(jnp.int32, (TM, NC), 0)
    for c in range(D_MODEL // NC):
        cols = slice(NC * c, NC * (c + 1))
        gate_c = jnp.dot(xn, w_in_ref[:, D_MODEL + NC * c:D_MODEL + NC * (c + 1)],
                         preferred_element_type=jnp.float32)
        val = jnp.dot(xn, w_in_ref[:, 2 * D_MODEL + NC * c:2 * D_MODEL + NC * (c + 1)],
                      preferred_element_type=jnp.float32)
        u = gate_c * val
        tail = carry_scr[:, cols]
        c1 = tail[7:8, :]
        c2 = tail[6:7, :]
        um1 = jnp.where(row == 0, c1, pltpu.roll(u, 1, 0))
        um2 = jnp.where(row == 0, c2, jnp.where(row == 1, c1, pltpu.roll(u, 2, 0)))
        conv = um2 * cw_ref[0:1, cols] + um1 * cw_ref[1:2, cols] + u * cw_ref[2:3, cols]

        @pl.when(s == 0)
        def _():
            carry_scr[:, cols] = u[N_META - 8:N_META, :]
            meta_carry_scr[:, cols] = u[N_META - 8:N_META, :]

        @pl.when(s > 0)
        def _():
            carry_scr[:, cols] = u[TM - 8:TM, :]

        gate_b = jnp.dot(xn, w_in_ref[:, cols], preferred_element_type=jnp.float32)
        gated = (gate_b * conv).astype(jnp.bfloat16)
        part = jnp.dot(gated, w_out_ref[cols, :], preferred_element_type=jnp.float32)
        if c == 0:
            acc_scr[...] = h0 + part
        else:
            acc_scr[...] = acc_scr[...] + part

    h1 = acc_scr[...]
    h1_ref[...] = h1
    tile = jnp.where(s == 0, N_XT, s - 1)
    _ffn_prologue(h1, g_ffn_ref, wr_ref, br_ref, run_scr, tile * TM, T_VALID0,
                  xp_ref, code_ref, wts_ref, cnt_ref)


def _tile_first_meta(s):
    return jnp.where(s == 0, N_XT, s - 1)


_RESIDENT = pl.BlockSpec(memory_space=pltpu.VMEM)


def _mixer0(x2d, meta, g_conv, w_in, cw, w_out, g_ffn, wr, br):
    out_shape = [
        jax.ShapeDtypeStruct((T_PAD0, D_MODEL), jnp.float32),
        jax.ShapeDtypeStruct((T_PAD0 * SLAB, LANES), jnp.uint32),
        jax.ShapeDtypeStruct((2, T_PAD0), jnp.int32),
        jax.ShapeDtypeStruct((2, T_PAD0), jnp.float32),
        jax.ShapeDtypeStruct((N_KEYS, LANES), jnp.int32),
    ]
    return pl.pallas_call(
        _mixer0_kernel,
        grid=(N_XT + 1,),
        in_specs=[pl.BlockSpec((TM, D_MODEL), lambda s: (jnp.maximum(s - 1, 0), 0))] + [_RESIDENT] * 8,
        out_specs=[
            pl.BlockSpec((TM, D_MODEL), lambda s: (_tile_first_meta(s), 0)),
            pl.BlockSpec((TM * SLAB, LANES), lambda s: (_tile_first_meta(s), 0)),
            pl.BlockSpec((2, TM), lambda s: (0, _tile_first_meta(s))),
            pl.BlockSpec((2, TM), lambda s: (0, _tile_first_meta(s))),
            _RESIDENT,
        ],
        out_shape=out_shape,
        scratch_shapes=[
            pltpu.VMEM((TM, D_MODEL), jnp.float32),
            pltpu.VMEM((TM, D_MODEL), jnp.float32),
            pltpu.VMEM((8, D_MODEL), jnp.float32),
            pltpu.VMEM((8, D_MODEL), jnp.float32),
            pltpu.VMEM((N_KEYS, 1), jnp.float32),
        ],
        compiler_params=pltpu.CompilerParams(
            dimension_semantics=("arbitrary",), vmem_limit_bytes=VMEM_LIMIT),
        name="mixer0_route",
    )(x2d, meta, g_conv, w_in, cw, w_out, g_ffn, wr, br)


def _n_tiles(t_valid):
    return _cdiv(2 * t_valid, TMG)


def _n_visits(t_valid):
    return _n_tiles(t_valid) + N_KEYS - 1


def _positions_kernel(t_pad, t_valid, cnt_ref, code_ref, pos_ref, gstart_ref, vtile_ref, vgroup_ref,
                      vslot_ref, vnext_ref, nvis_ref, nextg_scr):
    n_vis = _n_visits(t_valid)

    def offs(g, acc):
        gstart_ref[g] = acc
        return acc + cnt_ref[g]

    total = lax.fori_loop(0, N_KEYS, offs, jnp.int32(0))
    gstart_ref[N_KEYS] = total

    def next_nonempty(i, nxt):
        g = N_KEYS - 1 - i
        nextg_scr[g] = nxt
        return jnp.where(cnt_ref[g] > 0, g, nxt)

    lax.fori_loop(0, N_KEYS, next_nonempty, jnp.int32(-1))

    def per_group(g, carry):
        v, last_g, rank = carry
        c = cnt_ref[g]
        start = gstart_ref[g]
        t0 = start // TMG
        t1 = jnp.where(c > 0, (start + c - 1) // TMG + 1, t0)

        def per_tile(tt, vv):
            vtile_ref[vv] = tt
            vgroup_ref[vv] = g
            vslot_ref[vv] = rank % 2
            vnext_ref[vv] = nextg_scr[g]
            return vv + 1

        v = lax.fori_loop(t0, t1, per_tile, v)
        return v, jnp.where(c > 0, g, last_g), jnp.where(c > 0, rank + 1, rank)

    nvis, last_g, _ = lax.fori_loop(0, N_KEYS, per_group, (jnp.int32(0), jnp.int32(0), jnp.int32(0)))
    nvis_ref[0] = nvis

    def pad(vv, c):
        vtile_ref[vv] = _n_tiles(t_valid) - 1
        vgroup_ref[vv] = last_g
        vslot_ref[vv] = 0
        vnext_ref[vv] = -1
        return c

    lax.fori_loop(nvis, n_vis, pad, 0)

    code = code_ref[...]
    key = code >> 16
    pos = code & 0xFFFF
    for g in range(N_KEYS):
        pos = pos + jnp.where(key == g, gstart_ref[g], 0)
    pos_ref[...] = pos


def _positions(cnt, code, t_pad, t_valid):
    n_vis = _n_visits(t_valid)
    smem = pl.BlockSpec(memory_space=pltpu.SMEM)
    return pl.pallas_call(
        functools.partial(_positions_kernel, t_pad, t_valid),
        in_specs=[smem, pl.BlockSpec(memory_space=pltpu.VMEM)],
        out_specs=[pl.BlockSpec(memory_space=pltpu.VMEM), smem, smem, smem, smem, smem, smem],
        out_shape=[
            jax.ShapeDtypeStruct((2, t_pad), jnp.int32),
            jax.ShapeDtypeStruct((N_KEYS + 1,), jnp.int32),
            jax.ShapeDtypeStruct((n_vis,), jnp.int32),
            jax.ShapeDtypeStruct((n_vis,), jnp.int32),
            jax.ShapeDtypeStruct((n_vis,), jnp.int32),
            jax.ShapeDtypeStruct((n_vis,), jnp.int32),
            jax.ShapeDtypeStruct((1,), jnp.int32),
        ],
        scratch_shapes=[pltpu.SMEM((N_KEYS,), jnp.int32)],
        name="sort_positions",
    )(cnt, code)


SRC_UNROLL = 8


def _invert_kernel(t_valid, r_pad, pos1_ref, pos2_ref, src_ref):
    def tail(r, c):
        src_ref[r] = 0
        return c

    lax.fori_loop(2 * t_valid, r_pad, tail, 0)

    def body(i, c):
        for j in range(SRC_UNROLL):
            t = i * SRC_UNROLL + j
            src_ref[pos1_ref[t]] = t
            src_ref[pos2_ref[t]] = t
        return c

    lax.fori_loop(0, t_valid // SRC_UNROLL, body, 0)


def _invert(pos1, pos2, t_valid):
    r_pad = _n_tiles(t_valid) * TMG
    smem = pl.BlockSpec(memory_space=pltpu.SMEM)
    return pl.pallas_call(
        functools.partial(_invert_kernel, t_valid, r_pad),
        in_specs=[smem, smem],
        out_specs=smem,
        out_shape=jax.ShapeDtypeStruct((r_pad,), jnp.int32),
        name="sort_invert",
    )(pos1, pos2)


GATHER_UNROLL = 8


def _ffn_kernel(layer, n_tiles, vtile_ref, vgroup_ref, vslot_ref, vnext_ref, nvis_ref, gstart_ref, src_ref,
                xp_hbm, wg_hbm, wu_hbm, wd_hbm, ys_hbm,
                xp_scr, wg_scr, wu_scr, wd_scr, xnext_scr, xs_scr, acc_scr, out_scr, xsem, wsem, osem):
    v = pl.program_id(0)
    nvis = nvis_ref[0]
    active = v < nvis
    tile = vtile_ref[v]
    g = vgroup_ref[v]
    slot = vslot_ref[v]
    prev = jnp.maximum(v - 1, 0)
    tile_first = jnp.logical_or(v == 0, tile != vtile_ref[prev])
    tile_last = jnp.logical_or(v == nvis - 1, tile != vtile_ref[jnp.minimum(v + 1, pl.num_programs(0) - 1)])
    group_first = jnp.logical_or(v == 0, g != vgroup_ref[prev])

    def weight_copies(group, sl):
        e = group % N_EXPERTS
        return (pltpu.make_async_copy(wg_hbm.at[layer, e], wg_scr.at[sl], wsem.at[sl, 0]),
                pltpu.make_async_copy(wu_hbm.at[layer, e], wu_scr.at[sl], wsem.at[sl, 1]),
                pltpu.make_async_copy(wd_hbm.at[layer, e], wd_scr.at[sl], wsem.at[sl, 2]))

    def out_copy(t, sl):
        return pltpu.make_async_copy(
            out_scr.at[sl], ys_hbm.at[pl.ds(pl.multiple_of(t * (TMG * SLAB), TMG * SLAB), TMG * SLAB), :],
            osem.at[sl])

    def gather_rows(t, unrolled):
        base = jnp.minimum(t, n_tiles - 1) * TMG

        def one(r):
            tok = src_ref[base + r]
            xnext_scr[pl.ds(pl.multiple_of(r * SLAB, SLAB), SLAB), :] = (
                xp_scr[pl.ds(pl.multiple_of(tok * SLAB, SLAB), SLAB), :])

        if unrolled:
            for r in range(TMG):
                one(r)
        else:
            def chunk(i, c):
                for j in range(GATHER_UNROLL):
                    one(i * GATHER_UNROLL + j)
                return c

            lax.fori_loop(0, TMG // GATHER_UNROLL, chunk, 0)

    @pl.when(v == 0)
    def _():
        cp = pltpu.make_async_copy(xp_hbm, xp_scr, xsem)
        cp.start()
        for c in weight_copies(g, slot):
            c.start()
        cp.wait()
        gather_rows(tile, False)

    @pl.when(jnp.logical_and(active, group_first))
    def _():
        for c in weight_copies(g, slot):
            c.wait()
        nxt = vnext_ref[v]

        @pl.when(nxt >= 0)
        def _():
            for c in weight_copies(nxt, 1 - slot):
                c.start()

    def visit(xs, prev_acc):
        hg = jnp.dot(xs, wg_scr[slot].astype(jnp.bfloat16), preferred_element_type=jnp.float32)
        hu = jnp.dot(xs, wu_scr[slot].astype(jnp.bfloat16), preferred_element_type=jnp.float32)
        hdn = (hg * jax.nn.sigmoid(hg) * hu).astype(jnp.bfloat16)
        y = jnp.dot(hdn, wd_scr[slot].astype(jnp.bfloat16), preferred_element_type=jnp.float32)
        rows = tile * TMG + lax.broadcasted_iota(jnp.int32, (TMG, 1), 0)
        mine = jnp.logical_and(rows >= gstart_ref[g], rows < gstart_ref[g + 1])
        acc_scr[...] = jnp.where(mine, y, prev_acc)

    @pl.when(jnp.logical_and(active, tile_first))
    def _():
        lo, hi = _unpack_words(_load_slabs(xnext_scr, TMG))
        xs = jnp.concatenate([lo, hi], axis=1).astype(jnp.bfloat16)
        xs_scr[...] = xs
        gather_rows(tile + 1, True)
        visit(xs, 0.0)

    @pl.when(jnp.logical_and(active, jnp.logical_not(tile_first)))
    def _():
        visit(xs_scr[...], acc_scr[...])

    @pl.when(jnp.logical_and(active, tile_last))
    def _():
        sl = tile % 2

        @pl.when(tile >= 2)
        def _():
            out_copy(tile - 2, sl).wait()

        _store_slabs(out_scr.at[sl], _pack_rows(acc_scr[...]))
        out_copy(tile, sl).start()

    @pl.when(v == nvis - 1)
    def _():
        out_copy(n_tiles - 2, (n_tiles - 2) % 2).wait()
        out_copy(n_tiles - 1, (n_tiles - 1) % 2).wait()


def _ffn(vtile, vgroup, vslot, vnext, nvis, gstart, src, xp, wg, wu, wd, layer, t_pad, t_valid):
    n_vis = _n_visits(t_valid)
    n_tiles = _n_tiles(t_valid)
    any_spec = pl.BlockSpec(memory_space=pl.ANY)
    return pl.pallas_call(
        functools.partial(_ffn_kernel, layer, n_tiles),
        grid_spec=pltpu.PrefetchScalarGridSpec(
            num_scalar_prefetch=7,
            grid=(n_vis,),
            in_specs=[any_spec, any_spec, any_spec, any_spec],
            out_specs=any_spec,
            scratch_shapes=[
                pltpu.VMEM((t_pad * SLAB, LANES), jnp.uint32),
                pltpu.VMEM((2, D_MODEL, D_EXPERT), jnp.float32),
                pltpu.VMEM((2, D_MODEL, D_EXPERT), jnp.float32),
                pltpu.VMEM((2, D_EXPERT, D_MODEL), jnp.float32),
                pltpu.VMEM((TMG * SLAB, LANES), jnp.uint32),
                pltpu.VMEM((TMG, D_MODEL), jnp.bfloat16),
                pltpu.VMEM((TMG, D_MODEL), jnp.float32),
                pltpu.VMEM((2, TMG * SLAB, LANES), jnp.uint32),
                pltpu.SemaphoreType.DMA,
                pltpu.SemaphoreType.DMA((2, 3)),
                pltpu.SemaphoreType.DMA((2,)),
            ],
        ),
        out_shape=jax.ShapeDtypeStruct((n_tiles * TMG * SLAB, LANES), jnp.uint32),
        compiler_params=pltpu.CompilerParams(
            dimension_semantics=("arbitrary",), vmem_limit_bytes=VMEM_LIMIT),
        name="expert_ffn",
    )(vtile, vgroup, vslot, vnext, nvis, gstart, src, xp, wg, wu, wd)


TILES_PER_HALF = HALF_SPLIT // TM


def _load_half(ys_hbm, ys_scr, sem, half, half0_rows, half1_rows):
    start, rows = (0, half0_rows) if half == 0 else (half0_rows, half1_rows)
    cp = pltpu.make_async_copy(ys_hbm.at[pl.ds(start * SLAB, rows * SLAB), :],
                               ys_scr.at[pl.ds(0, rows * SLAB), :], sem)
    cp.start()
    cp.wait()


def _gather_pairs(pos1_ref, pos2_ref, ys_scr, y1_scr, y2_scr, tok0, base, n, unrolled):
    def one(r):
        p1 = pos1_ref[tok0 + r] - base
        p2 = pos2_ref[tok0 + r] - base
        dst = pl.ds(pl.multiple_of(r * SLAB, SLAB), SLAB)
        y1_scr[dst, :] = ys_scr[pl.ds(pl.multiple_of(p1 * SLAB, SLAB), SLAB), :]
        y2_scr[dst, :] = ys_scr[pl.ds(pl.multiple_of(p2 * SLAB, SLAB), SLAB), :]

    if unrolled:
        for r in range(n):
            one(r)
    else:
        def chunk(i, c):
            for j in range(GATHER_UNROLL):
                one(i * GATHER_UNROLL + j)
            return c

        lax.fori_loop(0, n // GATHER_UNROLL, chunk, 0)


def _weighted_sum(y1_scr, y2_scr, w_ref, m):
    lo1, hi1 = _unpack_words(_load_slabs(y1_scr, m))
    lo2, hi2 = _unpack_words(_load_slabs(y2_scr, m))
    w1 = w_ref[0:m, 0:1]
    w2 = w_ref[0:m, 1:2]
    return jnp.concatenate([w1 * lo1 + w2 * lo2, w1 * hi1 + w2 * hi2], axis=1)


def _next_x_tile(tile):
    nxt = tile + 1
    return jnp.where(jnp.logical_or(nxt == TILES_PER_HALF, nxt == N_XT), tile, nxt)


def _final_kernel(half_rows, pos1_ref, pos2_ref, h_ref, w_ref, g_ref, ys_hbm,
                  o_ref, ys_scr, y1_scr, y2_scr, sem):
    tile = pl.program_id(0)

    @pl.when(tile == 0)
    def _():
        _load_half(ys_hbm, ys_scr, sem, 0, half_rows, half_rows)
        _gather_pairs(pos1_ref, pos2_ref, ys_scr, y1_scr, y2_scr, 0, 0, TM, False)

    @pl.when(tile == TILES_PER_HALF)
    def _():
        _load_half(ys_hbm, ys_scr, sem, 1, half_rows, half_rows)
        _gather_pairs(pos1_ref, pos2_ref, ys_scr, y1_scr, y2_scr, TILES_PER_HALF * TM, half_rows, TM, False)

    h = h_ref[...] + _weighted_sum(y1_scr, y2_scr, w_ref, TM)
    nxt = _next_x_tile(tile)
    _gather_pairs(pos1_ref, pos2_ref, ys_scr, y1_scr, y2_scr, nxt * TM,
                  jnp.where(nxt >= TILES_PER_HALF, half_rows, 0), TM, True)
    o_ref[...] = _rms_hat(h) * g_ref[...]


def _final(pos1, pos2, h, wcols, g, ys):
    half_rows = 2 * HALF_SPLIT
    return pl.pallas_call(
        functools.partial(_final_kernel, half_rows),
        grid_spec=pltpu.PrefetchScalarGridSpec(
            num_scalar_prefetch=2,
            grid=(N_XT,),
            in_specs=[
                pl.BlockSpec((TM, D_MODEL), lambda i, p1, p2: (i, 0)),
                pl.BlockSpec((TM, 2), lambda i, p1, p2: (i, 0)),
                _RESIDENT,
                pl.BlockSpec(memory_space=pl.ANY),
            ],
            out_specs=pl.BlockSpec((TM, D_MODEL), lambda i, p1, p2: (i, 0)),
            scratch_shapes=[
                pltpu.VMEM((half_rows * SLAB, LANES), jnp.uint32),
                pltpu.VMEM((TM * SLAB, LANES), jnp.uint32),
                pltpu.VMEM((TM * SLAB, LANES), jnp.uint32),
                pltpu.SemaphoreType.DMA,
            ],
        ),
        out_shape=jax.ShapeDtypeStruct((T_X, D_MODEL), jnp.float32),
        compiler_params=pltpu.CompilerParams(
            dimension_semantics=("arbitrary",), vmem_limit_bytes=VMEM_LIMIT),
        name="moe_combine_final",
    )(pos1, pos2, h, wcols, g, ys)


QB = WINDOW
KV_W = N_KV_HEADS * HEAD_DIM
N_QB = TM // QB
META_ROW0 = QB - N_META


def _rope(x, cos, sin_signed):
    q = lax.broadcasted_iota(jnp.int32, x.shape, 1) // (HEAD_DIM // 2)
    swapped = jnp.where(q % 2 == 0, pltpu.roll(x, LANES - HEAD_DIM // 2, 1),
                        pltpu.roll(x, HEAD_DIM // 2, 1))
    return x * cos + swapped * sin_signed


def _dup_heads(blk):
    lane = lax.broadcasted_iota(jnp.int32, blk.shape, 1)
    rolled = pltpu.roll(blk, HEAD_DIM, 1)
    return jnp.where(lane < HEAD_DIM, blk, rolled), jnp.where(lane < HEAD_DIM, rolled, blk)


def _kv_rows(xhat, g_kv_ref, w_kv_ref, cos, sin_signed):
    xk = (xhat * g_kv_ref[...]).astype(jnp.bfloat16)
    kv = jnp.dot(xk, w_kv_ref[...], preferred_element_type=jnp.float32)
    ks, vs = [], []
    for b in range(KV_W // LANES):
        kb = _rope(kv[:, LANES * b:LANES * (b + 1)], cos, sin_signed)
        vb = kv[:, KV_W + LANES * b:KV_W + LANES * (b + 1)]
        ks.extend(_dup_heads(kb))
        vs.extend(_dup_heads(vb))
    return [k.astype(jnp.bfloat16) for k in ks], [v.astype(jnp.bfloat16) for v in vs]


HALF0_ROWS0 = 2 * (HALF_SPLIT + N_META)
HALF1_ROWS0 = 2 * HALF_SPLIT


def _attn_kernel(pos1_ref, pos2_ref,
                 h_ref, w_ref, ys_hbm, cos_ref, sin_ref, g_attn_ref, g_kv_ref, w_q_ref, w_kv_ref, w_o_ref,
                 sink_ref, g_ffn_ref, wr_ref, br_ref,
                 h3_ref, xp_ref, code_ref, wts_ref, cnt_ref,
                 k_scr, v_scr, mk_scr, mv_scr, q_scr, o_scr, bias_scr, h_scr, run_scr,
                 ys_scr, y1_scr, y2_scr, ysem):
    s = pl.program_id(0)
    tiles_per_batch = N_XT // BATCH
    gather = functools.partial(_gather_pairs, pos1_ref, pos2_ref, ys_scr, y1_scr, y2_scr)

    @pl.when(s == 0)
    def _():
        run_scr[...] = jnp.zeros_like(run_scr)
        _load_half(ys_hbm, ys_scr, ysem, 0, HALF0_ROWS0, HALF1_ROWS0)
        gather(N_XT * TM, 0, N_META, True)
        h_meta = h_ref[0:N_META, :] + _weighted_sum(y1_scr, y2_scr, w_ref, N_META)
        gather(0, 0, TM, False)
        xhat = _rms_hat(h_meta)
        ks, vs = _kv_rows(xhat, g_kv_ref, w_kv_ref, cos_ref[0:N_META, :], sin_ref[0:N_META, :])
        mk_scr[...] = jnp.zeros_like(mk_scr)
        mv_scr[...] = jnp.zeros_like(mv_scr)
        for kvh in range(N_KV_HEADS):
            mk_scr[kvh, META_ROW0:QB, :] = ks[kvh]
            mv_scr[kvh, META_ROW0:QB, :] = vs[kvh]
        qi = lax.broadcasted_iota(jnp.int32, (QB, 2 * QB), 0)
        kj = lax.broadcasted_iota(jnp.int32, (QB, 2 * QB), 1)
        band = jnp.logical_and(kj > qi, kj <= qi + QB)
        bias_scr[0] = jnp.where(band, 0.0, NEG_INF)
        bias_scr[1] = jnp.where(jnp.logical_and(band, kj >= META_ROW0), 0.0, NEG_INF)

    @pl.when(s > 0)
    def _():
        batch_first = jnp.logical_or(s == 1, s == 1 + tiles_per_batch)

        @pl.when(batch_first)
        def _():
            k_scr[:, 0:QB, :] = mk_scr[...]
            v_scr[:, 0:QB, :] = mv_scr[...]

        tile = s - 1

        @pl.when(tile == TILES_PER_HALF)
        def _():
            _load_half(ys_hbm, ys_scr, ysem, 1, HALF0_ROWS0, HALF1_ROWS0)
            gather(TILES_PER_HALF * TM, HALF0_ROWS0, TM, False)

        h = h_ref[...] + _weighted_sum(y1_scr, y2_scr, w_ref, TM)
        h_scr[...] = h
        nxt = _next_x_tile(tile)
        gather(nxt * TM, jnp.where(nxt >= TILES_PER_HALF, HALF0_ROWS0, 0), TM, True)
        xhat = _rms_hat(h)
        ks, vs = _kv_rows(xhat, g_kv_ref, w_kv_ref, cos_ref[...], sin_ref[...])
        for kvh in range(N_KV_HEADS):
            k_scr[kvh, QB:QB + TM, :] = ks[kvh]
            v_scr[kvh, QB:QB + TM, :] = vs[kvh]

        xq = (xhat * g_attn_ref[...]).astype(jnp.bfloat16)
        q = jnp.dot(xq, w_q_ref[...], preferred_element_type=jnp.float32)
        for hb in range(N_HEADS // 2):
            q_scr[hb] = q[:, LANES * hb:LANES * (hb + 1)]
        first_bias = jnp.where(batch_first, 1, 0)

        def head_pair(hb, carry):
            kvh = hb // (N_HEADS // N_KV_HEADS // 2)
            lane = lax.broadcasted_iota(jnp.int32, (QB, LANES), 1)
            q_all = _rope(q_scr[hb], cos_ref[...], sin_ref[...]) * (HEAD_DIM ** -0.5)
            for b in range(N_QB):
                qb = q_all[QB * b:QB * (b + 1), :]
                qs = jnp.concatenate([jnp.where(lane < HEAD_DIM, qb, 0.0),
                                      jnp.where(lane < HEAD_DIM, 0.0, qb)], axis=0).astype(jnp.bfloat16)
                kk = k_scr[kvh, QB * b:QB * (b + 2), :]
                vv = v_scr[kvh, QB * b:QB * (b + 2), :]
                sc = lax.dot_general(qs, kk, (((1,), (1,)), ((), ())),
                                     preferred_element_type=jnp.float32)
                bias = bias_scr[first_bias] if b == 0 else bias_scr[0]
                outs = []
                for j in range(2):
                    sj = sc[QB * j:QB * (j + 1), :] + bias
                    sink = sink_ref[2 * hb + j]
                    m = jnp.maximum(jnp.max(sj, axis=-1, keepdims=True), sink)
                    p = jnp.exp(sj - m)
                    den = jnp.sum(p, axis=-1, keepdims=True) + jnp.exp(sink - m)
                    pv = jnp.dot(p.astype(jnp.bfloat16), vv, preferred_element_type=jnp.float32)
                    outs.append(pv * (1.0 / den))
                o_scr[hb, QB * b:QB * (b + 1), :] = jnp.where(
                    lane < HEAD_DIM, outs[0], outs[1]).astype(jnp.bfloat16)
            return carry

        lax.fori_loop(0, N_HEADS // 2, head_pair, 0)

        k_scr[:, 0:QB, :] = k_scr[:, TM:TM + QB, :]
        v_scr[:, 0:QB, :] = v_scr[:, TM:TM + QB, :]

        o = jnp.concatenate([o_scr[hb] for hb in range(N_HEADS // 2)], axis=1)
        h3 = h_scr[...] + jnp.dot(o, w_o_ref[...], preferred_element_type=jnp.float32)
        h3_ref[...] = h3
        _ffn_prologue(h3, g_ffn_ref, wr_ref, br_ref, run_scr, (s - 1) * TM, T_X,
                      xp_ref, code_ref, wts_ref, cnt_ref)


def _attn(pos1, pos2, h1, wcols, ys, cos_t, sin_t, g_attn, g_kv, w_q, w_kv, w_o, sinks, g_ffn, wr, br):
    def tile_x(s, p1, p2):
        return jnp.maximum(s - 1, 0)

    def tile_in(s, p1, p2):
        return (_tile_first_meta(s), 0)

    def rope_tile(s, p1, p2):
        per_batch = N_XT // BATCH
        return (jnp.where(s == 0, per_batch, (s - 1) % per_batch), 0)

    out_shape = [
        jax.ShapeDtypeStruct((T_X, D_MODEL), jnp.float32),
        jax.ShapeDtypeStruct((T_X * SLAB, LANES), jnp.uint32),
        jax.ShapeDtypeStruct((2, T_X), jnp.int32),
        jax.ShapeDtypeStruct((2, T_X), jnp.float32),
        jax.ShapeDtypeStruct((N_KEYS, LANES), jnp.int32),
    ]
    return pl.pallas_call(
        _attn_kernel,
        grid_spec=pltpu.PrefetchScalarGridSpec(
            num_scalar_prefetch=2,
            grid=(N_XT + 1,),
            in_specs=[
                pl.BlockSpec((TM, D_MODEL), tile_in),
                pl.BlockSpec((TM, 2), tile_in),
                pl.BlockSpec(memory_space=pl.ANY),
                pl.BlockSpec((TM, LANES), rope_tile),
                pl.BlockSpec((TM, LANES), rope_tile),
                _RESIDENT, _RESIDENT, _RESIDENT, _RESIDENT, _RESIDENT,
                pl.BlockSpec(memory_space=pltpu.SMEM),
                _RESIDENT, _RESIDENT, _RESIDENT,
            ],
            out_specs=[
                pl.BlockSpec((TM, D_MODEL), lambda s, p1, p2: (tile_x(s, p1, p2), 0)),
                pl.BlockSpec((TM * SLAB, LANES), lambda s, p1, p2: (tile_x(s, p1, p2), 0)),
                pl.BlockSpec((2, TM), lambda s, p1, p2: (0, tile_x(s, p1, p2))),
                pl.BlockSpec((2, TM), lambda s, p1, p2: (0, tile_x(s, p1, p2))),
                _RESIDENT,
            ],
            scratch_shapes=[
                pltpu.VMEM((N_KV_HEADS, QB + TM, LANES), jnp.bfloat16),
                pltpu.VMEM((N_KV_HEADS, QB + TM, LANES), jnp.bfloat16),
                pltpu.VMEM((N_KV_HEADS, QB, LANES), jnp.bfloat16),
                pltpu.VMEM((N_KV_HEADS, QB, LANES), jnp.bfloat16),
                pltpu.VMEM((N_HEADS // 2, TM, LANES), jnp.float32),
                pltpu.VMEM((N_HEADS // 2, TM, LANES), jnp.bfloat16),
                pltpu.VMEM((2, QB, 2 * QB), jnp.float32),
                pltpu.VMEM((TM, D_MODEL), jnp.float32),
                pltpu.VMEM((N_KEYS, 1), jnp.float32),
                pltpu.VMEM((HALF0_ROWS0 * SLAB, LANES), jnp.uint32),
                pltpu.VMEM((TM * SLAB, LANES), jnp.uint32),
                pltpu.VMEM((TM * SLAB, LANES), jnp.uint32),
                pltpu.SemaphoreType.DMA,
            ],
        ),
        out_shape=out_shape,
        compiler_params=pltpu.CompilerParams(
            dimension_semantics=("arbitrary",), vmem_limit_bytes=VMEM_LIMIT),
        name="attn_route",
    )(pos1, pos2, h1, wcols, ys, cos_t, sin_t, g_attn, g_kv, w_q, w_kv, w_o, sinks, g_ffn, wr, br)


def _router_rows(rg_w, rg_b, re_w, re_b):
    wr = jnp.zeros((N_KEYS, D_MODEL), jnp.float32)
    wr = wr.at[0:N_GROUPS].set(rg_w.T).at[8:8 + N_EXPERTS].set(re_w.T)
    br = jnp.zeros((N_KEYS, 1), jnp.float32)
    br = br.at[0:N_GROUPS, 0].set(rg_b).at[8:8 + N_EXPERTS, 0].set(re_b)
    return wr, br


def _rope_tables():
    half = HEAD_DIM // 2
    inv_freq = ROPE_THETA ** (-jnp.arange(half, dtype=jnp.float32) / half)
    pos = jnp.concatenate([N_META + jnp.arange(SEQ), jnp.arange(TM)]).astype(jnp.float32)
    ang = pos[:, None] * inv_freq[None, :]
    cos = jnp.cos(ang)
    sin = jnp.sin(ang)
    return jnp.tile(cos, (1, 4)), jnp.concatenate([-sin, sin, -sin, sin], axis=1)


def _moe(cnt, code, xp, wg, wu, wd, layer, t_pad, t_valid):
    pos, gstart, vtile, vgroup, vslot, vnext, nvis = _positions(cnt[:, 0], code, t_pad, t_valid)
    src = _invert(pos[0], pos[1], t_valid)
    ys = _ffn(vtile, vgroup, vslot, vnext, nvis, gstart, src, xp, wg, wu, wd, layer, t_pad, t_valid)
    return pos, ys


def kernel(x, meta_tokens, conv_norm_g, conv_w_in, conv_w, conv_w_out, kv_norm_g, w_kv, attn_norm_g,
           w_q, w_o, sinks, ffn_norm_g, router_group_w, router_group_b, router_expert_w,
           router_expert_b, w_gate, w_up, w_down, final_norm_g):
    bf = jnp.bfloat16
    x2d = x.reshape(T_X, D_MODEL)
    wr0, br0 = _router_rows(router_group_w[0], router_group_b[0], router_expert_w[0], router_expert_b[0])
    wr1, br1 = _router_rows(router_group_w[1], router_group_b[1], router_expert_w[1], router_expert_b[1])

    h1, xp0, code0, wts0, cnt0 = _mixer0(
        x2d, meta_tokens, conv_norm_g[0].reshape(1, D_MODEL), conv_w_in[0].astype(bf), conv_w[0],
        conv_w_out[0].astype(bf), ffn_norm_g[0].reshape(1, D_MODEL), wr0, br0)
    pos0, ys0 = _moe(cnt0, code0, xp0, w_gate, w_up, w_down, 0, T_PAD0, T_VALID0)

    cos_t, sin_t = _rope_tables()
    h3, xp1, code1, wts1, cnt1 = _attn(
        pos0[0], pos0[1], h1, wts0.T, ys0, cos_t, sin_t, attn_norm_g[0].reshape(1, D_MODEL),
        kv_norm_g.reshape(1, D_MODEL), w_q[0].astype(bf), w_kv.astype(bf), w_o[0].astype(bf), sinks[0],
        ffn_norm_g[1].reshape(1, D_MODEL), wr1, br1)
    pos1, ys1 = _moe(cnt1, code1, xp1, w_gate, w_up, w_down, 1, T_X, T_X)
    out = _final(pos1[0], pos1[1], h3, wts1.T, final_norm_g.reshape(1, D_MODEL), ys1)
    return out.reshape(BATCH, SEQ, D_MODEL)
```

```python
import functools

import jax
import jax.numpy as jnp
from jax import lax
from jax.experimental import pallas as pl
from jax.experimental.pallas import tpu as pltpu

D_MODEL = 1024
BATCH = 2
SEQ = 8192
N_META = 16
N_HEADS = 16
HEAD_DIM = 64
N_KV_HEADS = 4
WINDOW = 128
ROPE_THETA = 10000.0
N_GROUPS = 4
EXPERTS_PER_GROUP = 8
N_EXPERTS = N_GROUPS * EXPERTS_PER_GROUP
D_EXPERT = 256
NORM_EPS = 1e-5
NEG_INF = -1e30

TM = 512
N_XT = BATCH * SEQ // TM
T_X = BATCH * SEQ
T_PAD0 = (N_XT + 1) * TM
T_VALID0 = T_X + N_META
HALF_SPLIT = SEQ
N_KEYS = 2 * N_EXPERTS
TMG = 256
LANES = 128
SLAB = D_MODEL // 2 // LANES
VMEM_V7X = 64 * 1024 * 1024
VMEM_LIMIT = VMEM_V7X - 2 * 1024 * 1024


def _cdiv(a, b):
    return (a + b - 1) // b


def _rms_hat(x):
    return x * lax.rsqrt(jnp.mean(x * x, axis=-1, keepdims=True) + NORM_EPS)


def _pack_rows(xn):
    half = D_MODEL // 2
    return pltpu.pack_elementwise([xn[:, :half], xn[:, half:]], packed_dtype=jnp.bfloat16)


def _store_slabs(ref, words):
    m = words.shape[0]
    for k in range(SLAB):
        ref[pl.ds(k, m, stride=SLAB), :] = words[:, LANES * k:LANES * (k + 1)]


def _load_slabs(ref, m):
    return jnp.concatenate([ref[pl.ds(k, m, stride=SLAB), :] for k in range(SLAB)], axis=1)


def _unpack_words(words):
    lo = pltpu.unpack_elementwise(words, index=0, packed_dtype=jnp.bfloat16, unpacked_dtype=jnp.float32)
    hi = pltpu.unpack_elementwise(words, index=1, packed_dtype=jnp.bfloat16, unpacked_dtype=jnp.float32)
    return lo, hi


def _route(xn, wr_ref, br_ref, run_scr, tok_base, valid_limit):
    def split(a):
        hi = a.astype(jnp.bfloat16)
        return hi, (a - hi.astype(jnp.float32)).astype(jnp.bfloat16)

    def nt_dot(a, b):
        return lax.dot_general(a, b, (((1,), (1,)), ((), ())), preferred_element_type=jnp.float32)

    w_hi, w_lo = split(wr_ref[...])
    x_hi, x_lo = split(xn)
    logits = nt_dot(w_hi, x_hi) + nt_dot(w_hi, x_lo) + nt_dot(w_lo, x_hi) + br_ref[...]
    g = logits[0:N_GROUPS]
    gmax = jnp.max(g, axis=0, keepdims=True)
    rid_g = lax.broadcasted_iota(jnp.int32, g.shape, 0).astype(jnp.float32)
    g_idx = jnp.min(jnp.where(g == gmax, rid_g, float(N_GROUPS)), axis=0, keepdims=True).astype(jnp.int32)
    g_w = 1.0 / jnp.sum(jnp.exp(g - gmax), axis=0, keepdims=True)
    e_sel = logits[8:8 + EXPERTS_PER_GROUP]
    for gi in range(1, N_GROUPS):
        lo = 8 + EXPERTS_PER_GROUP * gi
        e_sel = jnp.where(g_idx == gi, logits[lo:lo + EXPERTS_PER_GROUP], e_sel)
    rid_e = lax.broadcasted_iota(jnp.int32, e_sel.shape, 0).astype(jnp.float32)
    none = float(EXPERTS_PER_GROUP)
    m1 = jnp.max(e_sel, axis=0, keepdims=True)
    i1f = jnp.min(jnp.where(e_sel == m1, rid_e, none), axis=0, keepdims=True)
    e_rest = jnp.where(rid_e == i1f, -jnp.inf, e_sel)
    m2 = jnp.max(e_rest, axis=0, keepdims=True)
    i2 = jnp.min(jnp.where(e_rest == m2, rid_e, none), axis=0, keepdims=True).astype(jnp.int32)
    i1 = i1f.astype(jnp.int32)
    ex = jnp.exp(m2 - m1)
    den = 1.0 / (1.0 + ex)
    w1 = den * g_w
    w2 = ex * den * g_w

    tok = tok_base + lax.broadcasted_iota(jnp.int32, (1, TM), 1)
    half = jnp.where(jnp.logical_and(tok >= HALF_SPLIT, tok < T_X), N_EXPERTS, 0)
    key1 = half + g_idx * EXPERTS_PER_GROUP + i1
    key2 = half + g_idx * EXPERTS_PER_GROUP + i2
    kid = lax.broadcasted_iota(jnp.int32, (N_KEYS, TM), 0)
    validf = jnp.where(tok < valid_limit, 1.0, 0.0)
    oh1 = jnp.where(kid == key1, validf, 0.0)
    oh2 = jnp.where(kid == key2, validf, 0.0)
    cnt = oh1 + oh2
    tri = jnp.where(lax.broadcasted_iota(jnp.int32, (TM, TM), 0)
                    <= lax.broadcasted_iota(jnp.int32, (TM, TM), 1), 1.0, 0.0).astype(jnp.bfloat16)
    cum = jnp.dot(cnt.astype(jnp.bfloat16), tri, preferred_element_type=jnp.float32)
    before = run_scr[...] + (cum - cnt)
    rank1 = jnp.sum(oh1 * before, axis=0, keepdims=True).astype(jnp.int32)
    rank2 = jnp.sum(oh2 * before, axis=0, keepdims=True).astype(jnp.int32)
    run_scr[...] = run_scr[...] + cum[:, TM - 1:TM]
    code = jnp.concatenate([key1 * 65536 + rank1, key2 * 65536 + rank2], axis=0)
    wts = jnp.concatenate([w1, w2], axis=0)
    return code, wts


def _ffn_prologue(h_new, g_ffn_ref, wr_ref, br_ref, run_scr, tok_base, valid_limit,
                  xp_ref, code_ref, wts_ref, cnt_ref):
    xn2 = _rms_hat(h_new) * g_ffn_ref[...]
    _store_slabs(xp_ref, _pack_rows(xn2))
    code, wts = _route(xn2, wr_ref, br_ref, run_scr, tok_base, valid_limit)
    code_ref[...] = code
    wts_ref[...] = wts
    cnt_ref[...] = jnp.broadcast_to(run_scr[...], cnt_ref.shape).astype(jnp.int32)


NC = 512


def _mixer0_kernel(x_ref, meta_ref, g_conv_ref, w_in_ref, cw_ref, w_out_ref,
                   g_ffn_ref, wr_ref, br_ref,
                   h1_ref, xp_ref, code_ref, wts_ref, cnt_ref,
                   h0_scr, acc_scr, carry_scr, meta_carry_scr, run_scr):
    s = pl.program_id(0)

    @pl.when(s == 0)
    def _():
        h0_scr[...] = jnp.zeros_like(h0_scr)
        h0_scr[0:N_META, :] = meta_ref[...]
        carry_scr[...] = jnp.zeros_like(carry_scr)
        run_scr[...] = jnp.zeros_like(run_scr)

    @pl.when(s > 0)
    def _():
        h0_scr[...] = x_ref[...]

    @pl.when(s == 1 + N_XT // BATCH)
    def _():
        carry_scr[...] = meta_carry_scr[...]

    h0 = h0_scr[...]
    xn = (_rms_hat(h0) * g_conv_ref[...]).astype(jnp.bfloat16)
    row = lax.broadcasted_iota(jnp.int32, (TM, NC), 0)
    for c in range(D_MODEL // NC):
        cols = slice(NC * c, NC * (c + 1))
        gate_c = jnp.dot(xn, w_in_ref[:, D_MODEL + NC * c:D_MODEL + NC * (c + 1)],
                         preferred_element_type=jnp.float32)
        val = jnp.dot(xn, w_in_ref[:, 2 * D_MODEL + NC * c:2 * D_MODEL + NC * (c + 1)],
                      preferred_element_type=jnp.float32)
        u = gate_c * val
        tail = carry_scr[:, cols]
        c1 = tail[7:8, :]
        c2 = tail[6:7, :]
        um1 = jnp.where(row == 0, c1, pltpu.roll(u, 1, 0))
        um2 = jnp.where(row == 0, c2, jnp.where(row == 1, c1, pltpu.roll(u, 2, 0)))
        conv = um2 * cw_ref[0:1, cols] + um1 * cw_ref[1:2, cols] + u * cw_ref[2:3, cols]

        @pl.when(s == 0)
        def _():
            carry_scr[:, cols] = u[N_META - 8:N_META, :]
            meta_carry_scr[:, cols] = u[N_META - 8:N_META, :]

        @pl.when(s > 0)
        def _():
            carry_scr[:, cols] = u[TM - 8:TM, :]

        gate_b = jnp.dot(xn, w_in_ref[:, cols], preferred_element_type=jnp.float32)
        gated = (gate_b * conv).astype(jnp.bfloat16)
        part = jnp.dot(gated, w_out_ref[cols, :], preferred_element_type=jnp.float32)
        if c == 0:
            acc_scr[...] = h0 + part
        else:
            acc_scr[...] = acc_scr[...] + part

    h1 = acc_scr[...]
    h1_ref[...] = h1
    tile = jnp.where(s == 0, N_XT, s - 1)
    _ffn_prologue(h1, g_ffn_ref, wr_ref, br_ref, run_scr, tile * TM, T_VALID0,
                  xp_ref, code_ref, wts_ref, cnt_ref)


def _tile_first_meta(s):
    return jnp.where(s == 0, N_XT, s - 1)


_RESIDENT = pl.BlockSpec(memory_space=pltpu.VMEM)


def _mixer0(x2d, meta, g_conv, w_in, cw, w_out, g_ffn, wr, br):
    out_shape = [
        jax.ShapeDtypeStruct((T_PAD0, D_MODEL), jnp.float32),
        jax.ShapeDtypeStruct((T_PAD0 * SLAB, LANES), jnp.uint32),
        jax.ShapeDtypeStruct((2, T_PAD0), jnp.int32),
        jax.ShapeDtypeStruct((2, T_PAD0), jnp.float32),
        jax.ShapeDtypeStruct((N_KEYS, LANES), jnp.int32),
    ]
    return pl.pallas_call(
        _mixer0_kernel,
        grid=(N_XT + 1,),
        in_specs=[pl.BlockSpec((TM, D_MODEL), lambda s: (jnp.maximum(s - 1, 0), 0))] + [_RESIDENT] * 8,
        out_specs=[
            pl.BlockSpec((TM, D_MODEL), lambda s: (_tile_first_meta(s), 0)),
            pl.BlockSpec((TM * SLAB, LANES), lambda s: (_tile_first_meta(s), 0)),
            pl.BlockSpec((2, TM), lambda s: (0, _tile_first_meta(s))),
            pl.BlockSpec((2, TM), lambda s: (0, _tile_first_meta(s))),
            _RESIDENT,
        ],
        out_shape=out_shape,
        scratch_shapes=[
            pltpu.VMEM((TM, D_MODEL), jnp.float32),
            pltpu.VMEM((TM, D_MODEL), jnp.float32),
            pltpu.VMEM((8, D_MODEL), jnp.float32),
            pltpu.VMEM((8, D_MODEL), jnp.float32),
            pltpu.VMEM((N_KEYS, 1), jnp.float32),
        ],
        compiler_params=pltpu.CompilerParams(
            dimension_semantics=("arbitrary",), vmem_limit_bytes=VMEM_LIMIT),
        name="mixer0_route",
    )(x2d, meta, g_conv, w_in, cw, w_out, g_ffn, wr, br)


def _n_tiles(t_valid):
    return _cdiv(2 * t_valid, TMG)


def _n_visits(t_valid):
    return _n_tiles(t_valid) + N_KEYS


def _sorted_rows(t_valid):
    return (_n_tiles(t_valid) + 1) * TMG


def _positions_kernel(t_pad, t_valid, cnt_ref, code_ref, pos_ref, gstart_ref, vrow_ref, vgroup_ref,
                      vslot_ref, vnext_ref, nvis_ref, nextg_scr):
    n_vis = _n_visits(t_valid)

    def offs(g, acc):
        gstart_ref[g] = acc
        return acc + cnt_ref[g]

    total = lax.fori_loop(0, N_KEYS, offs, jnp.int32(0))
    gstart_ref[N_KEYS] = total

    def next_nonempty(i, nxt):
        g = N_KEYS - 1 - i
        nextg_scr[g] = nxt
        return jnp.where(cnt_ref[g] > 0, g, nxt)

    lax.fori_loop(0, N_KEYS, next_nonempty, jnp.int32(-1))

    def per_group(g, carry):
        v, last_g, rank = carry
        c = cnt_ref[g]
        start = gstart_ref[g]

        def per_window(k, vv):
            vrow_ref[vv] = start + k * TMG
            vgroup_ref[vv] = g
            vslot_ref[vv] = rank % 2
            vnext_ref[vv] = nextg_scr[g]
            return vv + 1

        v = lax.fori_loop(0, (c + TMG - 1) // TMG, per_window, v)
        return v, jnp.where(c > 0, g, last_g), jnp.where(c > 0, rank + 1, rank)

    nvis, last_g, _ = lax.fori_loop(0, N_KEYS, per_group, (jnp.int32(0), jnp.int32(0), jnp.int32(0)))
    nvis_ref[0] = nvis

    def pad(vv, c):
        vrow_ref[vv] = 0
        vgroup_ref[vv] = last_g
        vslot_ref[vv] = 0
        vnext_ref[vv] = -1
        return c

    lax.fori_loop(nvis, n_vis, pad, 0)

    code = code_ref[...]
    key = code >> 16
    pos = code & 0xFFFF
    for g in range(N_KEYS):
        pos = pos + jnp.where(key == g, gstart_ref[g], 0)
    pos_ref[...] = pos


def _positions(cnt, code, t_pad, t_valid):
    n_vis = _n_visits(t_valid)
    smem = pl.BlockSpec(memory_space=pltpu.SMEM)
    return pl.pallas_call(
        functools.partial(_positions_kernel, t_pad, t_valid),
        in_specs=[smem, pl.BlockSpec(memory_space=pltpu.VMEM)],
        out_specs=[pl.BlockSpec(memory_space=pltpu.VMEM), smem, smem, smem, smem, smem, smem],
        out_shape=[
            jax.ShapeDtypeStruct((2, t_pad), jnp.int32),
            jax.ShapeDtypeStruct((N_KEYS + 1,), jnp.int32),
            jax.ShapeDtypeStruct((n_vis,), jnp.int32),
            jax.ShapeDtypeStruct((n_vis,), jnp.int32),
            jax.ShapeDtypeStruct((n_vis,), jnp.int32),
            jax.ShapeDtypeStruct((n_vis,), jnp.int32),
            jax.ShapeDtypeStruct((1,), jnp.int32),
        ],
        scratch_shapes=[pltpu.SMEM((N_KEYS,), jnp.int32)],
        name="sort_positions",
    )(cnt, code)


SRC_UNROLL = 8


def _invert_kernel(t_valid, r_pad, pos1_ref, pos2_ref, src_ref):
    def tail(r, c):
        src_ref[r] = 0
        return c

    lax.fori_loop(2 * t_valid, r_pad, tail, 0)

    def body(i, c):
        for j in range(SRC_UNROLL):
            t = i * SRC_UNROLL + j
            src_ref[pos1_ref[t]] = t
            src_ref[pos2_ref[t]] = t
        return c

    lax.fori_loop(0, t_valid // SRC_UNROLL, body, 0)


def _invert(pos1, pos2, t_valid):
    r_pad = _sorted_rows(t_valid)
    smem = pl.BlockSpec(memory_space=pltpu.SMEM)
    return pl.pallas_call(
        functools.partial(_invert_kernel, t_valid, r_pad),
        in_specs=[smem, smem],
        out_specs=smem,
        out_shape=jax.ShapeDtypeStruct((r_pad,), jnp.int32),
        name="sort_invert",
    )(pos1, pos2)


GATHER_UNROLL = 8


def _ffn_kernel(layer, vrow_ref, vgroup_ref, vslot_ref, vnext_ref, nvis_ref, src_ref,
                xp_hbm, wg_hbm, wu_hbm, wd_hbm, ys_hbm,
                xp_scr, wg_scr, wu_scr, wd_scr, xnext_scr, out_scr, xsem, wsem, osem):
    v = pl.program_id(0)
    nvis = nvis_ref[0]
    active = v < nvis
    g = vgroup_ref[v]
    slot = vslot_ref[v]
    group_first = jnp.logical_or(v == 0, g != vgroup_ref[jnp.maximum(v - 1, 0)])

    def weight_copies(group, sl):
        e = group % N_EXPERTS
        return (pltpu.make_async_copy(wg_hbm.at[layer, e], wg_scr.at[sl], wsem.at[sl, 0]),
                pltpu.make_async_copy(wu_hbm.at[layer, e], wu_scr.at[sl], wsem.at[sl, 1]),
                pltpu.make_async_copy(wd_hbm.at[layer, e], wd_scr.at[sl], wsem.at[sl, 2]))

    def out_copy(visit):
        sl = visit % 2
        return pltpu.make_async_copy(
            out_scr.at[sl], ys_hbm.at[pl.ds(pl.multiple_of(vrow_ref[visit] * SLAB, SLAB), TMG * SLAB), :],
            osem.at[sl])

    def gather_rows(visit, unrolled):
        base = vrow_ref[visit]

        def one(r):
            tok = src_ref[base + r]
            xnext_scr[pl.ds(pl.multiple_of(r * SLAB, SLAB), SLAB), :] = (
                xp_scr[pl.ds(pl.multiple_of(tok * SLAB, SLAB), SLAB), :])

        if unrolled:
            for r in range(TMG):
                one(r)
        else:
            def chunk(i, c):
                for j in range(GATHER_UNROLL):
                    one(i * GATHER_UNROLL + j)
                return c

            lax.fori_loop(0, TMG // GATHER_UNROLL, chunk, 0)

    @pl.when(v == 0)
    def _():
        cp = pltpu.make_async_copy(xp_hbm, xp_scr, xsem)
        cp.start()
        for c in weight_copies(g, slot):
            c.start()
        out_scr[...] = jnp.zeros_like(out_scr)
        cap = ys_hbm.shape[0]
        for sl in range(2):
            fill = pltpu.make_async_copy(
                out_scr.at[sl], ys_hbm.at[pl.ds(cap - (2 - sl) * TMG * SLAB, TMG * SLAB), :], osem.at[sl])
            fill.start()
            fill.wait()
        cp.wait()
        gather_rows(0, False)

    @pl.when(jnp.logical_and(active, group_first))
    def _():
        for c in weight_copies(g, slot):
            c.wait()
        nxt = vnext_ref[v]

        @pl.when(nxt >= 0)
        def _():
            for c in weight_copies(nxt, 1 - slot):
                c.start()

    @pl.when(active)
    def _():
        lo, hi = _unpack_words(_load_slabs(xnext_scr, TMG))
        xs = jnp.concatenate([lo, hi], axis=1).astype(jnp.bfloat16)
        gather_rows(jnp.minimum(v + 1, nvis - 1), True)
        hg = jnp.dot(xs, wg_scr[slot].astype(jnp.bfloat16), preferred_element_type=jnp.float32)
        hu = jnp.dot(xs, wu_scr[slot].astype(jnp.bfloat16), preferred_element_type=jnp.float32)
        hdn = (hg * jax.nn.sigmoid(hg) * hu).astype(jnp.bfloat16)
        y = jnp.dot(hdn, wd_scr[slot].astype(jnp.bfloat16), preferred_element_type=jnp.float32)
        _store_slabs(out_scr.at[v % 2], _pack_rows(y))

        @pl.when(v > 0)
        def _():
            out_copy(v - 1).wait()

        out_copy(v).start()

    @pl.when(v == nvis - 1)
    def _():
        out_copy(v).wait()


def _ffn(vrow, vgroup, vslot, vnext, nvis, src, xp, wg, wu, wd, layer, t_pad, t_valid):
    any_spec = pl.BlockSpec(memory_space=pl.ANY)
    return pl.pallas_call(
        functools.partial(_ffn_kernel, layer),
        grid_spec=pltpu.PrefetchScalarGridSpec(
            num_scalar_prefetch=6,
            grid=(_n_visits(t_valid),),
            in_specs=[any_spec, any_spec, any_spec, any_spec],
            out_specs=any_spec,
            scratch_shapes=[
                pltpu.VMEM((t_pad * SLAB, LANES), jnp.uint32),
                pltpu.VMEM((2, D_MODEL, D_EXPERT), jnp.float32),
                pltpu.VMEM((2, D_MODEL, D_EXPERT), jnp.float32),
                pltpu.VMEM((2, D_EXPERT, D_MODEL), jnp.float32),
                pltpu.VMEM((TMG * SLAB, LANES), jnp.uint32),
                pltpu.VMEM((2, TMG * SLAB, LANES), jnp.uint32),
                pltpu.SemaphoreType.DMA,
                pltpu.SemaphoreType.DMA((2, 3)),
                pltpu.SemaphoreType.DMA((2,)),
            ],
        ),
        out_shape=jax.ShapeDtypeStruct((_sorted_rows(t_valid) * SLAB, LANES), jnp.uint32),
        compiler_params=pltpu.CompilerParams(
            dimension_semantics=("arbitrary",), vmem_limit_bytes=VMEM_LIMIT),
        name="expert_ffn",
    )(vrow, vgroup, vslot, vnext, nvis, src, xp, wg, wu, wd)


TILES_PER_HALF = HALF_SPLIT // TM


def _load_half(ys_hbm, ys_scr, sem, half, half0_rows, half1_rows):
    start, rows = (0, half0_rows) if half == 0 else (half0_rows, half1_rows)
    cp = pltpu.make_async_copy(ys_hbm.at[pl.ds(start * SLAB, rows * SLAB), :],
                               ys_scr.at[pl.ds(0, rows * SLAB), :], sem)
    cp.start()
    cp.wait()


def _gather_pairs(pos1_ref, pos2_ref, ys_scr, y1_scr, y2_scr, tok0, base, n, unrolled):
    def one(r):
        p1 = pos1_ref[tok0 + r] - base
        p2 = pos2_ref[tok0 + r] - base
        dst = pl.ds(pl.multiple_of(r * SLAB, SLAB), SLAB)
        y1_scr[dst, :] = ys_scr[pl.ds(pl.multiple_of(p1 * SLAB, SLAB), SLAB), :]
        y2_scr[dst, :] = ys_scr[pl.ds(pl.multiple_of(p2 * SLAB, SLAB), SLAB), :]

    if unrolled:
        for r in range(n):
            one(r)
    else:
        def chunk(i, c):
            for j in range(GATHER_UNROLL):
                one(i * GATHER_UNROLL + j)
            return c

        lax.fori_loop(0, n // GATHER_UNROLL, chunk, 0)


def _weighted_sum(y1_scr, y2_scr, w_ref, m):
    lo1, hi1 = _unpack_words(_load_slabs(y1_scr, m))
    lo2, hi2 = _unpack_words(_load_slabs(y2_scr, m))
    w1 = w_ref[0:m, 0:1]
    w2 = w_ref[0:m, 1:2]
    return jnp.concatenate([w1 * lo1 + w2 * lo2, w1 * hi1 + w2 * hi2], axis=1)


def _next_x_tile(tile):
    nxt = tile + 1
    return jnp.where(jnp.logical_or(nxt == TILES_PER_HALF, nxt == N_XT), tile, nxt)


def _final_kernel(half_rows, pos1_ref, pos2_ref, h_ref, w_ref, g_ref, ys_hbm,
                  o_ref, ys_scr, y1_scr, y2_scr, sem):
    tile = pl.program_id(0)

    @pl.when(tile == 0)
    def _():
        _load_half(ys_hbm, ys_scr, sem, 0, half_rows, half_rows)
        _gather_pairs(pos1_ref, pos2_ref, ys_scr, y1_scr, y2_scr, 0, 0, TM, False)

    @pl.when(tile == TILES_PER_HALF)
    def _():
        _load_half(ys_hbm, ys_scr, sem, 1, half_rows, half_rows)
        _gather_pairs(pos1_ref, pos2_ref, ys_scr, y1_scr, y2_scr, TILES_PER_HALF * TM, half_rows, TM, False)

    h = h_ref[...] + _weighted_sum(y1_scr, y2_scr, w_ref, TM)
    nxt = _next_x_tile(tile)
    _gather_pairs(pos1_ref, pos2_ref, ys_scr, y1_scr, y2_scr, nxt * TM,
                  jnp.where(nxt >= TILES_PER_HALF, half_rows, 0), TM, True)
    o_ref[...] = _rms_hat(h) * g_ref[...]


def _final(pos1, pos2, h, wcols, g, ys):
    half_rows = 2 * HALF_SPLIT
    return pl.pallas_call(
        functools.partial(_final_kernel, half_rows),
        grid_spec=pltpu.PrefetchScalarGridSpec(
            num_scalar_prefetch=2,
            grid=(N_XT,),
            in_specs=[
                pl.BlockSpec((TM, D_MODEL), lambda i, p1, p2: (i, 0)),
                pl.BlockSpec((TM, 2), lambda i, p1, p2: (i, 0)),
                _RESIDENT,
                pl.BlockSpec(memory_space=pl.ANY),
            ],
            out_specs=pl.BlockSpec((TM, D_MODEL), lambda i, p1, p2: (i, 0)),
            scratch_shapes=[
                pltpu.VMEM((half_rows * SLAB, LANES), jnp.uint32),
                pltpu.VMEM((TM * SLAB, LANES), jnp.uint32),
                pltpu.VMEM((TM * SLAB, LANES), jnp.uint32),
                pltpu.SemaphoreType.DMA,
            ],
        ),
        out_shape=jax.ShapeDtypeStruct((T_X, D_MODEL), jnp.float32),
        compiler_params=pltpu.CompilerParams(
            dimension_semantics=("arbitrary",), vmem_limit_bytes=VMEM_LIMIT),
        name="moe_combine_final",
    )(pos1, pos2, h, wcols, g, ys)


QB = WINDOW
KV_W = N_KV_HEADS * HEAD_DIM
N_QB = TM // QB
META_ROW0 = QB - N_META


def _rope(x, cos, sin_signed):
    q = lax.broadcasted_iota(jnp.int32, x.shape, 1) // (HEAD_DIM // 2)
    swapped = jnp.where(q % 2 == 0, pltpu.roll(x, LANES - HEAD_DIM // 2, 1),
                        pltpu.roll(x, HEAD_DIM // 2, 1))
    return x * cos + swapped * sin_signed


def _dup_heads(blk):
    lane = lax.broadcasted_iota(jnp.int32, blk.shape, 1)
    rolled = pltpu.roll(blk, HEAD_DIM, 1)
    return jnp.where(lane < HEAD_DIM, blk, rolled), jnp.where(lane < HEAD_DIM, rolled, blk)


def _kv_rows(xhat, g_kv_ref, w_kv_ref, cos, sin_signed):
    xk = (xhat * g_kv_ref[...]).astype(jnp.bfloat16)
    kv = jnp.dot(xk, w_kv_ref[...], preferred_element_type=jnp.float32)
    ks, vs = [], []
    for b in range(KV_W // LANES):
        kb = _rope(kv[:, LANES * b:LANES * (b + 1)], cos, sin_signed)
        vb = kv[:, KV_W + LANES * b:KV_W + LANES * (b + 1)]
        ks.extend(_dup_heads(kb))
        vs.extend(_dup_heads(vb))
    return [k.astype(jnp.bfloat16) for k in ks], [v.astype(jnp.bfloat16) for v in vs]


HALF0_ROWS0 = 2 * (HALF_SPLIT + N_META)
HALF1_ROWS0 = 2 * HALF_SPLIT


def _attn_kernel(pos1_ref, pos2_ref,
                 h_ref, w_ref, ys_hbm, cos_ref, sin_ref, g_attn_ref, g_kv_ref, w_q_ref, w_kv_ref, w_o_ref,
                 sink_ref, g_ffn_ref, wr_ref, br_ref,
                 h3_ref, xp_ref, code_ref, wts_ref, cnt_ref,
                 k_scr, v_scr, mk_scr, mv_scr, q_scr, o_scr, bias_scr, h_scr, run_scr,
                 ys_scr, y1_scr, y2_scr, ysem):
    s = pl.program_id(0)
    tiles_per_batch = N_XT // BATCH
    gather = functools.partial(_gather_pairs, pos1_ref, pos2_ref, ys_scr, y1_scr, y2_scr)

    @pl.when(s == 0)
    def _():
        run_scr[...] = jnp.zeros_like(run_scr)
        _load_half(ys_hbm, ys_scr, ysem, 0, HALF0_ROWS0, HALF1_ROWS0)
        gather(N_XT * TM, 0, N_META, True)
        h_meta = h_ref[0:N_META, :] + _weighted_sum(y1_scr, y2_scr, w_ref, N_META)
        gather(0, 0, TM, False)
        xhat = _rms_hat(h_meta)
        ks, vs = _kv_rows(xhat, g_kv_ref, w_kv_ref, cos_ref[0:N_META, :], sin_ref[0:N_META, :])
        mk_scr[...] = jnp.zeros_like(mk_scr)
        mv_scr[...] = jnp.zeros_like(mv_scr)
        for kvh in range(N_KV_HEADS):
            mk_scr[kvh, META_ROW0:QB, :] = ks[kvh]
            mv_scr[kvh, META_ROW0:QB, :] = vs[kvh]
        qi = lax.broadcasted_iota(jnp.int32, (QB, 2 * QB), 0)
        kj = lax.broadcasted_iota(jnp.int32, (QB, 2 * QB), 1)
        band = jnp.logical_and(kj > qi, kj <= qi + QB)
        bias_scr[0] = jnp.where(band, 0.0, NEG_INF)
        bias_scr[1] = jnp.where(jnp.logical_and(band, kj >= META_ROW0), 0.0, NEG_INF)

    @pl.when(s > 0)
    def _():
        batch_first = jnp.logical_or(s == 1, s == 1 + tiles_per_batch)

        @pl.when(batch_first)
        def _():
            k_scr[:, 0:QB, :] = mk_scr[...]
            v_scr[:, 0:QB, :] = mv_scr[...]

        tile = s - 1

        @pl.when(tile == TILES_PER_HALF)
        def _():
            _load_half(ys_hbm, ys_scr, ysem, 1, HALF0_ROWS0, HALF1_ROWS0)
            gather(TILES_PER_HALF * TM, HALF0_ROWS0, TM, False)

        h = h_ref[...] + _weighted_sum(y1_scr, y2_scr, w_ref, TM)
        h_scr[...] = h
        nxt = _next_x_tile(tile)
        gather(nxt * TM, jnp.where(nxt >= TILES_PER_HALF, HALF0_ROWS0, 0), TM, True)
        xhat = _rms_hat(h)
        ks, vs = _kv_rows(xhat, g_kv_ref, w_kv_ref, cos_ref[...], sin_ref[...])
        for kvh in range(N_KV_HEADS):
            k_scr[kvh, QB:QB + TM, :] = ks[kvh]
            v_scr[kvh, QB:QB + TM, :] = vs[kvh]

        xq = (xhat * g_attn_ref[...]).astype(jnp.bfloat16)
        q = jnp.dot(xq, w_q_ref[...], preferred_element_type=jnp.float32)
        for hb in range(N_HEADS // 2):
            q_scr[hb] = q[:, LANES * hb:LANES * (hb + 1)]
        first_bias = jnp.where(batch_first, 1, 0)

        def head_pair(hb, carry):
            kvh = hb // (N_HEADS // N_KV_HEADS // 2)
            lane = lax.broadcasted_iota(jnp.int32, (QB, LANES), 1)
            q_all = _rope(q_scr[hb], cos_ref[...], sin_ref[...]) * (HEAD_DIM ** -0.5)
            for b in range(N_QB):
                qb = q_all[QB * b:QB * (b + 1), :]
                qs = jnp.concatenate([jnp.where(lane < HEAD_DIM, qb, 0.0),
                                      jnp.where(lane < HEAD_DIM, 0.0, qb)], axis=0).astype(jnp.bfloat16)
                kk = k_scr[kvh, QB * b:QB * (b + 2), :]
                vv = v_scr[kvh, QB * b:QB * (b + 2), :]
                sc = lax.dot_general(qs, kk, (((1,), (1,)), ((), ())),
                                     preferred_element_type=jnp.float32)
                bias = bias_scr[first_bias] if b == 0 else bias_scr[0]
                outs = []
                for j in range(2):
                    sj = sc[QB * j:QB * (j + 1), :] + bias
                    sink = sink_ref[2 * hb + j]
                    m = jnp.maximum(jnp.max(sj, axis=-1, keepdims=True), sink)
                    p = jnp.exp(sj - m)
                    den = jnp.sum(p, axis=-1, keepdims=True) + jnp.exp(sink - m)
                    pv = jnp.dot(p.astype(jnp.bfloat16), vv, preferred_element_type=jnp.float32)
                    outs.append(pv * (1.0 / den))
                o_scr[hb, QB * b:QB * (b + 1), :] = jnp.where(
                    lane < HEAD_DIM, outs[0], outs[1]).astype(jnp.bfloat16)
            return carry

        lax.fori_loop(0, N_HEADS // 2, head_pair, 0)

        k_scr[:, 0:QB, :] = k_scr[:, TM:TM + QB, :]
        v_scr[:, 0:QB, :] = v_scr[:, TM:TM + QB, :]

        o = jnp.concatenate([o_scr[hb] for hb in range(N_HEADS // 2)], axis=1)
        h3 = h_scr[...] + jnp.dot(o, w_o_ref[...], preferred_element_type=jnp.float32)
        h3_ref[...] = h3
        _ffn_prologue(h3, g_ffn_ref, wr_ref, br_ref, run_scr, (s - 1) * TM, T_X,
                      xp_ref, code_ref, wts_ref, cnt_ref)


def _attn(pos1, pos2, h1, wcols, ys, cos_t, sin_t, g_attn, g_kv, w_q, w_kv, w_o, sinks, g_ffn, wr, br):
    def tile_x(s, p1, p2):
        return jnp.maximum(s - 1, 0)

    def tile_in(s, p1, p2):
        return (_tile_first_meta(s), 0)

    def rope_tile(s, p1, p2):
        per_batch = N_XT // BATCH
        return (jnp.where(s == 0, per_batch, (s - 1) % per_batch), 0)

    out_shape = [
        jax.ShapeDtypeStruct((T_X, D_MODEL), jnp.float32),
        jax.ShapeDtypeStruct((T_X * SLAB, LANES), jnp.uint32),
        jax.ShapeDtypeStruct((2, T_X), jnp.int32),
        jax.ShapeDtypeStruct((2, T_X), jnp.float32),
        jax.ShapeDtypeStruct((N_KEYS, LANES), jnp.int32),
    ]
    return pl.pallas_call(
        _attn_kernel,
        grid_spec=pltpu.PrefetchScalarGridSpec(
            num_scalar_prefetch=2,
            grid=(N_XT + 1,),
            in_specs=[
                pl.BlockSpec((TM, D_MODEL), tile_in),
                pl.BlockSpec((TM, 2), tile_in),
                pl.BlockSpec(memory_space=pl.ANY),
                pl.BlockSpec((TM, LANES), rope_tile),
                pl.BlockSpec((TM, LANES), rope_tile),
                _RESIDENT, _RESIDENT, _RESIDENT, _RESIDENT, _RESIDENT,
                pl.BlockSpec(memory_space=pltpu.SMEM),
                _RESIDENT, _RESIDENT, _RESIDENT,
            ],
            out_specs=[
                pl.BlockSpec((TM, D_MODEL), lambda s, p1, p2: (tile_x(s, p1, p2), 0)),
                pl.BlockSpec((TM * SLAB, LANES), lambda s, p1, p2: (tile_x(s, p1, p2), 0)),
                pl.BlockSpec((2, TM), lambda s, p1, p2: (0, tile_x(s, p1, p2))),
                pl.BlockSpec((2, TM), lambda s, p1, p2: (0, tile_x(s, p1, p2))),
                _RESIDENT,
            ],
            scratch_shapes=[
                pltpu.VMEM((N_KV_HEADS, QB + TM, LANES), jnp.bfloat16),
                pltpu.VMEM((N_KV_HEADS, QB + TM, LANES), jnp.bfloat16),
                pltpu.VMEM((N_KV_HEADS, QB, LANES), jnp.bfloat16),
                pltpu.VMEM((N_KV_HEADS, QB, LANES), jnp.bfloat16),
                pltpu.VMEM((N_HEADS // 2, TM, LANES), jnp.float32),
                pltpu.VMEM((N_HEADS // 2, TM, LANES), jnp.bfloat16),
                pltpu.VMEM((2, QB, 2 * QB), jnp.float32),
                pltpu.VMEM((TM, D_MODEL), jnp.float32),
                pltpu.VMEM((N_KEYS, 1), jnp.float32),
                pltpu.VMEM((HALF0_ROWS0 * SLAB, LANES), jnp.uint32),
                pltpu.VMEM((TM * SLAB, LANES), jnp.uint32),
                pltpu.VMEM((TM * SLAB, LANES), jnp.uint32),
                pltpu.SemaphoreType.DMA,
            ],
        ),
        out_shape=out_shape,
        compiler_params=pltpu.CompilerParams(
            dimension_semantics=("arbitrary",), vmem_limit_bytes=VMEM_LIMIT),
        name="attn_route",
    )(pos1, pos2, h1, wcols, ys, cos_t, sin_t, g_attn, g_kv, w_q, w_kv, w_o, sinks, g_ffn, wr, br)


def _router_rows(rg_w, rg_b, re_w, re_b):
    wr = jnp.zeros((N_KEYS, D_MODEL), jnp.float32)
    wr = wr.at[0:N_GROUPS].set(rg_w.T).at[8:8 + N_EXPERTS].set(re_w.T)
    br = jnp.zeros((N_KEYS, 1), jnp.float32)
    br = br.at[0:N_GROUPS, 0].set(rg_b).at[8:8 + N_EXPERTS, 0].set(re_b)
    return wr, br


def _rope_tables():
    half = HEAD_DIM // 2
    inv_freq = ROPE_THETA ** (-jnp.arange(half, dtype=jnp.float32) / half)
    pos = jnp.concatenate([N_META + jnp.arange(SEQ), jnp.arange(TM)]).astype(jnp.float32)
    ang = pos[:, None] * inv_freq[None, :]
    cos = jnp.cos(ang)
    sin = jnp.sin(ang)
    return jnp.tile(cos, (1, 4)), jnp.concatenate([-sin, sin, -sin, sin], axis=1)


def _moe(cnt, code, xp, wg, wu, wd, layer, t_pad, t_valid):
    pos, _, vrow, vgroup, vslot, vnext, nvis = _positions(cnt[:, 0], code, t_pad, t_valid)
    src = _invert(pos[0], pos[1], t_valid)
    ys = _ffn(vrow, vgroup, vslot, vnext, nvis, src, xp, wg, wu, wd, layer, t_pad, t_valid)
    return pos, ys


def kernel(x, meta_tokens, conv_norm_g, conv_w_in, conv_w, conv_w_out, kv_norm_g, w_kv, attn_norm_g,
           w_q, w_o, sinks, ffn_norm_g, router_group_w, router_group_b, router_expert_w,
           router_expert_b, w_gate, w_up, w_down, final_norm_g):
    bf = jnp.bfloat16
    x2d = x.reshape(T_X, D_MODEL)
    wr0, br0 = _router_rows(router_group_w[0], router_group_b[0], router_expert_w[0], router_expert_b[0])
    wr1, br1 = _router_rows(router_group_w[1], router_group_b[1], router_expert_w[1], router_expert_b[1])

    h1, xp0, code0, wts0, cnt0 = _mixer0(
        x2d, meta_tokens, conv_norm_g[0].reshape(1, D_MODEL), conv_w_in[0].astype(bf), conv_w[0],
        conv_w_out[0].astype(bf), ffn_norm_g[0].reshape(1, D_MODEL), wr0, br0)
    pos0, ys0 = _moe(cnt0, code0, xp0, w_gate, w_up, w_down, 0, T_PAD0, T_VALID0)

    cos_t, sin_t = _rope_tables()
    h3, xp1, code1, wts1, cnt1 = _attn(
        pos0[0], pos0[1], h1, wts0.T, ys0, cos_t, sin_t, attn_norm_g[0].reshape(1, D_MODEL),
        kv_norm_g.reshape(1, D_MODEL), w_q[0].astype(bf), w_kv.astype(bf), w_o[0].astype(bf), sinks[0],
        ffn_norm_g[1].reshape(1, D_MODEL), wr1, br1)
    pos1, ys1 = _moe(cnt1, code1, xp1, w_gate, w_up, w_down, 1, T_X, T_X)
    out = _final(pos1[0], pos1[1], h3, wts1.T, final_norm_g.reshape(1, D_MODEL), ys1)
    return out.reshape(BATCH, SEQ, D_MODEL)
```

```python
import functools

import jax
import jax.numpy as jnp
from jax import lax
from jax.experimental import pallas as pl
from jax.experimental.pallas import tpu as pltpu

D_MODEL = 1024
BATCH = 2
SEQ = 8192
N_META = 16
N_HEADS = 16
HEAD_DIM = 64
N_KV_HEADS = 4
WINDOW = 128
ROPE_THETA = 10000.0
N_GROUPS = 4
EXPERTS_PER_GROUP = 8
N_EXPERTS = N_GROUPS * EXPERTS_PER_GROUP
D_EXPERT = 256
NORM_EPS = 1e-5
NEG_INF = -1e30

TM = 512
N_XT = BATCH * SEQ // TM
T_X = BATCH * SEQ
T_PAD0 = (N_XT + 1) * TM
T_VALID0 = T_X + N_META
HALF_SPLIT = SEQ
N_KEYS = 2 * N_EXPERTS
TMG = 256
LANES = 128
SLAB = D_MODEL // 2 // LANES
VMEM_V7X = 64 * 1024 * 1024
VMEM_LIMIT = VMEM_V7X - 2 * 1024 * 1024


def _cdiv(a, b):
    return (a + b - 1) // b


def _rms_hat(x):
    return x * lax.rsqrt(jnp.mean(x * x, axis=-1, keepdims=True) + NORM_EPS)


def _pack_rows(xn):
    half = D_MODEL // 2
    return pltpu.pack_elementwise([xn[:, :half], xn[:, half:]], packed_dtype=jnp.bfloat16)


def _store_slabs(ref, words):
    m = words.shape[0]
    for k in range(SLAB):
        ref[pl.ds(k, m, stride=SLAB), :] = words[:, LANES * k:LANES * (k + 1)]


def _load_slabs(ref, m):
    return jnp.concatenate([ref[pl.ds(k, m, stride=SLAB), :] for k in range(SLAB)], axis=1)


def _unpack_words(words):
    lo = pltpu.unpack_elementwise(words, index=0, packed_dtype=jnp.bfloat16, unpacked_dtype=jnp.float32)
    hi = pltpu.unpack_elementwise(words, index=1, packed_dtype=jnp.bfloat16, unpacked_dtype=jnp.float32)
    return lo, hi


def _route(xn, wr_ref, br_ref, run_scr, tok_base, valid_limit):
    def split(a):
        hi = a.astype(jnp.bfloat16)
        return hi, (a - hi.astype(jnp.float32)).astype(jnp.bfloat16)

    def nt_dot(a, b):
        return lax.dot_general(a, b, (((1,), (1,)), ((), ())), preferred_element_type=jnp.float32)

    w_hi, w_lo = split(wr_ref[...])
    x_hi, x_lo = split(xn)
    logits = nt_dot(w_hi, x_hi) + nt_dot(w_hi, x_lo) + nt_dot(w_lo, x_hi) + br_ref[...]
    g = logits[0:N_GROUPS]
    gmax = jnp.max(g, axis=0, keepdims=True)
    rid_g = lax.broadcasted_iota(jnp.int32, g.shape, 0).astype(jnp.float32)
    g_idx = jnp.min(jnp.where(g == gmax, rid_g, float(N_GROUPS)), axis=0, keepdims=True).astype(jnp.int32)
    g_w = 1.0 / jnp.sum(jnp.exp(g - gmax), axis=0, keepdims=True)
    e_sel = logits[8:8 + EXPERTS_PER_GROUP]
    for gi in range(1, N_GROUPS):
        lo = 8 + EXPERTS_PER_GROUP * gi
        e_sel = jnp.where(g_idx == gi, logits[lo:lo + EXPERTS_PER_GROUP], e_sel)
    rid_e = lax.broadcasted_iota(jnp.int32, e_sel.shape, 0).astype(jnp.float32)
    none = float(EXPERTS_PER_GROUP)
    m1 = jnp.max(e_sel, axis=0, keepdims=True)
    i1f = jnp.min(jnp.where(e_sel == m1, rid_e, none), axis=0, keepdims=True)
    e_rest = jnp.where(rid_e == i1f, -jnp.inf, e_sel)
    m2 = jnp.max(e_rest, axis=0, keepdims=True)
    i2 = jnp.min(jnp.where(e_rest == m2, rid_e, none), axis=0, keepdims=True).astype(jnp.int32)
    i1 = i1f.astype(jnp.int32)
    ex = jnp.exp(m2 - m1)
    den = 1.0 / (1.0 + ex)
    w1 = den * g_w
    w2 = ex * den * g_w

    tok = tok_base + lax.broadcasted_iota(jnp.int32, (1, TM), 1)
    half = jnp.where(jnp.logical_and(tok >= HALF_SPLIT, tok < T_X), N_EXPERTS, 0)
    key1 = half + g_idx * EXPERTS_PER_GROUP + i1
    key2 = half + g_idx * EXPERTS_PER_GROUP + i2
    kid = lax.broadcasted_iota(jnp.int32, (N_KEYS, TM), 0)
    validf = jnp.where(tok < valid_limit, 1.0, 0.0)
    oh1 = jnp.where(kid == key1, validf, 0.0)
    oh2 = jnp.where(kid == key2, validf, 0.0)
    cnt = oh1 + oh2
    tri = jnp.where(lax.broadcasted_iota(jnp.int32, (TM, TM), 0)
                    <= lax.broadcasted_iota(jnp.int32, (TM, TM), 1), 1.0, 0.0).astype(jnp.bfloat16)
    cum = jnp.dot(cnt.astype(jnp.bfloat16), tri, preferred_element_type=jnp.float32)
    before = run_scr[...] + (cum - cnt)
    rank1 = jnp.sum(oh1 * before, axis=0, keepdims=True).astype(jnp.int32)
    rank2 = jnp.sum(oh2 * before, axis=0, keepdims=True).astype(jnp.int32)
    run_scr[...] = run_scr[...] + cum[:, TM - 1:TM]
    code = jnp.concatenate([key1 * 65536 + rank1, key2 * 65536 + rank2], axis=0)
    wts = jnp.concatenate([w1, w2], axis=0)
    return code, wts


def _ffn_prologue(h_new, g_ffn_ref, wr_ref, br_ref, run_scr, tok_base, valid_limit,
                  xp_ref, code_ref, wts_ref, cnt_ref):
    xn2 = _rms_hat(h_new) * g_ffn_ref[...]
    _store_slabs(xp_ref, _pack_rows(xn2))
    code, wts = _route(xn2, wr_ref, br_ref, run_scr, tok_base, valid_limit)
    code_ref[...] = code
    wts_ref[...] = wts
    cnt_ref[...] = jnp.broadcast_to(run_scr[...], cnt_ref.shape).astype(jnp.int32)


NC = 512


def _mixer0_kernel(x_ref, meta_ref, g_conv_ref, w_in_ref, cw_ref, w_out_ref,
                   g_ffn_ref, wr_ref, br_ref,
                   h1_ref, xp_ref, code_ref, wts_ref, cnt_ref,
                   h0_scr, acc_scr, carry_scr, meta_carry_scr, run_scr):
    s = pl.program_id(0)

    @pl.when(s == 0)
    def _():
        h0_scr[...] = jnp.zeros_like(h0_scr)
        h0_scr[0:N_META, :] = meta_ref[...]
        carry_scr[...] = jnp.zeros_like(carry_scr)
        run_scr[...] = jnp.zeros_like(run_scr)

    @pl.when(s > 0)
    def _():
        h0_scr[...] = x_ref[...]

    @pl.when(s == 1 + N_XT // BATCH)
    def _():
        carry_scr[...] = meta_carry_scr[...]

    h0 = h0_scr[...]
    xn = (_rms_hat(h0) * g_conv_ref[...]).astype(jnp.bfloat16)
    row = lax.broadcasted_iota(jnp.int32, (TM, NC), 0)
    for c in range(D_MODEL // NC):
        cols = slice(NC * c, NC * (c + 1))
        gate_c = jnp.dot(xn, w_in_ref[:, D_MODEL + NC * c:D_MODEL + NC * (c + 1)],
                         preferred_element_type=jnp.float32)
        val = jnp.dot(xn, w_in_ref[:, 2 * D_MODEL + NC * c:2 * D_MODEL + NC * (c + 1)],
                      preferred_element_type=jnp.float32)
        u = gate_c * val
        tail = carry_scr[:, cols]
        c1 = tail[7:8, :]
        c2 = tail[6:7, :]
        um1 = jnp.where(row == 0, c1, pltpu.roll(u, 1, 0))
        um2 = jnp.where(row == 0, c2, jnp.where(row == 1, c1, pltpu.roll(u, 2, 0)))
        conv = um2 * cw_ref[0:1, cols] + um1 * cw_ref[1:2, cols] + u * cw_ref[2:3, cols]

        @pl.when(s == 0)
        def _():
            carry_scr[:, cols] = u[N_META - 8:N_META, :]
            meta_carry_scr[:, cols] = u[N_META - 8:N_META, :]

        @pl.when(s > 0)
        def _():
            carry_scr[:, cols] = u[TM - 8:TM, :]

        gate_b = jnp.dot(xn, w_in_ref[:, cols], preferred_element_type=jnp.float32)
        gated = (gate_b * conv).astype(jnp.bfloat16)
        part = jnp.dot(gated, w_out_ref[cols, :], preferred_element_type=jnp.float32)
        if c == 0:
            acc_scr[...] = h0 + part
        else:
            acc_scr[...] = acc_scr[...] + part

    h1 = acc_scr[...]
    h1_ref[...] = h1
    tile = jnp.where(s == 0, N_XT, s - 1)
    _ffn_prologue(h1, g_ffn_ref, wr_ref, br_ref, run_scr, tile * TM, T_VALID0,
                  xp_ref, code_ref, wts_ref, cnt_ref)


def _tile_first_meta(s):
    return jnp.where(s == 0, N_XT, s - 1)


_RESIDENT = pl.BlockSpec(memory_space=pltpu.VMEM)


def _mixer0(x2d, meta, g_conv, w_in, cw, w_out, g_ffn, wr, br):
    out_shape = [
        jax.ShapeDtypeStruct((T_PAD0, D_MODEL), jnp.float32),
        jax.ShapeDtypeStruct((T_PAD0 * SLAB, LANES), jnp.uint32),
        jax.ShapeDtypeStruct((2, T_PAD0), jnp.int32),
        jax.ShapeDtypeStruct((2, T_PAD0), jnp.float32),
        jax.ShapeDtypeStruct((N_KEYS, LANES), jnp.int32),
    ]
    return pl.pallas_call(
        _mixer0_kernel,
        grid=(N_XT + 1,),
        in_specs=[pl.BlockSpec((TM, D_MODEL), lambda s: (jnp.maximum(s - 1, 0), 0))] + [_RESIDENT] * 8,
        out_specs=[
            pl.BlockSpec((TM, D_MODEL), lambda s: (_tile_first_meta(s), 0)),
            pl.BlockSpec((TM * SLAB, LANES), lambda s: (_tile_first_meta(s), 0)),
            pl.BlockSpec((2, TM), lambda s: (0, _tile_first_meta(s))),
            pl.BlockSpec((2, TM), lambda s: (0, _tile_first_meta(s))),
            _RESIDENT,
        ],
        out_shape=out_shape,
        scratch_shapes=[
            pltpu.VMEM((TM, D_MODEL), jnp.float32),
            pltpu.VMEM((TM, D_MODEL), jnp.float32),
            pltpu.VMEM((8, D_MODEL), jnp.float32),
            pltpu.VMEM((8, D_MODEL), jnp.float32),
            pltpu.VMEM((N_KEYS, 1), jnp.float32),
        ],
        compiler_params=pltpu.CompilerParams(
            dimension_semantics=("arbitrary",), vmem_limit_bytes=VMEM_LIMIT),
        name="mixer0_route",
    )(x2d, meta, g_conv, w_in, cw, w_out, g_ffn, wr, br)


def _n_tiles(t_valid):
    return _cdiv(2 * t_valid, TMG)


def _n_visits(t_valid):
    return _n_tiles(t_valid) + N_KEYS


def _half0_rows(t_valid):
    return 2 * (HALF_SPLIT + t_valid - T_X)


def _half1_base(half0_rows):
    return half0_rows + TMG


def _sorted_rows(t_valid):
    return (_n_tiles(t_valid) + 2) * TMG


def _positions_kernel(t_pad, t_valid, cnt_ref, code_ref, pos_ref, gstart_ref, vrow_ref, vgroup_ref,
                      vslot_ref, vnext_ref, nvis_ref, nexte_scr):
    n_vis = _n_visits(t_valid)

    def offs(g, acc):
        acc = jnp.where(g == N_EXPERTS, _half1_base(_half0_rows(t_valid)), acc)
        gstart_ref[g] = acc
        return acc + cnt_ref[g]

    total = lax.fori_loop(0, N_KEYS, offs, jnp.int32(0))
    gstart_ref[N_KEYS] = total

    def rows_of(e):
        return cnt_ref[e] + cnt_ref[e + N_EXPERTS]

    def next_nonempty(i, nxt):
        e = N_EXPERTS - 1 - i
        nexte_scr[e] = nxt
        return jnp.where(rows_of(e) > 0, e, nxt)

    lax.fori_loop(0, N_EXPERTS, next_nonempty, jnp.int32(-1))

    def per_expert(e, carry):
        v, last_g, rank = carry
        for half in range(2):
            g = e + half * N_EXPERTS
            c = cnt_ref[g]
            start = gstart_ref[g]

            def per_window(k, vv, g=g, start=start):
                vrow_ref[vv] = start + k * TMG
                vgroup_ref[vv] = g
                vslot_ref[vv] = rank % 2
                vnext_ref[vv] = nexte_scr[e]
                return vv + 1

            v = lax.fori_loop(0, (c + TMG - 1) // TMG, per_window, v)
            last_g = jnp.where(c > 0, g, last_g)
        return v, last_g, jnp.where(rows_of(e) > 0, rank + 1, rank)

    nvis, last_g, _ = lax.fori_loop(0, N_EXPERTS, per_expert, (jnp.int32(0), jnp.int32(0), jnp.int32(0)))
    nvis_ref[0] = nvis

    def pad(vv, c):
        vrow_ref[vv] = 0
        vgroup_ref[vv] = last_g
        vslot_ref[vv] = 0
        vnext_ref[vv] = -1
        return c

    lax.fori_loop(nvis, n_vis, pad, 0)

    code = code_ref[...]
    key = code >> 16
    pos = code & 0xFFFF
    for g in range(N_KEYS):
        pos = pos + jnp.where(key == g, gstart_ref[g], 0)
    pos_ref[...] = pos


def _positions(cnt, code, t_pad, t_valid):
    n_vis = _n_visits(t_valid)
    smem = pl.BlockSpec(memory_space=pltpu.SMEM)
    return pl.pallas_call(
        functools.partial(_positions_kernel, t_pad, t_valid),
        in_specs=[smem, pl.BlockSpec(memory_space=pltpu.VMEM)],
        out_specs=[pl.BlockSpec(memory_space=pltpu.VMEM), smem, smem, smem, smem, smem, smem],
        out_shape=[
            jax.ShapeDtypeStruct((2, t_pad), jnp.int32),
            jax.ShapeDtypeStruct((N_KEYS + 1,), jnp.int32),
            jax.ShapeDtypeStruct((n_vis,), jnp.int32),
            jax.ShapeDtypeStruct((n_vis,), jnp.int32),
            jax.ShapeDtypeStruct((n_vis,), jnp.int32),
            jax.ShapeDtypeStruct((n_vis,), jnp.int32),
            jax.ShapeDtypeStruct((1,), jnp.int32),
        ],
        scratch_shapes=[pltpu.SMEM((N_EXPERTS,), jnp.int32)],
        name="sort_positions",
    )(cnt, code)


SRC_UNROLL = 8


def _invert_kernel(t_valid, r_pad, pos1_ref, pos2_ref, src_ref):
    def slack(r, c):
        src_ref[r] = 0
        return c

    half0_rows = _half0_rows(t_valid)
    lax.fori_loop(half0_rows, _half1_base(half0_rows), slack, 0)
    lax.fori_loop(2 * t_valid + TMG, r_pad, slack, 0)

    def body(i, c):
        for j in range(SRC_UNROLL):
            t = i * SRC_UNROLL + j
            src_ref[pos1_ref[t]] = t
            src_ref[pos2_ref[t]] = t
        return c

    lax.fori_loop(0, t_valid // SRC_UNROLL, body, 0)


def _invert(pos1, pos2, t_valid):
    r_pad = _sorted_rows(t_valid)
    smem = pl.BlockSpec(memory_space=pltpu.SMEM)
    return pl.pallas_call(
        functools.partial(_invert_kernel, t_valid, r_pad),
        in_specs=[smem, smem],
        out_specs=smem,
        out_shape=jax.ShapeDtypeStruct((r_pad,), jnp.int32),
        name="sort_invert",
    )(pos1, pos2)


GATHER_UNROLL = 8


def _ffn_kernel(layer, half0_rows, vrow_ref, vgroup_ref, vslot_ref, vnext_ref, nvis_ref, src_ref,
                xp_hbm, wg_hbm, wu_hbm, wd_hbm, ys_hbm,
                xp_scr, wg_scr, wu_scr, wd_scr, xnext_scr, out_scr, xsem, wsem, osem):
    v = pl.program_id(0)
    nvis = nvis_ref[0]
    active = v < nvis
    e = vgroup_ref[v] % N_EXPERTS
    slot = vslot_ref[v]
    expert_first = jnp.logical_or(v == 0, e != vgroup_ref[jnp.maximum(v - 1, 0)] % N_EXPERTS)

    def weight_copies(e, sl):
        return (pltpu.make_async_copy(wg_hbm.at[layer, e], wg_scr.at[sl], wsem.at[sl, 0]),
                pltpu.make_async_copy(wu_hbm.at[layer, e], wu_scr.at[sl], wsem.at[sl, 1]),
                pltpu.make_async_copy(wd_hbm.at[layer, e], wd_scr.at[sl], wsem.at[sl, 2]))

    def out_copy(visit):
        sl = visit % 2
        return pltpu.make_async_copy(
            out_scr.at[sl], ys_hbm.at[pl.ds(pl.multiple_of(vrow_ref[visit] * SLAB, SLAB), TMG * SLAB), :],
            osem.at[sl])

    def gather_rows(visit, unrolled):
        base = vrow_ref[visit]

        def one(r):
            tok = src_ref[base + r]
            xnext_scr[pl.ds(pl.multiple_of(r * SLAB, SLAB), SLAB), :] = (
                xp_scr[pl.ds(pl.multiple_of(tok * SLAB, SLAB), SLAB), :])

        if unrolled:
            for r in range(TMG):
                one(r)
        else:
            def chunk(i, c):
                for j in range(GATHER_UNROLL):
                    one(i * GATHER_UNROLL + j)
                return c

            lax.fori_loop(0, TMG // GATHER_UNROLL, chunk, 0)

    @pl.when(v == 0)
    def _():
        cp = pltpu.make_async_copy(xp_hbm, xp_scr, xsem)
        cp.start()
        for c in weight_copies(e, slot):
            c.start()
        out_scr[...] = jnp.zeros_like(out_scr)
        cap = ys_hbm.shape[0] // SLAB
        for first_row in (half0_rows, cap - 2 * TMG, cap - TMG):
            fill = pltpu.make_async_copy(
                out_scr.at[0], ys_hbm.at[pl.ds(first_row * SLAB, TMG * SLAB), :], osem.at[0])
            fill.start()
            fill.wait()
        cp.wait()
        gather_rows(0, False)

    @pl.when(jnp.logical_and(active, expert_first))
    def _():
        for c in weight_copies(e, slot):
            c.wait()
        nxt = vnext_ref[v]

        @pl.when(nxt >= 0)
        def _():
            for c in weight_copies(nxt, 1 - slot):
                c.start()

    @pl.when(active)
    def _():
        lo, hi = _unpack_words(_load_slabs(xnext_scr, TMG))
        xs = jnp.concatenate([lo, hi], axis=1).astype(jnp.bfloat16)
        gather_rows(jnp.minimum(v + 1, nvis - 1), True)
        hg = jnp.dot(xs, wg_scr[slot].astype(jnp.bfloat16), preferred_element_type=jnp.float32)
        hu = jnp.dot(xs, wu_scr[slot].astype(jnp.bfloat16), preferred_element_type=jnp.float32)
        hdn = (hg * jax.nn.sigmoid(hg) * hu).astype(jnp.bfloat16)
        y = jnp.dot(hdn, wd_scr[slot].astype(jnp.bfloat16), preferred_element_type=jnp.float32)
        _store_slabs(out_scr.at[v % 2], _pack_rows(y))

        @pl.when(v > 0)
        def _():
            out_copy(v - 1).wait()

        out_copy(v).start()

    @pl.when(v == nvis - 1)
    def _():
        out_copy(v).wait()


def _ffn(vrow, vgroup, vslot, vnext, nvis, src, xp, wg, wu, wd, layer, t_pad, t_valid):
    any_spec = pl.BlockSpec(memory_space=pl.ANY)
    return pl.pallas_call(
        functools.partial(_ffn_kernel, layer, _half0_rows(t_valid)),
        grid_spec=pltpu.PrefetchScalarGridSpec(
            num_scalar_prefetch=6,
            grid=(_n_visits(t_valid),),
            in_specs=[any_spec, any_spec, any_spec, any_spec],
            out_specs=any_spec,
            scratch_shapes=[
                pltpu.VMEM((t_pad * SLAB, LANES), jnp.uint32),
                pltpu.VMEM((2, D_MODEL, D_EXPERT), jnp.float32),
                pltpu.VMEM((2, D_MODEL, D_EXPERT), jnp.float32),
                pltpu.VMEM((2, D_EXPERT, D_MODEL), jnp.float32),
                pltpu.VMEM((TMG * SLAB, LANES), jnp.uint32),
                pltpu.VMEM((2, TMG * SLAB, LANES), jnp.uint32),
                pltpu.SemaphoreType.DMA,
                pltpu.SemaphoreType.DMA((2, 3)),
                pltpu.SemaphoreType.DMA((2,)),
            ],
        ),
        out_shape=jax.ShapeDtypeStruct((_sorted_rows(t_valid) * SLAB, LANES), jnp.uint32),
        compiler_params=pltpu.CompilerParams(
            dimension_semantics=("arbitrary",), vmem_limit_bytes=VMEM_LIMIT),
        name="expert_ffn",
    )(vrow, vgroup, vslot, vnext, nvis, src, xp, wg, wu, wd)


TILES_PER_HALF = HALF_SPLIT // TM


def _load_half(ys_hbm, ys_scr, sem, half, half0_rows, half1_rows):
    start, rows = (0, half0_rows) if half == 0 else (_half1_base(half0_rows), half1_rows)
    cp = pltpu.make_async_copy(ys_hbm.at[pl.ds(start * SLAB, rows * SLAB), :],
                               ys_scr.at[pl.ds(0, rows * SLAB), :], sem)
    cp.start()
    cp.wait()


def _gather_pairs(pos1_ref, pos2_ref, ys_scr, y1_scr, y2_scr, tok0, base, n, unrolled):
    def one(r):
        p1 = pos1_ref[tok0 + r] - base
        p2 = pos2_ref[tok0 + r] - base
        dst = pl.ds(pl.multiple_of(r * SLAB, SLAB), SLAB)
        y1_scr[dst, :] = ys_scr[pl.ds(pl.multiple_of(p1 * SLAB, SLAB), SLAB), :]
        y2_scr[dst, :] = ys_scr[pl.ds(pl.multiple_of(p2 * SLAB, SLAB), SLAB), :]

    if unrolled:
        for r in range(n):
            one(r)
    else:
        def chunk(i, c):
            for j in range(GATHER_UNROLL):
                one(i * GATHER_UNROLL + j)
            return c

        lax.fori_loop(0, n // GATHER_UNROLL, chunk, 0)


def _weighted_sum(y1_scr, y2_scr, w_ref, m):
    lo1, hi1 = _unpack_words(_load_slabs(y1_scr, m))
    lo2, hi2 = _unpack_words(_load_slabs(y2_scr, m))
    w1 = w_ref[0:m, 0:1]
    w2 = w_ref[0:m, 1:2]
    return jnp.concatenate([w1 * lo1 + w2 * lo2, w1 * hi1 + w2 * hi2], axis=1)


def _next_x_tile(tile):
    nxt = tile + 1
    return jnp.where(jnp.logical_or(nxt == TILES_PER_HALF, nxt == N_XT), tile, nxt)


def _final_kernel(half_rows, pos1_ref, pos2_ref, h_ref, w_ref, g_ref, ys_hbm,
                  o_ref, ys_scr, y1_scr, y2_scr, sem):
    tile = pl.program_id(0)
    base1 = _half1_base(half_rows)

    @pl.when(tile == 0)
    def _():
        _load_half(ys_hbm, ys_scr, sem, 0, half_rows, half_rows)
        _gather_pairs(pos1_ref, pos2_ref, ys_scr, y1_scr, y2_scr, 0, 0, TM, False)

    @pl.when(tile == TILES_PER_HALF)
    def _():
        _load_half(ys_hbm, ys_scr, sem, 1, half_rows, half_rows)
        _gather_pairs(pos1_ref, pos2_ref, ys_scr, y1_scr, y2_scr, TILES_PER_HALF * TM, base1, TM, False)

    h = h_ref[...] + _weighted_sum(y1_scr, y2_scr, w_ref, TM)
    nxt = _next_x_tile(tile)
    _gather_pairs(pos1_ref, pos2_ref, ys_scr, y1_scr, y2_scr, nxt * TM,
                  jnp.where(nxt >= TILES_PER_HALF, base1, 0), TM, True)
    o_ref[...] = _rms_hat(h) * g_ref[...]


def _final(pos1, pos2, h, wcols, g, ys):
    half_rows = 2 * HALF_SPLIT
    return pl.pallas_call(
        functools.partial(_final_kernel, half_rows),
        grid_spec=pltpu.PrefetchScalarGridSpec(
            num_scalar_prefetch=2,
            grid=(N_XT,),
            in_specs=[
                pl.BlockSpec((TM, D_MODEL), lambda i, p1, p2: (i, 0)),
                pl.BlockSpec((TM, 2), lambda i, p1, p2: (i, 0)),
                _RESIDENT,
                pl.BlockSpec(memory_space=pl.ANY),
            ],
            out_specs=pl.BlockSpec((TM, D_MODEL), lambda i, p1, p2: (i, 0)),
            scratch_shapes=[
                pltpu.VMEM((half_rows * SLAB, LANES), jnp.uint32),
                pltpu.VMEM((TM * SLAB, LANES), jnp.uint32),
                pltpu.VMEM((TM * SLAB, LANES), jnp.uint32),
                pltpu.SemaphoreType.DMA,
            ],
        ),
        out_shape=jax.ShapeDtypeStruct((T_X, D_MODEL), jnp.float32),
        compiler_params=pltpu.CompilerParams(
            dimension_semantics=("arbitrary",), vmem_limit_bytes=VMEM_LIMIT),
        name="moe_combine_final",
    )(pos1, pos2, h, wcols, g, ys)


QB = WINDOW
KV_W = N_KV_HEADS * HEAD_DIM
N_QB = TM // QB
META_ROW0 = QB - N_META


def _rope(x, cos, sin_signed):
    q = lax.broadcasted_iota(jnp.int32, x.shape, 1) // (HEAD_DIM // 2)
    swapped = jnp.where(q % 2 == 0, pltpu.roll(x, LANES - HEAD_DIM // 2, 1),
                        pltpu.roll(x, HEAD_DIM // 2, 1))
    return x * cos + swapped * sin_signed


def _dup_heads(blk):
    lane = lax.broadcasted_iota(jnp.int32, blk.shape, 1)
    rolled = pltpu.roll(blk, HEAD_DIM, 1)
    return jnp.where(lane < HEAD_DIM, blk, rolled), jnp.where(lane < HEAD_DIM, rolled, blk)


def _kv_rows(xhat, g_kv_ref, w_kv_ref, cos, sin_signed):
    xk = (xhat * g_kv_ref[...]).astype(jnp.bfloat16)
    kv = jnp.dot(xk, w_kv_ref[...], preferred_element_type=jnp.float32)
    ks, vs = [], []
    for b in range(KV_W // LANES):
        kb = _rope(kv[:, LANES * b:LANES * (b + 1)], cos, sin_signed)
        vb = kv[:, KV_W + LANES * b:KV_W + LANES * (b + 1)]
        ks.extend(_dup_heads(kb))
        vs.extend(_dup_heads(vb))
    return [k.astype(jnp.bfloat16) for k in ks], [v.astype(jnp.bfloat16) for v in vs]


HALF0_ROWS0 = 2 * (HALF_SPLIT + N_META)
HALF1_ROWS0 = 2 * HALF_SPLIT


def _attn_kernel(pos1_ref, pos2_ref,
                 h_ref, w_ref, ys_hbm, cos_ref, sin_ref, g_attn_ref, g_kv_ref, w_q_ref, w_kv_ref, w_o_ref,
                 sink_ref, g_ffn_ref, wr_ref, br_ref,
                 h3_ref, xp_ref, code_ref, wts_ref, cnt_ref,
                 k_scr, v_scr, mk_scr, mv_scr, q_scr, o_scr, bias_scr, h_scr, run_scr,
                 ys_scr, y1_scr, y2_scr, ysem):
    s = pl.program_id(0)
    tiles_per_batch = N_XT // BATCH
    gather = functools.partial(_gather_pairs, pos1_ref, pos2_ref, ys_scr, y1_scr, y2_scr)
    base1 = _half1_base(HALF0_ROWS0)

    @pl.when(s == 0)
    def _():
        run_scr[...] = jnp.zeros_like(run_scr)
        _load_half(ys_hbm, ys_scr, ysem, 0, HALF0_ROWS0, HALF1_ROWS0)
        gather(N_XT * TM, 0, N_META, True)
        h_meta = h_ref[0:N_META, :] + _weighted_sum(y1_scr, y2_scr, w_ref, N_META)
        gather(0, 0, TM, False)
        xhat = _rms_hat(h_meta)
        ks, vs = _kv_rows(xhat, g_kv_ref, w_kv_ref, cos_ref[0:N_META, :], sin_ref[0:N_META, :])
        mk_scr[...] = jnp.zeros_like(mk_scr)
        mv_scr[...] = jnp.zeros_like(mv_scr)
        for kvh in range(N_KV_HEADS):
            mk_scr[kvh, META_ROW0:QB, :] = ks[kvh]
            mv_scr[kvh, META_ROW0:QB, :] = vs[kvh]
        qi = lax.broadcasted_iota(jnp.int32, (QB, 2 * QB), 0)
        kj = lax.broadcasted_iota(jnp.int32, (QB, 2 * QB), 1)
        band = jnp.logical_and(kj > qi, kj <= qi + QB)
        bias_scr[0] = jnp.where(band, 0.0, NEG_INF)
        bias_scr[1] = jnp.where(jnp.logical_and(band, kj >= META_ROW0), 0.0, NEG_INF)

    @pl.when(s > 0)
    def _():
        batch_first = jnp.logical_or(s == 1, s == 1 + tiles_per_batch)

        @pl.when(batch_first)
        def _():
            k_scr[:, 0:QB, :] = mk_scr[...]
            v_scr[:, 0:QB, :] = mv_scr[...]

        tile = s - 1

        @pl.when(tile == TILES_PER_HALF)
        def _():
            _load_half(ys_hbm, ys_scr, ysem, 1, HALF0_ROWS0, HALF1_ROWS0)
            gather(TILES_PER_HALF * TM, base1, TM, False)

        h = h_ref[...] + _weighted_sum(y1_scr, y2_scr, w_ref, TM)
        h_scr[...] = h
        nxt = _next_x_tile(tile)
        gather(nxt * TM, jnp.where(nxt >= TILES_PER_HALF, base1, 0), TM, True)
        xhat = _rms_hat(h)
        ks, vs = _kv_rows(xhat, g_kv_ref, w_kv_ref, cos_ref[...], sin_ref[...])
        for kvh in range(N_KV_HEADS):
            k_scr[kvh, QB:QB + TM, :] = ks[kvh]
            v_scr[kvh, QB:QB + TM, :] = vs[kvh]

        xq = (xhat * g_attn_ref[...]).astype(jnp.bfloat16)
        q = jnp.dot(xq, w_q_ref[...], preferred_element_type=jnp.float32)
        for hb in range(N_HEADS // 2):
            q_scr[hb] = q[:, LANES * hb:LANES * (hb + 1)]
        first_bias = jnp.where(batch_first, 1, 0)

        def head_pair(hb, carry):
            kvh = hb // (N_HEADS // N_KV_HEADS // 2)
            lane = lax.broadcasted_iota(jnp.int32, (QB, LANES), 1)
            q_all = _rope(q_scr[hb], cos_ref[...], sin_ref[...]) * (HEAD_DIM ** -0.5)
            for b in range(N_QB):
                qb = q_all[QB * b:QB * (b + 1), :]
                qs = jnp.concatenate([jnp.where(lane < HEAD_DIM, qb, 0.0),
                                      jnp.where(lane < HEAD_DIM, 0.0, qb)], axis=0).astype(jnp.bfloat16)
                kk = k_scr[kvh, QB * b:QB * (b + 2), :]
                vv = v_scr[kvh, QB * b:QB * (b + 2), :]
                sc = lax.dot_general(qs, kk, (((1,), (1,)), ((), ())),
                                     preferred_element_type=jnp.float32)
                bias = bias_scr[first_bias] if b == 0 else bias_scr[0]
                outs = []
                for j in range(2):
                    sj = sc[QB * j:QB * (j + 1), :] + bias
                    sink = sink_ref[2 * hb + j]
                    m = jnp.maximum(jnp.max(sj, axis=-1, keepdims=True), sink)
                    p = jnp.exp(sj - m)
                    den = jnp.sum(p, axis=-1, keepdims=True) + jnp.exp(sink - m)
                    pv = jnp.dot(p.astype(jnp.bfloat16), vv, preferred_element_type=jnp.float32)
                    outs.append(pv * (1.0 / den))
                o_scr[hb, QB * b:QB * (b + 1), :] = jnp.where(
                    lane < HEAD_DIM, outs[0], outs[1]).astype(jnp.bfloat16)
            return carry

        lax.fori_loop(0, N_HEADS // 2, head_pair, 0)

        k_scr[:, 0:QB, :] = k_scr[:, TM:TM + QB, :]
        v_scr[:, 0:QB, :] = v_scr[:, TM:TM + QB, :]

        o = jnp.concatenate([o_scr[hb] for hb in range(N_HEADS // 2)], axis=1)
        h3 = h_scr[...] + jnp.dot(o, w_o_ref[...], preferred_element_type=jnp.float32)
        h3_ref[...] = h3
        _ffn_prologue(h3, g_ffn_ref, wr_ref, br_ref, run_scr, (s - 1) * TM, T_X,
                      xp_ref, code_ref, wts_ref, cnt_ref)


def _attn(pos1, pos2, h1, wcols, ys, cos_t, sin_t, g_attn, g_kv, w_q, w_kv, w_o, sinks, g_ffn, wr, br):
    def tile_x(s, p1, p2):
        return jnp.maximum(s - 1, 0)

    def tile_in(s, p1, p2):
        return (_tile_first_meta(s), 0)

    def rope_tile(s, p1, p2):
        per_batch = N_XT // BATCH
        return (jnp.where(s == 0, per_batch, (s - 1) % per_batch), 0)

    out_shape = [
        jax.ShapeDtypeStruct((T_X, D_MODEL), jnp.float32),
        jax.ShapeDtypeStruct((T_X * SLAB, LANES), jnp.uint32),
        jax.ShapeDtypeStruct((2, T_X), jnp.int32),
        jax.ShapeDtypeStruct((2, T_X), jnp.float32),
        jax.ShapeDtypeStruct((N_KEYS, LANES), jnp.int32),
    ]
    return pl.pallas_call(
        _attn_kernel,
        grid_spec=pltpu.PrefetchScalarGridSpec(
            num_scalar_prefetch=2,
            grid=(N_XT + 1,),
            in_specs=[
                pl.BlockSpec((TM, D_MODEL), tile_in),
                pl.BlockSpec((TM, 2), tile_in),
                pl.BlockSpec(memory_space=pl.ANY),
                pl.BlockSpec((TM, LANES), rope_tile),
                pl.BlockSpec((TM, LANES), rope_tile),
                _RESIDENT, _RESIDENT, _RESIDENT, _RESIDENT, _RESIDENT,
                pl.BlockSpec(memory_space=pltpu.SMEM),
                _RESIDENT, _RESIDENT, _RESIDENT,
            ],
            out_specs=[
                pl.BlockSpec((TM, D_MODEL), lambda s, p1, p2: (tile_x(s, p1, p2), 0)),
                pl.BlockSpec((TM * SLAB, LANES), lambda s, p1, p2: (tile_x(s, p1, p2), 0)),
                pl.BlockSpec((2, TM), lambda s, p1, p2: (0, tile_x(s, p1, p2))),
                pl.BlockSpec((2, TM), lambda s, p1, p2: (0, tile_x(s, p1, p2))),
                _RESIDENT,
            ],
            scratch_shapes=[
                pltpu.VMEM((N_KV_HEADS, QB + TM, LANES), jnp.bfloat16),
                pltpu.VMEM((N_KV_HEADS, QB + TM, LANES), jnp.bfloat16),
                pltpu.VMEM((N_KV_HEADS, QB, LANES), jnp.bfloat16),
                pltpu.VMEM((N_KV_HEADS, QB, LANES), jnp.bfloat16),
                pltpu.VMEM((N_HEADS // 2, TM, LANES), jnp.float32),
                pltpu.VMEM((N_HEADS // 2, TM, LANES), jnp.bfloat16),
                pltpu.VMEM((2, QB, 2 * QB), jnp.float32),
                pltpu.VMEM((TM, D_MODEL), jnp.float32),
                pltpu.VMEM((N_KEYS, 1), jnp.float32),
                pltpu.VMEM((HALF0_ROWS0 * SLAB, LANES), jnp.uint32),
                pltpu.VMEM((TM * SLAB, LANES), jnp.uint32),
                pltpu.VMEM((TM * SLAB, LANES), jnp.uint32),
                pltpu.SemaphoreType.DMA,
            ],
        ),
        out_shape=out_shape,
        compiler_params=pltpu.CompilerParams(
            dimension_semantics=("arbitrary",), vmem_limit_bytes=VMEM_LIMIT),
        name="attn_route",
    )(pos1, pos2, h1, wcols, ys, cos_t, sin_t, g_attn, g_kv, w_q, w_kv, w_o, sinks, g_ffn, wr, br)


def _router_rows(rg_w, rg_b, re_w, re_b):
    wr = jnp.zeros((N_KEYS, D_MODEL), jnp.float32)
    wr = wr.at[0:N_GROUPS].set(rg_w.T).at[8:8 + N_EXPERTS].set(re_w.T)
    br = jnp.zeros((N_KEYS, 1), jnp.float32)
    br = br.at[0:N_GROUPS, 0].set(rg_b).at[8:8 + N_EXPERTS, 0].set(re_b)
    return wr, br


def _rope_tables():
    half = HEAD_DIM // 2
    inv_freq = ROPE_THETA ** (-jnp.arange(half, dtype=jnp.float32) / half)
    n_hi = _cdiv(N_META + SEQ, QB)
    ang_hi = (QB * jnp.arange(n_hi)).astype(jnp.float32)[:, None] * inv_freq[None, :]
    ang_lo = jnp.arange(QB).astype(jnp.float32)[:, None] * inv_freq[None, :]
    c_hi, s_hi = jnp.cos(ang_hi)[:, None, :], jnp.sin(ang_hi)[:, None, :]
    c_lo, s_lo = jnp.cos(ang_lo)[None, :, :], jnp.sin(ang_lo)[None, :, :]
    cos_p = (c_hi * c_lo - s_hi * s_lo).reshape(n_hi * QB, half)
    sin_p = (s_hi * c_lo + c_hi * s_lo).reshape(n_hi * QB, half)
    cos = jnp.concatenate([cos_p[N_META:N_META + SEQ], cos_p[0:TM]], axis=0)
    sin = jnp.concatenate([sin_p[N_META:N_META + SEQ], sin_p[0:TM]], axis=0)
    return jnp.tile(cos, (1, 4)), jnp.concatenate([-sin, sin, -sin, sin], axis=1)


def _moe(cnt, code, xp, wg, wu, wd, layer, t_pad, t_valid):
    pos, _, vrow, vgroup, vslot, vnext, nvis = _positions(cnt[:, 0], code, t_pad, t_valid)
    src = _invert(pos[0], pos[1], t_valid)
    ys = _ffn(vrow, vgroup, vslot, vnext, nvis, src, xp, wg, wu, wd, layer, t_pad, t_valid)
    return pos, ys


def kernel(x, meta_tokens, conv_norm_g, conv_w_in, conv_w, conv_w_out, kv_norm_g, w_kv, attn_norm_g,
           w_q, w_o, sinks, ffn_norm_g, router_group_w, router_group_b, router_expert_w,
           router_expert_b, w_gate, w_up, w_down, final_norm_g):
    bf = jnp.bfloat16
    x2d = x.reshape(T_X, D_MODEL)
    wr0, br0 = _router_rows(router_group_w[0], router_group_b[0], router_expert_w[0], router_expert_b[0])
    wr1, br1 = _router_rows(router_group_w[1], router_group_b[1], router_expert_w[1], router_expert_b[1])

    h1, xp0, code0, wts0, cnt0 = _mixer0(
        x2d, meta_tokens, conv_norm_g[0].reshape(1, D_MODEL), conv_w_in[0].astype(bf), conv_w[0],
        conv_w_out[0].astype(bf), ffn_norm_g[0].reshape(1, D_MODEL), wr0, br0)
    pos0, ys0 = _moe(cnt0, code0, xp0, w_gate, w_up, w_down, 0, T_PAD0, T_VALID0)

    cos_t, sin_t = _rope_tables()
    h3, xp1, code1, wts1, cnt1 = _attn(
        pos0[0], pos0[1], h1, wts0.T, ys0, cos_t, sin_t, attn_norm_g[0].reshape(1, D_MODEL),
        kv_norm_g.reshape(1, D_MODEL), w_q[0].astype(bf), w_kv.astype(bf), w_o[0].astype(bf), sinks[0],
        ffn_norm_g[1].reshape(1, D_MODEL), wr1, br1)
    pos1, ys1 = _moe(cnt1, code1, xp1, w_gate, w_up, w_down, 1, T_X, T_X)
    out = _final(pos1[0], pos1[1], h3, wts1.T, final_norm_g.reshape(1, D_MODEL), ys1)
    return out.reshape(BATCH, SEQ, D_MODEL)
```

```python
import functools

import jax
import jax.numpy as jnp
from jax import lax
from jax.experimental import pallas as pl
from jax.experimental.pallas import tpu as pltpu

D_MODEL = 1024
BATCH = 2
SEQ = 8192
N_META = 16
N_HEADS = 16
HEAD_DIM = 64
N_KV_HEADS = 4
WINDOW = 128
ROPE_THETA = 10000.0
N_GROUPS = 4
EXPERTS_PER_GROUP = 8
N_EXPERTS = N_GROUPS * EXPERTS_PER_GROUP
D_EXPERT = 256
NORM_EPS = 1e-5
NEG_INF = -1e30

TM = 512
N_XT = BATCH * SEQ // TM
T_X = BATCH * SEQ
T_PAD0 = (N_XT + 1) * TM
T_VALID0 = T_X + N_META
HALF_SPLIT = SEQ
N_KEYS = 2 * N_EXPERTS
TMG = 256
LANES = 128
SLAB = D_MODEL // 2 // LANES
VMEM_V7X = 64 * 1024 * 1024
VMEM_LIMIT = VMEM_V7X - 2 * 1024 * 1024


def _cdiv(a, b):
    return (a + b - 1) // b


def _rms_hat(x):
    return x * lax.rsqrt(jnp.mean(x * x, axis=-1, keepdims=True) + NORM_EPS)


def _pack_rows(xn):
    half = D_MODEL // 2
    return pltpu.pack_elementwise([xn[:, :half], xn[:, half:]], packed_dtype=jnp.bfloat16)


def _store_slabs(ref, words):
    m = words.shape[0]
    for k in range(SLAB):
        ref[pl.ds(k, m, stride=SLAB), :] = words[:, LANES * k:LANES * (k + 1)]


def _load_slabs(ref, m):
    return jnp.concatenate([ref[pl.ds(k, m, stride=SLAB), :] for k in range(SLAB)], axis=1)


def _unpack_words(words):
    lo = pltpu.unpack_elementwise(words, index=0, packed_dtype=jnp.bfloat16, unpacked_dtype=jnp.float32)
    hi = pltpu.unpack_elementwise(words, index=1, packed_dtype=jnp.bfloat16, unpacked_dtype=jnp.float32)
    return lo, hi


def _route(xn, wr_ref, br_ref, run_scr, tok_base, valid_limit):
    def split(a):
        hi = a.astype(jnp.bfloat16)
        return hi, (a - hi.astype(jnp.float32)).astype(jnp.bfloat16)

    def nt_dot(a, b):
        return lax.dot_general(a, b, (((1,), (1,)), ((), ())), preferred_element_type=jnp.float32)

    w_hi, w_lo = split(wr_ref[...])
    x_hi, x_lo = split(xn)
    logits = nt_dot(w_hi, x_hi) + nt_dot(w_hi, x_lo) + nt_dot(w_lo, x_hi) + br_ref[...]
    g = logits[0:N_GROUPS]
    gmax = jnp.max(g, axis=0, keepdims=True)
    rid_g = lax.broadcasted_iota(jnp.int32, g.shape, 0).astype(jnp.float32)
    g_idx = jnp.min(jnp.where(g == gmax, rid_g, float(N_GROUPS)), axis=0, keepdims=True).astype(jnp.int32)
    g_w = 1.0 / jnp.sum(jnp.exp(g - gmax), axis=0, keepdims=True)
    e_sel = logits[8:8 + EXPERTS_PER_GROUP]
    for gi in range(1, N_GROUPS):
        lo = 8 + EXPERTS_PER_GROUP * gi
        e_sel = jnp.where(g_idx == gi, logits[lo:lo + EXPERTS_PER_GROUP], e_sel)
    rid_e = lax.broadcasted_iota(jnp.int32, e_sel.shape, 0).astype(jnp.float32)
    none = float(EXPERTS_PER_GROUP)
    m1 = jnp.max(e_sel, axis=0, keepdims=True)
    i1f = jnp.min(jnp.where(e_sel == m1, rid_e, none), axis=0, keepdims=True)
    e_rest = jnp.where(rid_e == i1f, -jnp.inf, e_sel)
    m2 = jnp.max(e_rest, axis=0, keepdims=True)
    i2 = jnp.min(jnp.where(e_rest == m2, rid_e, none), axis=0, keepdims=True).astype(jnp.int32)
    i1 = i1f.astype(jnp.int32)
    ex = jnp.exp(m2 - m1)
    den = 1.0 / (1.0 + ex)
    w1 = den * g_w
    w2 = ex * den * g_w

    tok = tok_base + lax.broadcasted_iota(jnp.int32, (1, TM), 1)
    half = jnp.where(jnp.logical_and(tok >= HALF_SPLIT, tok < T_X), N_EXPERTS, 0)
    key1 = half + g_idx * EXPERTS_PER_GROUP + i1
    key2 = half + g_idx * EXPERTS_PER_GROUP + i2
    kid = lax.broadcasted_iota(jnp.int32, (N_KEYS, TM), 0)
    validf = jnp.where(tok < valid_limit, 1.0, 0.0)
    oh1 = jnp.where(kid == key1, validf, 0.0)
    oh2 = jnp.where(kid == key2, validf, 0.0)
    cnt = oh1 + oh2
    tri = jnp.where(lax.broadcasted_iota(jnp.int32, (TM, TM), 0)
                    <= lax.broadcasted_iota(jnp.int32, (TM, TM), 1), 1.0, 0.0).astype(jnp.bfloat16)
    cum = jnp.dot(cnt.astype(jnp.bfloat16), tri, preferred_element_type=jnp.float32)
    before = run_scr[...] + (cum - cnt)
    rank1 = jnp.sum(oh1 * before, axis=0, keepdims=True).astype(jnp.int32)
    rank2 = jnp.sum(oh2 * before, axis=0, keepdims=True).astype(jnp.int32)
    run_scr[...] = run_scr[...] + cum[:, TM - 1:TM]
    code = jnp.concatenate([key1 * 65536 + rank1, key2 * 65536 + rank2], axis=0)
    wts = jnp.concatenate([w1, w2], axis=0)
    return code, wts


def _ffn_prologue(h_new, g_ffn_ref, wr_ref, br_ref, run_scr, tok_base, valid_limit,
                  xp_ref, code_ref, wts_ref, cnt_ref):
    xn2 = _rms_hat(h_new) * g_ffn_ref[...]
    _store_slabs(xp_ref, _pack_rows(xn2))
    code, wts = _route(xn2, wr_ref, br_ref, run_scr, tok_base, valid_limit)
    code_ref[...] = code
    wts_ref[...] = wts
    cnt_ref[...] = jnp.broadcast_to(run_scr[...], cnt_ref.shape).astype(jnp.int32)


NC = 512


def _mixer0_kernel(x_ref, meta_ref, g_conv_ref, w_in_ref, cw_ref, w_out_ref,
                   g_ffn_ref, wr_ref, br_ref,
                   h1_ref, xp_ref, code_ref, wts_ref, cnt_ref,
                   h0_scr, acc_scr, carry_scr, meta_carry_scr, run_scr):
    s = pl.program_id(0)

    @pl.when(s == 0)
    def _():
        h0_scr[...] = jnp.zeros_like(h0_scr)
        h0_scr[0:N_META, :] = meta_ref[...]
        carry_scr[...] = jnp.zeros_like(carry_scr)
        meta_carry_scr[...] = jnp.zeros_like(meta_carry_scr)
        run_scr[...] = jnp.zeros_like(run_scr)

    @pl.when(s > 0)
    def _():
        h0_scr[...] = x_ref[...]

    @pl.when(s == 1 + N_XT // BATCH)
    def _():
        carry_scr[...] = meta_carry_scr[...]

    h0 = h0_scr[...]
    xn = (_rms_hat(h0) * g_conv_ref[...]).astype(jnp.bfloat16)
    row = lax.broadcasted_iota(jnp.int32, (TM, NC), 0)
    for c in range(D_MODEL // NC):
        cols = slice(NC * c, NC * (c + 1))
        gate_c = jnp.dot(xn, w_in_ref[:, D_MODEL + NC * c:D_MODEL + NC * (c + 1)],
                         preferred_element_type=jnp.float32)
        val = jnp.dot(xn, w_in_ref[:, 2 * D_MODEL + NC * c:2 * D_MODEL + NC * (c + 1)],
                      preferred_element_type=jnp.float32)
        u = gate_c * val
        tail = carry_scr[:, cols]
        c1 = tail[7:8, :]
        c2 = tail[6:7, :]
        um1 = jnp.where(row == 0, c1, pltpu.roll(u, 1, 0))
        um2 = jnp.where(row == 0, c2, jnp.where(row == 1, c1, pltpu.roll(u, 2, 0)))
        conv = um2 * cw_ref[0:1, cols] + um1 * cw_ref[1:2, cols] + u * cw_ref[2:3, cols]

        is_meta = s == 0
        meta_tail = u[N_META - 8:N_META, :]
        carry_scr[:, cols] = jnp.where(is_meta, meta_tail, u[TM - 8:TM, :])
        meta_carry_scr[:, cols] = jnp.where(is_meta, meta_tail, meta_carry_scr[:, cols])

        gate_b = jnp.dot(xn, w_in_ref[:, cols], preferred_element_type=jnp.float32)
        gated = (gate_b * conv).astype(jnp.bfloat16)
        part = jnp.dot(gated, w_out_ref[cols, :], preferred_element_type=jnp.float32)
        if c == 0:
            acc_scr[...] = h0 + part
        else:
            acc_scr[...] = acc_scr[...] + part

    h1 = acc_scr[...]
    h1_ref[...] = h1
    tile = jnp.where(s == 0, N_XT, s - 1)
    _ffn_prologue(h1, g_ffn_ref, wr_ref, br_ref, run_scr, tile * TM, T_VALID0,
                  xp_ref, code_ref, wts_ref, cnt_ref)


def _tile_first_meta(s):
    return jnp.where(s == 0, N_XT, s - 1)


_RESIDENT = pl.BlockSpec(memory_space=pltpu.VMEM)


def _mixer0(x2d, meta, g_conv, w_in, cw, w_out, g_ffn, wr, br):
    out_shape = [
        jax.ShapeDtypeStruct((T_PAD0, D_MODEL), jnp.float32),
        jax.ShapeDtypeStruct((T_PAD0 * SLAB, LANES), jnp.uint32),
        jax.ShapeDtypeStruct((2, T_PAD0), jnp.int32),
        jax.ShapeDtypeStruct((2, T_PAD0), jnp.float32),
        jax.ShapeDtypeStruct((N_KEYS, LANES), jnp.int32),
    ]
    return pl.pallas_call(
        _mixer0_kernel,
        grid=(N_XT + 1,),
        in_specs=[pl.BlockSpec((TM, D_MODEL), lambda s: (jnp.maximum(s - 1, 0), 0))] + [_RESIDENT] * 8,
        out_specs=[
            pl.BlockSpec((TM, D_MODEL), lambda s: (_tile_first_meta(s), 0)),
            pl.BlockSpec((TM * SLAB, LANES), lambda s: (_tile_first_meta(s), 0)),
            pl.BlockSpec((2, TM), lambda s: (0, _tile_first_meta(s))),
            pl.BlockSpec((2, TM), lambda s: (0, _tile_first_meta(s))),
            _RESIDENT,
        ],
        out_shape=out_shape,
        scratch_shapes=[
            pltpu.VMEM((TM, D_MODEL), jnp.float32),
            pltpu.VMEM((TM, D_MODEL), jnp.float32),
            pltpu.VMEM((8, D_MODEL), jnp.float32),
            pltpu.VMEM((8, D_MODEL), jnp.float32),
            pltpu.VMEM((N_KEYS, 1), jnp.float32),
        ],
        compiler_params=pltpu.CompilerParams(
            dimension_semantics=("arbitrary",), vmem_limit_bytes=VMEM_LIMIT),
        name="mixer0_route",
    )(x2d, meta, g_conv, w_in, cw, w_out, g_ffn, wr, br)


def _n_tiles(t_valid):
    return _cdiv(2 * t_valid, TMG)


def _n_visits(t_valid):
    return _n_tiles(t_valid) + N_KEYS


def _half0_rows(t_valid):
    return 2 * (HALF_SPLIT + t_valid - T_X)


def _half1_base(half0_rows):
    return half0_rows + TMG


def _sorted_rows(t_valid):
    return (_n_tiles(t_valid) + 2) * TMG


def _positions_kernel(t_pad, t_valid, cnt_ref, code_ref, pos_ref, gstart_ref, vrow_ref, vgroup_ref,
                      vslot_ref, vnext_ref, nvis_ref, nexte_scr):
    n_vis = _n_visits(t_valid)

    def offs(g, acc):
        acc = jnp.where(g == N_EXPERTS, _half1_base(_half0_rows(t_valid)), acc)
        gstart_ref[g] = acc
        return acc + cnt_ref[g]

    total = lax.fori_loop(0, N_KEYS, offs, jnp.int32(0))
    gstart_ref[N_KEYS] = total

    def rows_of(e):
        return cnt_ref[e] + cnt_ref[e + N_EXPERTS]

    def next_nonempty(i, nxt):
        e = N_EXPERTS - 1 - i
        nexte_scr[e] = nxt
        return jnp.where(rows_of(e) > 0, e, nxt)

    lax.fori_loop(0, N_EXPERTS, next_nonempty, jnp.int32(-1))

    def per_expert(e, carry):
        v, last_g, rank = carry
        for half in range(2):
            g = e + half * N_EXPERTS
            c = cnt_ref[g]
            start = gstart_ref[g]

            def per_window(k, vv, g=g, start=start):
                vrow_ref[vv] = start + k * TMG
                vgroup_ref[vv] = g
                vslot_ref[vv] = rank % 2
                vnext_ref[vv] = nexte_scr[e]
                return vv + 1

            v = lax.fori_loop(0, (c + TMG - 1) // TMG, per_window, v)
            last_g = jnp.where(c > 0, g, last_g)
        return v, last_g, jnp.where(rows_of(e) > 0, rank + 1, rank)

    nvis, last_g, _ = lax.fori_loop(0, N_EXPERTS, per_expert, (jnp.int32(0), jnp.int32(0), jnp.int32(0)))
    nvis_ref[0] = nvis

    def pad(vv, c):
        vrow_ref[vv] = 0
        vgroup_ref[vv] = last_g
        vslot_ref[vv] = 0
        vnext_ref[vv] = -1
        return c

    lax.fori_loop(nvis, n_vis, pad, 0)

    code = code_ref[...]
    key = code >> 16
    pos = code & 0xFFFF
    for g in range(N_KEYS):
        pos = pos + jnp.where(key == g, gstart_ref[g], 0)
    pos_ref[...] = pos


def _positions(cnt, code, t_pad, t_valid):
    n_vis = _n_visits(t_valid)
    smem = pl.BlockSpec(memory_space=pltpu.SMEM)
    return pl.pallas_call(
        functools.partial(_positions_kernel, t_pad, t_valid),
        in_specs=[smem, pl.BlockSpec(memory_space=pltpu.VMEM)],
        out_specs=[pl.BlockSpec(memory_space=pltpu.VMEM), smem, smem, smem, smem, smem, smem],
        out_shape=[
            jax.ShapeDtypeStruct((2, t_pad), jnp.int32),
            jax.ShapeDtypeStruct((N_KEYS + 1,), jnp.int32),
            jax.ShapeDtypeStruct((n_vis,), jnp.int32),
            jax.ShapeDtypeStruct((n_vis,), jnp.int32),
            jax.ShapeDtypeStruct((n_vis,), jnp.int32),
            jax.ShapeDtypeStruct((n_vis,), jnp.int32),
            jax.ShapeDtypeStruct((1,), jnp.int32),
        ],
        scratch_shapes=[pltpu.SMEM((N_EXPERTS,), jnp.int32)],
        name="sort_positions",
    )(cnt, code)


SRC_UNROLL = 8


def _invert_kernel(t_valid, r_pad, pos1_ref, pos2_ref, src_ref):
    def slack(r, c):
        src_ref[r] = 0
        return c

    half0_rows = _half0_rows(t_valid)
    lax.fori_loop(half0_rows, _half1_base(half0_rows), slack, 0)
    lax.fori_loop(2 * t_valid + TMG, r_pad, slack, 0)

    def body(i, c):
        for j in range(SRC_UNROLL):
            t = i * SRC_UNROLL + j
            src_ref[pos1_ref[t]] = t
            src_ref[pos2_ref[t]] = t
        return c

    lax.fori_loop(0, t_valid // SRC_UNROLL, body, 0)


def _invert(pos1, pos2, t_valid):
    r_pad = _sorted_rows(t_valid)
    smem = pl.BlockSpec(memory_space=pltpu.SMEM)
    return pl.pallas_call(
        functools.partial(_invert_kernel, t_valid, r_pad),
        in_specs=[smem, smem],
        out_specs=smem,
        out_shape=jax.ShapeDtypeStruct((r_pad,), jnp.int32),
        name="sort_invert",
    )(pos1, pos2)


GATHER_UNROLL = 8


def _ffn_kernel(layer, half0_rows, vrow_ref, vgroup_ref, vslot_ref, vnext_ref, nvis_ref, src_ref,
                xp_hbm, wg_hbm, wu_hbm, wd_hbm, ys_hbm,
                xp_scr, wg_scr, wu_scr, wd_scr, xnext_scr, out_scr, xsem, wsem, osem):
    v = pl.program_id(0)
    nvis = nvis_ref[0]
    active = v < nvis
    e = vgroup_ref[v] % N_EXPERTS
    slot = vslot_ref[v]
    expert_first = jnp.logical_or(v == 0, e != vgroup_ref[jnp.maximum(v - 1, 0)] % N_EXPERTS)

    def weight_copies(e, sl):
        return (pltpu.make_async_copy(wg_hbm.at[layer, e], wg_scr.at[sl], wsem.at[sl, 0]),
                pltpu.make_async_copy(wu_hbm.at[layer, e], wu_scr.at[sl], wsem.at[sl, 1]),
                pltpu.make_async_copy(wd_hbm.at[layer, e], wd_scr.at[sl], wsem.at[sl, 2]))

    def out_copy(visit):
        sl = visit % 2
        return pltpu.make_async_copy(
            out_scr.at[sl], ys_hbm.at[pl.ds(pl.multiple_of(vrow_ref[visit] * SLAB, SLAB), TMG * SLAB), :],
            osem.at[sl])

    def gather_rows(visit, unrolled):
        base = vrow_ref[visit]

        def one(r):
            tok = src_ref[base + r]
            xnext_scr[pl.ds(pl.multiple_of(r * SLAB, SLAB), SLAB), :] = (
                xp_scr[pl.ds(pl.multiple_of(tok * SLAB, SLAB), SLAB), :])

        if unrolled:
            for r in range(TMG):
                one(r)
        else:
            def chunk(i, c):
                for j in range(GATHER_UNROLL):
                    one(i * GATHER_UNROLL + j)
                return c

            lax.fori_loop(0, TMG // GATHER_UNROLL, chunk, 0)

    @pl.when(v == 0)
    def _():
        cp = pltpu.make_async_copy(xp_hbm, xp_scr, xsem)
        cp.start()
        for c in weight_copies(e, slot):
            c.start()
        out_scr[...] = jnp.zeros_like(out_scr)
        cap = ys_hbm.shape[0] // SLAB
        for first_row in (half0_rows, cap - 2 * TMG, cap - TMG):
            fill = pltpu.make_async_copy(
                out_scr.at[0], ys_hbm.at[pl.ds(first_row * SLAB, TMG * SLAB), :], osem.at[0])
            fill.start()
            fill.wait()
        cp.wait()
        gather_rows(0, False)

    @pl.when(jnp.logical_and(active, expert_first))
    def _():
        for c in weight_copies(e, slot):
            c.wait()
        nxt = vnext_ref[v]

        @pl.when(nxt >= 0)
        def _():
            for c in weight_copies(nxt, 1 - slot):
                c.start()

    @pl.when(active)
    def _():
        lo, hi = _unpack_words(_load_slabs(xnext_scr, TMG))
        xs = jnp.concatenate([lo, hi], axis=1).astype(jnp.bfloat16)
        gather_rows(jnp.minimum(v + 1, nvis - 1), True)
        hg = jnp.dot(xs, wg_scr[slot].astype(jnp.bfloat16), preferred_element_type=jnp.float32)
        hu = jnp.dot(xs, wu_scr[slot].astype(jnp.bfloat16), preferred_element_type=jnp.float32)
        hdn = (hg * jax.nn.sigmoid(hg) * hu).astype(jnp.bfloat16)
        y = jnp.dot(hdn, wd_scr[slot].astype(jnp.bfloat16), preferred_element_type=jnp.float32)
        _store_slabs(out_scr.at[v % 2], _pack_rows(y))

        @pl.when(v > 0)
        def _():
            out_copy(v - 1).wait()

        out_copy(v).start()

    @pl.when(v == nvis - 1)
    def _():
        out_copy(v).wait()


def _ffn(vrow, vgroup, vslot, vnext, nvis, src, xp, wg, wu, wd, layer, t_pad, t_valid):
    any_spec = pl.BlockSpec(memory_space=pl.ANY)
    return pl.pallas_call(
        functools.partial(_ffn_kernel, layer, _half0_rows(t_valid)),
        grid_spec=pltpu.PrefetchScalarGridSpec(
            num_scalar_prefetch=6,
            grid=(_n_visits(t_valid),),
            in_specs=[any_spec, any_spec, any_spec, any_spec],
            out_specs=any_spec,
            scratch_shapes=[
                pltpu.VMEM((t_pad * SLAB, LANES), jnp.uint32),
                pltpu.VMEM((2, D_MODEL, D_EXPERT), jnp.float32),
                pltpu.VMEM((2, D_MODEL, D_EXPERT), jnp.float32),
                pltpu.VMEM((2, D_EXPERT, D_MODEL), jnp.float32),
                pltpu.VMEM((TMG * SLAB, LANES), jnp.uint32),
                pltpu.VMEM((2, TMG * SLAB, LANES), jnp.uint32),
                pltpu.SemaphoreType.DMA,
                pltpu.SemaphoreType.DMA((2, 3)),
                pltpu.SemaphoreType.DMA((2,)),
            ],
        ),
        out_shape=jax.ShapeDtypeStruct((_sorted_rows(t_valid) * SLAB, LANES), jnp.uint32),
        compiler_params=pltpu.CompilerParams(
            dimension_semantics=("arbitrary",), vmem_limit_bytes=VMEM_LIMIT),
        name="expert_ffn",
    )(vrow, vgroup, vslot, vnext, nvis, src, xp, wg, wu, wd)


TILES_PER_HALF = HALF_SPLIT // TM


def _load_half(ys_hbm, ys_scr, sem, half, half0_rows, half1_rows):
    start, rows = (0, half0_rows) if half == 0 else (_half1_base(half0_rows), half1_rows)
    cp = pltpu.make_async_copy(ys_hbm.at[pl.ds(start * SLAB, rows * SLAB), :],
                               ys_scr.at[pl.ds(0, rows * SLAB), :], sem)
    cp.start()
    cp.wait()


def _gather_pairs(pos1_ref, pos2_ref, ys_scr, y1_scr, y2_scr, tok0, base, n, unrolled):
    def one(r):
        p1 = pos1_ref[tok0 + r] - base
        p2 = pos2_ref[tok0 + r] - base
        dst = pl.ds(pl.multiple_of(r * SLAB, SLAB), SLAB)
        y1_scr[dst, :] = ys_scr[pl.ds(pl.multiple_of(p1 * SLAB, SLAB), SLAB), :]
        y2_scr[dst, :] = ys_scr[pl.ds(pl.multiple_of(p2 * SLAB, SLAB), SLAB), :]

    if unrolled:
        for r in range(n):
            one(r)
    else:
        def chunk(i, c):
            for j in range(GATHER_UNROLL):
                one(i * GATHER_UNROLL + j)
            return c

        lax.fori_loop(0, n // GATHER_UNROLL, chunk, 0)


def _weighted_sum(y1_scr, y2_scr, w_ref, m):
    lo1, hi1 = _unpack_words(_load_slabs(y1_scr, m))
    lo2, hi2 = _unpack_words(_load_slabs(y2_scr, m))
    w1 = w_ref[0:m, 0:1]
    w2 = w_ref[0:m, 1:2]
    return jnp.concatenate([w1 * lo1 + w2 * lo2, w1 * hi1 + w2 * hi2], axis=1)


def _next_x_tile(tile):
    nxt = tile + 1
    return jnp.where(jnp.logical_or(nxt == TILES_PER_HALF, nxt == N_XT), tile, nxt)


def _final_kernel(half_rows, pos1_ref, pos2_ref, h_ref, w_ref, g_ref, ys_hbm,
                  o_ref, ys_scr, y1_scr, y2_scr, sem):
    tile = pl.program_id(0)
    base1 = _half1_base(half_rows)

    @pl.when(tile == 0)
    def _():
        _load_half(ys_hbm, ys_scr, sem, 0, half_rows, half_rows)
        _gather_pairs(pos1_ref, pos2_ref, ys_scr, y1_scr, y2_scr, 0, 0, TM, False)

    @pl.when(tile == TILES_PER_HALF)
    def _():
        _load_half(ys_hbm, ys_scr, sem, 1, half_rows, half_rows)
        _gather_pairs(pos1_ref, pos2_ref, ys_scr, y1_scr, y2_scr, TILES_PER_HALF * TM, base1, TM, False)

    h = h_ref[...] + _weighted_sum(y1_scr, y2_scr, w_ref, TM)
    nxt = _next_x_tile(tile)
    _gather_pairs(pos1_ref, pos2_ref, ys_scr, y1_scr, y2_scr, nxt * TM,
                  jnp.where(nxt >= TILES_PER_HALF, base1, 0), TM, True)
    o_ref[...] = _rms_hat(h) * g_ref[...]


def _final(pos1, pos2, h, wcols, g, ys):
    half_rows = 2 * HALF_SPLIT
    return pl.pallas_call(
        functools.partial(_final_kernel, half_rows),
        grid_spec=pltpu.PrefetchScalarGridSpec(
            num_scalar_prefetch=2,
            grid=(N_XT,),
            in_specs=[
                pl.BlockSpec((TM, D_MODEL), lambda i, p1, p2: (i, 0)),
                pl.BlockSpec((TM, 2), lambda i, p1, p2: (i, 0)),
                _RESIDENT,
                pl.BlockSpec(memory_space=pl.ANY),
            ],
            out_specs=pl.BlockSpec((TM, D_MODEL), lambda i, p1, p2: (i, 0)),
            scratch_shapes=[
                pltpu.VMEM((half_rows * SLAB, LANES), jnp.uint32),
                pltpu.VMEM((TM * SLAB, LANES), jnp.uint32),
                pltpu.VMEM((TM * SLAB, LANES), jnp.uint32),
                pltpu.SemaphoreType.DMA,
            ],
        ),
        out_shape=jax.ShapeDtypeStruct((T_X, D_MODEL), jnp.float32),
        compiler_params=pltpu.CompilerParams(
            dimension_semantics=("arbitrary",), vmem_limit_bytes=VMEM_LIMIT),
        name="moe_combine_final",
    )(pos1, pos2, h, wcols, g, ys)


QB = WINDOW
KV_W = N_KV_HEADS * HEAD_DIM
N_QB = TM // QB
META_ROW0 = QB - N_META


def _rope(x, cos, sin_signed):
    q = lax.broadcasted_iota(jnp.int32, x.shape, 1) // (HEAD_DIM // 2)
    swapped = jnp.where(q % 2 == 0, pltpu.roll(x, LANES - HEAD_DIM // 2, 1),
                        pltpu.roll(x, HEAD_DIM // 2, 1))
    return x * cos + swapped * sin_signed


def _dup_heads(blk):
    lane = lax.broadcasted_iota(jnp.int32, blk.shape, 1)
    rolled = pltpu.roll(blk, HEAD_DIM, 1)
    return jnp.where(lane < HEAD_DIM, blk, rolled), jnp.where(lane < HEAD_DIM, rolled, blk)


def _kv_rows(xhat, g_kv_ref, w_kv_ref, cos, sin_signed):
    xk = (xhat * g_kv_ref[...]).astype(jnp.bfloat16)
    kv = jnp.dot(xk, w_kv_ref[...], preferred_element_type=jnp.float32)
    ks, vs = [], []
    for b in range(KV_W // LANES):
        kb = _rope(kv[:, LANES * b:LANES * (b + 1)], cos, sin_signed)
        vb = kv[:, KV_W + LANES * b:KV_W + LANES * (b + 1)]
        ks.extend(_dup_heads(kb))
        vs.extend(_dup_heads(vb))
    return [k.astype(jnp.bfloat16) for k in ks], [v.astype(jnp.bfloat16) for v in vs]


HALF0_ROWS0 = 2 * (HALF_SPLIT + N_META)
HALF1_ROWS0 = 2 * HALF_SPLIT


def _attn_kernel(pos1_ref, pos2_ref,
                 h_ref, w_ref, ys_hbm, cos_ref, sin_ref, g_attn_ref, g_kv_ref, w_q_ref, w_kv_ref, w_o_ref,
                 sink_ref, g_ffn_ref, wr_ref, br_ref,
                 h3_ref, xp_ref, code_ref, wts_ref, cnt_ref,
                 k_scr, v_scr, mk_scr, mv_scr, q_scr, o_scr, bias_scr, h_scr, run_scr,
                 ys_scr, y1_scr, y2_scr, ysem):
    s = pl.program_id(0)
    tiles_per_batch = N_XT // BATCH
    gather = functools.partial(_gather_pairs, pos1_ref, pos2_ref, ys_scr, y1_scr, y2_scr)
    base1 = _half1_base(HALF0_ROWS0)

    @pl.when(s == 0)
    def _():
        run_scr[...] = jnp.zeros_like(run_scr)
        _load_half(ys_hbm, ys_scr, ysem, 0, HALF0_ROWS0, HALF1_ROWS0)
        gather(N_XT * TM, 0, N_META, True)
        h_meta = h_ref[0:N_META, :] + _weighted_sum(y1_scr, y2_scr, w_ref, N_META)
        gather(0, 0, TM, False)
        xhat = _rms_hat(h_meta)
        ks, vs = _kv_rows(xhat, g_kv_ref, w_kv_ref, cos_ref[0:N_META, :], sin_ref[0:N_META, :])
        mk_scr[...] = jnp.zeros_like(mk_scr)
        mv_scr[...] = jnp.zeros_like(mv_scr)
        for kvh in range(N_KV_HEADS):
            mk_scr[kvh, META_ROW0:QB, :] = ks[kvh]
            mv_scr[kvh, META_ROW0:QB, :] = vs[kvh]
        qi = lax.broadcasted_iota(jnp.int32, (QB, 2 * QB), 0)
        kj = lax.broadcasted_iota(jnp.int32, (QB, 2 * QB), 1)
        band = jnp.logical_and(kj > qi, kj <= qi + QB)
        bias_scr[0] = jnp.where(band, 0.0, NEG_INF)
        bias_scr[1] = jnp.where(jnp.logical_and(band, kj >= META_ROW0), 0.0, NEG_INF)

    @pl.when(s > 0)
    def _():
        batch_first = jnp.logical_or(s == 1, s == 1 + tiles_per_batch)

        @pl.when(batch_first)
        def _():
            k_scr[:, 0:QB, :] = mk_scr[...]
            v_scr[:, 0:QB, :] = mv_scr[...]

        tile = s - 1

        @pl.when(tile == TILES_PER_HALF)
        def _():
            _load_half(ys_hbm, ys_scr, ysem, 1, HALF0_ROWS0, HALF1_ROWS0)
            gather(TILES_PER_HALF * TM, base1, TM, False)

        h = h_ref[...] + _weighted_sum(y1_scr, y2_scr, w_ref, TM)
        h_scr[...] = h
        nxt = _next_x_tile(tile)
        gather(nxt * TM, jnp.where(nxt >= TILES_PER_HALF, base1, 0), TM, True)
        xhat = _rms_hat(h)
        ks, vs = _kv_rows(xhat, g_kv_ref, w_kv_ref, cos_ref[...], sin_ref[...])
        for kvh in range(N_KV_HEADS):
            k_scr[kvh, QB:QB + TM, :] = ks[kvh]
            v_scr[kvh, QB:QB + TM, :] = vs[kvh]

        xq = (xhat * g_attn_ref[...]).astype(jnp.bfloat16)
        q = jnp.dot(xq, w_q_ref[...], preferred_element_type=jnp.float32)
        for hb in range(N_HEADS // 2):
            q_scr[hb] = q[:, LANES * hb:LANES * (hb + 1)]
        first_bias = jnp.where(batch_first, 1, 0)

        def head_pair(hb, carry):
            kvh = hb // (N_HEADS // N_KV_HEADS // 2)
            lane = lax.broadcasted_iota(jnp.int32, (QB, LANES), 1)
            q_all = _rope(q_scr[hb], cos_ref[...], sin_ref[...]) * (HEAD_DIM ** -0.5)
            for b in range(N_QB):
                qb = q_all[QB * b:QB * (b + 1), :]
                qs = jnp.concatenate([jnp.where(lane < HEAD_DIM, qb, 0.0),
                                      jnp.where(lane < HEAD_DIM, 0.0, qb)], axis=0).astype(jnp.bfloat16)
                kk = k_scr[kvh, QB * b:QB * (b + 2), :]
                vv = v_scr[kvh, QB * b:QB * (b + 2), :]
                sc = lax.dot_general(qs, kk, (((1,), (1,)), ((), ())),
                                     preferred_element_type=jnp.float32)
                bias = bias_scr[first_bias] if b == 0 else bias_scr[0]
                outs = []
                for j in range(2):
                    sj = sc[QB * j:QB * (j + 1), :] + bias
                    sink = sink_ref[2 * hb + j]
                    m = jnp.maximum(jnp.max(sj, axis=-1, keepdims=True), sink)
                    p = jnp.exp(sj - m)
                    den = jnp.sum(p, axis=-1, keepdims=True) + jnp.exp(sink - m)
                    pv = jnp.dot(p.astype(jnp.bfloat16), vv, preferred_element_type=jnp.float32)
                    outs.append(pv * (1.0 / den))
                o_scr[hb, QB * b:QB * (b + 1), :] = jnp.where(
                    lane < HEAD_DIM, outs[0], outs[1]).astype(jnp.bfloat16)
            return carry

        lax.fori_loop(0, N_HEADS // 2, head_pair, 0)

        k_scr[:, 0:QB, :] = k_scr[:, TM:TM + QB, :]
        v_scr[:, 0:QB, :] = v_scr[:, TM:TM + QB, :]

        o = jnp.concatenate([o_scr[hb] for hb in range(N_HEADS // 2)], axis=1)
        h3 = h_scr[...] + jnp.dot(o, w_o_ref[...], preferred_element_type=jnp.float32)
        h3_ref[...] = h3
        _ffn_prologue(h3, g_ffn_ref, wr_ref, br_ref, run_scr, (s - 1) * TM, T_X,
                      xp_ref, code_ref, wts_ref, cnt_ref)


def _attn(pos1, pos2, h1, wcols, ys, cos_t, sin_t, g_attn, g_kv, w_q, w_kv, w_o, sinks, g_ffn, wr, br):
    def tile_x(s, p1, p2):
        return jnp.maximum(s - 1, 0)

    def tile_in(s, p1, p2):
        return (_tile_first_meta(s), 0)

    def rope_tile(s, p1, p2):
        per_batch = N_XT // BATCH
        return (jnp.where(s == 0, per_batch, (s - 1) % per_batch), 0)

    out_shape = [
        jax.ShapeDtypeStruct((T_X, D_MODEL), jnp.float32),
        jax.ShapeDtypeStruct((T_X * SLAB, LANES), jnp.uint32),
        jax.ShapeDtypeStruct((2, T_X), jnp.int32),
        jax.ShapeDtypeStruct((2, T_X), jnp.float32),
        jax.ShapeDtypeStruct((N_KEYS, LANES), jnp.int32),
    ]
    return pl.pallas_call(
        _attn_kernel,
        grid_spec=pltpu.PrefetchScalarGridSpec(
            num_scalar_prefetch=2,
            grid=(N_XT + 1,),
            in_specs=[
                pl.BlockSpec((TM, D_MODEL), tile_in),
                pl.BlockSpec((TM, 2), tile_in),
                pl.BlockSpec(memory_space=pl.ANY),
                pl.BlockSpec((TM, LANES), rope_tile),
                pl.BlockSpec((TM, LANES), rope_tile),
                _RESIDENT, _RESIDENT, _RESIDENT, _RESIDENT, _RESIDENT,
                pl.BlockSpec(memory_space=pltpu.SMEM),
                _RESIDENT, _RESIDENT, _RESIDENT,
            ],
            out_specs=[
                pl.BlockSpec((TM, D_MODEL), lambda s, p1, p2: (tile_x(s, p1, p2), 0)),
                pl.BlockSpec((TM * SLAB, LANES), lambda s, p1, p2: (tile_x(s, p1, p2), 0)),
                pl.BlockSpec((2, TM), lambda s, p1, p2: (0, tile_x(s, p1, p2))),
                pl.BlockSpec((2, TM), lambda s, p1, p2: (0, tile_x(s, p1, p2))),
                _RESIDENT,
            ],
            scratch_shapes=[
                pltpu.VMEM((N_KV_HEADS, QB + TM, LANES), jnp.bfloat16),
                pltpu.VMEM((N_KV_HEADS, QB + TM, LANES), jnp.bfloat16),
                pltpu.VMEM((N_KV_HEADS, QB, LANES), jnp.bfloat16),
                pltpu.VMEM((N_KV_HEADS, QB, LANES), jnp.bfloat16),
                pltpu.VMEM((N_HEADS // 2, TM, LANES), jnp.float32),
                pltpu.VMEM((N_HEADS // 2, TM, LANES), jnp.bfloat16),
                pltpu.VMEM((2, QB, 2 * QB), jnp.float32),
                pltpu.VMEM((TM, D_MODEL), jnp.float32),
                pltpu.VMEM((N_KEYS, 1), jnp.float32),
                pltpu.VMEM((HALF0_ROWS0 * SLAB, LANES), jnp.uint32),
                pltpu.VMEM((TM * SLAB, LANES), jnp.uint32),
                pltpu.VMEM((TM * SLAB, LANES), jnp.uint32),
                pltpu.SemaphoreType.DMA,
            ],
        ),
        out_shape=out_shape,
        compiler_params=pltpu.CompilerParams(
            dimension_semantics=("arbitrary",), vmem_limit_bytes=VMEM_LIMIT),
        name="attn_route",
    )(pos1, pos2, h1, wcols, ys, cos_t, sin_t, g_attn, g_kv, w_q, w_kv, w_o, sinks, g_ffn, wr, br)


def _router_rows(rg_w, rg_b, re_w, re_b):
    wr = jnp.zeros((N_KEYS, D_MODEL), jnp.float32)
    wr = wr.at[0:N_GROUPS].set(rg_w.T).at[8:8 + N_EXPERTS].set(re_w.T)
    br = jnp.zeros((N_KEYS, 1), jnp.float32)
    br = br.at[0:N_GROUPS, 0].set(rg_b).at[8:8 + N_EXPERTS, 0].set(re_b)
    return wr, br


def _rope_tables():
    half = HEAD_DIM // 2
    inv_freq = ROPE_THETA ** (-jnp.arange(half, dtype=jnp.float32) / half)
    n_hi = _cdiv(N_META + SEQ, QB)
    ang_hi = (QB * jnp.arange(n_hi)).astype(jnp.float32)[:, None] * inv_freq[None, :]
    ang_lo = jnp.arange(QB).astype(jnp.float32)[:, None] * inv_freq[None, :]
    c_hi, s_hi = jnp.cos(ang_hi)[:, None, :], jnp.sin(ang_hi)[:, None, :]
    c_lo, s_lo = jnp.cos(ang_lo)[None, :, :], jnp.sin(ang_lo)[None, :, :]
    cos_p = (c_hi * c_lo - s_hi * s_lo).reshape(n_hi * QB, half)
    sin_p = (s_hi * c_lo + c_hi * s_lo).reshape(n_hi * QB, half)
    cos = jnp.concatenate([cos_p[N_META:N_META + SEQ], cos_p[0:TM]], axis=0)
    sin = jnp.concatenate([sin_p[N_META:N_META + SEQ], sin_p[0:TM]], axis=0)
    return jnp.tile(cos, (1, 4)), jnp.concatenate([-sin, sin, -sin, sin], axis=1)


def _moe(cnt, code, xp, wg, wu, wd, layer, t_pad, t_valid):
    pos, _, vrow, vgroup, vslot, vnext, nvis = _positions(cnt[:, 0], code, t_pad, t_valid)
    src = _invert(pos[0], pos[1], t_valid)
    ys = _ffn(vrow, vgroup, vslot, vnext, nvis, src, xp, wg, wu, wd, layer, t_pad, t_valid)
    return pos, ys


def kernel(x, meta_tokens, conv_norm_g, conv_w_in, conv_w, conv_w_out, kv_norm_g, w_kv, attn_norm_g,
           w_q, w_o, sinks, ffn_norm_g, router_group_w, router_group_b, router_expert_w,
           router_expert_b, w_gate, w_up, w_down, final_norm_g):
    bf = jnp.bfloat16
    x2d = x.reshape(T_X, D_MODEL)
    wr0, br0 = _router_rows(router_group_w[0], router_group_b[0], router_expert_w[0], router_expert_b[0])
    wr1, br1 = _router_rows(router_group_w[1], router_group_b[1], router_expert_w[1], router_expert_b[1])

    h1, xp0, code0, wts0, cnt0 = _mixer0(
        x2d, meta_tokens, conv_norm_g[0].reshape(1, D_MODEL), conv_w_in[0].astype(bf), conv_w[0],
        conv_w_out[0].astype(bf), ffn_norm_g[0].reshape(1, D_MODEL), wr0, br0)
    pos0, ys0 = _moe(cnt0, code0, xp0, w_gate, w_up, w_down, 0, T_PAD0, T_VALID0)

    cos_t, sin_t = _rope_tables()
    h3, xp1, code1, wts1, cnt1 = _attn(
        pos0[0], pos0[1], h1, wts0.T, ys0, cos_t, sin_t, attn_norm_g[0].reshape(1, D_MODEL),
        kv_norm_g.reshape(1, D_MODEL), w_q[0].astype(bf), w_kv.astype(bf), w_o[0].astype(bf), sinks[0],
        ffn_norm_g[1].reshape(1, D_MODEL), wr1, br1)
    pos1, ys1 = _moe(cnt1, code1, xp1, w_gate, w_up, w_down, 1, T_X, T_X)
    out = _final(pos1[0], pos1[1], h3, wts1.T, final_norm_g.reshape(1, D_MODEL), ys1)
    return out.reshape(BATCH, SEQ, D_MODEL)
```

```python
import functools

import jax
import jax.numpy as jnp
from jax import lax
from jax.experimental import pallas as pl
from jax.experimental.pallas import tpu as pltpu

D_MODEL = 1024
BATCH = 2
SEQ = 8192
N_META = 16
N_HEADS = 16
HEAD_DIM = 64
N_KV_HEADS = 4
WINDOW = 128
ROPE_THETA = 10000.0
N_GROUPS = 4
EXPERTS_PER_GROUP = 8
N_EXPERTS = N_GROUPS * EXPERTS_PER_GROUP
D_EXPERT = 256
NORM_EPS = 1e-5
NEG_INF = -1e30

TM = 512
N_XT = BATCH * SEQ // TM
T_X = BATCH * SEQ
T_PAD0 = (N_XT + 1) * TM
T_VALID0 = T_X + N_META
HALF_SPLIT = SEQ
N_KEYS = 2 * N_EXPERTS
TMG = 256
LANES = 128
SLAB = D_MODEL // 2 // LANES
VMEM_V7X = 64 * 1024 * 1024
VMEM_LIMIT = VMEM_V7X - 2 * 1024 * 1024


def _cdiv(a, b):
    return (a + b - 1) // b


def _rms_hat(x):
    return x * lax.rsqrt(jnp.mean(x * x, axis=-1, keepdims=True) + NORM_EPS)


def _pack_rows(xn):
    half = D_MODEL // 2
    return pltpu.pack_elementwise([xn[:, :half], xn[:, half:]], packed_dtype=jnp.bfloat16)


def _store_slabs(ref, words):
    m = words.shape[0]
    for k in range(SLAB):
        ref[pl.ds(k, m, stride=SLAB), :] = words[:, LANES * k:LANES * (k + 1)]


def _load_slabs(ref, m):
    return jnp.concatenate([ref[pl.ds(k, m, stride=SLAB), :] for k in range(SLAB)], axis=1)


def _unpack_words(words):
    lo = pltpu.unpack_elementwise(words, index=0, packed_dtype=jnp.bfloat16, unpacked_dtype=jnp.float32)
    hi = pltpu.unpack_elementwise(words, index=1, packed_dtype=jnp.bfloat16, unpacked_dtype=jnp.float32)
    return lo, hi


def _route(xn, wr_ref, br_ref, run_scr, tok_base, valid_limit):
    def split(a):
        hi = a.astype(jnp.bfloat16)
        return hi, (a - hi.astype(jnp.float32)).astype(jnp.bfloat16)

    def nt_dot(a, b):
        return lax.dot_general(a, b, (((1,), (1,)), ((), ())), preferred_element_type=jnp.float32)

    w_hi, w_lo = split(wr_ref[...])
    x_hi, x_lo = split(xn)
    logits = nt_dot(w_hi, x_hi) + nt_dot(w_hi, x_lo) + nt_dot(w_lo, x_hi) + br_ref[...]
    g = logits[0:N_GROUPS]
    gmax = jnp.max(g, axis=0, keepdims=True)
    rid_g = lax.broadcasted_iota(jnp.int32, g.shape, 0).astype(jnp.float32)
    g_idx = jnp.min(jnp.where(g == gmax, rid_g, float(N_GROUPS)), axis=0, keepdims=True).astype(jnp.int32)
    g_w = 1.0 / jnp.sum(jnp.exp(g - gmax), axis=0, keepdims=True)
    e_sel = logits[8:8 + EXPERTS_PER_GROUP]
    for gi in range(1, N_GROUPS):
        lo = 8 + EXPERTS_PER_GROUP * gi
        e_sel = jnp.where(g_idx == gi, logits[lo:lo + EXPERTS_PER_GROUP], e_sel)
    rid_e = lax.broadcasted_iota(jnp.int32, e_sel.shape, 0).astype(jnp.float32)
    none = float(EXPERTS_PER_GROUP)
    m1 = jnp.max(e_sel, axis=0, keepdims=True)
    i1f = jnp.min(jnp.where(e_sel == m1, rid_e, none), axis=0, keepdims=True)
    e_rest = jnp.where(rid_e == i1f, -jnp.inf, e_sel)
    m2 = jnp.max(e_rest, axis=0, keepdims=True)
    i2 = jnp.min(jnp.where(e_rest == m2, rid_e, none), axis=0, keepdims=True).astype(jnp.int32)
    i1 = i1f.astype(jnp.int32)
    ex = jnp.exp(m2 - m1)
    den = 1.0 / (1.0 + ex)
    w1 = den * g_w
    w2 = ex * den * g_w

    tok = tok_base + lax.broadcasted_iota(jnp.int32, (1, TM), 1)
    half = jnp.where(jnp.logical_and(tok >= HALF_SPLIT, tok < T_X), N_EXPERTS, 0)
    key1 = half + g_idx * EXPERTS_PER_GROUP + i1
    key2 = half + g_idx * EXPERTS_PER_GROUP + i2
    kid = lax.broadcasted_iota(jnp.int32, (N_KEYS, TM), 0)
    validf = jnp.where(tok < valid_limit, 1.0, 0.0)
    oh1 = jnp.where(kid == key1, validf, 0.0)
    oh2 = jnp.where(kid == key2, validf, 0.0)
    cnt = oh1 + oh2
    tri = jnp.where(lax.broadcasted_iota(jnp.int32, (TM, TM), 0)
                    <= lax.broadcasted_iota(jnp.int32, (TM, TM), 1), 1.0, 0.0).astype(jnp.bfloat16)
    cum = jnp.dot(cnt.astype(jnp.bfloat16), tri, preferred_element_type=jnp.float32)
    before = run_scr[...] + (cum - cnt)
    rank1 = jnp.sum(oh1 * before, axis=0, keepdims=True).astype(jnp.int32)
    rank2 = jnp.sum(oh2 * before, axis=0, keepdims=True).astype(jnp.int32)
    run_scr[...] = run_scr[...] + cum[:, TM - 1:TM]
    code = jnp.concatenate([key1 * 65536 + rank1, key2 * 65536 + rank2], axis=0)
    wts = jnp.concatenate([w1, w2], axis=0)
    return code, wts


def _ffn_prologue(h_new, g_ffn_ref, wr_ref, br_ref, run_scr, tok_base, valid_limit,
                  xp_ref, code_ref, wts_ref, cnt_ref):
    xn2 = _rms_hat(h_new) * g_ffn_ref[...]
    _store_slabs(xp_ref, _pack_rows(xn2))
    code, wts = _route(xn2, wr_ref, br_ref, run_scr, tok_base, valid_limit)
    code_ref[...] = code
    wts_ref[...] = wts
    cnt_ref[...] = jnp.broadcast_to(run_scr[...], cnt_ref.shape).astype(jnp.int32)


NC = 512


def _mixer0_kernel(x_ref, meta_ref, g_conv_ref, w_in_ref, cw_ref, w_out_ref,
                   g_ffn_ref, wr_ref, br_ref,
                   h1_ref, xp_ref, code_ref, wts_ref, cnt_ref,
                   h0_scr, acc_scr, carry_scr, meta_carry_scr, run_scr):
    s = pl.program_id(0)

    @pl.when(s == 0)
    def _():
        h0_scr[...] = jnp.zeros_like(h0_scr)
        h0_scr[0:N_META, :] = meta_ref[...]
        carry_scr[...] = jnp.zeros_like(carry_scr)
        meta_carry_scr[...] = jnp.zeros_like(meta_carry_scr)
        run_scr[...] = jnp.zeros_like(run_scr)

    @pl.when(s > 0)
    def _():
        h0_scr[...] = x_ref[...]

    @pl.when(s == 1 + N_XT // BATCH)
    def _():
        carry_scr[...] = meta_carry_scr[...]

    h0 = h0_scr[...]
    xn = (_rms_hat(h0) * g_conv_ref[...]).astype(jnp.bfloat16)
    row = lax.broadcasted_iota(jnp.int32, (TM, NC), 0)
    for c in range(D_MODEL // NC):
        cols = slice(NC * c, NC * (c + 1))
        gate_c = jnp.dot(xn, w_in_ref[:, D_MODEL + NC * c:D_MODEL + NC * (c + 1)],
                         preferred_element_type=jnp.float32)
        val = jnp.dot(xn, w_in_ref[:, 2 * D_MODEL + NC * c:2 * D_MODEL + NC * (c + 1)],
                      preferred_element_type=jnp.float32)
        u = gate_c * val
        tail = carry_scr[:, cols]
        c1 = tail[7:8, :]
        c2 = tail[6:7, :]
        um1 = jnp.where(row == 0, c1, pltpu.roll(u, 1, 0))
        um2 = jnp.where(row == 0, c2, jnp.where(row == 1, c1, pltpu.roll(u, 2, 0)))
        conv = um2 * cw_ref[0:1, cols] + um1 * cw_ref[1:2, cols] + u * cw_ref[2:3, cols]

        is_meta = s == 0
        meta_tail = u[N_META - 8:N_META, :]
        carry_scr[:, cols] = jnp.where(is_meta, meta_tail, u[TM - 8:TM, :])
        meta_carry_scr[:, cols] = jnp.where(is_meta, meta_tail, meta_carry_scr[:, cols])

        gate_b = jnp.dot(xn, w_in_ref[:, cols], preferred_element_type=jnp.float32)
        gated = (gate_b * conv).astype(jnp.bfloat16)
        part = jnp.dot(gated, w_out_ref[cols, :], preferred_element_type=jnp.float32)
        if c == 0:
            acc_scr[...] = h0 + part
        else:
            acc_scr[...] = acc_scr[...] + part

    h1 = acc_scr[...]
    h1_ref[...] = h1
    tile = jnp.where(s == 0, N_XT, s - 1)
    _ffn_prologue(h1, g_ffn_ref, wr_ref, br_ref, run_scr, tile * TM, T_VALID0,
                  xp_ref, code_ref, wts_ref, cnt_ref)


def _tile_first_meta(s):
    return jnp.where(s == 0, N_XT, s - 1)


_RESIDENT = pl.BlockSpec(memory_space=pltpu.VMEM)


def _mixer0(x2d, meta, g_conv, w_in, cw, w_out, g_ffn, wr, br):
    out_shape = [
        jax.ShapeDtypeStruct((T_PAD0, D_MODEL), jnp.float32),
        jax.ShapeDtypeStruct((T_PAD0 * SLAB, LANES), jnp.uint32),
        jax.ShapeDtypeStruct((2, T_PAD0), jnp.int32),
        jax.ShapeDtypeStruct((2, T_PAD0), jnp.float32),
        jax.ShapeDtypeStruct((N_KEYS, LANES), jnp.int32),
    ]
    return pl.pallas_call(
        _mixer0_kernel,
        grid=(N_XT + 1,),
        in_specs=[pl.BlockSpec((TM, D_MODEL), lambda s: (jnp.maximum(s - 1, 0), 0))] + [_RESIDENT] * 8,
        out_specs=[
            pl.BlockSpec((TM, D_MODEL), lambda s: (_tile_first_meta(s), 0)),
            pl.BlockSpec((TM * SLAB, LANES), lambda s: (_tile_first_meta(s), 0)),
            pl.BlockSpec((2, TM), lambda s: (0, _tile_first_meta(s))),
            pl.BlockSpec((2, TM), lambda s: (0, _tile_first_meta(s))),
            _RESIDENT,
        ],
        out_shape=out_shape,
        scratch_shapes=[
            pltpu.VMEM((TM, D_MODEL), jnp.float32),
            pltpu.VMEM((TM, D_MODEL), jnp.float32),
            pltpu.VMEM((8, D_MODEL), jnp.float32),
            pltpu.VMEM((8, D_MODEL), jnp.float32),
            pltpu.VMEM((N_KEYS, 1), jnp.float32),
        ],
        compiler_params=pltpu.CompilerParams(
            dimension_semantics=("arbitrary",), vmem_limit_bytes=VMEM_LIMIT),
        name="mixer0_route",
    )(x2d, meta, g_conv, w_in, cw, w_out, g_ffn, wr, br)


def _n_tiles(t_valid):
    return _cdiv(2 * t_valid, TMG)


def _n_visits(t_valid):
    return _n_tiles(t_valid) + N_KEYS


def _half0_rows(t_valid):
    return 2 * (HALF_SPLIT + t_valid - T_X)


def _half1_base(half0_rows):
    return half0_rows + TMG


def _sorted_rows(t_valid):
    return (_n_tiles(t_valid) + 2) * TMG


def _positions_kernel(t_pad, t_valid, cnt_ref, code_ref, pos_ref, off_ref, gstart_ref, vrow_ref, vgroup_ref,
                      vslot_ref, vnext_ref, nvis_ref, nexte_scr):
    n_vis = _n_visits(t_valid)

    def offs(g, acc):
        acc = jnp.where(g == N_EXPERTS, _half1_base(_half0_rows(t_valid)), acc)
        gstart_ref[g] = acc
        return acc + cnt_ref[g]

    total = lax.fori_loop(0, N_KEYS, offs, jnp.int32(0))
    gstart_ref[N_KEYS] = total

    def rows_of(e):
        return cnt_ref[e] + cnt_ref[e + N_EXPERTS]

    def next_nonempty(i, nxt):
        e = N_EXPERTS - 1 - i
        nexte_scr[e] = nxt
        return jnp.where(rows_of(e) > 0, e, nxt)

    lax.fori_loop(0, N_EXPERTS, next_nonempty, jnp.int32(-1))

    def per_expert(e, carry):
        v, last_g, rank = carry
        for half in range(2):
            g = e + half * N_EXPERTS
            c = cnt_ref[g]
            start = gstart_ref[g]

            def per_window(k, vv, g=g, start=start):
                vrow_ref[vv] = start + k * TMG
                vgroup_ref[vv] = g
                vslot_ref[vv] = rank % 2
                vnext_ref[vv] = nexte_scr[e]
                return vv + 1

            v = lax.fori_loop(0, (c + TMG - 1) // TMG, per_window, v)
            last_g = jnp.where(c > 0, g, last_g)
        return v, last_g, jnp.where(rows_of(e) > 0, rank + 1, rank)

    nvis, last_g, _ = lax.fori_loop(0, N_EXPERTS, per_expert, (jnp.int32(0), jnp.int32(0), jnp.int32(0)))
    nvis_ref[0] = nvis

    def pad(vv, c):
        vrow_ref[vv] = 0
        vgroup_ref[vv] = last_g
        vslot_ref[vv] = 0
        vnext_ref[vv] = -1
        return c

    lax.fori_loop(nvis, n_vis, pad, 0)

    code = code_ref[...]
    key = code >> 16
    pos = code & 0xFFFF
    for g in range(N_KEYS):
        pos = pos + jnp.where(key == g, gstart_ref[g], 0)
    pos_ref[...] = pos
    half_base = jnp.where(key >= N_EXPERTS, _half1_base(_half0_rows(t_valid)), 0)
    off_ref[...] = (pos - half_base) * SLAB


def _positions(cnt, code, t_pad, t_valid):
    n_vis = _n_visits(t_valid)
    smem = pl.BlockSpec(memory_space=pltpu.SMEM)
    vmem = pl.BlockSpec(memory_space=pltpu.VMEM)
    return pl.pallas_call(
        functools.partial(_positions_kernel, t_pad, t_valid),
        in_specs=[smem, vmem],
        out_specs=[vmem, vmem, smem, smem, smem, smem, smem, smem],
        out_shape=[
            jax.ShapeDtypeStruct((2, t_pad), jnp.int32),
            jax.ShapeDtypeStruct((2, t_pad), jnp.int32),
            jax.ShapeDtypeStruct((N_KEYS + 1,), jnp.int32),
            jax.ShapeDtypeStruct((n_vis,), jnp.int32),
            jax.ShapeDtypeStruct((n_vis,), jnp.int32),
            jax.ShapeDtypeStruct((n_vis,), jnp.int32),
            jax.ShapeDtypeStruct((n_vis,), jnp.int32),
            jax.ShapeDtypeStruct((1,), jnp.int32),
        ],
        scratch_shapes=[pltpu.SMEM((N_EXPERTS,), jnp.int32)],
        name="sort_positions",
    )(cnt, code)


SRC_UNROLL = 8


def _invert_kernel(t_valid, r_pad, pos1_ref, pos2_ref, src_ref):
    def slack(r, c):
        src_ref[r] = 0
        return c

    half0_rows = _half0_rows(t_valid)
    lax.fori_loop(half0_rows, _half1_base(half0_rows), slack, 0)
    lax.fori_loop(2 * t_valid + TMG, r_pad, slack, 0)

    def body(i, c):
        for j in range(SRC_UNROLL):
            t = i * SRC_UNROLL + j
            src_ref[pos1_ref[t]] = t
            src_ref[pos2_ref[t]] = t
        return c

    lax.fori_loop(0, t_valid // SRC_UNROLL, body, 0)


def _invert(pos1, pos2, t_valid):
    r_pad = _sorted_rows(t_valid)
    smem = pl.BlockSpec(memory_space=pltpu.SMEM)
    return pl.pallas_call(
        functools.partial(_invert_kernel, t_valid, r_pad),
        in_specs=[smem, smem],
        out_specs=smem,
        out_shape=jax.ShapeDtypeStruct((r_pad,), jnp.int32),
        name="sort_invert",
    )(pos1, pos2)


GATHER_UNROLL = 8


def _ffn_kernel(layer, half0_rows, vrow_ref, vgroup_ref, vslot_ref, vnext_ref, nvis_ref, src_ref,
                xp_hbm, wg_hbm, wu_hbm, wd_hbm, ys_hbm,
                xp_scr, wg_scr, wu_scr, wd_scr, xnext_scr, out_scr, xsem, wsem, osem):
    nvis = nvis_ref[0]

    def expert_of(visit):
        return vgroup_ref[visit] % N_EXPERTS

    def weight_copies(e, sl):
        return (pltpu.make_async_copy(wg_hbm.at[layer, e], wg_scr.at[sl], wsem.at[sl, 0]),
                pltpu.make_async_copy(wu_hbm.at[layer, e], wu_scr.at[sl], wsem.at[sl, 1]),
                pltpu.make_async_copy(wd_hbm.at[layer, e], wd_scr.at[sl], wsem.at[sl, 2]))

    def out_copy(visit):
        sl = visit % 2
        return pltpu.make_async_copy(
            out_scr.at[sl], ys_hbm.at[pl.ds(pl.multiple_of(vrow_ref[visit] * SLAB, SLAB), TMG * SLAB), :],
            osem.at[sl])

    def gather_rows(visit, unrolled):
        base = vrow_ref[visit]

        def one(r):
            tok = src_ref[base + r]
            xnext_scr[pl.ds(pl.multiple_of(r * SLAB, SLAB), SLAB), :] = (
                xp_scr[pl.ds(pl.multiple_of(tok * SLAB, SLAB), SLAB), :])

        if unrolled:
            for r in range(TMG):
                one(r)
        else:
            def chunk(i, c):
                for j in range(GATHER_UNROLL):
                    one(i * GATHER_UNROLL + j)
                return c

            lax.fori_loop(0, TMG // GATHER_UNROLL, chunk, 0)

    cp = pltpu.make_async_copy(xp_hbm, xp_scr, xsem)
    cp.start()
    for c in weight_copies(expert_of(0), vslot_ref[0]):
        c.start()
    out_scr[...] = jnp.zeros_like(out_scr)
    cap = ys_hbm.shape[0] // SLAB
    for first_row in (half0_rows, cap - 2 * TMG, cap - TMG):
        fill = pltpu.make_async_copy(
            out_scr.at[0], ys_hbm.at[pl.ds(first_row * SLAB, TMG * SLAB), :], osem.at[0])
        fill.start()
        fill.wait()
    cp.wait()
    gather_rows(0, False)

    def visit(v, carry):
        e = expert_of(v)
        slot = vslot_ref[v]

        @pl.when(jnp.logical_or(v == 0, e != expert_of(jnp.maximum(v - 1, 0))))
        def _():
            for c in weight_copies(e, slot):
                c.wait()
            nxt = vnext_ref[v]

            @pl.when(nxt >= 0)
            def _():
                for c in weight_copies(nxt, 1 - slot):
                    c.start()

        lo, hi = _unpack_words(_load_slabs(xnext_scr, TMG))
        xs = jnp.concatenate([lo, hi], axis=1).astype(jnp.bfloat16)
        gather_rows(jnp.minimum(v + 1, nvis - 1), True)
        hg = jnp.dot(xs, wg_scr[slot].astype(jnp.bfloat16), preferred_element_type=jnp.float32)
        hu = jnp.dot(xs, wu_scr[slot].astype(jnp.bfloat16), preferred_element_type=jnp.float32)
        hdn = (hg * jax.nn.sigmoid(hg) * hu).astype(jnp.bfloat16)
        y = jnp.dot(hdn, wd_scr[slot].astype(jnp.bfloat16), preferred_element_type=jnp.float32)
        _store_slabs(out_scr.at[v % 2], _pack_rows(y))

        @pl.when(v > 0)
        def _():
            out_copy(v - 1).wait()

        out_copy(v).start()
        return carry

    lax.fori_loop(0, nvis, visit, 0)
    out_copy(nvis - 1).wait()


def _ffn(vrow, vgroup, vslot, vnext, nvis, src, xp, wg, wu, wd, layer, t_pad, t_valid):
    any_spec = pl.BlockSpec(memory_space=pl.ANY)
    return pl.pallas_call(
        functools.partial(_ffn_kernel, layer, _half0_rows(t_valid)),
        grid_spec=pltpu.PrefetchScalarGridSpec(
            num_scalar_prefetch=6,
            grid=(1,),
            in_specs=[any_spec, any_spec, any_spec, any_spec],
            out_specs=any_spec,
            scratch_shapes=[
                pltpu.VMEM((t_pad * SLAB, LANES), jnp.uint32),
                pltpu.VMEM((2, D_MODEL, D_EXPERT), jnp.float32),
                pltpu.VMEM((2, D_MODEL, D_EXPERT), jnp.float32),
                pltpu.VMEM((2, D_EXPERT, D_MODEL), jnp.float32),
                pltpu.VMEM((TMG * SLAB, LANES), jnp.uint32),
                pltpu.VMEM((2, TMG * SLAB, LANES), jnp.uint32),
                pltpu.SemaphoreType.DMA,
                pltpu.SemaphoreType.DMA((2, 3)),
                pltpu.SemaphoreType.DMA((2,)),
            ],
        ),
        out_shape=jax.ShapeDtypeStruct((_sorted_rows(t_valid) * SLAB, LANES), jnp.uint32),
        compiler_params=pltpu.CompilerParams(
            dimension_semantics=("arbitrary",), vmem_limit_bytes=VMEM_LIMIT),
        name="expert_ffn",
    )(vrow, vgroup, vslot, vnext, nvis, src, xp, wg, wu, wd)


TILES_PER_HALF = HALF_SPLIT // TM


def _load_half(ys_hbm, ys_scr, sem, half, half0_rows, half1_rows):
    start, rows = (0, half0_rows) if half == 0 else (_half1_base(half0_rows), half1_rows)
    cp = pltpu.make_async_copy(ys_hbm.at[pl.ds(start * SLAB, rows * SLAB), :],
                               ys_scr.at[pl.ds(0, rows * SLAB), :], sem)
    cp.start()
    cp.wait()


def _gather_pairs(off1_ref, off2_ref, ys_scr, y1_scr, y2_scr, tok0, n, unrolled):
    def one(r):
        dst = pl.ds(pl.multiple_of(r * SLAB, SLAB), SLAB)
        y1_scr[dst, :] = ys_scr[pl.ds(pl.multiple_of(off1_ref[tok0 + r], SLAB), SLAB), :]
        y2_scr[dst, :] = ys_scr[pl.ds(pl.multiple_of(off2_ref[tok0 + r], SLAB), SLAB), :]

    if unrolled:
        for r in range(n):
            one(r)
    else:
        def chunk(i, c):
            for j in range(GATHER_UNROLL):
                one(i * GATHER_UNROLL + j)
            return c

        lax.fori_loop(0, n // GATHER_UNROLL, chunk, 0)


def _weighted_sum(y1_scr, y2_scr, w_ref, m):
    lo1, hi1 = _unpack_words(_load_slabs(y1_scr, m))
    lo2, hi2 = _unpack_words(_load_slabs(y2_scr, m))
    w1 = w_ref[0:m, 0:1]
    w2 = w_ref[0:m, 1:2]
    return jnp.concatenate([w1 * lo1 + w2 * lo2, w1 * hi1 + w2 * hi2], axis=1)


def _next_x_tile(tile):
    nxt = tile + 1
    return jnp.where(jnp.logical_or(nxt == TILES_PER_HALF, nxt == N_XT), tile, nxt)


def _final_kernel(half_rows, off1_ref, off2_ref, h_ref, w_ref, g_ref, ys_hbm,
                  o_ref, ys_scr, y1_scr, y2_scr, sem):
    tile = pl.program_id(0)
    gather = functools.partial(_gather_pairs, off1_ref, off2_ref, ys_scr, y1_scr, y2_scr)

    @pl.when(tile == 0)
    def _():
        _load_half(ys_hbm, ys_scr, sem, 0, half_rows, half_rows)
        gather(0, TM, False)

    @pl.when(tile == TILES_PER_HALF)
    def _():
        _load_half(ys_hbm, ys_scr, sem, 1, half_rows, half_rows)
        gather(TILES_PER_HALF * TM, TM, False)

    h = h_ref[...] + _weighted_sum(y1_scr, y2_scr, w_ref, TM)
    gather(_next_x_tile(tile) * TM, TM, True)
    o_ref[...] = _rms_hat(h) * g_ref[...]


def _final(pos1, pos2, h, wcols, g, ys):
    half_rows = 2 * HALF_SPLIT
    return pl.pallas_call(
        functools.partial(_final_kernel, half_rows),
        grid_spec=pltpu.PrefetchScalarGridSpec(
            num_scalar_prefetch=2,
            grid=(N_XT,),
            in_specs=[
                pl.BlockSpec((TM, D_MODEL), lambda i, p1, p2: (i, 0)),
                pl.BlockSpec((TM, 2), lambda i, p1, p2: (i, 0)),
                _RESIDENT,
                pl.BlockSpec(memory_space=pl.ANY),
            ],
            out_specs=pl.BlockSpec((TM, D_MODEL), lambda i, p1, p2: (i, 0)),
            scratch_shapes=[
                pltpu.VMEM((half_rows * SLAB, LANES), jnp.uint32),
                pltpu.VMEM((TM * SLAB, LANES), jnp.uint32),
                pltpu.VMEM((TM * SLAB, LANES), jnp.uint32),
                pltpu.SemaphoreType.DMA,
            ],
        ),
        out_shape=jax.ShapeDtypeStruct((T_X, D_MODEL), jnp.float32),
        compiler_params=pltpu.CompilerParams(
            dimension_semantics=("arbitrary",), vmem_limit_bytes=VMEM_LIMIT),
        name="moe_combine_final",
    )(pos1, pos2, h, wcols, g, ys)


QB = WINDOW
KV_W = N_KV_HEADS * HEAD_DIM
N_QB = TM // QB
META_ROW0 = QB - N_META


def _rope(x, cos, sin_signed):
    q = lax.broadcasted_iota(jnp.int32, x.shape, 1) // (HEAD_DIM // 2)
    swapped = jnp.where(q % 2 == 0, pltpu.roll(x, LANES - HEAD_DIM // 2, 1),
                        pltpu.roll(x, HEAD_DIM // 2, 1))
    return x * cos + swapped * sin_signed


def _dup_heads(blk):
    lane = lax.broadcasted_iota(jnp.int32, blk.shape, 1)
    rolled = pltpu.roll(blk, HEAD_DIM, 1)
    return jnp.where(lane < HEAD_DIM, blk, rolled), jnp.where(lane < HEAD_DIM, rolled, blk)


def _kv_rows(xhat, g_kv_ref, w_kv_ref, cos, sin_signed):
    xk = (xhat * g_kv_ref[...]).astype(jnp.bfloat16)
    kv = jnp.dot(xk, w_kv_ref[...], preferred_element_type=jnp.float32)
    ks, vs = [], []
    for b in range(KV_W // LANES):
        kb = _rope(kv[:, LANES * b:LANES * (b + 1)], cos, sin_signed)
        vb = kv[:, KV_W + LANES * b:KV_W + LANES * (b + 1)]
        ks.extend(_dup_heads(kb))
        vs.extend(_dup_heads(vb))
    return [k.astype(jnp.bfloat16) for k in ks], [v.astype(jnp.bfloat16) for v in vs]


HALF0_ROWS0 = 2 * (HALF_SPLIT + N_META)
HALF1_ROWS0 = 2 * HALF_SPLIT


def _attn_kernel(off1_ref, off2_ref,
                 h_ref, w_ref, ys_hbm, cos_ref, sin_ref, g_attn_ref, g_kv_ref, w_q_ref, w_kv_ref, w_o_ref,
                 sink_ref, g_ffn_ref, wr_ref, br_ref,
                 h3_ref, xp_ref, code_ref, wts_ref, cnt_ref,
                 k_scr, v_scr, mk_scr, mv_scr, q_scr, o_scr, bias_scr, h_scr, run_scr,
                 ys_scr, y1_scr, y2_scr, ysem):
    s = pl.program_id(0)
    tiles_per_batch = N_XT // BATCH
    gather = functools.partial(_gather_pairs, off1_ref, off2_ref, ys_scr, y1_scr, y2_scr)

    @pl.when(s == 0)
    def _():
        run_scr[...] = jnp.zeros_like(run_scr)
        _load_half(ys_hbm, ys_scr, ysem, 0, HALF0_ROWS0, HALF1_ROWS0)
        gather(N_XT * TM, N_META, True)
        h_meta = h_ref[0:N_META, :] + _weighted_sum(y1_scr, y2_scr, w_ref, N_META)
        gather(0, TM, False)
        xhat = _rms_hat(h_meta)
        ks, vs = _kv_rows(xhat, g_kv_ref, w_kv_ref, cos_ref[0:N_META, :], sin_ref[0:N_META, :])
        mk_scr[...] = jnp.zeros_like(mk_scr)
        mv_scr[...] = jnp.zeros_like(mv_scr)
        for kvh in range(N_KV_HEADS):
            mk_scr[kvh, META_ROW0:QB, :] = ks[kvh]
            mv_scr[kvh, META_ROW0:QB, :] = vs[kvh]
        qi = lax.broadcasted_iota(jnp.int32, (QB, 2 * QB), 0)
        kj = lax.broadcasted_iota(jnp.int32, (QB, 2 * QB), 1)
        band = jnp.logical_and(kj > qi, kj <= qi + QB)
        bias_scr[0] = jnp.where(band, 0.0, NEG_INF)
        bias_scr[1] = jnp.where(jnp.logical_and(band, kj >= META_ROW0), 0.0, NEG_INF)

    @pl.when(s > 0)
    def _():
        batch_first = jnp.logical_or(s == 1, s == 1 + tiles_per_batch)

        @pl.when(batch_first)
        def _():
            k_scr[:, 0:QB, :] = mk_scr[...]
            v_scr[:, 0:QB, :] = mv_scr[...]

        tile = s - 1

        @pl.when(tile == TILES_PER_HALF)
        def _():
            _load_half(ys_hbm, ys_scr, ysem, 1, HALF0_ROWS0, HALF1_ROWS0)
            gather(TILES_PER_HALF * TM, TM, False)

        h = h_ref[...] + _weighted_sum(y1_scr, y2_scr, w_ref, TM)
        h_scr[...] = h
        gather(_next_x_tile(tile) * TM, TM, True)
        xhat = _rms_hat(h)
        ks, vs = _kv_rows(xhat, g_kv_ref, w_kv_ref, cos_ref[...], sin_ref[...])
        for kvh in range(N_KV_HEADS):
            k_scr[kvh, QB:QB + TM, :] = ks[kvh]
            v_scr[kvh, QB:QB + TM, :] = vs[kvh]

        xq = (xhat * g_attn_ref[...]).astype(jnp.bfloat16)
        q = jnp.dot(xq, w_q_ref[...], preferred_element_type=jnp.float32)
        for hb in range(N_HEADS // 2):
            q_scr[hb] = q[:, LANES * hb:LANES * (hb + 1)]
        first_bias = jnp.where(batch_first, 1, 0)

        def head_pair(hb, carry):
            kvh = hb // (N_HEADS // N_KV_HEADS // 2)
            lane = lax.broadcasted_iota(jnp.int32, (QB, LANES), 1)
            q_all = _rope(q_scr[hb], cos_ref[...], sin_ref[...]) * (HEAD_DIM ** -0.5)
            for b in range(N_QB):
                qb = q_all[QB * b:QB * (b + 1), :]
                qs = jnp.concatenate([jnp.where(lane < HEAD_DIM, qb, 0.0),
                                      jnp.where(lane < HEAD_DIM, 0.0, qb)], axis=0).astype(jnp.bfloat16)
                kk = k_scr[kvh, QB * b:QB * (b + 2), :]
                vv = v_scr[kvh, QB * b:QB * (b + 2), :]
                sc = lax.dot_general(qs, kk, (((1,), (1,)), ((), ())),
                                     preferred_element_type=jnp.float32)
                bias = bias_scr[first_bias] if b == 0 else bias_scr[0]
                outs = []
                for j in range(2):
                    sj = sc[QB * j:QB * (j + 1), :] + bias
                    sink = sink_ref[2 * hb + j]
                    m = jnp.maximum(jnp.max(sj, axis=-1, keepdims=True), sink)
                    p = jnp.exp(sj - m)
                    den = jnp.sum(p, axis=-1, keepdims=True) + jnp.exp(sink - m)
                    pv = jnp.dot(p.astype(jnp.bfloat16), vv, preferred_element_type=jnp.float32)
                    outs.append(pv * (1.0 / den))
                o_scr[hb, QB * b:QB * (b + 1), :] = jnp.where(
                    lane < HEAD_DIM, outs[0], outs[1]).astype(jnp.bfloat16)
            return carry

        lax.fori_loop(0, N_HEADS // 2, head_pair, 0)

        k_scr[:, 0:QB, :] = k_scr[:, TM:TM + QB, :]
        v_scr[:, 0:QB, :] = v_scr[:, TM:TM + QB, :]

        o = jnp.concatenate([o_scr[hb] for hb in range(N_HEADS // 2)], axis=1)
        h3 = h_scr[...] + jnp.dot(o, w_o_ref[...], preferred_element_type=jnp.float32)
        h3_ref[...] = h3
        _ffn_prologue(h3, g_ffn_ref, wr_ref, br_ref, run_scr, (s - 1) * TM, T_X,
                      xp_ref, code_ref, wts_ref, cnt_ref)


def _attn(pos1, pos2, h1, wcols, ys, cos_t, sin_t, g_attn, g_kv, w_q, w_kv, w_o, sinks, g_ffn, wr, br):
    def tile_x(s, p1, p2):
        return jnp.maximum(s - 1, 0)

    def tile_in(s, p1, p2):
        return (_tile_first_meta(s), 0)

    def rope_tile(s, p1, p2):
        per_batch = N_XT // BATCH
        return (jnp.where(s == 0, per_batch, (s - 1) % per_batch), 0)

    out_shape = [
        jax.ShapeDtypeStruct((T_X, D_MODEL), jnp.float32),
        jax.ShapeDtypeStruct((T_X * SLAB, LANES), jnp.uint32),
        jax.ShapeDtypeStruct((2, T_X), jnp.int32),
        jax.ShapeDtypeStruct((2, T_X), jnp.float32),
        jax.ShapeDtypeStruct((N_KEYS, LANES), jnp.int32),
    ]
    return pl.pallas_call(
        _attn_kernel,
        grid_spec=pltpu.PrefetchScalarGridSpec(
            num_scalar_prefetch=2,
            grid=(N_XT + 1,),
            in_specs=[
                pl.BlockSpec((TM, D_MODEL), tile_in),
                pl.BlockSpec((TM, 2), tile_in),
                pl.BlockSpec(memory_space=pl.ANY),
                pl.BlockSpec((TM, LANES), rope_tile),
                pl.BlockSpec((TM, LANES), rope_tile),
                _RESIDENT, _RESIDENT, _RESIDENT, _RESIDENT, _RESIDENT,
                pl.BlockSpec(memory_space=pltpu.SMEM),
                _RESIDENT, _RESIDENT, _RESIDENT,
            ],
            out_specs=[
                pl.BlockSpec((TM, D_MODEL), lambda s, p1, p2: (tile_x(s, p1, p2), 0)),
                pl.BlockSpec((TM * SLAB, LANES), lambda s, p1, p2: (tile_x(s, p1, p2), 0)),
                pl.BlockSpec((2, TM), lambda s, p1, p2: (0, tile_x(s, p1, p2))),
                pl.BlockSpec((2, TM), lambda s, p1, p2: (0, tile_x(s, p1, p2))),
                _RESIDENT,
            ],
            scratch_shapes=[
                pltpu.VMEM((N_KV_HEADS, QB + TM, LANES), jnp.bfloat16),
                pltpu.VMEM((N_KV_HEADS, QB + TM, LANES), jnp.bfloat16),
                pltpu.VMEM((N_KV_HEADS, QB, LANES), jnp.bfloat16),
                pltpu.VMEM((N_KV_HEADS, QB, LANES), jnp.bfloat16),
                pltpu.VMEM((N_HEADS // 2, TM, LANES), jnp.float32),
                pltpu.VMEM((N_HEADS // 2, TM, LANES), jnp.bfloat16),
                pltpu.VMEM((2, QB, 2 * QB), jnp.float32),
                pltpu.VMEM((TM, D_MODEL), jnp.float32),
                pltpu.VMEM((N_KEYS, 1), jnp.float32),
                pltpu.VMEM((HALF0_ROWS0 * SLAB, LANES), jnp.uint32),
                pltpu.VMEM((TM * SLAB, LANES), jnp.uint32),
                pltpu.VMEM((TM * SLAB, LANES), jnp.uint32),
                pltpu.SemaphoreType.DMA,
            ],
        ),
        out_shape=out_shape,
        compiler_params=pltpu.CompilerParams(
            dimension_semantics=("arbitrary",), vmem_limit_bytes=VMEM_LIMIT),
        name="attn_route",
    )(pos1, pos2, h1, wcols, ys, cos_t, sin_t, g_attn, g_kv, w_q, w_kv, w_o, sinks, g_ffn, wr, br)


def _router_rows(rg_w, rg_b, re_w, re_b):
    wr = jnp.zeros((N_KEYS, D_MODEL), jnp.float32)
    wr = wr.at[0:N_GROUPS].set(rg_w.T).at[8:8 + N_EXPERTS].set(re_w.T)
    br = jnp.zeros((N_KEYS, 1), jnp.float32)
    br = br.at[0:N_GROUPS, 0].set(rg_b).at[8:8 + N_EXPERTS, 0].set(re_b)
    return wr, br


def _rope_tables():
    half = HEAD_DIM // 2
    inv_freq = ROPE_THETA ** (-jnp.arange(half, dtype=jnp.float32) / half)
    n_hi = _cdiv(N_META + SEQ, QB)
    ang_hi = (QB * jnp.arange(n_hi)).astype(jnp.float32)[:, None] * inv_freq[None, :]
    ang_lo = jnp.arange(QB).astype(jnp.float32)[:, None] * inv_freq[None, :]
    c_hi, s_hi = jnp.cos(ang_hi)[:, None, :], jnp.sin(ang_hi)[:, None, :]
    c_lo, s_lo = jnp.cos(ang_lo)[None, :, :], jnp.sin(ang_lo)[None, :, :]
    cos_p = (c_hi * c_lo - s_hi * s_lo).reshape(n_hi * QB, half)
    sin_p = (s_hi * c_lo + c_hi * s_lo).reshape(n_hi * QB, half)
    cos = jnp.concatenate([cos_p[N_META:N_META + SEQ], cos_p[0:TM]], axis=0)
    sin = jnp.concatenate([sin_p[N_META:N_META + SEQ], sin_p[0:TM]], axis=0)
    return jnp.tile(cos, (1, 4)), jnp.concatenate([-sin, sin, -sin, sin], axis=1)


def _moe(cnt, code, xp, wg, wu, wd, layer, t_pad, t_valid):
    pos, off, _, vrow, vgroup, vslot, vnext, nvis = _positions(cnt[:, 0], code, t_pad, t_valid)
    src = _invert(pos[0], pos[1], t_valid)
    ys = _ffn(vrow, vgroup, vslot, vnext, nvis, src, xp, wg, wu, wd, layer, t_pad, t_valid)
    return off, ys


def kernel(x, meta_tokens, conv_norm_g, conv_w_in, conv_w, conv_w_out, kv_norm_g, w_kv, attn_norm_g,
           w_q, w_o, sinks, ffn_norm_g, router_group_w, router_group_b, router_expert_w,
           router_expert_b, w_gate, w_up, w_down, final_norm_g):
    bf = jnp.bfloat16
    x2d = x.reshape(T_X, D_MODEL)
    wr0, br0 = _router_rows(router_group_w[0], router_group_b[0], router_expert_w[0], router_expert_b[0])
    wr1, br1 = _router_rows(router_group_w[1], router_group_b[1], router_expert_w[1], router_expert_b[1])

    h1, xp0, code0, wts0, cnt0 = _mixer0(
        x2d, meta_tokens, conv_norm_g[0].reshape(1, D_MODEL), conv_w_in[0].astype(bf), conv_w[0],
        conv_w_out[0].astype(bf), ffn_norm_g[0].reshape(1, D_MODEL), wr0, br0)
    pos0, ys0 = _moe(cnt0, code0, xp0, w_gate, w_up, w_down, 0, T_PAD0, T_VALID0)

    cos_t, sin_t = _rope_tables()
    h3, xp1, code1, wts1, cnt1 = _attn(
        pos0[0], pos0[1], h1, wts0.T, ys0, cos_t, sin_t, attn_norm_g[0].reshape(1, D_MODEL),
        kv_norm_g.reshape(1, D_MODEL), w_q[0].astype(bf), w_kv.astype(bf), w_o[0].astype(bf), sinks[0],
        ffn_norm_g[1].reshape(1, D_MODEL), wr1, br1)
    pos1, ys1 = _moe(cnt1, code1, xp1, w_gate, w_up, w_down, 1, T_X, T_X)
    out = _final(pos1[0], pos1[1], h3, wts1.T, final_norm_g.reshape(1, D_MODEL), ys1)
    return out.reshape(BATCH, SEQ, D_MODEL)
```

```python
import functools

import jax
import jax.numpy as jnp
from jax import lax
from jax.experimental import pallas as pl
from jax.experimental.pallas import tpu as pltpu

D_MODEL = 1024
BATCH = 2
SEQ = 8192
N_META = 16
N_HEADS = 16
HEAD_DIM = 64
N_KV_HEADS = 4
WINDOW = 128
ROPE_THETA = 10000.0
N_GROUPS = 4
EXPERTS_PER_GROUP = 8
N_EXPERTS = N_GROUPS * EXPERTS_PER_GROUP
D_EXPERT = 256
NORM_EPS = 1e-5
NEG_INF = -1e30

TM = 512
N_XT = BATCH * SEQ // TM
T_X = BATCH * SEQ
T_PAD0 = (N_XT + 1) * TM
T_VALID0 = T_X + N_META
HALF_SPLIT = SEQ
N_KEYS = 2 * N_EXPERTS
TMG = 256
LANES = 128
SLAB = D_MODEL // 2 // LANES
VMEM_V7X = 64 * 1024 * 1024
VMEM_LIMIT = VMEM_V7X - 2 * 1024 * 1024


def _cdiv(a, b):
    return (a + b - 1) // b


def _rms_hat(x):
    return x * lax.rsqrt(jnp.mean(x * x, axis=-1, keepdims=True) + NORM_EPS)


def _pack_rows(xn):
    half = D_MODEL // 2
    return pltpu.pack_elementwise([xn[:, :half], xn[:, half:]], packed_dtype=jnp.bfloat16)


def _store_slabs(ref, words):
    m = words.shape[0]
    for k in range(SLAB):
        ref[pl.ds(k, m, stride=SLAB), :] = words[:, LANES * k:LANES * (k + 1)]


def _load_slabs(ref, m):
    return jnp.concatenate([ref[pl.ds(k, m, stride=SLAB), :] for k in range(SLAB)], axis=1)


def _unpack_words(words):
    lo = pltpu.unpack_elementwise(words, index=0, packed_dtype=jnp.bfloat16, unpacked_dtype=jnp.float32)
    hi = pltpu.unpack_elementwise(words, index=1, packed_dtype=jnp.bfloat16, unpacked_dtype=jnp.float32)
    return lo, hi


def _route(xn, wr_ref, br_ref, run_scr, tok_base, valid_limit):
    def split(a):
        hi = a.astype(jnp.bfloat16)
        return hi, (a - hi.astype(jnp.float32)).astype(jnp.bfloat16)

    def nt_dot(a, b):
        return lax.dot_general(a, b, (((1,), (1,)), ((), ())), preferred_element_type=jnp.float32)

    w_hi, w_lo = split(wr_ref[...])
    x_hi, x_lo = split(xn)
    logits = nt_dot(w_hi, x_hi) + nt_dot(w_hi, x_lo) + nt_dot(w_lo, x_hi) + br_ref[...]
    g = logits[0:N_GROUPS]
    gmax = jnp.max(g, axis=0, keepdims=True)
    rid_g = lax.broadcasted_iota(jnp.int32, g.shape, 0).astype(jnp.float32)
    g_idx = jnp.min(jnp.where(g == gmax, rid_g, float(N_GROUPS)), axis=0, keepdims=True).astype(jnp.int32)
    g_w = 1.0 / jnp.sum(jnp.exp(g - gmax), axis=0, keepdims=True)
    e_sel = logits[8:8 + EXPERTS_PER_GROUP]
    for gi in range(1, N_GROUPS):
        lo = 8 + EXPERTS_PER_GROUP * gi
        e_sel = jnp.where(g_idx == gi, logits[lo:lo + EXPERTS_PER_GROUP], e_sel)
    rid_e = lax.broadcasted_iota(jnp.int32, e_sel.shape, 0).astype(jnp.float32)
    none = float(EXPERTS_PER_GROUP)
    m1 = jnp.max(e_sel, axis=0, keepdims=True)
    i1f = jnp.min(jnp.where(e_sel == m1, rid_e, none), axis=0, keepdims=True)
    e_rest = jnp.where(rid_e == i1f, -jnp.inf, e_sel)
    m2 = jnp.max(e_rest, axis=0, keepdims=True)
    i2 = jnp.min(jnp.where(e_rest == m2, rid_e, none), axis=0, keepdims=True).astype(jnp.int32)
    i1 = i1f.astype(jnp.int32)
    ex = jnp.exp(m2 - m1)
    den = 1.0 / (1.0 + ex)
    w1 = den * g_w
    w2 = ex * den * g_w

    tok = tok_base + lax.broadcasted_iota(jnp.int32, (1, TM), 1)
    half = jnp.where(jnp.logical_and(tok >= HALF_SPLIT, tok < T_X), N_EXPERTS, 0)
    key1 = half + g_idx * EXPERTS_PER_GROUP + i1
    key2 = half + g_idx * EXPERTS_PER_GROUP + i2
    kid = lax.broadcasted_iota(jnp.int32, (N_KEYS, TM), 0)
    validf = jnp.where(tok < valid_limit, 1.0, 0.0)
    oh1 = jnp.where(kid == key1, validf, 0.0)
    oh2 = jnp.where(kid == key2, validf, 0.0)
    cnt = oh1 + oh2
    tri = jnp.where(lax.broadcasted_iota(jnp.int32, (TM, TM), 0)
                    <= lax.broadcasted_iota(jnp.int32, (TM, TM), 1), 1.0, 0.0).astype(jnp.bfloat16)
    cum = jnp.dot(cnt.astype(jnp.bfloat16), tri, preferred_element_type=jnp.float32)
    before = run_scr[...] + (cum - cnt)
    rank1 = jnp.sum(oh1 * before, axis=0, keepdims=True).astype(jnp.int32)
    rank2 = jnp.sum(oh2 * before, axis=0, keepdims=True).astype(jnp.int32)
    run_scr[...] = run_scr[...] + cum[:, TM - 1:TM]
    code = jnp.concatenate([key1 * 65536 + rank1, key2 * 65536 + rank2], axis=0)
    wts = jnp.concatenate([w1, w2], axis=0)
    return code, wts


def _ffn_prologue(h_new, g_ffn_ref, wr_ref, br_ref, run_scr, tok_base, valid_limit,
                  xp_ref, code_ref, wts_ref, cnt_ref):
    xn2 = _rms_hat(h_new) * g_ffn_ref[...]
    _store_slabs(xp_ref, _pack_rows(xn2))
    code, wts = _route(xn2, wr_ref, br_ref, run_scr, tok_base, valid_limit)
    code_ref[...] = code
    wts_ref[...] = wts
    cnt_ref[...] = jnp.broadcast_to(run_scr[...], cnt_ref.shape).astype(jnp.int32)


NC = 512


def _mixer0_kernel(x_ref, meta_ref, g_conv_ref, w_in_ref, cw_ref, w_out_ref,
                   g_ffn_ref, wr_ref, br_ref,
                   h1_ref, xp_ref, code_ref, wts_ref, cnt_ref,
                   h0_scr, acc_scr, carry_scr, meta_carry_scr, run_scr):
    s = pl.program_id(0)

    @pl.when(s == 0)
    def _():
        h0_scr[...] = jnp.zeros_like(h0_scr)
        h0_scr[0:N_META, :] = meta_ref[...]
        carry_scr[...] = jnp.zeros_like(carry_scr)
        meta_carry_scr[...] = jnp.zeros_like(meta_carry_scr)
        run_scr[...] = jnp.zeros_like(run_scr)

    @pl.when(s > 0)
    def _():
        h0_scr[...] = x_ref[...]

    @pl.when(s == 1 + N_XT // BATCH)
    def _():
        carry_scr[...] = meta_carry_scr[...]

    h0 = h0_scr[...]
    xn = (_rms_hat(h0) * g_conv_ref[...]).astype(jnp.bfloat16)
    row = lax.broadcasted_iota(jnp.int32, (TM, NC), 0)
    for c in range(D_MODEL // NC):
        cols = slice(NC * c, NC * (c + 1))
        gate_c = jnp.dot(xn, w_in_ref[:, D_MODEL + NC * c:D_MODEL + NC * (c + 1)],
                         preferred_element_type=jnp.float32)
        val = jnp.dot(xn, w_in_ref[:, 2 * D_MODEL + NC * c:2 * D_MODEL + NC * (c + 1)],
                      preferred_element_type=jnp.float32)
        u = gate_c * val
        tail = carry_scr[:, cols]
        c1 = tail[7:8, :]
        c2 = tail[6:7, :]
        um1 = jnp.where(row == 0, c1, pltpu.roll(u, 1, 0))
        um2 = jnp.where(row == 0, c2, jnp.where(row == 1, c1, pltpu.roll(u, 2, 0)))
        conv = um2 * cw_ref[0:1, cols] + um1 * cw_ref[1:2, cols] + u * cw_ref[2:3, cols]

        is_meta = s == 0
        meta_tail = u[N_META - 8:N_META, :]
        carry_scr[:, cols] = jnp.where(is_meta, meta_tail, u[TM - 8:TM, :])
        meta_carry_scr[:, cols] = jnp.where(is_meta, meta_tail, meta_carry_scr[:, cols])

        gate_b = jnp.dot(xn, w_in_ref[:, cols], preferred_element_type=jnp.float32)
        gated = (gate_b * conv).astype(jnp.bfloat16)
        part = jnp.dot(gated, w_out_ref[cols, :], preferred_element_type=jnp.float32)
        if c == 0:
            acc_scr[...] = h0 + part
        else:
            acc_scr[...] = acc_scr[...] + part

    h1 = acc_scr[...]
    h1_ref[...] = h1
    tile = jnp.where(s == 0, N_XT, s - 1)
    _ffn_prologue(h1, g_ffn_ref, wr_ref, br_ref, run_scr, tile * TM, T_VALID0,
                  xp_ref, code_ref, wts_ref, cnt_ref)


def _tile_first_meta(s):
    return jnp.where(s == 0, N_XT, s - 1)


_RESIDENT = pl.BlockSpec(memory_space=pltpu.VMEM)


def _mixer0(x2d, meta, g_conv, w_in, cw, w_out, g_ffn, wr, br):
    out_shape = [
        jax.ShapeDtypeStruct((T_PAD0, D_MODEL), jnp.float32),
        jax.ShapeDtypeStruct((T_PAD0 * SLAB, LANES), jnp.uint32),
        jax.ShapeDtypeStruct((2, T_PAD0), jnp.int32),
        jax.ShapeDtypeStruct((2, T_PAD0), jnp.float32),
        jax.ShapeDtypeStruct((N_KEYS, LANES), jnp.int32),
    ]
    return pl.pallas_call(
        _mixer0_kernel,
        grid=(N_XT + 1,),
        in_specs=[pl.BlockSpec((TM, D_MODEL), lambda s: (jnp.maximum(s - 1, 0), 0))] + [_RESIDENT] * 8,
        out_specs=[
            pl.BlockSpec((TM, D_MODEL), lambda s: (_tile_first_meta(s), 0)),
            pl.BlockSpec((TM * SLAB, LANES), lambda s: (_tile_first_meta(s), 0)),
            pl.BlockSpec((2, TM), lambda s: (0, _tile_first_meta(s))),
            pl.BlockSpec((2, TM), lambda s: (0, _tile_first_meta(s))),
            _RESIDENT,
        ],
        out_shape=out_shape,
        scratch_shapes=[
            pltpu.VMEM((TM, D_MODEL), jnp.float32),
            pltpu.VMEM((TM, D_MODEL), jnp.float32),
            pltpu.VMEM((8, D_MODEL), jnp.float32),
            pltpu.VMEM((8, D_MODEL), jnp.float32),
            pltpu.VMEM((N_KEYS, 1), jnp.float32),
        ],
        compiler_params=pltpu.CompilerParams(
            dimension_semantics=("arbitrary",), vmem_limit_bytes=VMEM_LIMIT),
        name="mixer0_route",
    )(x2d, meta, g_conv, w_in, cw, w_out, g_ffn, wr, br)


def _n_tiles(t_valid):
    return _cdiv(2 * t_valid, TMG)


def _n_visits(t_valid):
    return _n_tiles(t_valid) + N_KEYS


def _half0_rows(t_valid):
    return 2 * (HALF_SPLIT + t_valid - T_X)


def _half1_base(half0_rows):
    return half0_rows + TMG


def _sorted_rows(t_valid):
    return (_n_tiles(t_valid) + 2) * TMG


def _positions_kernel(t_pad, t_valid, cnt_ref, code_ref, pos_ref, off_ref, gstart_ref, vrow_ref, vgroup_ref,
                      vslot_ref, vnext_ref, nvis_ref, nexte_scr):
    n_vis = _n_visits(t_valid)

    def offs(g, acc):
        acc = jnp.where(g == N_EXPERTS, _half1_base(_half0_rows(t_valid)), acc)
        gstart_ref[g] = acc
        return acc + cnt_ref[g]

    total = lax.fori_loop(0, N_KEYS, offs, jnp.int32(0))
    gstart_ref[N_KEYS] = total

    def rows_of(e):
        return cnt_ref[e] + cnt_ref[e + N_EXPERTS]

    def next_nonempty(i, nxt):
        e = N_EXPERTS - 1 - i
        nexte_scr[e] = nxt
        return jnp.where(rows_of(e) > 0, e, nxt)

    lax.fori_loop(0, N_EXPERTS, next_nonempty, jnp.int32(-1))

    def per_expert(e, carry):
        v, last_g, rank = carry
        for half in range(2):
            g = e + half * N_EXPERTS
            c = cnt_ref[g]
            start = gstart_ref[g]

            def per_window(k, vv, g=g, start=start):
                vrow_ref[vv] = start + k * TMG
                vgroup_ref[vv] = g
                vslot_ref[vv] = rank % 2
                vnext_ref[vv] = nexte_scr[e]
                return vv + 1

            v = lax.fori_loop(0, (c + TMG - 1) // TMG, per_window, v)
            last_g = jnp.where(c > 0, g, last_g)
        return v, last_g, jnp.where(rows_of(e) > 0, rank + 1, rank)

    nvis, last_g, _ = lax.fori_loop(0, N_EXPERTS, per_expert, (jnp.int32(0), jnp.int32(0), jnp.int32(0)))
    nvis_ref[0] = nvis

    def pad(vv, c):
        vrow_ref[vv] = 0
        vgroup_ref[vv] = last_g
        vslot_ref[vv] = 0
        vnext_ref[vv] = -1
        return c

    lax.fori_loop(nvis, n_vis, pad, 0)

    code = code_ref[...]
    key = code >> 16
    pos = code & 0xFFFF
    for g in range(N_KEYS):
        pos = pos + jnp.where(key == g, gstart_ref[g], 0)
    pos_ref[...] = pos
    half_base = jnp.where(key >= N_EXPERTS, _half1_base(_half0_rows(t_valid)), 0)
    off_ref[...] = (pos - half_base) * SLAB


def _positions(cnt, code, t_pad, t_valid):
    n_vis = _n_visits(t_valid)
    smem = pl.BlockSpec(memory_space=pltpu.SMEM)
    vmem = pl.BlockSpec(memory_space=pltpu.VMEM)
    return pl.pallas_call(
        functools.partial(_positions_kernel, t_pad, t_valid),
        in_specs=[smem, vmem],
        out_specs=[vmem, vmem, smem, smem, smem, smem, smem, smem],
        out_shape=[
            jax.ShapeDtypeStruct((2, t_pad), jnp.int32),
            jax.ShapeDtypeStruct((2, t_pad), jnp.int32),
            jax.ShapeDtypeStruct((N_KEYS + 1,), jnp.int32),
            jax.ShapeDtypeStruct((n_vis,), jnp.int32),
            jax.ShapeDtypeStruct((n_vis,), jnp.int32),
            jax.ShapeDtypeStruct((n_vis,), jnp.int32),
            jax.ShapeDtypeStruct((n_vis,), jnp.int32),
            jax.ShapeDtypeStruct((1,), jnp.int32),
        ],
        scratch_shapes=[pltpu.SMEM((N_EXPERTS,), jnp.int32)],
        name="sort_positions",
    )(cnt, code)


SRC_UNROLL = 8


def _invert_kernel(t_valid, r_pad, pos1_ref, pos2_ref, src_ref):
    def slack(r, c):
        src_ref[r] = 0
        return c

    half0_rows = _half0_rows(t_valid)
    lax.fori_loop(half0_rows, _half1_base(half0_rows), slack, 0)
    lax.fori_loop(2 * t_valid + TMG, r_pad, slack, 0)

    def body(i, c):
        for j in range(SRC_UNROLL):
            t = i * SRC_UNROLL + j
            src_ref[pos1_ref[t]] = t
            src_ref[pos2_ref[t]] = t
        return c

    lax.fori_loop(0, t_valid // SRC_UNROLL, body, 0)


def _invert(pos1, pos2, t_valid):
    r_pad = _sorted_rows(t_valid)
    smem = pl.BlockSpec(memory_space=pltpu.SMEM)
    return pl.pallas_call(
        functools.partial(_invert_kernel, t_valid, r_pad),
        in_specs=[smem, smem],
        out_specs=smem,
        out_shape=jax.ShapeDtypeStruct((r_pad,), jnp.int32),
        name="sort_invert",
    )(pos1, pos2)


GATHER_UNROLL = 8


def _ffn_kernel(layer, half0_rows, vrow_ref, vgroup_ref, vslot_ref, vnext_ref, nvis_ref, src_ref,
                xp_hbm, wg_hbm, wu_hbm, wd_hbm, ys_hbm,
                xp_scr, wg_scr, wu_scr, wd_scr, wgb_scr, wub_scr, wdb_scr, xnext_scr, out_scr, xsem, wsem, osem):
    nvis = nvis_ref[0]

    def expert_of(visit):
        return vgroup_ref[visit] % N_EXPERTS

    def weight_copies(e, sl):
        return (pltpu.make_async_copy(wg_hbm.at[layer, e], wg_scr.at[sl], wsem.at[sl, 0]),
                pltpu.make_async_copy(wu_hbm.at[layer, e], wu_scr.at[sl], wsem.at[sl, 1]),
                pltpu.make_async_copy(wd_hbm.at[layer, e], wd_scr.at[sl], wsem.at[sl, 2]))

    def out_copy(visit):
        sl = visit % 2
        return pltpu.make_async_copy(
            out_scr.at[sl], ys_hbm.at[pl.ds(pl.multiple_of(vrow_ref[visit] * SLAB, SLAB), TMG * SLAB), :],
            osem.at[sl])

    def gather_rows(visit, unrolled):
        base = vrow_ref[visit]

        def one(r):
            tok = src_ref[base + r]
            xnext_scr[pl.ds(pl.multiple_of(r * SLAB, SLAB), SLAB), :] = (
                xp_scr[pl.ds(pl.multiple_of(tok * SLAB, SLAB), SLAB), :])

        if unrolled:
            for r in range(TMG):
                one(r)
        else:
            def chunk(i, c):
                for j in range(GATHER_UNROLL):
                    one(i * GATHER_UNROLL + j)
                return c

            lax.fori_loop(0, TMG // GATHER_UNROLL, chunk, 0)

    cp = pltpu.make_async_copy(xp_hbm, xp_scr, xsem)
    cp.start()
    for c in weight_copies(expert_of(0), vslot_ref[0]):
        c.start()
    out_scr[...] = jnp.zeros_like(out_scr)
    cap = ys_hbm.shape[0] // SLAB
    for first_row in (half0_rows, cap - 2 * TMG, cap - TMG):
        fill = pltpu.make_async_copy(
            out_scr.at[0], ys_hbm.at[pl.ds(first_row * SLAB, TMG * SLAB), :], osem.at[0])
        fill.start()
        fill.wait()
    cp.wait()
    gather_rows(0, False)

    def visit(v, carry):
        e = expert_of(v)
        slot = vslot_ref[v]

        @pl.when(jnp.logical_or(v == 0, e != expert_of(jnp.maximum(v - 1, 0))))
        def _():
            for c in weight_copies(e, slot):
                c.wait()
            nxt = vnext_ref[v]

            @pl.when(nxt >= 0)
            def _():
                for c in weight_copies(nxt, 1 - slot):
                    c.start()

            wgb_scr[...] = wg_scr[slot].astype(jnp.bfloat16)
            wub_scr[...] = wu_scr[slot].astype(jnp.bfloat16)
            wdb_scr[...] = wd_scr[slot].astype(jnp.bfloat16)

        lo, hi = _unpack_words(_load_slabs(xnext_scr, TMG))
        xs = jnp.concatenate([lo, hi], axis=1).astype(jnp.bfloat16)
        gather_rows(jnp.minimum(v + 1, nvis - 1), True)
        hg = jnp.dot(xs, wgb_scr[...], preferred_element_type=jnp.float32)
        hu = jnp.dot(xs, wub_scr[...], preferred_element_type=jnp.float32)
        hdn = (hg * jax.nn.sigmoid(hg) * hu).astype(jnp.bfloat16)
        y = jnp.dot(hdn, wdb_scr[...], preferred_element_type=jnp.float32)
        _store_slabs(out_scr.at[v % 2], _pack_rows(y))

        @pl.when(v > 0)
        def _():
            out_copy(v - 1).wait()

        out_copy(v).start()
        return carry

    lax.fori_loop(0, nvis, visit, 0)
    out_copy(nvis - 1).wait()


def _ffn(vrow, vgroup, vslot, vnext, nvis, src, xp, wg, wu, wd, layer, t_pad, t_valid):
    any_spec = pl.BlockSpec(memory_space=pl.ANY)
    return pl.pallas_call(
        functools.partial(_ffn_kernel, layer, _half0_rows(t_valid)),
        grid_spec=pltpu.PrefetchScalarGridSpec(
            num_scalar_prefetch=6,
            grid=(1,),
            in_specs=[any_spec, any_spec, any_spec, any_spec],
            out_specs=any_spec,
            scratch_shapes=[
                pltpu.VMEM((t_pad * SLAB, LANES), jnp.uint32),
                pltpu.VMEM((2, D_MODEL, D_EXPERT), jnp.float32),
                pltpu.VMEM((2, D_MODEL, D_EXPERT), jnp.float32),
                pltpu.VMEM((2, D_EXPERT, D_MODEL), jnp.float32),
                pltpu.VMEM((D_MODEL, D_EXPERT), jnp.bfloat16),
                pltpu.VMEM((D_MODEL, D_EXPERT), jnp.bfloat16),
                pltpu.VMEM((D_EXPERT, D_MODEL), jnp.bfloat16),
                pltpu.VMEM((TMG * SLAB, LANES), jnp.uint32),
                pltpu.VMEM((2, TMG * SLAB, LANES), jnp.uint32),
                pltpu.SemaphoreType.DMA,
                pltpu.SemaphoreType.DMA((2, 3)),
                pltpu.SemaphoreType.DMA((2,)),
            ],
        ),
        out_shape=jax.ShapeDtypeStruct((_sorted_rows(t_valid) * SLAB, LANES), jnp.uint32),
        compiler_params=pltpu.CompilerParams(
            dimension_semantics=("arbitrary",), vmem_limit_bytes=VMEM_LIMIT),
        name="expert_ffn",
    )(vrow, vgroup, vslot, vnext, nvis, src, xp, wg, wu, wd)


TILES_PER_HALF = HALF_SPLIT // TM


def _load_half(ys_hbm, ys_scr, sem, half, half0_rows, half1_rows):
    start, rows = (0, half0_rows) if half == 0 else (_half1_base(half0_rows), half1_rows)
    cp = pltpu.make_async_copy(ys_hbm.at[pl.ds(start * SLAB, rows * SLAB), :],
                               ys_scr.at[pl.ds(0, rows * SLAB), :], sem)
    cp.start()
    cp.wait()


def _gather_pairs(off1_ref, off2_ref, ys_scr, y1_scr, y2_scr, tok0, n, unrolled):
    def one(r):
        dst = pl.ds(pl.multiple_of(r * SLAB, SLAB), SLAB)
        y1_scr[dst, :] = ys_scr[pl.ds(pl.multiple_of(off1_ref[tok0 + r], SLAB), SLAB), :]
        y2_scr[dst, :] = ys_scr[pl.ds(pl.multiple_of(off2_ref[tok0 + r], SLAB), SLAB), :]

    if unrolled:
        for r in range(n):
            one(r)
    else:
        def chunk(i, c):
            for j in range(GATHER_UNROLL):
                one(i * GATHER_UNROLL + j)
            return c

        lax.fori_loop(0, n // GATHER_UNROLL, chunk, 0)


def _weighted_sum(y1_scr, y2_scr, w_ref, m):
    lo1, hi1 = _unpack_words(_load_slabs(y1_scr, m))
    lo2, hi2 = _unpack_words(_load_slabs(y2_scr, m))
    w1 = w_ref[0:m, 0:1]
    w2 = w_ref[0:m, 1:2]
    return jnp.concatenate([w1 * lo1 + w2 * lo2, w1 * hi1 + w2 * hi2], axis=1)


def _next_x_tile(tile):
    nxt = tile + 1
    return jnp.where(jnp.logical_or(nxt == TILES_PER_HALF, nxt == N_XT), tile, nxt)


def _final_kernel(half_rows, off1_ref, off2_ref, h_ref, w_ref, g_ref, ys_hbm,
                  o_ref, ys_scr, y1_scr, y2_scr, sem):
    tile = pl.program_id(0)
    gather = functools.partial(_gather_pairs, off1_ref, off2_ref, ys_scr, y1_scr, y2_scr)

    @pl.when(tile == 0)
    def _():
        _load_half(ys_hbm, ys_scr, sem, 0, half_rows, half_rows)
        gather(0, TM, False)

    @pl.when(tile == TILES_PER_HALF)
    def _():
        _load_half(ys_hbm, ys_scr, sem, 1, half_rows, half_rows)
        gather(TILES_PER_HALF * TM, TM, False)

    h = h_ref[...] + _weighted_sum(y1_scr, y2_scr, w_ref, TM)
    gather(_next_x_tile(tile) * TM, TM, True)
    o_ref[...] = _rms_hat(h) * g_ref[...]


def _final(pos1, pos2, h, wcols, g, ys):
    half_rows = 2 * HALF_SPLIT
    return pl.pallas_call(
        functools.partial(_final_kernel, half_rows),
        grid_spec=pltpu.PrefetchScalarGridSpec(
            num_scalar_prefetch=2,
            grid=(N_XT,),
            in_specs=[
                pl.BlockSpec((TM, D_MODEL), lambda i, p1, p2: (i, 0)),
                pl.BlockSpec((TM, 2), lambda i, p1, p2: (i, 0)),
                _RESIDENT,
                pl.BlockSpec(memory_space=pl.ANY),
            ],
            out_specs=pl.BlockSpec((TM, D_MODEL), lambda i, p1, p2: (i, 0)),
            scratch_shapes=[
                pltpu.VMEM((half_rows * SLAB, LANES), jnp.uint32),
                pltpu.VMEM((TM * SLAB, LANES), jnp.uint32),
                pltpu.VMEM((TM * SLAB, LANES), jnp.uint32),
                pltpu.SemaphoreType.DMA,
            ],
        ),
        out_shape=jax.ShapeDtypeStruct((T_X, D_MODEL), jnp.float32),
        compiler_params=pltpu.CompilerParams(
            dimension_semantics=("arbitrary",), vmem_limit_bytes=VMEM_LIMIT),
        name="moe_combine_final",
    )(pos1, pos2, h, wcols, g, ys)


QB = WINDOW
KV_W = N_KV_HEADS * HEAD_DIM
N_QB = TM // QB
META_ROW0 = QB - N_META


def _rope(x, cos, sin_signed):
    q = lax.broadcasted_iota(jnp.int32, x.shape, 1) // (HEAD_DIM // 2)
    swapped = jnp.where(q % 2 == 0, pltpu.roll(x, LANES - HEAD_DIM // 2, 1),
                        pltpu.roll(x, HEAD_DIM // 2, 1))
    return x * cos + swapped * sin_signed


def _dup_heads(blk):
    lane = lax.broadcasted_iota(jnp.int32, blk.shape, 1)
    rolled = pltpu.roll(blk, HEAD_DIM, 1)
    return jnp.where(lane < HEAD_DIM, blk, rolled), jnp.where(lane < HEAD_DIM, rolled, blk)


def _kv_rows(xhat, g_kv_ref, w_kv_ref, cos, sin_signed):
    xk = (xhat * g_kv_ref[...]).astype(jnp.bfloat16)
    kv = jnp.dot(xk, w_kv_ref[...], preferred_element_type=jnp.float32)
    ks, vs = [], []
    for b in range(KV_W // LANES):
        kb = _rope(kv[:, LANES * b:LANES * (b + 1)], cos, sin_signed)
        vb = kv[:, KV_W + LANES * b:KV_W + LANES * (b + 1)]
        ks.extend(_dup_heads(kb))
        vs.extend(_dup_heads(vb))
    return [k.astype(jnp.bfloat16) for k in ks], [v.astype(jnp.bfloat16) for v in vs]


HALF0_ROWS0 = 2 * (HALF_SPLIT + N_META)
HALF1_ROWS0 = 2 * HALF_SPLIT


def _attn_kernel(off1_ref, off2_ref,
                 h_ref, w_ref, ys_hbm, cos_ref, sin_ref, g_attn_ref, g_kv_ref, w_q_ref, w_kv_ref, w_o_ref,
                 sink_ref, g_ffn_ref, wr_ref, br_ref,
                 h3_ref, xp_ref, code_ref, wts_ref, cnt_ref,
                 k_scr, v_scr, mk_scr, mv_scr, q_scr, o_scr, bias_scr, h_scr, run_scr,
                 ys_scr, y1_scr, y2_scr, ysem):
    s = pl.program_id(0)
    tiles_per_batch = N_XT // BATCH
    gather = functools.partial(_gather_pairs, off1_ref, off2_ref, ys_scr, y1_scr, y2_scr)

    @pl.when(s == 0)
    def _():
        run_scr[...] = jnp.zeros_like(run_scr)
        _load_half(ys_hbm, ys_scr, ysem, 0, HALF0_ROWS0, HALF1_ROWS0)
        gather(N_XT * TM, N_META, True)
        h_meta = h_ref[0:N_META, :] + _weighted_sum(y1_scr, y2_scr, w_ref, N_META)
        gather(0, TM, False)
        xhat = _rms_hat(h_meta)
        ks, vs = _kv_rows(xhat, g_kv_ref, w_kv_ref, cos_ref[0:N_META, :], sin_ref[0:N_META, :])
        mk_scr[...] = jnp.zeros_like(mk_scr)
        mv_scr[...] = jnp.zeros_like(mv_scr)
        for kvh in range(N_KV_HEADS):
            mk_scr[kvh, META_ROW0:QB, :] = ks[kvh]
            mv_scr[kvh, META_ROW0:QB, :] = vs[kvh]
        qi = lax.broadcasted_iota(jnp.int32, (QB, 2 * QB), 0)
        kj = lax.broadcasted_iota(jnp.int32, (QB, 2 * QB), 1)
        band = jnp.logical_and(kj > qi, kj <= qi + QB)
        bias_scr[0] = jnp.where(band, 0.0, NEG_INF)
        bias_scr[1] = jnp.where(jnp.logical_and(band, kj >= META_ROW0), 0.0, NEG_INF)

    @pl.when(s > 0)
    def _():
        batch_first = jnp.logical_or(s == 1, s == 1 + tiles_per_batch)

        @pl.when(batch_first)
        def _():
            k_scr[:, 0:QB, :] = mk_scr[...]
            v_scr[:, 0:QB, :] = mv_scr[...]

        tile = s - 1

        @pl.when(tile == TILES_PER_HALF)
        def _():
            _load_half(ys_hbm, ys_scr, ysem, 1, HALF0_ROWS0, HALF1_ROWS0)
            gather(TILES_PER_HALF * TM, TM, False)

        h = h_ref[...] + _weighted_sum(y1_scr, y2_scr, w_ref, TM)
        h_scr[...] = h
        gather(_next_x_tile(tile) * TM, TM, True)
        xhat = _rms_hat(h)
        ks, vs = _kv_rows(xhat, g_kv_ref, w_kv_ref, cos_ref[...], sin_ref[...])
        for kvh in range(N_KV_HEADS):
            k_scr[kvh, QB:QB + TM, :] = ks[kvh]
            v_scr[kvh, QB:QB + TM, :] = vs[kvh]

        xq = (xhat * g_attn_ref[...]).astype(jnp.bfloat16)
        q = jnp.dot(xq, w_q_ref[...], preferred_element_type=jnp.float32)
        for hb in range(N_HEADS // 2):
            q_scr[hb] = q[:, LANES * hb:LANES * (hb + 1)]
        first_bias = jnp.where(batch_first, 1, 0)

        def head_pair(hb, carry):
            kvh = hb // (N_HEADS // N_KV_HEADS // 2)
            lane = lax.broadcasted_iota(jnp.int32, (QB, LANES), 1)
            q_all = _rope(q_scr[hb], cos_ref[...], sin_ref[...]) * (HEAD_DIM ** -0.5)
            for b in range(N_QB):
                qb = q_all[QB * b:QB * (b + 1), :]
                qs = jnp.concatenate([jnp.where(lane < HEAD_DIM, qb, 0.0),
                                      jnp.where(lane < HEAD_DIM, 0.0, qb)], axis=0).astype(jnp.bfloat16)
                kk = k_scr[kvh, QB * b:QB * (b + 2), :]
                vv = v_scr[kvh, QB * b:QB * (b + 2), :]
                sc = lax.dot_general(qs, kk, (((1,), (1,)), ((), ())),
                                     preferred_element_type=jnp.float32)
                bias = bias_scr[first_bias] if b == 0 else bias_scr[0]
                outs = []
                for j in range(2):
                    sj = sc[QB * j:QB * (j + 1), :] + bias
                    sink = sink_ref[2 * hb + j]
                    m = jnp.maximum(jnp.max(sj, axis=-1, keepdims=True), sink)
                    p = jnp.exp(sj - m)
                    den = jnp.sum(p, axis=-1, keepdims=True) + jnp.exp(sink - m)
                    pv = jnp.dot(p.astype(jnp.bfloat16), vv, preferred_element_type=jnp.float32)
                    outs.append(pv * (1.0 / den))
                o_scr[hb, QB * b:QB * (b + 1), :] = jnp.where(
                    lane < HEAD_DIM, outs[0], outs[1]).astype(jnp.bfloat16)
            return carry

        lax.fori_loop(0, N_HEADS // 2, head_pair, 0)

        k_scr[:, 0:QB, :] = k_scr[:, TM:TM + QB, :]
        v_scr[:, 0:QB, :] = v_scr[:, TM:TM + QB, :]

        o = jnp.concatenate([o_scr[hb] for hb in range(N_HEADS // 2)], axis=1)
        h3 = h_scr[...] + jnp.dot(o, w_o_ref[...], preferred_element_type=jnp.float32)
        h3_ref[...] = h3
        _ffn_prologue(h3, g_ffn_ref, wr_ref, br_ref, run_scr, (s - 1) * TM, T_X,
                      xp_ref, code_ref, wts_ref, cnt_ref)


def _attn(pos1, pos2, h1, wcols, ys, cos_t, sin_t, g_attn, g_kv, w_q, w_kv, w_o, sinks, g_ffn, wr, br):
    def tile_x(s, p1, p2):
        return jnp.maximum(s - 1, 0)

    def tile_in(s, p1, p2):
        return (_tile_first_meta(s), 0)

    def rope_tile(s, p1, p2):
        per_batch = N_XT // BATCH
        return (jnp.where(s == 0, per_batch, (s - 1) % per_batch), 0)

    out_shape = [
        jax.ShapeDtypeStruct((T_X, D_MODEL), jnp.float32),
        jax.ShapeDtypeStruct((T_X * SLAB, LANES), jnp.uint32),
        jax.ShapeDtypeStruct((2, T_X), jnp.int32),
        jax.ShapeDtypeStruct((2, T_X), jnp.float32),
        jax.ShapeDtypeStruct((N_KEYS, LANES), jnp.int32),
    ]
    return pl.pallas_call(
        _attn_kernel,
        grid_spec=pltpu.PrefetchScalarGridSpec(
            num_scalar_prefetch=2,
            grid=(N_XT + 1,),
            in_specs=[
                pl.BlockSpec((TM, D_MODEL), tile_in),
                pl.BlockSpec((TM, 2), tile_in),
                pl.BlockSpec(memory_space=pl.ANY),
                pl.BlockSpec((TM, LANES), rope_tile),
                pl.BlockSpec((TM, LANES), rope_tile),
                _RESIDENT, _RESIDENT, _RESIDENT, _RESIDENT, _RESIDENT,
                pl.BlockSpec(memory_space=pltpu.SMEM),
                _RESIDENT, _RESIDENT, _RESIDENT,
            ],
            out_specs=[
                pl.BlockSpec((TM, D_MODEL), lambda s, p1, p2: (tile_x(s, p1, p2), 0)),
                pl.BlockSpec((TM * SLAB, LANES), lambda s, p1, p2: (tile_x(s, p1, p2), 0)),
                pl.BlockSpec((2, TM), lambda s, p1, p2: (0, tile_x(s, p1, p2))),
                pl.BlockSpec((2, TM), lambda s, p1, p2: (0, tile_x(s, p1, p2))),
                _RESIDENT,
            ],
            scratch_shapes=[
                pltpu.VMEM((N_KV_HEADS, QB + TM, LANES), jnp.bfloat16),
                pltpu.VMEM((N_KV_HEADS, QB + TM, LANES), jnp.bfloat16),
                pltpu.VMEM((N_KV_HEADS, QB, LANES), jnp.bfloat16),
                pltpu.VMEM((N_KV_HEADS, QB, LANES), jnp.bfloat16),
                pltpu.VMEM((N_HEADS // 2, TM, LANES), jnp.float32),
                pltpu.VMEM((N_HEADS // 2, TM, LANES), jnp.bfloat16),
                pltpu.VMEM((2, QB, 2 * QB), jnp.float32),
                pltpu.VMEM((TM, D_MODEL), jnp.float32),
                pltpu.VMEM((N_KEYS, 1), jnp.float32),
                pltpu.VMEM((HALF0_ROWS0 * SLAB, LANES), jnp.uint32),
                pltpu.VMEM((TM * SLAB, LANES), jnp.uint32),
                pltpu.VMEM((TM * SLAB, LANES), jnp.uint32),
                pltpu.SemaphoreType.DMA,
            ],
        ),
        out_shape=out_shape,
        compiler_params=pltpu.CompilerParams(
            dimension_semantics=("arbitrary",), vmem_limit_bytes=VMEM_LIMIT),
        name="attn_route",
    )(pos1, pos2, h1, wcols, ys, cos_t, sin_t, g_attn, g_kv, w_q, w_kv, w_o, sinks, g_ffn, wr, br)


def _router_rows(rg_w, rg_b, re_w, re_b):
    wr = jnp.zeros((N_KEYS, D_MODEL), jnp.float32)
    wr = wr.at[0:N_GROUPS].set(rg_w.T).at[8:8 + N_EXPERTS].set(re_w.T)
    br = jnp.zeros((N_KEYS, 1), jnp.float32)
    br = br.at[0:N_GROUPS, 0].set(rg_b).at[8:8 + N_EXPERTS, 0].set(re_b)
    return wr, br


def _rope_tables():
    half = HEAD_DIM // 2
    inv_freq = ROPE_THETA ** (-jnp.arange(half, dtype=jnp.float32) / half)
    n_hi = _cdiv(N_META + SEQ, QB)
    ang_hi = (QB * jnp.arange(n_hi)).astype(jnp.float32)[:, None] * inv_freq[None, :]
    ang_lo = jnp.arange(QB).astype(jnp.float32)[:, None] * inv_freq[None, :]
    c_hi, s_hi = jnp.cos(ang_hi)[:, None, :], jnp.sin(ang_hi)[:, None, :]
    c_lo, s_lo = jnp.cos(ang_lo)[None, :, :], jnp.sin(ang_lo)[None, :, :]
    cos_p = (c_hi * c_lo - s_hi * s_lo).reshape(n_hi * QB, half)
    sin_p = (s_hi * c_lo + c_hi * s_lo).reshape(n_hi * QB, half)
    cos = jnp.concatenate([cos_p[N_META:N_META + SEQ], cos_p[0:TM]], axis=0)
    sin = jnp.concatenate([sin_p[N_META:N_META + SEQ], sin_p[0:TM]], axis=0)
    return jnp.tile(cos, (1, 4)), jnp.concatenate([-sin, sin, -sin, sin], axis=1)


def _moe(cnt, code, xp, wg, wu, wd, layer, t_pad, t_valid):
    pos, off, _, vrow, vgroup, vslot, vnext, nvis = _positions(cnt[:, 0], code, t_pad, t_valid)
    src = _invert(pos[0], pos[1], t_valid)
    ys = _ffn(vrow, vgroup, vslot, vnext, nvis, src, xp, wg, wu, wd, layer, t_pad, t_valid)
    return off, ys


def kernel(x, meta_tokens, conv_norm_g, conv_w_in, conv_w, conv_w_out, kv_norm_g, w_kv, attn_norm_g,
           w_q, w_o, sinks, ffn_norm_g, router_group_w, router_group_b, router_expert_w,
           router_expert_b, w_gate, w_up, w_down, final_norm_g):
    bf = jnp.bfloat16
    x2d = x.reshape(T_X, D_MODEL)
    wr0, br0 = _router_rows(router_group_w[0], router_group_b[0], router_expert_w[0], router_expert_b[0])
    wr1, br1 = _router_rows(router_group_w[1], router_group_b[1], router_expert_w[1], router_expert_b[1])

    h1, xp0, code0, wts0, cnt0 = _mixer0(
        x2d, meta_tokens, conv_norm_g[0].reshape(1, D_MODEL), conv_w_in[0].astype(bf), conv_w[0],
        conv_w_out[0].astype(bf), ffn_norm_g[0].reshape(1, D_MODEL), wr0, br0)
    pos0, ys0 = _moe(cnt0, code0, xp0, w_gate, w_up, w_down, 0, T_PAD0, T_VALID0)

    cos_t, sin_t = _rope_tables()
    h3, xp1, code1, wts1, cnt1 = _attn(
        pos0[0], pos0[1], h1, wts0.T, ys0, cos_t, sin_t, attn_norm_g[0].reshape(1, D_MODEL),
        kv_norm_g.reshape(1, D_MODEL), w_q[0].astype(bf), w_kv.astype(bf), w_o[0].astype(bf), sinks[0],
        ffn_norm_g[1].reshape(1, D_MODEL), wr1, br1)
    pos1, ys1 = _moe(cnt1, code1, xp1, w_gate, w_up, w_down, 1, T_X, T_X)
    out = _final(pos1[0], pos1[1], h3, wts1.T, final_norm_g.reshape(1, D_MODEL), ys1)
    return out.reshape(BATCH, SEQ, D_MODEL)
```

```python
import functools

import jax
import jax.numpy as jnp
from jax import lax
from jax.experimental import pallas as pl
from jax.experimental.pallas import tpu as pltpu

D_MODEL = 1024
BATCH = 2
SEQ = 8192
N_META = 16
N_HEADS = 16
HEAD_DIM = 64
N_KV_HEADS = 4
WINDOW = 128
ROPE_THETA = 10000.0
N_GROUPS = 4
EXPERTS_PER_GROUP = 8
N_EXPERTS = N_GROUPS * EXPERTS_PER_GROUP
D_EXPERT = 256
NORM_EPS = 1e-5
NEG_INF = -1e30

TM = 512
N_XT = BATCH * SEQ // TM
T_X = BATCH * SEQ
T_PAD0 = (N_XT + 1) * TM
T_VALID0 = T_X + N_META
HALF_SPLIT = SEQ
N_KEYS = 2 * N_EXPERTS
TMG = 256
LANES = 128
SLAB = D_MODEL // 2 // LANES
VMEM_V7X = 64 * 1024 * 1024
VMEM_LIMIT = VMEM_V7X - 1024 * 1024


def _cdiv(a, b):
    return (a + b - 1) // b


def _rms_hat(x):
    return x * lax.rsqrt(jnp.mean(x * x, axis=-1, keepdims=True) + NORM_EPS)


def _pack_rows(xn):
    half = D_MODEL // 2
    return pltpu.pack_elementwise([xn[:, :half], xn[:, half:]], packed_dtype=jnp.bfloat16)


def _store_slabs(ref, words):
    m = words.shape[0]
    for k in range(SLAB):
        ref[pl.ds(k, m, stride=SLAB), :] = words[:, LANES * k:LANES * (k + 1)]


def _load_slabs(ref, m):
    return jnp.concatenate([ref[pl.ds(k, m, stride=SLAB), :] for k in range(SLAB)], axis=1)


def _unpack_words(words):
    lo = pltpu.unpack_elementwise(words, index=0, packed_dtype=jnp.bfloat16, unpacked_dtype=jnp.float32)
    hi = pltpu.unpack_elementwise(words, index=1, packed_dtype=jnp.bfloat16, unpacked_dtype=jnp.float32)
    return lo, hi


def _route(xn, wr_ref, br_ref, run_scr, tok_base, valid_limit):
    def split(a):
        hi = a.astype(jnp.bfloat16)
        return hi, (a - hi.astype(jnp.float32)).astype(jnp.bfloat16)

    def nt_dot(a, b):
        return lax.dot_general(a, b, (((1,), (1,)), ((), ())), preferred_element_type=jnp.float32)

    w_hi, w_lo = split(wr_ref[...])
    x_hi, x_lo = split(xn)
    logits = nt_dot(w_hi, x_hi) + nt_dot(w_hi, x_lo) + nt_dot(w_lo, x_hi) + br_ref[...]
    g = logits[0:N_GROUPS]
    gmax = jnp.max(g, axis=0, keepdims=True)
    rid_g = lax.broadcasted_iota(jnp.int32, g.shape, 0).astype(jnp.float32)
    g_idx = jnp.min(jnp.where(g == gmax, rid_g, float(N_GROUPS)), axis=0, keepdims=True).astype(jnp.int32)
    g_w = 1.0 / jnp.sum(jnp.exp(g - gmax), axis=0, keepdims=True)
    e_sel = logits[8:8 + EXPERTS_PER_GROUP]
    for gi in range(1, N_GROUPS):
        lo = 8 + EXPERTS_PER_GROUP * gi
        e_sel = jnp.where(g_idx == gi, logits[lo:lo + EXPERTS_PER_GROUP], e_sel)
    rid_e = lax.broadcasted_iota(jnp.int32, e_sel.shape, 0).astype(jnp.float32)
    none = float(EXPERTS_PER_GROUP)
    m1 = jnp.max(e_sel, axis=0, keepdims=True)
    i1f = jnp.min(jnp.where(e_sel == m1, rid_e, none), axis=0, keepdims=True)
    e_rest = jnp.where(rid_e == i1f, -jnp.inf, e_sel)
    m2 = jnp.max(e_rest, axis=0, keepdims=True)
    i2 = jnp.min(jnp.where(e_rest == m2, rid_e, none), axis=0, keepdims=True).astype(jnp.int32)
    i1 = i1f.astype(jnp.int32)
    ex = jnp.exp(m2 - m1)
    den = 1.0 / (1.0 + ex)
    w1 = den * g_w
    w2 = ex * den * g_w

    tok = tok_base + lax.broadcasted_iota(jnp.int32, (1, TM), 1)
    half = jnp.where(jnp.logical_and(tok >= HALF_SPLIT, tok < T_X), N_EXPERTS, 0)
    key1 = half + g_idx * EXPERTS_PER_GROUP + i1
    key2 = half + g_idx * EXPERTS_PER_GROUP + i2
    kid = lax.broadcasted_iota(jnp.int32, (N_KEYS, TM), 0)
    validf = jnp.where(tok < valid_limit, 1.0, 0.0)
    oh1 = jnp.where(kid == key1, validf, 0.0)
    oh2 = jnp.where(kid == key2, validf, 0.0)
    cnt = oh1 + oh2
    tri = jnp.where(lax.broadcasted_iota(jnp.int32, (TM, TM), 0)
                    <= lax.broadcasted_iota(jnp.int32, (TM, TM), 1), 1.0, 0.0).astype(jnp.bfloat16)
    cum = jnp.dot(cnt.astype(jnp.bfloat16), tri, preferred_element_type=jnp.float32)
    before = run_scr[...] + (cum - cnt)
    rank1 = jnp.sum(oh1 * before, axis=0, keepdims=True).astype(jnp.int32)
    rank2 = jnp.sum(oh2 * before, axis=0, keepdims=True).astype(jnp.int32)
    run_scr[...] = run_scr[...] + cum[:, TM - 1:TM]
    code = jnp.concatenate([key1 * 65536 + rank1, key2 * 65536 + rank2], axis=0)
    wts = jnp.concatenate([w1, w2], axis=0)
    return code, wts


W_COLS = 8


def _ffn_prologue(h_new, g_ffn_ref, wr_ref, br_ref, run_scr, tok_base, valid_limit,
                  xp_ref, code_ref, wts_ref, cnt_ref):
    xn2 = _rms_hat(h_new) * g_ffn_ref[...]
    _store_slabs(xp_ref, _pack_rows(xn2))
    code, wts = _route(xn2, wr_ref, br_ref, run_scr, tok_base, valid_limit)
    code_ref[...] = code
    wts_ref[...] = jnp.concatenate([wts, jnp.zeros((W_COLS - 2, TM), jnp.float32)], axis=0).T
    cnt_ref[...] = jnp.broadcast_to(run_scr[...], cnt_ref.shape).astype(jnp.int32)


NC = 512


def _mixer0_kernel(x_ref, meta_ref, g_conv_ref, w_in_ref, cw_ref, w_out_ref,
                   g_ffn_ref, wr_ref, br_ref,
                   h1_ref, xp_ref, code_ref, wts_ref, cnt_ref,
                   h0_scr, acc_scr, carry_scr, meta_carry_scr, run_scr):
    s = pl.program_id(0)

    @pl.when(s == 0)
    def _():
        h0_scr[...] = jnp.zeros_like(h0_scr)
        h0_scr[0:N_META, :] = meta_ref[...]
        carry_scr[...] = jnp.zeros_like(carry_scr)
        meta_carry_scr[...] = jnp.zeros_like(meta_carry_scr)
        run_scr[...] = jnp.zeros_like(run_scr)

    @pl.when(s > 0)
    def _():
        h0_scr[...] = x_ref[...]

    @pl.when(s == 1 + N_XT // BATCH)
    def _():
        carry_scr[...] = meta_carry_scr[...]

    h0 = h0_scr[...]
    xn = (_rms_hat(h0) * g_conv_ref[...]).astype(jnp.bfloat16)
    row = lax.broadcasted_iota(jnp.int32, (TM, NC), 0)
    for c in range(D_MODEL // NC):
        cols = slice(NC * c, NC * (c + 1))
        gate_c = jnp.dot(xn, w_in_ref[:, D_MODEL + NC * c:D_MODEL + NC * (c + 1)],
                         preferred_element_type=jnp.float32)
        val = jnp.dot(xn, w_in_ref[:, 2 * D_MODEL + NC * c:2 * D_MODEL + NC * (c + 1)],
                      preferred_element_type=jnp.float32)
        u = gate_c * val
        tail = carry_scr[:, cols]
        c1 = tail[7:8, :]
        c2 = tail[6:7, :]
        um1 = jnp.where(row == 0, c1, pltpu.roll(u, 1, 0))
        um2 = jnp.where(row == 0, c2, jnp.where(row == 1, c1, pltpu.roll(u, 2, 0)))
        conv = um2 * cw_ref[0:1, cols] + um1 * cw_ref[1:2, cols] + u * cw_ref[2:3, cols]

        is_meta = s == 0
        meta_tail = u[N_META - 8:N_META, :]
        carry_scr[:, cols] = jnp.where(is_meta, meta_tail, u[TM - 8:TM, :])
        meta_carry_scr[:, cols] = jnp.where(is_meta, meta_tail, meta_carry_scr[:, cols])

        gate_b = jnp.dot(xn, w_in_ref[:, cols], preferred_element_type=jnp.float32)
        gated = (gate_b * conv).astype(jnp.bfloat16)
        part = jnp.dot(gated, w_out_ref[cols, :], preferred_element_type=jnp.float32)
        if c == 0:
            acc_scr[...] = h0 + part
        else:
            acc_scr[...] = acc_scr[...] + part

    h1 = acc_scr[...]
    h1_ref[...] = h1
    tile = jnp.where(s == 0, N_XT, s - 1)
    _ffn_prologue(h1, g_ffn_ref, wr_ref, br_ref, run_scr, tile * TM, T_VALID0,
                  xp_ref, code_ref, wts_ref, cnt_ref)


def _tile_first_meta(s):
    return jnp.where(s == 0, N_XT, s - 1)


_RESIDENT = pl.BlockSpec(memory_space=pltpu.VMEM)


def _mixer0(x2d, meta, g_conv, w_in, cw, w_out, g_ffn, wr, br):
    out_shape = [
        jax.ShapeDtypeStruct((T_PAD0, D_MODEL), jnp.float32),
        jax.ShapeDtypeStruct((T_PAD0 * SLAB, LANES), jnp.uint32),
        jax.ShapeDtypeStruct((2, T_PAD0), jnp.int32),
        jax.ShapeDtypeStruct((T_PAD0, W_COLS), jnp.float32),
        jax.ShapeDtypeStruct((N_KEYS, LANES), jnp.int32),
    ]
    return pl.pallas_call(
        _mixer0_kernel,
        grid=(N_XT + 1,),
        in_specs=[pl.BlockSpec((TM, D_MODEL), lambda s: (jnp.maximum(s - 1, 0), 0))] + [_RESIDENT] * 8,
        out_specs=[
            pl.BlockSpec((TM, D_MODEL), lambda s: (_tile_first_meta(s), 0)),
            pl.BlockSpec((TM * SLAB, LANES), lambda s: (_tile_first_meta(s), 0)),
            pl.BlockSpec((2, TM), lambda s: (0, _tile_first_meta(s))),
            pl.BlockSpec((TM, W_COLS), lambda s: (_tile_first_meta(s), 0)),
            _RESIDENT,
        ],
        out_shape=out_shape,
        scratch_shapes=[
            pltpu.VMEM((TM, D_MODEL), jnp.float32),
            pltpu.VMEM((TM, D_MODEL), jnp.float32),
            pltpu.VMEM((8, D_MODEL), jnp.float32),
            pltpu.VMEM((8, D_MODEL), jnp.float32),
            pltpu.VMEM((N_KEYS, 1), jnp.float32),
        ],
        compiler_params=pltpu.CompilerParams(
            dimension_semantics=("arbitrary",), vmem_limit_bytes=VMEM_LIMIT),
        name="mixer0_route",
    )(x2d, meta, g_conv, w_in, cw, w_out, g_ffn, wr, br)


def _n_tiles(t_valid):
    return _cdiv(2 * t_valid, TMG)


def _n_visits(t_valid):
    return _n_tiles(t_valid) + N_KEYS


def _half0_rows(t_valid):
    return 2 * (HALF_SPLIT + t_valid - T_X)


def _half1_base(half0_rows):
    return half0_rows + TMG


def _sorted_rows(t_valid):
    return (_n_tiles(t_valid) + 2) * TMG


def _positions_kernel(t_pad, t_valid, cnt_ref, code_ref, pos_ref, off_ref, gstart_ref, vrow_ref, vgroup_ref,
                      vslot_ref, vnext_ref, nvis_ref, nexte_scr):
    n_vis = _n_visits(t_valid)

    def offs(g, acc):
        acc = jnp.where(g == N_EXPERTS, _half1_base(_half0_rows(t_valid)), acc)
        gstart_ref[g] = acc
        return acc + cnt_ref[g]

    total = lax.fori_loop(0, N_KEYS, offs, jnp.int32(0))
    gstart_ref[N_KEYS] = total

    def rows_of(e):
        return cnt_ref[e] + cnt_ref[e + N_EXPERTS]

    def next_nonempty(i, nxt):
        e = N_EXPERTS - 1 - i
        nexte_scr[e] = nxt
        return jnp.where(rows_of(e) > 0, e, nxt)

    lax.fori_loop(0, N_EXPERTS, next_nonempty, jnp.int32(-1))

    def per_expert(e, carry):
        v, last_g, rank = carry
        for half in range(2):
            g = e + half * N_EXPERTS
            c = cnt_ref[g]
            start = gstart_ref[g]

            def per_window(k, vv, g=g, start=start):
                vrow_ref[vv] = start + k * TMG
                vgroup_ref[vv] = g
                vslot_ref[vv] = rank % 2
                vnext_ref[vv] = nexte_scr[e]
                return vv + 1

            v = lax.fori_loop(0, (c + TMG - 1) // TMG, per_window, v)
            last_g = jnp.where(c > 0, g, last_g)
        return v, last_g, jnp.where(rows_of(e) > 0, rank + 1, rank)

    nvis, last_g, _ = lax.fori_loop(0, N_EXPERTS, per_expert, (jnp.int32(0), jnp.int32(0), jnp.int32(0)))
    nvis_ref[0] = nvis

    def pad(vv, c):
        vrow_ref[vv] = 0
        vgroup_ref[vv] = last_g
        vslot_ref[vv] = 0
        vnext_ref[vv] = -1
        return c

    lax.fori_loop(nvis, n_vis, pad, 0)

    code = code_ref[...]
    key = code >> 16
    pos = code & 0xFFFF
    for g in range(N_KEYS):
        pos = pos + jnp.where(key == g, gstart_ref[g], 0)
    pos_ref[...] = pos
    half_base = jnp.where(key >= N_EXPERTS, _half1_base(_half0_rows(t_valid)), 0)
    off_ref[...] = (pos - half_base) * SLAB


def _positions(cnt, code, t_pad, t_valid):
    n_vis = _n_visits(t_valid)
    smem = pl.BlockSpec(memory_space=pltpu.SMEM)
    vmem = pl.BlockSpec(memory_space=pltpu.VMEM)
    return pl.pallas_call(
        functools.partial(_positions_kernel, t_pad, t_valid),
        in_specs=[smem, vmem],
        out_specs=[vmem, vmem, smem, smem, smem, smem, smem, smem],
        out_shape=[
            jax.ShapeDtypeStruct((2, t_pad), jnp.int32),
            jax.ShapeDtypeStruct((2, t_pad), jnp.int32),
            jax.ShapeDtypeStruct((N_KEYS + 1,), jnp.int32),
            jax.ShapeDtypeStruct((n_vis,), jnp.int32),
            jax.ShapeDtypeStruct((n_vis,), jnp.int32),
            jax.ShapeDtypeStruct((n_vis,), jnp.int32),
            jax.ShapeDtypeStruct((n_vis,), jnp.int32),
            jax.ShapeDtypeStruct((1,), jnp.int32),
        ],
        scratch_shapes=[pltpu.SMEM((N_EXPERTS,), jnp.int32)],
        name="sort_positions",
    )(cnt, code)


SRC_UNROLL = 8


def _invert_kernel(t_valid, r_pad, pos1_ref, pos2_ref, src_ref):
    def slack(r, c):
        src_ref[r] = 0
        return c

    half0_rows = _half0_rows(t_valid)
    lax.fori_loop(half0_rows, _half1_base(half0_rows), slack, 0)
    lax.fori_loop(2 * t_valid + TMG, r_pad, slack, 0)

    def body(i, c):
        for j in range(SRC_UNROLL):
            t = i * SRC_UNROLL + j
            src_ref[pos1_ref[t]] = t
            src_ref[pos2_ref[t]] = t
        return c

    lax.fori_loop(0, t_valid // SRC_UNROLL, body, 0)


def _invert(pos1, pos2, t_valid):
    r_pad = _sorted_rows(t_valid)
    smem = pl.BlockSpec(memory_space=pltpu.SMEM)
    return pl.pallas_call(
        functools.partial(_invert_kernel, t_valid, r_pad),
        in_specs=[smem, smem],
        out_specs=smem,
        out_shape=jax.ShapeDtypeStruct((r_pad,), jnp.int32),
        name="sort_invert",
    )(pos1, pos2)


GATHER_UNROLL = 8


def _ffn_kernel(layer, half0_rows, vrow_ref, vgroup_ref, vslot_ref, vnext_ref, nvis_ref, src_ref,
                xp_hbm, wg_hbm, wu_hbm, wd_hbm, ys_hbm,
                xp_scr, wg_scr, wu_scr, wd_scr, xnext_scr, out_scr, xsem, wsem, osem):
    nvis = nvis_ref[0]

    def expert_of(visit):
        return vgroup_ref[visit] % N_EXPERTS

    def weight_copies(e, sl):
        return (pltpu.make_async_copy(wg_hbm.at[layer, e], wg_scr.at[sl], wsem.at[sl, 0]),
                pltpu.make_async_copy(wu_hbm.at[layer, e], wu_scr.at[sl], wsem.at[sl, 1]),
                pltpu.make_async_copy(wd_hbm.at[layer, e], wd_scr.at[sl], wsem.at[sl, 2]))

    def out_copy(visit):
        sl = visit % 2
        return pltpu.make_async_copy(
            out_scr.at[sl], ys_hbm.at[pl.ds(pl.multiple_of(vrow_ref[visit] * SLAB, SLAB), TMG * SLAB), :],
            osem.at[sl])

    def gather_rows(visit, unrolled):
        base = vrow_ref[visit]

        def one(r):
            tok = src_ref[base + r]
            xnext_scr[pl.ds(pl.multiple_of(r * SLAB, SLAB), SLAB), :] = (
                xp_scr[pl.ds(pl.multiple_of(tok * SLAB, SLAB), SLAB), :])

        if unrolled:
            for r in range(TMG):
                one(r)
        else:
            def chunk(i, c):
                for j in range(GATHER_UNROLL):
                    one(i * GATHER_UNROLL + j)
                return c

            lax.fori_loop(0, TMG // GATHER_UNROLL, chunk, 0)

    cp = pltpu.make_async_copy(xp_hbm, xp_scr, xsem)
    cp.start()
    for c in weight_copies(expert_of(0), vslot_ref[0]):
        c.start()
    out_scr[...] = jnp.zeros_like(out_scr)
    cap = ys_hbm.shape[0] // SLAB
    for first_row in (half0_rows, cap - 2 * TMG, cap - TMG):
        fill = pltpu.make_async_copy(
            out_scr.at[0], ys_hbm.at[pl.ds(first_row * SLAB, TMG * SLAB), :], osem.at[0])
        fill.start()
        fill.wait()
    cp.wait()
    gather_rows(0, False)

    def visit(v, carry):
        e = expert_of(v)
        slot = vslot_ref[v]

        @pl.when(jnp.logical_or(v == 0, e != expert_of(jnp.maximum(v - 1, 0))))
        def _():
            for c in weight_copies(e, slot):
                c.wait()
            nxt = vnext_ref[v]

            @pl.when(nxt >= 0)
            def _():
                for c in weight_copies(nxt, 1 - slot):
                    c.start()

        lo, hi = _unpack_words(_load_slabs(xnext_scr, TMG))
        xs = jnp.concatenate([lo, hi], axis=1).astype(jnp.bfloat16)
        gather_rows(jnp.minimum(v + 1, nvis - 1), True)
        hg = jnp.dot(xs, wg_scr[slot].astype(jnp.bfloat16), preferred_element_type=jnp.float32)
        hu = jnp.dot(xs, wu_scr[slot].astype(jnp.bfloat16), preferred_element_type=jnp.float32)
        hdn = (hg * jax.nn.sigmoid(hg) * hu).astype(jnp.bfloat16)
        y = jnp.dot(hdn, wd_scr[slot].astype(jnp.bfloat16), preferred_element_type=jnp.float32)
        _store_slabs(out_scr.at[v % 2], _pack_rows(y))

        @pl.when(v > 0)
        def _():
            out_copy(v - 1).wait()

        out_copy(v).start()
        return carry

    lax.fori_loop(0, nvis, visit, 0)
    out_copy(nvis - 1).wait()


def _ffn(vrow, vgroup, vslot, vnext, nvis, src, xp, wg, wu, wd, layer, t_pad, t_valid):
    any_spec = pl.BlockSpec(memory_space=pl.ANY)
    return pl.pallas_call(
        functools.partial(_ffn_kernel, layer, _half0_rows(t_valid)),
        grid_spec=pltpu.PrefetchScalarGridSpec(
            num_scalar_prefetch=6,
            grid=(1,),
            in_specs=[any_spec, any_spec, any_spec, any_spec],
            out_specs=any_spec,
            scratch_shapes=[
                pltpu.VMEM((t_pad * SLAB, LANES), jnp.uint32),
                pltpu.VMEM((2, D_MODEL, D_EXPERT), jnp.float32),
                pltpu.VMEM((2, D_MODEL, D_EXPERT), jnp.float32),
                pltpu.VMEM((2, D_EXPERT, D_MODEL), jnp.float32),
                pltpu.VMEM((TMG * SLAB, LANES), jnp.uint32),
                pltpu.VMEM((2, TMG * SLAB, LANES), jnp.uint32),
                pltpu.SemaphoreType.DMA,
                pltpu.SemaphoreType.DMA((2, 3)),
                pltpu.SemaphoreType.DMA((2,)),
            ],
        ),
        out_shape=jax.ShapeDtypeStruct((_sorted_rows(t_valid) * SLAB, LANES), jnp.uint32),
        compiler_params=pltpu.CompilerParams(
            dimension_semantics=("arbitrary",), vmem_limit_bytes=VMEM_LIMIT),
        name="expert_ffn",
    )(vrow, vgroup, vslot, vnext, nvis, src, xp, wg, wu, wd)


TILES_PER_HALF = HALF_SPLIT // TM


def _load_half(ys_hbm, ys_scr, sem, half, half0_rows, half1_rows):
    start, rows = (0, half0_rows) if half == 0 else (_half1_base(half0_rows), half1_rows)
    cp = pltpu.make_async_copy(ys_hbm.at[pl.ds(start * SLAB, rows * SLAB), :],
                               ys_scr.at[pl.ds(0, rows * SLAB), :], sem)
    cp.start()
    cp.wait()


def _gather_pairs(off1_ref, off2_ref, ys_scr, y1_scr, y2_scr, tok0, n, unrolled):
    def one(r):
        dst = pl.ds(pl.multiple_of(r * SLAB, SLAB), SLAB)
        y1_scr[dst, :] = ys_scr[pl.ds(pl.multiple_of(off1_ref[tok0 + r], SLAB), SLAB), :]
        y2_scr[dst, :] = ys_scr[pl.ds(pl.multiple_of(off2_ref[tok0 + r], SLAB), SLAB), :]

    if unrolled:
        for r in range(n):
            one(r)
    else:
        def chunk(i, c):
            for j in range(GATHER_UNROLL):
                one(i * GATHER_UNROLL + j)
            return c

        lax.fori_loop(0, n // GATHER_UNROLL, chunk, 0)


def _weighted_sum(y1_scr, y2_scr, w_ref, m):
    lo1, hi1 = _unpack_words(_load_slabs(y1_scr, m))
    lo2, hi2 = _unpack_words(_load_slabs(y2_scr, m))
    w1 = w_ref[0:m, 0:1]
    w2 = w_ref[0:m, 1:2]
    return jnp.concatenate([w1 * lo1 + w2 * lo2, w1 * hi1 + w2 * hi2], axis=1)


def _next_x_tile(tile):
    nxt = tile + 1
    return jnp.where(jnp.logical_or(nxt == TILES_PER_HALF, nxt == N_XT), tile, nxt)


def _final_kernel(half_rows, off1_ref, off2_ref, h_ref, w_ref, g_ref, ys_hbm,
                  o_ref, ys_scr, y1_scr, y2_scr, sem):
    tile = pl.program_id(0)
    gather = functools.partial(_gather_pairs, off1_ref, off2_ref, ys_scr, y1_scr, y2_scr)

    @pl.when(tile == 0)
    def _():
        _load_half(ys_hbm, ys_scr, sem, 0, half_rows, half_rows)
        gather(0, TM, False)

    @pl.when(tile == TILES_PER_HALF)
    def _():
        _load_half(ys_hbm, ys_scr, sem, 1, half_rows, half_rows)
        gather(TILES_PER_HALF * TM, TM, False)

    h = h_ref[...] + _weighted_sum(y1_scr, y2_scr, w_ref, TM)
    gather(_next_x_tile(tile) * TM, TM, True)
    o_ref[...] = _rms_hat(h) * g_ref[...]


def _final(pos1, pos2, h, wcols, g, ys):
    half_rows = 2 * HALF_SPLIT
    return pl.pallas_call(
        functools.partial(_final_kernel, half_rows),
        grid_spec=pltpu.PrefetchScalarGridSpec(
            num_scalar_prefetch=2,
            grid=(N_XT,),
            in_specs=[
                pl.BlockSpec((TM, D_MODEL), lambda i, p1, p2: (i, 0)),
                pl.BlockSpec((TM, W_COLS), lambda i, p1, p2: (i, 0)),
                _RESIDENT,
                pl.BlockSpec(memory_space=pl.ANY),
            ],
            out_specs=pl.BlockSpec((TM, D_MODEL), lambda i, p1, p2: (i, 0)),
            scratch_shapes=[
                pltpu.VMEM((half_rows * SLAB, LANES), jnp.uint32),
                pltpu.VMEM((TM * SLAB, LANES), jnp.uint32),
                pltpu.VMEM((TM * SLAB, LANES), jnp.uint32),
                pltpu.SemaphoreType.DMA,
            ],
        ),
        out_shape=jax.ShapeDtypeStruct((T_X, D_MODEL), jnp.float32),
        compiler_params=pltpu.CompilerParams(
            dimension_semantics=("arbitrary",), vmem_limit_bytes=VMEM_LIMIT),
        name="moe_combine_final",
    )(pos1, pos2, h, wcols, g, ys)


QB = WINDOW
KV_W = N_KV_HEADS * HEAD_DIM
N_QB = TM // QB
META_ROW0 = QB - N_META


def _rope(x, cos, sin_signed):
    q = lax.broadcasted_iota(jnp.int32, x.shape, 1) // (HEAD_DIM // 2)
    swapped = jnp.where(q % 2 == 0, pltpu.roll(x, LANES - HEAD_DIM // 2, 1),
                        pltpu.roll(x, HEAD_DIM // 2, 1))
    return x * cos + swapped * sin_signed


def _dup_heads(blk):
    lane = lax.broadcasted_iota(jnp.int32, blk.shape, 1)
    rolled = pltpu.roll(blk, HEAD_DIM, 1)
    return jnp.where(lane < HEAD_DIM, blk, rolled), jnp.where(lane < HEAD_DIM, rolled, blk)


def _kv_rows(xhat, g_kv_ref, w_kv_ref, cos, sin_signed):
    xk = (xhat * g_kv_ref[...]).astype(jnp.bfloat16)
    kv = jnp.dot(xk, w_kv_ref[...], preferred_element_type=jnp.float32)
    ks, vs = [], []
    for b in range(KV_W // LANES):
        kb = _rope(kv[:, LANES * b:LANES * (b + 1)], cos, sin_signed)
        vb = kv[:, KV_W + LANES * b:KV_W + LANES * (b + 1)]
        ks.extend(_dup_heads(kb))
        vs.extend(_dup_heads(vb))
    return [k.astype(jnp.bfloat16) for k in ks], [v.astype(jnp.bfloat16) for v in vs]


HALF0_ROWS0 = 2 * (HALF_SPLIT + N_META)
HALF1_ROWS0 = 2 * HALF_SPLIT


def _attn_kernel(off1_ref, off2_ref,
                 h_ref, w_ref, ys_hbm, cos_ref, sin_ref, cosm_ref, sinm_ref, g_attn_ref, g_kv_ref, w_q_ref,
                 w_kv_ref, w_o_ref, sink_ref, g_ffn_ref, wr_ref, br_ref,
                 h3_ref, xp_ref, code_ref, wts_ref, cnt_ref,
                 k_scr, v_scr, mk_scr, mv_scr, q_scr, o_scr, bias_scr, h_scr, run_scr,
                 ys_scr, y1_scr, y2_scr, ysem):
    s = pl.program_id(0)
    tiles_per_batch = N_XT // BATCH
    gather = functools.partial(_gather_pairs, off1_ref, off2_ref, ys_scr, y1_scr, y2_scr)

    @pl.when(s == 0)
    def _():
        run_scr[...] = jnp.zeros_like(run_scr)
        _load_half(ys_hbm, ys_scr, ysem, 0, HALF0_ROWS0, HALF1_ROWS0)
        gather(N_XT * TM, N_META, True)
        h_meta = h_ref[0:N_META, :] + _weighted_sum(y1_scr, y2_scr, w_ref, N_META)
        gather(0, TM, False)
        xhat = _rms_hat(h_meta)
        ks, vs = _kv_rows(xhat, g_kv_ref, w_kv_ref, cosm_ref[...], sinm_ref[...])
        mk_scr[...] = jnp.zeros_like(mk_scr)
        mv_scr[...] = jnp.zeros_like(mv_scr)
        for kvh in range(N_KV_HEADS):
            mk_scr[kvh, META_ROW0:QB, :] = ks[kvh]
            mv_scr[kvh, META_ROW0:QB, :] = vs[kvh]
        qi = lax.broadcasted_iota(jnp.int32, (QB, 2 * QB), 0)
        kj = lax.broadcasted_iota(jnp.int32, (QB, 2 * QB), 1)
        band = jnp.logical_and(kj > qi, kj <= qi + QB)
        bias_scr[0] = jnp.where(band, 0.0, NEG_INF)
        bias_scr[1] = jnp.where(jnp.logical_and(band, kj >= META_ROW0), 0.0, NEG_INF)

    @pl.when(s > 0)
    def _():
        batch_first = jnp.logical_or(s == 1, s == 1 + tiles_per_batch)

        @pl.when(batch_first)
        def _():
            k_scr[:, 0:QB, :] = mk_scr[...]
            v_scr[:, 0:QB, :] = mv_scr[...]

        tile = s - 1

        @pl.when(tile == TILES_PER_HALF)
        def _():
            _load_half(ys_hbm, ys_scr, ysem, 1, HALF0_ROWS0, HALF1_ROWS0)
            gather(TILES_PER_HALF * TM, TM, False)

        h = h_ref[...] + _weighted_sum(y1_scr, y2_scr, w_ref, TM)
        h_scr[...] = h
        gather(_next_x_tile(tile) * TM, TM, True)
        xhat = _rms_hat(h)
        ks, vs = _kv_rows(xhat, g_kv_ref, w_kv_ref, cos_ref[...], sin_ref[...])
        for kvh in range(N_KV_HEADS):
            k_scr[kvh, QB:QB + TM, :] = ks[kvh]
            v_scr[kvh, QB:QB + TM, :] = vs[kvh]

        xq = (xhat * g_attn_ref[...]).astype(jnp.bfloat16)
        q = jnp.dot(xq, w_q_ref[...], preferred_element_type=jnp.float32)
        for hb in range(N_HEADS // 2):
            q_scr[hb] = q[:, LANES * hb:LANES * (hb + 1)]
        first_bias = jnp.where(batch_first, 1, 0)

        def head_pair(hb, carry):
            kvh = hb // (N_HEADS // N_KV_HEADS // 2)
            lane = lax.broadcasted_iota(jnp.int32, (QB, LANES), 1)
            q_all = _rope(q_scr[hb], cos_ref[...], sin_ref[...]) * (HEAD_DIM ** -0.5)
            for b in range(N_QB):
                qb = q_all[QB * b:QB * (b + 1), :]
                qs = jnp.concatenate([jnp.where(lane < HEAD_DIM, qb, 0.0),
                                      jnp.where(lane < HEAD_DIM, 0.0, qb)], axis=0).astype(jnp.bfloat16)
                kk = k_scr[kvh, QB * b:QB * (b + 2), :]
                vv = v_scr[kvh, QB * b:QB * (b + 2), :]
                sc = lax.dot_general(qs, kk, (((1,), (1,)), ((), ())),
                                     preferred_element_type=jnp.float32)
                bias = bias_scr[first_bias] if b == 0 else bias_scr[0]
                outs = []
                for j in range(2):
                    sj = sc[QB * j:QB * (j + 1), :] + bias
                    sink = sink_ref[2 * hb + j]
                    m = jnp.maximum(jnp.max(sj, axis=-1, keepdims=True), sink)
                    p = jnp.exp(sj - m)
                    den = jnp.sum(p, axis=-1, keepdims=True) + jnp.exp(sink - m)
                    pv = jnp.dot(p.astype(jnp.bfloat16), vv, preferred_element_type=jnp.float32)
                    outs.append(pv * (1.0 / den))
                o_scr[hb, QB * b:QB * (b + 1), :] = jnp.where(
                    lane < HEAD_DIM, outs[0], outs[1]).astype(jnp.bfloat16)
            return carry

        lax.fori_loop(0, N_HEADS // 2, head_pair, 0)

        k_scr[:, 0:QB, :] = k_scr[:, TM:TM + QB, :]
        v_scr[:, 0:QB, :] = v_scr[:, TM:TM + QB, :]

        o = jnp.concatenate([o_scr[hb] for hb in range(N_HEADS // 2)], axis=1)
        h3 = h_scr[...] + jnp.dot(o, w_o_ref[...], preferred_element_type=jnp.float32)
        h3_ref[...] = h3
        _ffn_prologue(h3, g_ffn_ref, wr_ref, br_ref, run_scr, (s - 1) * TM, T_X,
                      xp_ref, code_ref, wts_ref, cnt_ref)


def _attn(pos1, pos2, h1, wcols, ys, rope, g_attn, g_kv, w_q, w_kv, w_o, sinks, g_ffn, wr, br):
    cos_x, sin_x, cos_m, sin_m = rope

    def tile_x(s, p1, p2):
        return jnp.maximum(s - 1, 0)

    def tile_in(s, p1, p2):
        return (_tile_first_meta(s), 0)

    def rope_tile(s, p1, p2):
        return (jnp.maximum(s - 1, 0) % (N_XT // BATCH), 0)

    out_shape = [
        jax.ShapeDtypeStruct((T_X, D_MODEL), jnp.float32),
        jax.ShapeDtypeStruct((T_X * SLAB, LANES), jnp.uint32),
        jax.ShapeDtypeStruct((2, T_X), jnp.int32),
        jax.ShapeDtypeStruct((T_X, W_COLS), jnp.float32),
        jax.ShapeDtypeStruct((N_KEYS, LANES), jnp.int32),
    ]
    return pl.pallas_call(
        _attn_kernel,
        grid_spec=pltpu.PrefetchScalarGridSpec(
            num_scalar_prefetch=2,
            grid=(N_XT + 1,),
            in_specs=[
                pl.BlockSpec((TM, D_MODEL), tile_in),
                pl.BlockSpec((TM, W_COLS), tile_in),
                pl.BlockSpec(memory_space=pl.ANY),
                pl.BlockSpec((TM, LANES), rope_tile),
                pl.BlockSpec((TM, LANES), rope_tile),
                _RESIDENT, _RESIDENT,
                _RESIDENT, _RESIDENT, _RESIDENT, _RESIDENT, _RESIDENT,
                pl.BlockSpec(memory_space=pltpu.SMEM),
                _RESIDENT, _RESIDENT, _RESIDENT,
            ],
            out_specs=[
                pl.BlockSpec((TM, D_MODEL), lambda s, p1, p2: (tile_x(s, p1, p2), 0)),
                pl.BlockSpec((TM * SLAB, LANES), lambda s, p1, p2: (tile_x(s, p1, p2), 0)),
                pl.BlockSpec((2, TM), lambda s, p1, p2: (0, tile_x(s, p1, p2))),
                pl.BlockSpec((TM, W_COLS), lambda s, p1, p2: (tile_x(s, p1, p2), 0)),
                _RESIDENT,
            ],
            scratch_shapes=[
                pltpu.VMEM((N_KV_HEADS, QB + TM, LANES), jnp.bfloat16),
                pltpu.VMEM((N_KV_HEADS, QB + TM, LANES), jnp.bfloat16),
                pltpu.VMEM((N_KV_HEADS, QB, LANES), jnp.bfloat16),
                pltpu.VMEM((N_KV_HEADS, QB, LANES), jnp.bfloat16),
                pltpu.VMEM((N_HEADS // 2, TM, LANES), jnp.float32),
                pltpu.VMEM((N_HEADS // 2, TM, LANES), jnp.bfloat16),
                pltpu.VMEM((2, QB, 2 * QB), jnp.float32),
                pltpu.VMEM((TM, D_MODEL), jnp.float32),
                pltpu.VMEM((N_KEYS, 1), jnp.float32),
                pltpu.VMEM((HALF0_ROWS0 * SLAB, LANES), jnp.uint32),
                pltpu.VMEM((TM * SLAB, LANES), jnp.uint32),
                pltpu.VMEM((TM * SLAB, LANES), jnp.uint32),
                pltpu.SemaphoreType.DMA,
            ],
        ),
        out_shape=out_shape,
        compiler_params=pltpu.CompilerParams(
            dimension_semantics=("arbitrary",), vmem_limit_bytes=VMEM_LIMIT),
        name="attn_route",
    )(pos1, pos2, h1, wcols, ys, cos_x, sin_x, cos_m, sin_m, g_attn, g_kv, w_q, w_kv, w_o, sinks, g_ffn, wr, br)


def _router_rows(rg_w, rg_b, re_w, re_b):
    wr = jnp.zeros((N_KEYS, D_MODEL), jnp.float32)
    wr = wr.at[0:N_GROUPS].set(rg_w.T).at[8:8 + N_EXPERTS].set(re_w.T)
    br = jnp.zeros((N_KEYS, 1), jnp.float32)
    br = br.at[0:N_GROUPS, 0].set(rg_b).at[8:8 + N_EXPERTS, 0].set(re_b)
    return wr, br


def _rope_tables():
    half = HEAD_DIM // 2
    inv_freq = jnp.tile(ROPE_THETA ** (-jnp.arange(half, dtype=jnp.float32) / half), 4)[None, :]
    sign = jnp.tile(jnp.concatenate([-jnp.ones(half), jnp.ones(half)]), 2).astype(jnp.float32)[None, :]
    ang_hi = (N_META + QB * jnp.arange(SEQ // QB)).astype(jnp.float32)[:, None] * inv_freq
    ang_lo = jnp.arange(QB).astype(jnp.float32)[:, None] * inv_freq
    c_hi, s_hi = jnp.cos(ang_hi)[:, None, :], jnp.sin(ang_hi)[:, None, :]
    c_lo, s_lo = jnp.cos(ang_lo)[None, :, :], jnp.sin(ang_lo)[None, :, :]
    cos_x = (c_hi * c_lo - s_hi * s_lo).reshape(SEQ, LANES)
    sin_x = ((s_hi * c_lo + c_hi * s_lo) * sign[None]).reshape(SEQ, LANES)
    ang_m = jnp.arange(N_META).astype(jnp.float32)[:, None] * inv_freq
    return cos_x, sin_x, jnp.cos(ang_m), jnp.sin(ang_m) * sign


def _moe(cnt, code, xp, wg, wu, wd, layer, t_pad, t_valid):
    pos, off, _, vrow, vgroup, vslot, vnext, nvis = _positions(cnt[:, 0], code, t_pad, t_valid)
    src = _invert(pos[0], pos[1], t_valid)
    ys = _ffn(vrow, vgroup, vslot, vnext, nvis, src, xp, wg, wu, wd, layer, t_pad, t_valid)
    return off, ys


def kernel(x, meta_tokens, conv_norm_g, conv_w_in, conv_w, conv_w_out, kv_norm_g, w_kv, attn_norm_g,
           w_q, w_o, sinks, ffn_norm_g, router_group_w, router_group_b, router_expert_w,
           router_expert_b, w_gate, w_up, w_down, final_norm_g):
    bf = jnp.bfloat16
    x2d = x.reshape(T_X, D_MODEL)
    wr0, br0 = _router_rows(router_group_w[0], router_group_b[0], router_expert_w[0], router_expert_b[0])
    wr1, br1 = _router_rows(router_group_w[1], router_group_b[1], router_expert_w[1], router_expert_b[1])

    h1, xp0, code0, wts0, cnt0 = _mixer0(
        x2d, meta_tokens, conv_norm_g[0].reshape(1, D_MODEL), conv_w_in[0].astype(bf), conv_w[0],
        conv_w_out[0].astype(bf), ffn_norm_g[0].reshape(1, D_MODEL), wr0, br0)
    pos0, ys0 = _moe(cnt0, code0, xp0, w_gate, w_up, w_down, 0, T_PAD0, T_VALID0)

    h3, xp1, code1, wts1, cnt1 = _attn(
        pos0[0], pos0[1], h1, wts0, ys0, _rope_tables(), attn_norm_g[0].reshape(1, D_MODEL),
        kv_norm_g.reshape(1, D_MODEL), w_q[0].astype(bf), w_kv.astype(bf), w_o[0].astype(bf), sinks[0],
        ffn_norm_g[1].reshape(1, D_MODEL), wr1, br1)
    pos1, ys1 = _moe(cnt1, code1, xp1, w_gate, w_up, w_down, 1, T_X, T_X)
    out = _final(pos1[0], pos1[1], h3, wts1, final_norm_g.reshape(1, D_MODEL), ys1)
    return out.reshape(BATCH, SEQ, D_MODEL)
```

```python
import functools

import jax
import jax.numpy as jnp
from jax import lax
from jax.experimental import pallas as pl
from jax.experimental.pallas import tpu as pltpu

D_MODEL = 1024
BATCH = 2
SEQ = 8192
N_META = 16
N_HEADS = 16
HEAD_DIM = 64
N_KV_HEADS = 4
WINDOW = 128
ROPE_THETA = 10000.0
N_GROUPS = 4
EXPERTS_PER_GROUP = 8
N_EXPERTS = N_GROUPS * EXPERTS_PER_GROUP
D_EXPERT = 256
NORM_EPS = 1e-5
NEG_INF = -1e30

TM = 512
N_XT = BATCH * SEQ // TM
T_X = BATCH * SEQ
T_PAD0 = (N_XT + 1) * TM
T_VALID0 = T_X + N_META
HALF_SPLIT = SEQ
N_KEYS = 2 * N_EXPERTS
TMG = 576
LANES = 128
SLAB = D_MODEL // 2 // LANES
VMEM_V7X = 64 * 1024 * 1024
VMEM_LIMIT = VMEM_V7X - 1024 * 1024


def _cdiv(a, b):
    return (a + b - 1) // b


def _rms_hat(x):
    return x * lax.rsqrt(jnp.mean(x * x, axis=-1, keepdims=True) + NORM_EPS)


def _pack_rows(xn):
    half = D_MODEL // 2
    return pltpu.pack_elementwise([xn[:, :half], xn[:, half:]], packed_dtype=jnp.bfloat16)


def _store_slabs(ref, words):
    m = words.shape[0]
    for k in range(SLAB):
        ref[pl.ds(k, m, stride=SLAB), :] = words[:, LANES * k:LANES * (k + 1)]


def _load_slabs(ref, m):
    return jnp.concatenate([ref[pl.ds(k, m, stride=SLAB), :] for k in range(SLAB)], axis=1)


def _unpack_words(words):
    lo = pltpu.unpack_elementwise(words, index=0, packed_dtype=jnp.bfloat16, unpacked_dtype=jnp.float32)
    hi = pltpu.unpack_elementwise(words, index=1, packed_dtype=jnp.bfloat16, unpacked_dtype=jnp.float32)
    return lo, hi


def _route(xn, wr_ref, br_ref, run_scr, tok_base, valid_limit):
    def split(a):
        hi = a.astype(jnp.bfloat16)
        return hi, (a - hi.astype(jnp.float32)).astype(jnp.bfloat16)

    def nt_dot(a, b):
        return lax.dot_general(a, b, (((1,), (1,)), ((), ())), preferred_element_type=jnp.float32)

    w_hi, w_lo = split(wr_ref[...])
    x_hi, x_lo = split(xn)
    logits = nt_dot(w_hi, x_hi) + nt_dot(w_hi, x_lo) + nt_dot(w_lo, x_hi) + br_ref[...]
    g = logits[0:N_GROUPS]
    gmax = jnp.max(g, axis=0, keepdims=True)
    rid_g = lax.broadcasted_iota(jnp.int32, g.shape, 0).astype(jnp.float32)
    g_idx = jnp.min(jnp.where(g == gmax, rid_g, float(N_GROUPS)), axis=0, keepdims=True).astype(jnp.int32)
    g_w = 1.0 / jnp.sum(jnp.exp(g - gmax), axis=0, keepdims=True)
    e_sel = logits[8:8 + EXPERTS_PER_GROUP]
    for gi in range(1, N_GROUPS):
        lo = 8 + EXPERTS_PER_GROUP * gi
        e_sel = jnp.where(g_idx == gi, logits[lo:lo + EXPERTS_PER_GROUP], e_sel)
    rid_e = lax.broadcasted_iota(jnp.int32, e_sel.shape, 0).astype(jnp.float32)
    none = float(EXPERTS_PER_GROUP)
    m1 = jnp.max(e_sel, axis=0, keepdims=True)
    i1f = jnp.min(jnp.where(e_sel == m1, rid_e, none), axis=0, keepdims=True)
    e_rest = jnp.where(rid_e == i1f, -jnp.inf, e_sel)
    m2 = jnp.max(e_rest, axis=0, keepdims=True)
    i2 = jnp.min(jnp.where(e_rest == m2, rid_e, none), axis=0, keepdims=True).astype(jnp.int32)
    i1 = i1f.astype(jnp.int32)
    ex = jnp.exp(m2 - m1)
    den = 1.0 / (1.0 + ex)
    w1 = den * g_w
    w2 = ex * den * g_w

    tok = tok_base + lax.broadcasted_iota(jnp.int32, (1, TM), 1)
    half = jnp.where(jnp.logical_and(tok >= HALF_SPLIT, tok < T_X), N_EXPERTS, 0)
    key1 = half + g_idx * EXPERTS_PER_GROUP + i1
    key2 = half + g_idx * EXPERTS_PER_GROUP + i2
    kid = lax.broadcasted_iota(jnp.int32, (N_KEYS, TM), 0)
    validf = jnp.where(tok < valid_limit, 1.0, 0.0)
    oh1 = jnp.where(kid == key1, validf, 0.0)
    oh2 = jnp.where(kid == key2, validf, 0.0)
    cnt = oh1 + oh2
    tri = jnp.where(lax.broadcasted_iota(jnp.int32, (TM, TM), 0)
                    <= lax.broadcasted_iota(jnp.int32, (TM, TM), 1), 1.0, 0.0).astype(jnp.bfloat16)
    cum = jnp.dot(cnt.astype(jnp.bfloat16), tri, preferred_element_type=jnp.float32)
    before = run_scr[...] + (cum - cnt)
    rank1 = jnp.sum(oh1 * before, axis=0, keepdims=True).astype(jnp.int32)
    rank2 = jnp.sum(oh2 * before, axis=0, keepdims=True).astype(jnp.int32)
    run_scr[...] = run_scr[...] + cum[:, TM - 1:TM]
    code = jnp.concatenate([key1 * 65536 + rank1, key2 * 65536 + rank2], axis=0)
    wts = jnp.concatenate([w1, w2], axis=0)
    return code, wts


W_COLS = 8


def _ffn_prologue(h_new, g_ffn_ref, wr_ref, br_ref, run_scr, tok_base, valid_limit,
                  xp_ref, code_ref, wts_ref, cnt_ref):
    xn2 = _rms_hat(h_new) * g_ffn_ref[...]
    _store_slabs(xp_ref, _pack_rows(xn2))
    code, wts = _route(xn2, wr_ref, br_ref, run_scr, tok_base, valid_limit)
    code_ref[...] = code
    wts_ref[...] = jnp.concatenate([wts, jnp.zeros((W_COLS - 2, TM), jnp.float32)], axis=0).T
    cnt_ref[...] = jnp.broadcast_to(run_scr[...], cnt_ref.shape).astype(jnp.int32)


NC = 512


def _mixer0_kernel(x_ref, meta_ref, g_conv_ref, w_in_ref, cw_ref, w_out_ref,
                   g_ffn_ref, wr_ref, br_ref,
                   h1_ref, xp_ref, code_ref, wts_ref, cnt_ref,
                   h0_scr, acc_scr, carry_scr, meta_carry_scr, run_scr):
    s = pl.program_id(0)

    @pl.when(s == 0)
    def _():
        h0_scr[...] = jnp.zeros_like(h0_scr)
        h0_scr[0:N_META, :] = meta_ref[...]
        carry_scr[...] = jnp.zeros_like(carry_scr)
        meta_carry_scr[...] = jnp.zeros_like(meta_carry_scr)
        run_scr[...] = jnp.zeros_like(run_scr)

    @pl.when(s > 0)
    def _():
        h0_scr[...] = x_ref[...]

    @pl.when(s == 1 + N_XT // BATCH)
    def _():
        carry_scr[...] = meta_carry_scr[...]

    h0 = h0_scr[...]
    xn = (_rms_hat(h0) * g_conv_ref[...]).astype(jnp.bfloat16)
    row = lax.broadcasted_iota(jnp.int32, (TM, NC), 0)
    for c in range(D_MODEL // NC):
        cols = slice(NC * c, NC * (c + 1))
        gate_c = jnp.dot(xn, w_in_ref[:, D_MODEL + NC * c:D_MODEL + NC * (c + 1)],
                         preferred_element_type=jnp.float32)
        val = jnp.dot(xn, w_in_ref[:, 2 * D_MODEL + NC * c:2 * D_MODEL + NC * (c + 1)],
                      preferred_element_type=jnp.float32)
        u = gate_c * val
        tail = carry_scr[:, cols]
        c1 = tail[7:8, :]
        c2 = tail[6:7, :]
        um1 = jnp.where(row == 0, c1, pltpu.roll(u, 1, 0))
        um2 = jnp.where(row == 0, c2, jnp.where(row == 1, c1, pltpu.roll(u, 2, 0)))
        conv = um2 * cw_ref[0:1, cols] + um1 * cw_ref[1:2, cols] + u * cw_ref[2:3, cols]

        is_meta = s == 0
        meta_tail = u[N_META - 8:N_META, :]
        carry_scr[:, cols] = jnp.where(is_meta, meta_tail, u[TM - 8:TM, :])
        meta_carry_scr[:, cols] = jnp.where(is_meta, meta_tail, meta_carry_scr[:, cols])

        gate_b = jnp.dot(xn, w_in_ref[:, cols], preferred_element_type=jnp.float32)
        gated = (gate_b * conv).astype(jnp.bfloat16)
        part = jnp.dot(gated, w_out_ref[cols, :], preferred_element_type=jnp.float32)
        if c == 0:
            acc_scr[...] = h0 + part
        else:
            acc_scr[...] = acc_scr[...] + part

    h1 = acc_scr[...]
    h1_ref[...] = h1
    tile = jnp.where(s == 0, N_XT, s - 1)
    _ffn_prologue(h1, g_ffn_ref, wr_ref, br_ref, run_scr, tile * TM, T_VALID0,
                  xp_ref, code_ref, wts_ref, cnt_ref)


def _tile_first_meta(s):
    return jnp.where(s == 0, N_XT, s - 1)


_RESIDENT = pl.BlockSpec(memory_space=pltpu.VMEM)


def _mixer0(x2d, meta, g_conv, w_in, cw, w_out, g_ffn, wr, br):
    out_shape = [
        jax.ShapeDtypeStruct((T_PAD0, D_MODEL), jnp.float32),
        jax.ShapeDtypeStruct((T_PAD0 * SLAB, LANES), jnp.uint32),
        jax.ShapeDtypeStruct((2, T_PAD0), jnp.int32),
        jax.ShapeDtypeStruct((T_PAD0, W_COLS), jnp.float32),
        jax.ShapeDtypeStruct((N_KEYS, LANES), jnp.int32),
    ]
    return pl.pallas_call(
        _mixer0_kernel,
        grid=(N_XT + 1,),
        in_specs=[pl.BlockSpec((TM, D_MODEL), lambda s: (jnp.maximum(s - 1, 0), 0))] + [_RESIDENT] * 8,
        out_specs=[
            pl.BlockSpec((TM, D_MODEL), lambda s: (_tile_first_meta(s), 0)),
            pl.BlockSpec((TM * SLAB, LANES), lambda s: (_tile_first_meta(s), 0)),
            pl.BlockSpec((2, TM), lambda s: (0, _tile_first_meta(s))),
            pl.BlockSpec((TM, W_COLS), lambda s: (_tile_first_meta(s), 0)),
            _RESIDENT,
        ],
        out_shape=out_shape,
        scratch_shapes=[
            pltpu.VMEM((TM, D_MODEL), jnp.float32),
            pltpu.VMEM((TM, D_MODEL), jnp.float32),
            pltpu.VMEM((8, D_MODEL), jnp.float32),
            pltpu.VMEM((8, D_MODEL), jnp.float32),
            pltpu.VMEM((N_KEYS, 1), jnp.float32),
        ],
        compiler_params=pltpu.CompilerParams(
            dimension_semantics=("arbitrary",), vmem_limit_bytes=VMEM_LIMIT),
        name="mixer0_route",
    )(x2d, meta, g_conv, w_in, cw, w_out, g_ffn, wr, br)


def _n_tiles(t_valid):
    return _cdiv(2 * t_valid, TMG)


def _n_visits(t_valid):
    return _n_tiles(t_valid) + N_KEYS


def _half0_rows(t_valid):
    return 2 * (HALF_SPLIT + t_valid - T_X)


def _half1_base(half0_rows):
    return half0_rows + TMG


def _sorted_rows(t_valid):
    return (_n_tiles(t_valid) + 2) * TMG


def _positions_kernel(t_pad, t_valid, cnt_ref, code_ref, pos_ref, off_ref, gstart_ref, vrow_ref, vgroup_ref,
                      vslot_ref, vnext_ref, nvis_ref, nexte_scr):
    n_vis = _n_visits(t_valid)

    def offs(g, acc):
        acc = jnp.where(g == N_EXPERTS, _half1_base(_half0_rows(t_valid)), acc)
        gstart_ref[g] = acc
        return acc + cnt_ref[g]

    total = lax.fori_loop(0, N_KEYS, offs, jnp.int32(0))
    gstart_ref[N_KEYS] = total

    def rows_of(e):
        return cnt_ref[e] + cnt_ref[e + N_EXPERTS]

    def next_nonempty(i, nxt):
        e = N_EXPERTS - 1 - i
        nexte_scr[e] = nxt
        return jnp.where(rows_of(e) > 0, e, nxt)

    lax.fori_loop(0, N_EXPERTS, next_nonempty, jnp.int32(-1))

    def per_expert(e, carry):
        v, last_g, rank = carry
        for half in range(2):
            g = e + half * N_EXPERTS
            c = cnt_ref[g]
            start = gstart_ref[g]

            def per_window(k, vv, g=g, start=start):
                vrow_ref[vv] = start + k * TMG
                vgroup_ref[vv] = g
                vslot_ref[vv] = rank % 2
                vnext_ref[vv] = nexte_scr[e]
                return vv + 1

            v = lax.fori_loop(0, (c + TMG - 1) // TMG, per_window, v)
            last_g = jnp.where(c > 0, g, last_g)
        return v, last_g, jnp.where(rows_of(e) > 0, rank + 1, rank)

    nvis, last_g, _ = lax.fori_loop(0, N_EXPERTS, per_expert, (jnp.int32(0), jnp.int32(0), jnp.int32(0)))
    nvis_ref[0] = nvis

    def pad(vv, c):
        vrow_ref[vv] = 0
        vgroup_ref[vv] = last_g
        vslot_ref[vv] = 0
        vnext_ref[vv] = -1
        return c

    lax.fori_loop(nvis, n_vis, pad, 0)

    code = code_ref[...]
    key = code >> 16
    pos = code & 0xFFFF
    for g in range(N_KEYS):
        pos = pos + jnp.where(key == g, gstart_ref[g], 0)
    pos_ref[...] = pos
    half_base = jnp.where(key >= N_EXPERTS, _half1_base(_half0_rows(t_valid)), 0)
    off_ref[...] = (pos - half_base) * SLAB


def _positions(cnt, code, t_pad, t_valid):
    n_vis = _n_visits(t_valid)
    smem = pl.BlockSpec(memory_space=pltpu.SMEM)
    vmem = pl.BlockSpec(memory_space=pltpu.VMEM)
    return pl.pallas_call(
        functools.partial(_positions_kernel, t_pad, t_valid),
        in_specs=[smem, vmem],
        out_specs=[vmem, vmem, smem, smem, smem, smem, smem, smem],
        out_shape=[
            jax.ShapeDtypeStruct((2, t_pad), jnp.int32),
            jax.ShapeDtypeStruct((2, t_pad), jnp.int32),
            jax.ShapeDtypeStruct((N_KEYS + 1,), jnp.int32),
            jax.ShapeDtypeStruct((n_vis,), jnp.int32),
            jax.ShapeDtypeStruct((n_vis,), jnp.int32),
            jax.ShapeDtypeStruct((n_vis,), jnp.int32),
            jax.ShapeDtypeStruct((n_vis,), jnp.int32),
            jax.ShapeDtypeStruct((1,), jnp.int32),
        ],
        scratch_shapes=[pltpu.SMEM((N_EXPERTS,), jnp.int32)],
        name="sort_positions",
    )(cnt, code)


SRC_UNROLL = 8


def _invert_kernel(t_valid, r_pad, pos1_ref, pos2_ref, src_ref):
    def slack(r, c):
        src_ref[r] = 0
        return c

    half0_rows = _half0_rows(t_valid)
    lax.fori_loop(half0_rows, _half1_base(half0_rows), slack, 0)
    lax.fori_loop(2 * t_valid + TMG, r_pad, slack, 0)

    def body(i, c):
        for j in range(SRC_UNROLL):
            t = i * SRC_UNROLL + j
            src_ref[pos1_ref[t]] = t
            src_ref[pos2_ref[t]] = t
        return c

    lax.fori_loop(0, t_valid // SRC_UNROLL, body, 0)


def _invert(pos1, pos2, t_valid):
    r_pad = _sorted_rows(t_valid)
    smem = pl.BlockSpec(memory_space=pltpu.SMEM)
    return pl.pallas_call(
        functools.partial(_invert_kernel, t_valid, r_pad),
        in_specs=[smem, smem],
        out_specs=smem,
        out_shape=jax.ShapeDtypeStruct((r_pad,), jnp.int32),
        name="sort_invert",
    )(pos1, pos2)


GATHER_UNROLL = 8


def _ffn_kernel(layer, half0_rows, vrow_ref, vgroup_ref, vslot_ref, vnext_ref, nvis_ref, src_ref,
                xp_hbm, wg_hbm, wu_hbm, wd_hbm, ys_hbm,
                xp_scr, wg_scr, wu_scr, wd_scr, xnext_scr, out_scr, xsem, wsem, osem):
    nvis = nvis_ref[0]

    def expert_of(visit):
        return vgroup_ref[visit] % N_EXPERTS

    def weight_copies(e, sl):
        return (pltpu.make_async_copy(wg_hbm.at[layer, e], wg_scr.at[sl], wsem.at[sl, 0]),
                pltpu.make_async_copy(wu_hbm.at[layer, e], wu_scr.at[sl], wsem.at[sl, 1]),
                pltpu.make_async_copy(wd_hbm.at[layer, e], wd_scr.at[sl], wsem.at[sl, 2]))

    def out_copy(visit):
        sl = visit % 2
        return pltpu.make_async_copy(
            out_scr.at[sl], ys_hbm.at[pl.ds(pl.multiple_of(vrow_ref[visit] * SLAB, SLAB), TMG * SLAB), :],
            osem.at[sl])

    def gather_rows(visit, unrolled):
        base = vrow_ref[visit]

        def one(r):
            tok = src_ref[base + r]
            xnext_scr[pl.ds(pl.multiple_of(r * SLAB, SLAB), SLAB), :] = (
                xp_scr[pl.ds(pl.multiple_of(tok * SLAB, SLAB), SLAB), :])

        if unrolled:
            for r in range(TMG):
                one(r)
        else:
            def chunk(i, c):
                for j in range(GATHER_UNROLL):
                    one(i * GATHER_UNROLL + j)
                return c

            lax.fori_loop(0, TMG // GATHER_UNROLL, chunk, 0)

    cp = pltpu.make_async_copy(xp_hbm, xp_scr, xsem)
    cp.start()
    for c in weight_copies(expert_of(0), vslot_ref[0]):
        c.start()
    out_scr[...] = jnp.zeros_like(out_scr)
    cap = ys_hbm.shape[0] // SLAB
    for first_row in (half0_rows, cap - 2 * TMG, cap - TMG):
        fill = pltpu.make_async_copy(
            out_scr.at[0], ys_hbm.at[pl.ds(first_row * SLAB, TMG * SLAB), :], osem.at[0])
        fill.start()
        fill.wait()
    cp.wait()
    gather_rows(0, False)

    def visit(v, carry):
        e = expert_of(v)
        slot = vslot_ref[v]

        @pl.when(jnp.logical_or(v == 0, e != expert_of(jnp.maximum(v - 1, 0))))
        def _():
            for c in weight_copies(e, slot):
                c.wait()
            nxt = vnext_ref[v]

            @pl.when(nxt >= 0)
            def _():
                for c in weight_copies(nxt, 1 - slot):
                    c.start()

        lo, hi = _unpack_words(_load_slabs(xnext_scr, TMG))
        xs = jnp.concatenate([lo, hi], axis=1).astype(jnp.bfloat16)
        gather_rows(jnp.minimum(v + 1, nvis - 1), True)
        hg = jnp.dot(xs, wg_scr[slot].astype(jnp.bfloat16), preferred_element_type=jnp.float32)
        hu = jnp.dot(xs, wu_scr[slot].astype(jnp.bfloat16), preferred_element_type=jnp.float32)
        hdn = (hg * jax.nn.sigmoid(hg) * hu).astype(jnp.bfloat16)
        y = jnp.dot(hdn, wd_scr[slot].astype(jnp.bfloat16), preferred_element_type=jnp.float32)
        _store_slabs(out_scr.at[v % 2], _pack_rows(y))

        @pl.when(v > 0)
        def _():
            out_copy(v - 1).wait()

        out_copy(v).start()
        return carry

    lax.fori_loop(0, nvis, visit, 0)
    out_copy(nvis - 1).wait()


def _ffn(vrow, vgroup, vslot, vnext, nvis, src, xp, wg, wu, wd, layer, t_pad, t_valid):
    any_spec = pl.BlockSpec(memory_space=pl.ANY)
    return pl.pallas_call(
        functools.partial(_ffn_kernel, layer, _half0_rows(t_valid)),
        grid_spec=pltpu.PrefetchScalarGridSpec(
            num_scalar_prefetch=6,
            grid=(1,),
            in_specs=[any_spec, any_spec, any_spec, any_spec],
            out_specs=any_spec,
            scratch_shapes=[
                pltpu.VMEM((t_pad * SLAB, LANES), jnp.uint32),
                pltpu.VMEM((2, D_MODEL, D_EXPERT), jnp.float32),
                pltpu.VMEM((2, D_MODEL, D_EXPERT), jnp.float32),
                pltpu.VMEM((2, D_EXPERT, D_MODEL), jnp.float32),
                pltpu.VMEM((TMG * SLAB, LANES), jnp.uint32),
                pltpu.VMEM((2, TMG * SLAB, LANES), jnp.uint32),
                pltpu.SemaphoreType.DMA,
                pltpu.SemaphoreType.DMA((2, 3)),
                pltpu.SemaphoreType.DMA((2,)),
            ],
        ),
        out_shape=jax.ShapeDtypeStruct((_sorted_rows(t_valid) * SLAB, LANES), jnp.uint32),
        compiler_params=pltpu.CompilerParams(
            dimension_semantics=("arbitrary",), vmem_limit_bytes=VMEM_LIMIT),
        name="expert_ffn",
    )(vrow, vgroup, vslot, vnext, nvis, src, xp, wg, wu, wd)


TILES_PER_HALF = HALF_SPLIT // TM


def _load_half(ys_hbm, ys_scr, sem, half, half0_rows, half1_rows):
    start, rows = (0, half0_rows) if half == 0 else (_half1_base(half0_rows), half1_rows)
    cp = pltpu.make_async_copy(ys_hbm.at[pl.ds(start * SLAB, rows * SLAB), :],
                               ys_scr.at[pl.ds(0, rows * SLAB), :], sem)
    cp.start()
    cp.wait()


def _gather_pairs(off1_ref, off2_ref, ys_scr, y1_scr, y2_scr, tok0, n, unrolled):
    def one(r):
        dst = pl.ds(pl.multiple_of(r * SLAB, SLAB), SLAB)
        y1_scr[dst, :] = ys_scr[pl.ds(pl.multiple_of(off1_ref[tok0 + r], SLAB), SLAB), :]
        y2_scr[dst, :] = ys_scr[pl.ds(pl.multiple_of(off2_ref[tok0 + r], SLAB), SLAB), :]

    if unrolled:
        for r in range(n):
            one(r)
    else:
        def chunk(i, c):
            for j in range(GATHER_UNROLL):
                one(i * GATHER_UNROLL + j)
            return c

        lax.fori_loop(0, n // GATHER_UNROLL, chunk, 0)


def _weighted_sum(y1_scr, y2_scr, w_ref, m):
    lo1, hi1 = _unpack_words(_load_slabs(y1_scr, m))
    lo2, hi2 = _unpack_words(_load_slabs(y2_scr, m))
    w1 = w_ref[0:m, 0:1]
    w2 = w_ref[0:m, 1:2]
    return jnp.concatenate([w1 * lo1 + w2 * lo2, w1 * hi1 + w2 * hi2], axis=1)


def _next_x_tile(tile):
    nxt = tile + 1
    return jnp.where(jnp.logical_or(nxt == TILES_PER_HALF, nxt == N_XT), tile, nxt)


def _final_kernel(half_rows, off1_ref, off2_ref, h_ref, w_ref, g_ref, ys_hbm,
                  o_ref, ys_scr, y1_scr, y2_scr, sem):
    tile = pl.program_id(0)
    gather = functools.partial(_gather_pairs, off1_ref, off2_ref, ys_scr, y1_scr, y2_scr)

    @pl.when(tile == 0)
    def _():
        _load_half(ys_hbm, ys_scr, sem, 0, half_rows, half_rows)
        gather(0, TM, False)

    @pl.when(tile == TILES_PER_HALF)
    def _():
        _load_half(ys_hbm, ys_scr, sem, 1, half_rows, half_rows)
        gather(TILES_PER_HALF * TM, TM, False)

    h = h_ref[...] + _weighted_sum(y1_scr, y2_scr, w_ref, TM)
    gather(_next_x_tile(tile) * TM, TM, True)
    o_ref[...] = _rms_hat(h) * g_ref[...]


def _final(pos1, pos2, h, wcols, g, ys):
    half_rows = 2 * HALF_SPLIT
    return pl.pallas_call(
        functools.partial(_final_kernel, half_rows),
        grid_spec=pltpu.PrefetchScalarGridSpec(
            num_scalar_prefetch=2,
            grid=(N_XT,),
            in_specs=[
                pl.BlockSpec((TM, D_MODEL), lambda i, p1, p2: (i, 0)),
                pl.BlockSpec((TM, W_COLS), lambda i, p1, p2: (i, 0)),
                _RESIDENT,
                pl.BlockSpec(memory_space=pl.ANY),
            ],
            out_specs=pl.BlockSpec((TM, D_MODEL), lambda i, p1, p2: (i, 0)),
            scratch_shapes=[
                pltpu.VMEM((half_rows * SLAB, LANES), jnp.uint32),
                pltpu.VMEM((TM * SLAB, LANES), jnp.uint32),
                pltpu.VMEM((TM * SLAB, LANES), jnp.uint32),
                pltpu.SemaphoreType.DMA,
            ],
        ),
        out_shape=jax.ShapeDtypeStruct((T_X, D_MODEL), jnp.float32),
        compiler_params=pltpu.CompilerParams(
            dimension_semantics=("arbitrary",), vmem_limit_bytes=VMEM_LIMIT),
        name="moe_combine_final",
    )(pos1, pos2, h, wcols, g, ys)


QB = WINDOW
KV_W = N_KV_HEADS * HEAD_DIM
N_QB = TM // QB
META_ROW0 = QB - N_META


def _rope(x, cos, sin_signed):
    q = lax.broadcasted_iota(jnp.int32, x.shape, 1) // (HEAD_DIM // 2)
    swapped = jnp.where(q % 2 == 0, pltpu.roll(x, LANES - HEAD_DIM // 2, 1),
                        pltpu.roll(x, HEAD_DIM // 2, 1))
    return x * cos + swapped * sin_signed


def _dup_heads(blk):
    lane = lax.broadcasted_iota(jnp.int32, blk.shape, 1)
    rolled = pltpu.roll(blk, HEAD_DIM, 1)
    return jnp.where(lane < HEAD_DIM, blk, rolled), jnp.where(lane < HEAD_DIM, rolled, blk)


def _kv_rows(xhat, g_kv_ref, w_kv_ref, cos, sin_signed):
    xk = (xhat * g_kv_ref[...]).astype(jnp.bfloat16)
    kv = jnp.dot(xk, w_kv_ref[...], preferred_element_type=jnp.float32)
    ks, vs = [], []
    for b in range(KV_W // LANES):
        kb = _rope(kv[:, LANES * b:LANES * (b + 1)], cos, sin_signed)
        vb = kv[:, KV_W + LANES * b:KV_W + LANES * (b + 1)]
        ks.extend(_dup_heads(kb))
        vs.extend(_dup_heads(vb))
    return [k.astype(jnp.bfloat16) for k in ks], [v.astype(jnp.bfloat16) for v in vs]


HALF0_ROWS0 = 2 * (HALF_SPLIT + N_META)
HALF1_ROWS0 = 2 * HALF_SPLIT


def _attn_kernel(off1_ref, off2_ref,
                 h_ref, w_ref, ys_hbm, cos_ref, sin_ref, cosm_ref, sinm_ref, g_attn_ref, g_kv_ref, w_q_ref,
                 w_kv_ref, w_o_ref, sink_ref, g_ffn_ref, wr_ref, br_ref,
                 h3_ref, xp_ref, code_ref, wts_ref, cnt_ref,
                 k_scr, v_scr, mk_scr, mv_scr, q_scr, o_scr, bias_scr, h_scr, run_scr,
                 ys_scr, y1_scr, y2_scr, ysem):
    s = pl.program_id(0)
    tiles_per_batch = N_XT // BATCH
    gather = functools.partial(_gather_pairs, off1_ref, off2_ref, ys_scr, y1_scr, y2_scr)

    @pl.when(s == 0)
    def _():
        run_scr[...] = jnp.zeros_like(run_scr)
        _load_half(ys_hbm, ys_scr, ysem, 0, HALF0_ROWS0, HALF1_ROWS0)
        gather(N_XT * TM, N_META, True)
        h_meta = h_ref[0:N_META, :] + _weighted_sum(y1_scr, y2_scr, w_ref, N_META)
        gather(0, TM, False)
        xhat = _rms_hat(h_meta)
        ks, vs = _kv_rows(xhat, g_kv_ref, w_kv_ref, cosm_ref[...], sinm_ref[...])
        mk_scr[...] = jnp.zeros_like(mk_scr)
        mv_scr[...] = jnp.zeros_like(mv_scr)
        for kvh in range(N_KV_HEADS):
            mk_scr[kvh, META_ROW0:QB, :] = ks[kvh]
            mv_scr[kvh, META_ROW0:QB, :] = vs[kvh]
        qi = lax.broadcasted_iota(jnp.int32, (QB, 2 * QB), 0)
        kj = lax.broadcasted_iota(jnp.int32, (QB, 2 * QB), 1)
        band = jnp.logical_and(kj > qi, kj <= qi + QB)
        bias_scr[0] = jnp.where(band, 0.0, NEG_INF)
        bias_scr[1] = jnp.where(jnp.logical_and(band, kj >= META_ROW0), 0.0, NEG_INF)

    @pl.when(s > 0)
    def _():
        batch_first = jnp.logical_or(s == 1, s == 1 + tiles_per_batch)

        @pl.when(batch_first)
        def _():
            k_scr[:, 0:QB, :] = mk_scr[...]
            v_scr[:, 0:QB, :] = mv_scr[...]

        tile = s - 1

        @pl.when(tile == TILES_PER_HALF)
        def _():
            _load_half(ys_hbm, ys_scr, ysem, 1, HALF0_ROWS0, HALF1_ROWS0)
            gather(TILES_PER_HALF * TM, TM, False)

        h = h_ref[...] + _weighted_sum(y1_scr, y2_scr, w_ref, TM)
        h_scr[...] = h
        gather(_next_x_tile(tile) * TM, TM, True)
        xhat = _rms_hat(h)
        ks, vs = _kv_rows(xhat, g_kv_ref, w_kv_ref, cos_ref[...], sin_ref[...])
        for kvh in range(N_KV_HEADS):
            k_scr[kvh, QB:QB + TM, :] = ks[kvh]
            v_scr[kvh, QB:QB + TM, :] = vs[kvh]

        xq = (xhat * g_attn_ref[...]).astype(jnp.bfloat16)
        q = jnp.dot(xq, w_q_ref[...], preferred_element_type=jnp.float32)
        for hb in range(N_HEADS // 2):
            q_scr[hb] = q[:, LANES * hb:LANES * (hb + 1)]
        first_bias = jnp.where(batch_first, 1, 0)

        def head_pair(hb, carry):
            kvh = hb // (N_HEADS // N_KV_HEADS // 2)
            lane = lax.broadcasted_iota(jnp.int32, (QB, LANES), 1)
            q_all = _rope(q_scr[hb], cos_ref[...], sin_ref[...]) * (HEAD_DIM ** -0.5)
            for b in range(N_QB):
                qb = q_all[QB * b:QB * (b + 1), :]
                qs = jnp.concatenate([jnp.where(lane < HEAD_DIM, qb, 0.0),
                                      jnp.where(lane < HEAD_DIM, 0.0, qb)], axis=0).astype(jnp.bfloat16)
                kk = k_scr[kvh, QB * b:QB * (b + 2), :]
                vv = v_scr[kvh, QB * b:QB * (b + 2), :]
                sc = lax.dot_general(qs, kk, (((1,), (1,)), ((), ())),
                                     preferred_element_type=jnp.float32)
                bias = bias_scr[first_bias] if b == 0 else bias_scr[0]
                outs = []
                for j in range(2):
                    sj = sc[QB * j:QB * (j + 1), :] + bias
                    sink = sink_ref[2 * hb + j]
                    m = jnp.maximum(jnp.max(sj, axis=-1, keepdims=True), sink)
                    p = jnp.exp(sj - m)
                    den = jnp.sum(p, axis=-1, keepdims=True) + jnp.exp(sink - m)
                    pv = jnp.dot(p.astype(jnp.bfloat16), vv, preferred_element_type=jnp.float32)
                    outs.append(pv * (1.0 / den))
                o_scr[hb, QB * b:QB * (b + 1), :] = jnp.where(
                    lane < HEAD_DIM, outs[0], outs[1]).astype(jnp.bfloat16)
            return carry

        lax.fori_loop(0, N_HEADS // 2, head_pair, 0)

        k_scr[:, 0:QB, :] = k_scr[:, TM:TM + QB, :]
        v_scr[:, 0:QB, :] = v_scr[:, TM:TM + QB, :]

        o = jnp.concatenate([o_scr[hb] for hb in range(N_HEADS // 2)], axis=1)
        h3 = h_scr[...] + jnp.dot(o, w_o_ref[...], preferred_element_type=jnp.float32)
        h3_ref[...] = h3
        _ffn_prologue(h3, g_ffn_ref, wr_ref, br_ref, run_scr, (s - 1) * TM, T_X,
                      xp_ref, code_ref, wts_ref, cnt_ref)


def _attn(pos1, pos2, h1, wcols, ys, rope, g_attn, g_kv, w_q, w_kv, w_o, sinks, g_ffn, wr, br):
    cos_x, sin_x, cos_m, sin_m = rope

    def tile_x(s, p1, p2):
        return jnp.maximum(s - 1, 0)

    def tile_in(s, p1, p2):
        return (_tile_first_meta(s), 0)

    def rope_tile(s, p1, p2):
        return (jnp.maximum(s - 1, 0) % (N_XT // BATCH), 0)

    out_shape = [
        jax.ShapeDtypeStruct((T_X, D_MODEL), jnp.float32),
        jax.ShapeDtypeStruct((T_X * SLAB, LANES), jnp.uint32),
        jax.ShapeDtypeStruct((2, T_X), jnp.int32),
        jax.ShapeDtypeStruct((T_X, W_COLS), jnp.float32),
        jax.ShapeDtypeStruct((N_KEYS, LANES), jnp.int32),
    ]
    return pl.pallas_call(
        _attn_kernel,
        grid_spec=pltpu.PrefetchScalarGridSpec(
            num_scalar_prefetch=2,
            grid=(N_XT + 1,),
            in_specs=[
                pl.BlockSpec((TM, D_MODEL), tile_in),
                pl.BlockSpec((TM, W_COLS), tile_in),
                pl.BlockSpec(memory_space=pl.ANY),
                pl.BlockSpec((TM, LANES), rope_tile),
                pl.BlockSpec((TM, LANES), rope_tile),
                _RESIDENT, _RESIDENT,
                _RESIDENT, _RESIDENT, _RESIDENT, _RESIDENT, _RESIDENT,
                pl.BlockSpec(memory_space=pltpu.SMEM),
                _RESIDENT, _RESIDENT, _RESIDENT,
            ],
            out_specs=[
                pl.BlockSpec((TM, D_MODEL), lambda s, p1, p2: (tile_x(s, p1, p2), 0)),
                pl.BlockSpec((TM * SLAB, LANES), lambda s, p1, p2: (tile_x(s, p1, p2), 0)),
                pl.BlockSpec((2, TM), lambda s, p1, p2: (0, tile_x(s, p1, p2))),
                pl.BlockSpec((TM, W_COLS), lambda s, p1, p2: (tile_x(s, p1, p2), 0)),
                _RESIDENT,
            ],
            scratch_shapes=[
                pltpu.VMEM((N_KV_HEADS, QB + TM, LANES), jnp.bfloat16),
                pltpu.VMEM((N_KV_HEADS, QB + TM, LANES), jnp.bfloat16),
                pltpu.VMEM((N_KV_HEADS, QB, LANES), jnp.bfloat16),
                pltpu.VMEM((N_KV_HEADS, QB, LANES), jnp.bfloat16),
                pltpu.VMEM((N_HEADS // 2, TM, LANES), jnp.float32),
                pltpu.VMEM((N_HEADS // 2, TM, LANES), jnp.bfloat16),
                pltpu.VMEM((2, QB, 2 * QB), jnp.float32),
                pltpu.VMEM((TM, D_MODEL), jnp.float32),
                pltpu.VMEM((N_KEYS, 1), jnp.float32),
                pltpu.VMEM((HALF0_ROWS0 * SLAB, LANES), jnp.uint32),
                pltpu.VMEM((TM * SLAB, LANES), jnp.uint32),
                pltpu.VMEM((TM * SLAB, LANES), jnp.uint32),
                pltpu.SemaphoreType.DMA,
            ],
        ),
        out_shape=out_shape,
        compiler_params=pltpu.CompilerParams(
            dimension_semantics=("arbitrary",), vmem_limit_bytes=VMEM_LIMIT),
        name="attn_route",
    )(pos1, pos2, h1, wcols, ys, cos_x, sin_x, cos_m, sin_m, g_attn, g_kv, w_q, w_kv, w_o, sinks, g_ffn, wr, br)


def _router_rows(rg_w, rg_b, re_w, re_b):
    wr = jnp.zeros((N_KEYS, D_MODEL), jnp.float32)
    wr = wr.at[0:N_GROUPS].set(rg_w.T).at[8:8 + N_EXPERTS].set(re_w.T)
    br = jnp.zeros((N_KEYS, 1), jnp.float32)
    br = br.at[0:N_GROUPS, 0].set(rg_b).at[8:8 + N_EXPERTS, 0].set(re_b)
    return wr, br


def _rope_tables():
    half = HEAD_DIM // 2
    inv_freq = jnp.tile(ROPE_THETA ** (-jnp.arange(half, dtype=jnp.float32) / half), 4)[None, :]
    sign = jnp.tile(jnp.concatenate([-jnp.ones(half), jnp.ones(half)]), 2).astype(jnp.float32)[None, :]
    ang_hi = (N_META + QB * jnp.arange(SEQ // QB)).astype(jnp.float32)[:, None] * inv_freq
    ang_lo = jnp.arange(QB).astype(jnp.float32)[:, None] * inv_freq
    c_hi, s_hi = jnp.cos(ang_hi)[:, None, :], jnp.sin(ang_hi)[:, None, :]
    c_lo, s_lo = jnp.cos(ang_lo)[None, :, :], jnp.sin(ang_lo)[None, :, :]
    cos_x = (c_hi * c_lo - s_hi * s_lo).reshape(SEQ, LANES)
    sin_x = ((s_hi * c_lo + c_hi * s_lo) * sign[None]).reshape(SEQ, LANES)
    ang_m = jnp.arange(N_META).astype(jnp.float32)[:, None] * inv_freq
    return cos_x, sin_x, jnp.cos(ang_m), jnp.sin(ang_m) * sign


def _moe(cnt, code, xp, wg, wu, wd, layer, t_pad, t_valid):
    pos, off, _, vrow, vgroup, vslot, vnext, nvis = _positions(cnt[:, 0], code, t_pad, t_valid)
    src = _invert(pos[0], pos[1], t_valid)
    ys = _ffn(vrow, vgroup, vslot, vnext, nvis, src, xp, wg, wu, wd, layer, t_pad, t_valid)
    return off, ys


def kernel(x, meta_tokens, conv_norm_g, conv_w_in, conv_w, conv_w_out, kv_norm_g, w_kv, attn_norm_g,
           w_q, w_o, sinks, ffn_norm_g, router_group_w, router_group_b, router_expert_w,
           router_expert_b, w_gate, w_up, w_down, final_norm_g):
    bf = jnp.bfloat16
    x2d = x.reshape(T_X, D_MODEL)
    wr0, br0 = _router_rows(router_group_w[0], router_group_b[0], router_expert_w[0], router_expert_b[0])
    wr1, br1 = _router_rows(router_group_w[1], router_group_b[1], router_expert_w[1], router_expert_b[1])

    h1, xp0, code0, wts0, cnt0 = _mixer0(
        x2d, meta_tokens, conv_norm_g[0].reshape(1, D_MODEL), conv_w_in[0].astype(bf), conv_w[0],
        conv_w_out[0].astype(bf), ffn_norm_g[0].reshape(1, D_MODEL), wr0, br0)
    pos0, ys0 = _moe(cnt0, code0, xp0, w_gate, w_up, w_down, 0, T_PAD0, T_VALID0)

    h3, xp1, code1, wts1, cnt1 = _attn(
        pos0[0], pos0[1], h1, wts0, ys0, _rope_tables(), attn_norm_g[0].reshape(1, D_MODEL),
        kv_norm_g.reshape(1, D_MODEL), w_q[0].astype(bf), w_kv.astype(bf), w_o[0].astype(bf), sinks[0],
        ffn_norm_g[1].reshape(1, D_MODEL), wr1, br1)
    pos1, ys1 = _moe(cnt1, code1, xp1, w_gate, w_up, w_down, 1, T_X, T_X)
    out = _final(pos1[0], pos1[1], h3, wts1, final_norm_g.reshape(1, D_MODEL), ys1)
    return out.reshape(BATCH, SEQ, D_MODEL)
```

```python
import functools

import jax
import jax.numpy as jnp
from jax import lax
from jax.experimental import pallas as pl
from jax.experimental.pallas import tpu as pltpu

D_MODEL = 1024
BATCH = 2
SEQ = 8192
N_META = 16
N_HEADS = 16
HEAD_DIM = 64
N_KV_HEADS = 4
WINDOW = 128
ROPE_THETA = 10000.0
N_GROUPS = 4
EXPERTS_PER_GROUP = 8
N_EXPERTS = N_GROUPS * EXPERTS_PER_GROUP
D_EXPERT = 256
NORM_EPS = 1e-5
NEG_INF = -1e30

TM = 512
N_XT = BATCH * SEQ // TM
T_X = BATCH * SEQ
T_PAD0 = (N_XT + 1) * TM
T_VALID0 = T_X + N_META
HALF_SPLIT = SEQ
N_KEYS = 2 * N_EXPERTS
TMG = 576
LANES = 128
SLAB = D_MODEL // 2 // LANES
VMEM_V7X = 64 * 1024 * 1024
VMEM_LIMIT = VMEM_V7X - 1024 * 1024


def _cdiv(a, b):
    return (a + b - 1) // b


def _rms_hat(x):
    return x * lax.rsqrt(jnp.mean(x * x, axis=-1, keepdims=True) + NORM_EPS)


def _pack_rows(xn):
    half = D_MODEL // 2
    return pltpu.pack_elementwise([xn[:, :half], xn[:, half:]], packed_dtype=jnp.bfloat16)


def _store_slabs(ref, words):
    m = words.shape[0]
    for k in range(SLAB):
        ref[pl.ds(k, m, stride=SLAB), :] = words[:, LANES * k:LANES * (k + 1)]


def _load_slabs(ref, m):
    return jnp.concatenate([ref[pl.ds(k, m, stride=SLAB), :] for k in range(SLAB)], axis=1)


def _unpack_words(words):
    lo = pltpu.unpack_elementwise(words, index=0, packed_dtype=jnp.bfloat16, unpacked_dtype=jnp.float32)
    hi = pltpu.unpack_elementwise(words, index=1, packed_dtype=jnp.bfloat16, unpacked_dtype=jnp.float32)
    return lo, hi


def _route(xn, wr_ref, br_ref, run_scr, tok_base, valid_limit):
    def split(a):
        hi = a.astype(jnp.bfloat16)
        return hi, (a - hi.astype(jnp.float32)).astype(jnp.bfloat16)

    def nt_dot(a, b):
        return lax.dot_general(a, b, (((1,), (1,)), ((), ())), preferred_element_type=jnp.float32)

    w_hi, w_lo = split(wr_ref[...])
    x_hi, x_lo = split(xn)
    logits = nt_dot(w_hi, x_hi) + nt_dot(w_hi, x_lo) + nt_dot(w_lo, x_hi) + br_ref[...]
    g = logits[0:N_GROUPS]
    gmax = jnp.max(g, axis=0, keepdims=True)
    rid_g = lax.broadcasted_iota(jnp.int32, g.shape, 0).astype(jnp.float32)
    g_idx = jnp.min(jnp.where(g == gmax, rid_g, float(N_GROUPS)), axis=0, keepdims=True).astype(jnp.int32)
    g_w = 1.0 / jnp.sum(jnp.exp(g - gmax), axis=0, keepdims=True)
    e_sel = logits[8:8 + EXPERTS_PER_GROUP]
    for gi in range(1, N_GROUPS):
        lo = 8 + EXPERTS_PER_GROUP * gi
        e_sel = jnp.where(g_idx == gi, logits[lo:lo + EXPERTS_PER_GROUP], e_sel)
    rid_e = lax.broadcasted_iota(jnp.int32, e_sel.shape, 0).astype(jnp.float32)
    none = float(EXPERTS_PER_GROUP)
    m1 = jnp.max(e_sel, axis=0, keepdims=True)
    i1f = jnp.min(jnp.where(e_sel == m1, rid_e, none), axis=0, keepdims=True)
    e_rest = jnp.where(rid_e == i1f, -jnp.inf, e_sel)
    m2 = jnp.max(e_rest, axis=0, keepdims=True)
    i2 = jnp.min(jnp.where(e_rest == m2, rid_e, none), axis=0, keepdims=True).astype(jnp.int32)
    i1 = i1f.astype(jnp.int32)
    ex = jnp.exp(m2 - m1)
    den = 1.0 / (1.0 + ex)
    w1 = den * g_w
    w2 = ex * den * g_w

    tok = tok_base + lax.broadcasted_iota(jnp.int32, (1, TM), 1)
    half = jnp.where(jnp.logical_and(tok >= HALF_SPLIT, tok < T_X), N_EXPERTS, 0)
    key1 = half + g_idx * EXPERTS_PER_GROUP + i1
    key2 = half + g_idx * EXPERTS_PER_GROUP + i2
    kid = lax.broadcasted_iota(jnp.int32, (N_KEYS, TM), 0)
    validf = jnp.where(tok < valid_limit, 1.0, 0.0)
    oh1 = jnp.where(kid == key1, validf, 0.0)
    oh2 = jnp.where(kid == key2, validf, 0.0)
    cnt = oh1 + oh2
    tri = jnp.where(lax.broadcasted_iota(jnp.int32, (TM, TM), 0)
                    <= lax.broadcasted_iota(jnp.int32, (TM, TM), 1), 1.0, 0.0).astype(jnp.bfloat16)
    cum = jnp.dot(cnt.astype(jnp.bfloat16), tri, preferred_element_type=jnp.float32)
    before = run_scr[...] + (cum - cnt)
    rank1 = jnp.sum(oh1 * before, axis=0, keepdims=True).astype(jnp.int32)
    rank2 = jnp.sum(oh2 * before, axis=0, keepdims=True).astype(jnp.int32)
    run_scr[...] = run_scr[...] + cum[:, TM - 1:TM]
    code = jnp.concatenate([key1 * 65536 + rank1, key2 * 65536 + rank2], axis=0)
    wts = jnp.concatenate([w1, w2], axis=0)
    return code, wts


W_COLS = 8


def _ffn_prologue(h_new, g_ffn_ref, wr_ref, br_ref, run_scr, tok_base, valid_limit,
                  xp_ref, code_ref, wts_ref, cnt_ref):
    xn2 = _rms_hat(h_new) * g_ffn_ref[...]
    _store_slabs(xp_ref, _pack_rows(xn2))
    code, wts = _route(xn2, wr_ref, br_ref, run_scr, tok_base, valid_limit)
    code_ref[...] = code
    wts_ref[...] = jnp.concatenate([wts, jnp.zeros((W_COLS - 2, TM), jnp.float32)], axis=0).T
    cnt_ref[...] = jnp.broadcast_to(run_scr[...], cnt_ref.shape).astype(jnp.int32)


NC = 512


def _mixer0_kernel(x_ref, meta_ref, g_conv_ref, w_in_ref, cw_ref, w_out_ref,
                   g_ffn_ref, wr_ref, br_ref,
                   h1_ref, xp_ref, code_ref, wts_ref, cnt_ref,
                   h0_scr, acc_scr, carry_scr, meta_carry_scr, run_scr):
    s = pl.program_id(0)

    @pl.when(s == 0)
    def _():
        h0_scr[...] = jnp.zeros_like(h0_scr)
        h0_scr[0:N_META, :] = meta_ref[...]
        carry_scr[...] = jnp.zeros_like(carry_scr)
        meta_carry_scr[...] = jnp.zeros_like(meta_carry_scr)
        run_scr[...] = jnp.zeros_like(run_scr)

    @pl.when(s > 0)
    def _():
        h0_scr[...] = x_ref[...]

    @pl.when(s == 1 + N_XT // BATCH)
    def _():
        carry_scr[...] = meta_carry_scr[...]

    h0 = h0_scr[...]
    xn = (_rms_hat(h0) * g_conv_ref[...]).astype(jnp.bfloat16)
    row = lax.broadcasted_iota(jnp.int32, (TM, NC), 0)
    for c in range(D_MODEL // NC):
        cols = slice(NC * c, NC * (c + 1))
        gate_c = jnp.dot(xn, w_in_ref[:, D_MODEL + NC * c:D_MODEL + NC * (c + 1)],
                         preferred_element_type=jnp.float32)
        val = jnp.dot(xn, w_in_ref[:, 2 * D_MODEL + NC * c:2 * D_MODEL + NC * (c + 1)],
                      preferred_element_type=jnp.float32)
        u = gate_c * val
        tail = carry_scr[:, cols]
        c1 = tail[7:8, :]
        c2 = tail[6:7, :]
        um1 = jnp.where(row == 0, c1, pltpu.roll(u, 1, 0))
        um2 = jnp.where(row == 0, c2, jnp.where(row == 1, c1, pltpu.roll(u, 2, 0)))
        conv = um2 * cw_ref[0:1, cols] + um1 * cw_ref[1:2, cols] + u * cw_ref[2:3, cols]

        is_meta = s == 0
        meta_tail = u[N_META - 8:N_META, :]
        carry_scr[:, cols] = jnp.where(is_meta, meta_tail, u[TM - 8:TM, :])
        meta_carry_scr[:, cols] = jnp.where(is_meta, meta_tail, meta_carry_scr[:, cols])

        gate_b = jnp.dot(xn, w_in_ref[:, cols], preferred_element_type=jnp.float32)
        gated = (gate_b * conv).astype(jnp.bfloat16)
        part = jnp.dot(gated, w_out_ref[cols, :], preferred_element_type=jnp.float32)
        if c == 0:
            acc_scr[...] = h0 + part
        else:
            acc_scr[...] = acc_scr[...] + part

    h1 = acc_scr[...]
    h1_ref[...] = h1
    tile = jnp.where(s == 0, N_XT, s - 1)
    _ffn_prologue(h1, g_ffn_ref, wr_ref, br_ref, run_scr, tile * TM, T_VALID0,
                  xp_ref, code_ref, wts_ref, cnt_ref)


def _tile_first_meta(s):
    return jnp.where(s == 0, N_XT, s - 1)


_RESIDENT = pl.BlockSpec(memory_space=pltpu.VMEM)


def _mixer0(x2d, meta, g_conv, w_in, cw, w_out, g_ffn, wr, br):
    out_shape = [
        jax.ShapeDtypeStruct((T_PAD0, D_MODEL), jnp.float32),
        jax.ShapeDtypeStruct((T_PAD0 * SLAB, LANES), jnp.uint32),
        jax.ShapeDtypeStruct((2, T_PAD0), jnp.int32),
        jax.ShapeDtypeStruct((T_PAD0, W_COLS), jnp.float32),
        jax.ShapeDtypeStruct((N_KEYS, LANES), jnp.int32),
    ]
    return pl.pallas_call(
        _mixer0_kernel,
        grid=(N_XT + 1,),
        in_specs=[pl.BlockSpec((TM, D_MODEL), lambda s: (jnp.maximum(s - 1, 0), 0))] + [_RESIDENT] * 8,
        out_specs=[
            pl.BlockSpec((TM, D_MODEL), lambda s: (_tile_first_meta(s), 0)),
            pl.BlockSpec((TM * SLAB, LANES), lambda s: (_tile_first_meta(s), 0)),
            pl.BlockSpec((2, TM), lambda s: (0, _tile_first_meta(s))),
            pl.BlockSpec((TM, W_COLS), lambda s: (_tile_first_meta(s), 0)),
            _RESIDENT,
        ],
        out_shape=out_shape,
        scratch_shapes=[
            pltpu.VMEM((TM, D_MODEL), jnp.float32),
            pltpu.VMEM((TM, D_MODEL), jnp.float32),
            pltpu.VMEM((8, D_MODEL), jnp.float32),
            pltpu.VMEM((8, D_MODEL), jnp.float32),
            pltpu.VMEM((N_KEYS, 1), jnp.float32),
        ],
        compiler_params=pltpu.CompilerParams(
            dimension_semantics=("arbitrary",), vmem_limit_bytes=VMEM_LIMIT),
        name="mixer0_route",
    )(x2d, meta, g_conv, w_in, cw, w_out, g_ffn, wr, br)


def _n_tiles(t_valid):
    return _cdiv(2 * t_valid, TMG)


def _n_visits(t_valid):
    return _n_tiles(t_valid) + N_KEYS


def _half0_rows(t_valid):
    return 2 * (HALF_SPLIT + t_valid - T_X)


def _half1_base(half0_rows):
    return half0_rows + TMG


def _sorted_rows(t_valid):
    return (_n_tiles(t_valid) + 2) * TMG


def _positions_kernel(t_pad, t_valid, cnt_ref, code_ref, pos_ref, off_ref, gstart_ref, vrow_ref, vgroup_ref,
                      vslot_ref, vnext_ref, nvis_ref, nexte_scr):
    n_vis = _n_visits(t_valid)

    def offs(g, acc):
        acc = jnp.where(g == N_EXPERTS, _half1_base(_half0_rows(t_valid)), acc)
        gstart_ref[g] = acc
        return acc + cnt_ref[g]

    total = lax.fori_loop(0, N_KEYS, offs, jnp.int32(0))
    gstart_ref[N_KEYS] = total

    def rows_of(e):
        return cnt_ref[e] + cnt_ref[e + N_EXPERTS]

    def next_nonempty(i, nxt):
        e = N_EXPERTS - 1 - i
        nexte_scr[e] = nxt
        return jnp.where(rows_of(e) > 0, e, nxt)

    lax.fori_loop(0, N_EXPERTS, next_nonempty, jnp.int32(-1))

    def per_expert(e, carry):
        v, last_g, rank = carry
        for half in range(2):
            g = e + half * N_EXPERTS
            c = cnt_ref[g]
            start = gstart_ref[g]

            def per_window(k, vv, g=g, start=start):
                vrow_ref[vv] = start + k * TMG
                vgroup_ref[vv] = g
                vslot_ref[vv] = rank % 2
                vnext_ref[vv] = nexte_scr[e]
                return vv + 1

            v = lax.fori_loop(0, (c + TMG - 1) // TMG, per_window, v)
            last_g = jnp.where(c > 0, g, last_g)
        return v, last_g, jnp.where(rows_of(e) > 0, rank + 1, rank)

    nvis, last_g, _ = lax.fori_loop(0, N_EXPERTS, per_expert, (jnp.int32(0), jnp.int32(0), jnp.int32(0)))
    nvis_ref[0] = nvis

    def pad(vv, c):
        vrow_ref[vv] = 0
        vgroup_ref[vv] = last_g
        vslot_ref[vv] = 0
        vnext_ref[vv] = -1
        return c

    lax.fori_loop(nvis, n_vis, pad, 0)

    code = code_ref[...]
    key = code >> 16
    pos = code & 0xFFFF
    for g in range(N_KEYS):
        pos = pos + jnp.where(key == g, gstart_ref[g], 0)
    pos_ref[...] = pos
    half_base = jnp.where(key >= N_EXPERTS, _half1_base(_half0_rows(t_valid)), 0)
    off_ref[...] = (pos - half_base) * SLAB


def _positions(cnt, code, t_pad, t_valid):
    n_vis = _n_visits(t_valid)
    smem = pl.BlockSpec(memory_space=pltpu.SMEM)
    vmem = pl.BlockSpec(memory_space=pltpu.VMEM)
    return pl.pallas_call(
        functools.partial(_positions_kernel, t_pad, t_valid),
        in_specs=[smem, vmem],
        out_specs=[vmem, vmem, smem, smem, smem, smem, smem, smem],
        out_shape=[
            jax.ShapeDtypeStruct((2, t_pad), jnp.int32),
            jax.ShapeDtypeStruct((2, t_pad), jnp.int32),
            jax.ShapeDtypeStruct((N_KEYS + 1,), jnp.int32),
            jax.ShapeDtypeStruct((n_vis,), jnp.int32),
            jax.ShapeDtypeStruct((n_vis,), jnp.int32),
            jax.ShapeDtypeStruct((n_vis,), jnp.int32),
            jax.ShapeDtypeStruct((n_vis,), jnp.int32),
            jax.ShapeDtypeStruct((1,), jnp.int32),
        ],
        scratch_shapes=[pltpu.SMEM((N_EXPERTS,), jnp.int32)],
        name="sort_positions",
    )(cnt, code)


SRC_UNROLL = 16


def _invert_positions(t_valid, pos1_ref, pos2_ref, src_ref):
    def slack(r, c):
        src_ref[r] = 0
        return c

    half0_rows = _half0_rows(t_valid)
    lax.fori_loop(half0_rows, _half1_base(half0_rows), slack, 0)
    lax.fori_loop(2 * t_valid + TMG, _sorted_rows(t_valid), slack, 0)

    def body(i, c):
        for j in range(SRC_UNROLL):
            t = i * SRC_UNROLL + j
            src_ref[pos1_ref[t]] = t
            src_ref[pos2_ref[t]] = t
        return c

    lax.fori_loop(0, t_valid // SRC_UNROLL, body, 0)


GATHER_UNROLL = 8


def _ffn_kernel(layer, t_valid, vrow_ref, vgroup_ref, vslot_ref, vnext_ref, nvis_ref, pos1_ref, pos2_ref,
                xp_hbm, wg_hbm, wu_hbm, wd_hbm, ys_hbm,
                xp_scr, wg_scr, wu_scr, wd_scr, xnext_scr, out_scr, src_ref, xsem, wsem, osem):
    nvis = nvis_ref[0]

    def expert_of(visit):
        return vgroup_ref[visit] % N_EXPERTS

    def weight_copies(e, sl):
        return (pltpu.make_async_copy(wg_hbm.at[layer, e], wg_scr.at[sl], wsem.at[sl, 0]),
                pltpu.make_async_copy(wu_hbm.at[layer, e], wu_scr.at[sl], wsem.at[sl, 1]),
                pltpu.make_async_copy(wd_hbm.at[layer, e], wd_scr.at[sl], wsem.at[sl, 2]))

    def out_copy(visit):
        sl = visit % 2
        return pltpu.make_async_copy(
            out_scr.at[sl], ys_hbm.at[pl.ds(pl.multiple_of(vrow_ref[visit] * SLAB, SLAB), TMG * SLAB), :],
            osem.at[sl])

    def gather_rows(visit, unrolled):
        base = vrow_ref[visit]

        def one(r):
            tok = src_ref[base + r]
            xnext_scr[pl.ds(pl.multiple_of(r * SLAB, SLAB), SLAB), :] = (
                xp_scr[pl.ds(pl.multiple_of(tok * SLAB, SLAB), SLAB), :])

        if unrolled:
            for r in range(TMG):
                one(r)
        else:
            def chunk(i, c):
                for j in range(GATHER_UNROLL):
                    one(i * GATHER_UNROLL + j)
                return c

            lax.fori_loop(0, TMG // GATHER_UNROLL, chunk, 0)

    cp = pltpu.make_async_copy(xp_hbm, xp_scr, xsem)
    cp.start()
    for c in weight_copies(expert_of(0), vslot_ref[0]):
        c.start()
    out_scr[...] = jnp.zeros_like(out_scr)
    cap = ys_hbm.shape[0] // SLAB
    for first_row in (_half0_rows(t_valid), cap - 2 * TMG, cap - TMG):
        fill = pltpu.make_async_copy(
            out_scr.at[0], ys_hbm.at[pl.ds(first_row * SLAB, TMG * SLAB), :], osem.at[0])
        fill.start()
        fill.wait()
    _invert_positions(t_valid, pos1_ref, pos2_ref, src_ref)
    cp.wait()
    gather_rows(0, False)

    def visit(v, carry):
        e = expert_of(v)
        slot = vslot_ref[v]

        @pl.when(jnp.logical_or(v == 0, e != expert_of(jnp.maximum(v - 1, 0))))
        def _():
            for c in weight_copies(e, slot):
                c.wait()
            nxt = vnext_ref[v]

            @pl.when(nxt >= 0)
            def _():
                for c in weight_copies(nxt, 1 - slot):
                    c.start()

        lo, hi = _unpack_words(_load_slabs(xnext_scr, TMG))
        xs = jnp.concatenate([lo, hi], axis=1).astype(jnp.bfloat16)
        gather_rows(jnp.minimum(v + 1, nvis - 1), True)
        hg = jnp.dot(xs, wg_scr[slot].astype(jnp.bfloat16), preferred_element_type=jnp.float32)
        hu = jnp.dot(xs, wu_scr[slot].astype(jnp.bfloat16), preferred_element_type=jnp.float32)
        hdn = (hg * jax.nn.sigmoid(hg) * hu).astype(jnp.bfloat16)
        y = jnp.dot(hdn, wd_scr[slot].astype(jnp.bfloat16), preferred_element_type=jnp.float32)
        _store_slabs(out_scr.at[v % 2], _pack_rows(y))

        @pl.when(v > 0)
        def _():
            out_copy(v - 1).wait()

        out_copy(v).start()
        return carry

    lax.fori_loop(0, nvis, visit, 0)
    out_copy(nvis - 1).wait()


def _ffn(vrow, vgroup, vslot, vnext, nvis, pos1, pos2, xp, wg, wu, wd, layer, t_pad, t_valid):
    any_spec = pl.BlockSpec(memory_space=pl.ANY)
    return pl.pallas_call(
        functools.partial(_ffn_kernel, layer, t_valid),
        grid_spec=pltpu.PrefetchScalarGridSpec(
            num_scalar_prefetch=7,
            grid=(1,),
            in_specs=[any_spec, any_spec, any_spec, any_spec],
            out_specs=any_spec,
            scratch_shapes=[
                pltpu.VMEM((t_pad * SLAB, LANES), jnp.uint32),
                pltpu.VMEM((2, D_MODEL, D_EXPERT), jnp.float32),
                pltpu.VMEM((2, D_MODEL, D_EXPERT), jnp.float32),
                pltpu.VMEM((2, D_EXPERT, D_MODEL), jnp.float32),
                pltpu.VMEM((TMG * SLAB, LANES), jnp.uint32),
                pltpu.VMEM((2, TMG * SLAB, LANES), jnp.uint32),
                pltpu.SMEM((_sorted_rows(t_valid),), jnp.int32),
                pltpu.SemaphoreType.DMA,
                pltpu.SemaphoreType.DMA((2, 3)),
                pltpu.SemaphoreType.DMA((2,)),
            ],
        ),
        out_shape=jax.ShapeDtypeStruct((_sorted_rows(t_valid) * SLAB, LANES), jnp.uint32),
        compiler_params=pltpu.CompilerParams(
            dimension_semantics=("arbitrary",), vmem_limit_bytes=VMEM_LIMIT),
        name="expert_ffn",
    )(vrow, vgroup, vslot, vnext, nvis, pos1, pos2, xp, wg, wu, wd)


TILES_PER_HALF = HALF_SPLIT // TM


def _load_half(ys_hbm, ys_scr, sem, half, half0_rows, half1_rows):
    start, rows = (0, half0_rows) if half == 0 else (_half1_base(half0_rows), half1_rows)
    cp = pltpu.make_async_copy(ys_hbm.at[pl.ds(start * SLAB, rows * SLAB), :],
                               ys_scr.at[pl.ds(0, rows * SLAB), :], sem)
    cp.start()
    cp.wait()


def _gather_pairs(off1_ref, off2_ref, ys_scr, y1_scr, y2_scr, tok0, n, unrolled):
    def one(r):
        dst = pl.ds(pl.multiple_of(r * SLAB, SLAB), SLAB)
        y1_scr[dst, :] = ys_scr[pl.ds(pl.multiple_of(off1_ref[tok0 + r], SLAB), SLAB), :]
        y2_scr[dst, :] = ys_scr[pl.ds(pl.multiple_of(off2_ref[tok0 + r], SLAB), SLAB), :]

    if unrolled:
        for r in range(n):
            one(r)
    else:
        def chunk(i, c):
            for j in range(GATHER_UNROLL):
                one(i * GATHER_UNROLL + j)
            return c

        lax.fori_loop(0, n // GATHER_UNROLL, chunk, 0)


def _weighted_sum(y1_scr, y2_scr, w_ref, m):
    lo1, hi1 = _unpack_words(_load_slabs(y1_scr, m))
    lo2, hi2 = _unpack_words(_load_slabs(y2_scr, m))
    w1 = w_ref[0:m, 0:1]
    w2 = w_ref[0:m, 1:2]
    return jnp.concatenate([w1 * lo1 + w2 * lo2, w1 * hi1 + w2 * hi2], axis=1)


def _next_x_tile(tile):
    nxt = tile + 1
    return jnp.where(jnp.logical_or(nxt == TILES_PER_HALF, nxt == N_XT), tile, nxt)


def _final_kernel(half_rows, off1_ref, off2_ref, h_ref, w_ref, g_ref, ys_hbm,
                  o_ref, ys_scr, y1_scr, y2_scr, sem):
    tile = pl.program_id(0)
    gather = functools.partial(_gather_pairs, off1_ref, off2_ref, ys_scr, y1_scr, y2_scr)

    @pl.when(tile == 0)
    def _():
        _load_half(ys_hbm, ys_scr, sem, 0, half_rows, half_rows)
        gather(0, TM, False)

    @pl.when(tile == TILES_PER_HALF)
    def _():
        _load_half(ys_hbm, ys_scr, sem, 1, half_rows, half_rows)
        gather(TILES_PER_HALF * TM, TM, False)

    h = h_ref[...] + _weighted_sum(y1_scr, y2_scr, w_ref, TM)
    gather(_next_x_tile(tile) * TM, TM, True)
    o_ref[...] = _rms_hat(h) * g_ref[...]


def _final(pos1, pos2, h, wcols, g, ys):
    half_rows = 2 * HALF_SPLIT
    return pl.pallas_call(
        functools.partial(_final_kernel, half_rows),
        grid_spec=pltpu.PrefetchScalarGridSpec(
            num_scalar_prefetch=2,
            grid=(N_XT,),
            in_specs=[
                pl.BlockSpec((TM, D_MODEL), lambda i, p1, p2: (i, 0)),
                pl.BlockSpec((TM, W_COLS), lambda i, p1, p2: (i, 0)),
                _RESIDENT,
                pl.BlockSpec(memory_space=pl.ANY),
            ],
            out_specs=pl.BlockSpec((TM, D_MODEL), lambda i, p1, p2: (i, 0)),
            scratch_shapes=[
                pltpu.VMEM((half_rows * SLAB, LANES), jnp.uint32),
                pltpu.VMEM((TM * SLAB, LANES), jnp.uint32),
                pltpu.VMEM((TM * SLAB, LANES), jnp.uint32),
                pltpu.SemaphoreType.DMA,
            ],
        ),
        out_shape=jax.ShapeDtypeStruct((T_X, D_MODEL), jnp.float32),
        compiler_params=pltpu.CompilerParams(
            dimension_semantics=("arbitrary",), vmem_limit_bytes=VMEM_LIMIT),
        name="moe_combine_final",
    )(pos1, pos2, h, wcols, g, ys)


QB = WINDOW
KV_W = N_KV_HEADS * HEAD_DIM
N_QB = TM // QB
META_ROW0 = QB - N_META


def _rope(x, cos, sin_signed):
    q = lax.broadcasted_iota(jnp.int32, x.shape, 1) // (HEAD_DIM // 2)
    swapped = jnp.where(q % 2 == 0, pltpu.roll(x, LANES - HEAD_DIM // 2, 1),
                        pltpu.roll(x, HEAD_DIM // 2, 1))
    return x * cos + swapped * sin_signed


def _dup_heads(blk):
    lane = lax.broadcasted_iota(jnp.int32, blk.shape, 1)
    rolled = pltpu.roll(blk, HEAD_DIM, 1)
    return jnp.where(lane < HEAD_DIM, blk, rolled), jnp.where(lane < HEAD_DIM, rolled, blk)


def _kv_rows(xhat, g_kv_ref, w_kv_ref, cos, sin_signed):
    xk = (xhat * g_kv_ref[...]).astype(jnp.bfloat16)
    kv = jnp.dot(xk, w_kv_ref[...], preferred_element_type=jnp.float32)
    ks, vs = [], []
    for b in range(KV_W // LANES):
        kb = _rope(kv[:, LANES * b:LANES * (b + 1)], cos, sin_signed)
        vb = kv[:, KV_W + LANES * b:KV_W + LANES * (b + 1)]
        ks.extend(_dup_heads(kb))
        vs.extend(_dup_heads(vb))
    return [k.astype(jnp.bfloat16) for k in ks], [v.astype(jnp.bfloat16) for v in vs]


HALF0_ROWS0 = 2 * (HALF_SPLIT + N_META)
HALF1_ROWS0 = 2 * HALF_SPLIT


def _attn_kernel(off1_ref, off2_ref,
                 h_ref, w_ref, ys_hbm, cos_ref, sin_ref, cosm_ref, sinm_ref, g_attn_ref, g_kv_ref, w_q_ref,
                 w_kv_ref, w_o_ref, sink_ref, g_ffn_ref, wr_ref, br_ref,
                 h3_ref, xp_ref, code_ref, wts_ref, cnt_ref,
                 k_scr, v_scr, mk_scr, mv_scr, q_scr, o_scr, bias_scr, h_scr, run_scr,
                 ys_scr, y1_scr, y2_scr, ysem):
    s = pl.program_id(0)
    tiles_per_batch = N_XT // BATCH
    gather = functools.partial(_gather_pairs, off1_ref, off2_ref, ys_scr, y1_scr, y2_scr)

    @pl.when(s == 0)
    def _():
        run_scr[...] = jnp.zeros_like(run_scr)
        _load_half(ys_hbm, ys_scr, ysem, 0, HALF0_ROWS0, HALF1_ROWS0)
        gather(N_XT * TM, N_META, True)
        h_meta = h_ref[0:N_META, :] + _weighted_sum(y1_scr, y2_scr, w_ref, N_META)
        gather(0, TM, False)
        xhat = _rms_hat(h_meta)
        ks, vs = _kv_rows(xhat, g_kv_ref, w_kv_ref, cosm_ref[...], sinm_ref[...])
        mk_scr[...] = jnp.zeros_like(mk_scr)
        mv_scr[...] = jnp.zeros_like(mv_scr)
        for kvh in range(N_KV_HEADS):
            mk_scr[kvh, META_ROW0:QB, :] = ks[kvh]
            mv_scr[kvh, META_ROW0:QB, :] = vs[kvh]
        qi = lax.broadcasted_iota(jnp.int32, (QB, 2 * QB), 0)
        kj = lax.broadcasted_iota(jnp.int32, (QB, 2 * QB), 1)
        band = jnp.logical_and(kj > qi, kj <= qi + QB)
        bias_scr[0] = jnp.where(band, 0.0, NEG_INF)
        bias_scr[1] = jnp.where(jnp.logical_and(band, kj >= META_ROW0), 0.0, NEG_INF)

    @pl.when(s > 0)
    def _():
        batch_first = jnp.logical_or(s == 1, s == 1 + tiles_per_batch)

        @pl.when(batch_first)
        def _():
            k_scr[:, 0:QB, :] = mk_scr[...]
            v_scr[:, 0:QB, :] = mv_scr[...]

        tile = s - 1

        @pl.when(tile == TILES_PER_HALF)
        def _():
            _load_half(ys_hbm, ys_scr, ysem, 1, HALF0_ROWS0, HALF1_ROWS0)
            gather(TILES_PER_HALF * TM, TM, False)

        h = h_ref[...] + _weighted_sum(y1_scr, y2_scr, w_ref, TM)
        h_scr[...] = h
        gather(_next_x_tile(tile) * TM, TM, True)
        xhat = _rms_hat(h)
        ks, vs = _kv_rows(xhat, g_kv_ref, w_kv_ref, cos_ref[...], sin_ref[...])
        for kvh in range(N_KV_HEADS):
            k_scr[kvh, QB:QB + TM, :] = ks[kvh]
            v_scr[kvh, QB:QB + TM, :] = vs[kvh]

        xq = (xhat * g_attn_ref[...]).astype(jnp.bfloat16)
        q = jnp.dot(xq, w_q_ref[...], preferred_element_type=jnp.float32)
        for hb in range(N_HEADS // 2):
            q_scr[hb] = q[:, LANES * hb:LANES * (hb + 1)]
        first_bias = jnp.where(batch_first, 1, 0)

        def head_pair(hb, carry):
            kvh = hb // (N_HEADS // N_KV_HEADS // 2)
            lane = lax.broadcasted_iota(jnp.int32, (QB, LANES), 1)
            q_all = _rope(q_scr[hb], cos_ref[...], sin_ref[...]) * (HEAD_DIM ** -0.5)
            for b in range(N_QB):
                qb = q_all[QB * b:QB * (b + 1), :]
                qs = jnp.concatenate([jnp.where(lane < HEAD_DIM, qb, 0.0),
                                      jnp.where(lane < HEAD_DIM, 0.0, qb)], axis=0).astype(jnp.bfloat16)
                kk = k_scr[kvh, QB * b:QB * (b + 2), :]
                vv = v_scr[kvh, QB * b:QB * (b + 2), :]
                sc = lax.dot_general(qs, kk, (((1,), (1,)), ((), ())),
                                     preferred_element_type=jnp.float32)
                bias = bias_scr[first_bias] if b == 0 else bias_scr[0]
                outs = []
                for j in range(2):
                    sj = sc[QB * j:QB * (j + 1), :] + bias
                    sink = sink_ref[2 * hb + j]
                    m = jnp.maximum(jnp.max(sj, axis=-1, keepdims=True), sink)
                    p = jnp.exp(sj - m)
                    den = jnp.sum(p, axis=-1, keepdims=True) + jnp.exp(sink - m)
                    pv = jnp.dot(p.astype(jnp.bfloat16), vv, preferred_element_type=jnp.float32)
                    outs.append(pv * (1.0 / den))
                o_scr[hb, QB * b:QB * (b + 1), :] = jnp.where(
                    lane < HEAD_DIM, outs[0], outs[1]).astype(jnp.bfloat16)
            return carry

        lax.fori_loop(0, N_HEADS // 2, head_pair, 0)

        k_scr[:, 0:QB, :] = k_scr[:, TM:TM + QB, :]
        v_scr[:, 0:QB, :] = v_scr[:, TM:TM + QB, :]

        o = jnp.concatenate([o_scr[hb] for hb in range(N_HEADS // 2)], axis=1)
        h3 = h_scr[...] + jnp.dot(o, w_o_ref[...], preferred_element_type=jnp.float32)
        h3_ref[...] = h3
        _ffn_prologue(h3, g_ffn_ref, wr_ref, br_ref, run_scr, (s - 1) * TM, T_X,
                      xp_ref, code_ref, wts_ref, cnt_ref)


def _attn(pos1, pos2, h1, wcols, ys, rope, g_attn, g_kv, w_q, w_kv, w_o, sinks, g_ffn, wr, br):
    cos_x, sin_x, cos_m, sin_m = rope

    def tile_x(s, p1, p2):
        return jnp.maximum(s - 1, 0)

    def tile_in(s, p1, p2):
        return (_tile_first_meta(s), 0)

    def rope_tile(s, p1, p2):
        return (jnp.maximum(s - 1, 0) % (N_XT // BATCH), 0)

    out_shape = [
        jax.ShapeDtypeStruct((T_X, D_MODEL), jnp.float32),
        jax.ShapeDtypeStruct((T_X * SLAB, LANES), jnp.uint32),
        jax.ShapeDtypeStruct((2, T_X), jnp.int32),
        jax.ShapeDtypeStruct((T_X, W_COLS), jnp.float32),
        jax.ShapeDtypeStruct((N_KEYS, LANES), jnp.int32),
    ]
    return pl.pallas_call(
        _attn_kernel,
        grid_spec=pltpu.PrefetchScalarGridSpec(
            num_scalar_prefetch=2,
            grid=(N_XT + 1,),
            in_specs=[
                pl.BlockSpec((TM, D_MODEL), tile_in),
                pl.BlockSpec((TM, W_COLS), tile_in),
                pl.BlockSpec(memory_space=pl.ANY),
                pl.BlockSpec((TM, LANES), rope_tile),
                pl.BlockSpec((TM, LANES), rope_tile),
                _RESIDENT, _RESIDENT,
                _RESIDENT, _RESIDENT, _RESIDENT, _RESIDENT, _RESIDENT,
                pl.BlockSpec(memory_space=pltpu.SMEM),
                _RESIDENT, _RESIDENT, _RESIDENT,
            ],
            out_specs=[
                pl.BlockSpec((TM, D_MODEL), lambda s, p1, p2: (tile_x(s, p1, p2), 0)),
                pl.BlockSpec((TM * SLAB, LANES), lambda s, p1, p2: (tile_x(s, p1, p2), 0)),
                pl.BlockSpec((2, TM), lambda s, p1, p2: (0, tile_x(s, p1, p2))),
                pl.BlockSpec((TM, W_COLS), lambda s, p1, p2: (tile_x(s, p1, p2), 0)),
                _RESIDENT,
            ],
            scratch_shapes=[
                pltpu.VMEM((N_KV_HEADS, QB + TM, LANES), jnp.bfloat16),
                pltpu.VMEM((N_KV_HEADS, QB + TM, LANES), jnp.bfloat16),
                pltpu.VMEM((N_KV_HEADS, QB, LANES), jnp.bfloat16),
                pltpu.VMEM((N_KV_HEADS, QB, LANES), jnp.bfloat16),
                pltpu.VMEM((N_HEADS // 2, TM, LANES), jnp.float32),
                pltpu.VMEM((N_HEADS // 2, TM, LANES), jnp.bfloat16),
                pltpu.VMEM((2, QB, 2 * QB), jnp.float32),
                pltpu.VMEM((TM, D_MODEL), jnp.float32),
                pltpu.VMEM((N_KEYS, 1), jnp.float32),
                pltpu.VMEM((HALF0_ROWS0 * SLAB, LANES), jnp.uint32),
                pltpu.VMEM((TM * SLAB, LANES), jnp.uint32),
                pltpu.VMEM((TM * SLAB, LANES), jnp.uint32),
                pltpu.SemaphoreType.DMA,
            ],
        ),
        out_shape=out_shape,
        compiler_params=pltpu.CompilerParams(
            dimension_semantics=("arbitrary",), vmem_limit_bytes=VMEM_LIMIT),
        name="attn_route",
    )(pos1, pos2, h1, wcols, ys, cos_x, sin_x, cos_m, sin_m, g_attn, g_kv, w_q, w_kv, w_o, sinks, g_ffn, wr, br)


def _router_rows(rg_w, rg_b, re_w, re_b):
    wr = jnp.zeros((N_KEYS, D_MODEL), jnp.float32)
    wr = wr.at[0:N_GROUPS].set(rg_w.T).at[8:8 + N_EXPERTS].set(re_w.T)
    br = jnp.zeros((N_KEYS, 1), jnp.float32)
    br = br.at[0:N_GROUPS, 0].set(rg_b).at[8:8 + N_EXPERTS, 0].set(re_b)
    return wr, br


def _rope_tables():
    half = HEAD_DIM // 2
    inv_freq = jnp.tile(ROPE_THETA ** (-jnp.arange(half, dtype=jnp.float32) / half), 4)[None, :]
    sign = jnp.tile(jnp.concatenate([-jnp.ones(half), jnp.ones(half)]), 2).astype(jnp.float32)[None, :]
    ang_hi = (N_META + QB * jnp.arange(SEQ // QB)).astype(jnp.float32)[:, None] * inv_freq
    ang_lo = jnp.arange(QB).astype(jnp.float32)[:, None] * inv_freq
    c_hi, s_hi = jnp.cos(ang_hi)[:, None, :], jnp.sin(ang_hi)[:, None, :]
    c_lo, s_lo = jnp.cos(ang_lo)[None, :, :], jnp.sin(ang_lo)[None, :, :]
    cos_x = (c_hi * c_lo - s_hi * s_lo).reshape(SEQ, LANES)
    sin_x = ((s_hi * c_lo + c_hi * s_lo) * sign[None]).reshape(SEQ, LANES)
    ang_m = jnp.arange(N_META).astype(jnp.float32)[:, None] * inv_freq
    return cos_x, sin_x, jnp.cos(ang_m), jnp.sin(ang_m) * sign


def _moe(cnt, code, xp, wg, wu, wd, layer, t_pad, t_valid):
    pos, off, _, vrow, vgroup, vslot, vnext, nvis = _positions(cnt[:, 0], code, t_pad, t_valid)
    ys = _ffn(vrow, vgroup, vslot, vnext, nvis, pos[0], pos[1], xp, wg, wu, wd, layer, t_pad, t_valid)
    return off, ys


def kernel(x, meta_tokens, conv_norm_g, conv_w_in, conv_w, conv_w_out, kv_norm_g, w_kv, attn_norm_g,
           w_q, w_o, sinks, ffn_norm_g, router_group_w, router_group_b, router_expert_w,
           router_expert_b, w_gate, w_up, w_down, final_norm_g):
    bf = jnp.bfloat16
    x2d = x.reshape(T_X, D_MODEL)
    wr0, br0 = _router_rows(router_group_w[0], router_group_b[0], router_expert_w[0], router_expert_b[0])
    wr1, br1 = _router_rows(router_group_w[1], router_group_b[1], router_expert_w[1], router_expert_b[1])

    h1, xp0, code0, wts0, cnt0 = _mixer0(
        x2d, meta_tokens, conv_norm_g[0].reshape(1, D_MODEL), conv_w_in[0].astype(bf), conv_w[0],
        conv_w_out[0].astype(bf), ffn_norm_g[0].reshape(1, D_MODEL), wr0, br0)
    pos0, ys0 = _moe(cnt0, code0, xp0, w_gate, w_up, w_down, 0, T_PAD0, T_VALID0)

    h3, xp1, code1, wts1, cnt1 = _attn(
        pos0[0], pos0[1], h1, wts0, ys0, _rope_tables(), attn_norm_g[0].reshape(1, D_MODEL),
        kv_norm_g.reshape(1, D_MODEL), w_q[0].astype(bf), w_kv.astype(bf), w_o[0].astype(bf), sinks[0],
        ffn_norm_g[1].reshape(1, D_MODEL), wr1, br1)
    pos1, ys1 = _moe(cnt1, code1, xp1, w_gate, w_up, w_down, 1, T_X, T_X)
    out = _final(pos1[0], pos1[1], h3, wts1, final_norm_g.reshape(1, D_MODEL), ys1)
    return out.reshape(BATCH, SEQ, D_MODEL)
```

```python
import functools

import jax
import jax.numpy as jnp
from jax import lax
from jax.experimental import pallas as pl
from jax.experimental.pallas import tpu as pltpu

D_MODEL = 1024
BATCH = 2
SEQ = 8192
N_META = 16
N_HEADS = 16
HEAD_DIM = 64
N_KV_HEADS = 4
WINDOW = 128
ROPE_THETA = 10000.0
N_GROUPS = 4
EXPERTS_PER_GROUP = 8
N_EXPERTS = N_GROUPS * EXPERTS_PER_GROUP
D_EXPERT = 256
NORM_EPS = 1e-5
NEG_INF = -1e30

TM = 512
N_XT = BATCH * SEQ // TM
T_X = BATCH * SEQ
T_PAD0 = (N_XT + 1) * TM
T_VALID0 = T_X + N_META
HALF_SPLIT = SEQ
N_KEYS = 2 * N_EXPERTS
TMG = 576
LANES = 128
SLAB = D_MODEL // 2 // LANES
VMEM_V7X = 64 * 1024 * 1024
VMEM_LIMIT = VMEM_V7X - 1024 * 1024


def _cdiv(a, b):
    return (a + b - 1) // b


def _rms_hat(x):
    return x * lax.rsqrt(jnp.mean(x * x, axis=-1, keepdims=True) + NORM_EPS)


def _pack_rows(xn):
    half = D_MODEL // 2
    return pltpu.pack_elementwise([xn[:, :half], xn[:, half:]], packed_dtype=jnp.bfloat16)


def _store_slabs(ref, words):
    m = words.shape[0]
    for k in range(SLAB):
        ref[pl.ds(k, m, stride=SLAB), :] = words[:, LANES * k:LANES * (k + 1)]


def _load_slabs(ref, m):
    return jnp.concatenate([ref[pl.ds(k, m, stride=SLAB), :] for k in range(SLAB)], axis=1)


def _unpack_words(words):
    lo = pltpu.unpack_elementwise(words, index=0, packed_dtype=jnp.bfloat16, unpacked_dtype=jnp.float32)
    hi = pltpu.unpack_elementwise(words, index=1, packed_dtype=jnp.bfloat16, unpacked_dtype=jnp.float32)
    return lo, hi


def _route(xn, wr_ref, br_ref, run_scr, tok_base, valid_limit):
    def split(a):
        hi = a.astype(jnp.bfloat16)
        return hi, (a - hi.astype(jnp.float32)).astype(jnp.bfloat16)

    def nt_dot(a, b):
        return lax.dot_general(a, b, (((1,), (1,)), ((), ())), preferred_element_type=jnp.float32)

    w_hi, w_lo = split(wr_ref[...])
    x_hi, x_lo = split(xn)
    logits = nt_dot(w_hi, x_hi) + nt_dot(w_hi, x_lo) + nt_dot(w_lo, x_hi) + br_ref[...]
    g = logits[0:N_GROUPS]
    gmax = jnp.max(g, axis=0, keepdims=True)
    rid_g = lax.broadcasted_iota(jnp.int32, g.shape, 0).astype(jnp.float32)
    g_idx = jnp.min(jnp.where(g == gmax, rid_g, float(N_GROUPS)), axis=0, keepdims=True).astype(jnp.int32)
    g_w = 1.0 / jnp.sum(jnp.exp(g - gmax), axis=0, keepdims=True)
    e_sel = logits[8:8 + EXPERTS_PER_GROUP]
    for gi in range(1, N_GROUPS):
        lo = 8 + EXPERTS_PER_GROUP * gi
        e_sel = jnp.where(g_idx == gi, logits[lo:lo + EXPERTS_PER_GROUP], e_sel)
    rid_e = lax.broadcasted_iota(jnp.int32, e_sel.shape, 0).astype(jnp.float32)
    none = float(EXPERTS_PER_GROUP)
    m1 = jnp.max(e_sel, axis=0, keepdims=True)
    i1f = jnp.min(jnp.where(e_sel == m1, rid_e, none), axis=0, keepdims=True)
    e_rest = jnp.where(rid_e == i1f, -jnp.inf, e_sel)
    m2 = jnp.max(e_rest, axis=0, keepdims=True)
    i2 = jnp.min(jnp.where(e_rest == m2, rid_e, none), axis=0, keepdims=True).astype(jnp.int32)
    i1 = i1f.astype(jnp.int32)
    ex = jnp.exp(m2 - m1)
    den = 1.0 / (1.0 + ex)
    w1 = den * g_w
    w2 = ex * den * g_w

    tok = tok_base + lax.broadcasted_iota(jnp.int32, (1, TM), 1)
    half = jnp.where(jnp.logical_and(tok >= HALF_SPLIT, tok < T_X), N_EXPERTS, 0)
    key1 = half + g_idx * EXPERTS_PER_GROUP + i1
    key2 = half + g_idx * EXPERTS_PER_GROUP + i2
    kid = lax.broadcasted_iota(jnp.int32, (N_KEYS, TM), 0)
    validf = jnp.where(tok < valid_limit, 1.0, 0.0)
    oh1 = jnp.where(kid == key1, validf, 0.0)
    oh2 = jnp.where(kid == key2, validf, 0.0)
    cnt = oh1 + oh2
    tri = jnp.where(lax.broadcasted_iota(jnp.int32, (TM, TM), 0)
                    <= lax.broadcasted_iota(jnp.int32, (TM, TM), 1), 1.0, 0.0).astype(jnp.bfloat16)
    cum = jnp.dot(cnt.astype(jnp.bfloat16), tri, preferred_element_type=jnp.float32)
    before = run_scr[...] + (cum - cnt)
    rank1 = jnp.sum(oh1 * before, axis=0, keepdims=True).astype(jnp.int32)
    rank2 = jnp.sum(oh2 * before, axis=0, keepdims=True).astype(jnp.int32)
    run_scr[...] = run_scr[...] + cum[:, TM - 1:TM]
    code = jnp.concatenate([key1 * 65536 + rank1, key2 * 65536 + rank2], axis=0)
    wts = jnp.concatenate([w1, w2], axis=0)
    return code, wts


W_COLS = 8


def _ffn_prologue(h_new, g_ffn_ref, wr_ref, br_ref, run_scr, tok_base, valid_limit,
                  xp_ref, code_ref, wts_ref, cnt_ref):
    xn2 = _rms_hat(h_new) * g_ffn_ref[...]
    _store_slabs(xp_ref, _pack_rows(xn2))
    code, wts = _route(xn2, wr_ref, br_ref, run_scr, tok_base, valid_limit)
    code_ref[...] = code
    wts_ref[...] = jnp.concatenate([wts, jnp.zeros((W_COLS - 2, TM), jnp.float32)], axis=0).T
    cnt_ref[...] = jnp.broadcast_to(run_scr[...], cnt_ref.shape).astype(jnp.int32)


NC = 1024


def _mixer0_kernel(x_ref, meta_ref, g_conv_ref, w_in_ref, cw_ref, w_out_ref,
                   g_ffn_ref, wr_ref, br_ref,
                   h1_ref, xp_ref, code_ref, wts_ref, cnt_ref,
                   h0_scr, acc_scr, carry_scr, meta_carry_scr, run_scr):
    s = pl.program_id(0)

    @pl.when(s == 0)
    def _():
        h0_scr[...] = jnp.zeros_like(h0_scr)
        h0_scr[0:N_META, :] = meta_ref[...]
        carry_scr[...] = jnp.zeros_like(carry_scr)
        meta_carry_scr[...] = jnp.zeros_like(meta_carry_scr)
        run_scr[...] = jnp.zeros_like(run_scr)

    @pl.when(s > 0)
    def _():
        h0_scr[...] = x_ref[...]

    @pl.when(s == 1 + N_XT // BATCH)
    def _():
        carry_scr[...] = meta_carry_scr[...]

    h0 = h0_scr[...]
    xn = (_rms_hat(h0) * g_conv_ref[...]).astype(jnp.bfloat16)
    row = lax.broadcasted_iota(jnp.int32, (TM, NC), 0)
    for c in range(D_MODEL // NC):
        cols = slice(NC * c, NC * (c + 1))
        gate_c = jnp.dot(xn, w_in_ref[:, D_MODEL + NC * c:D_MODEL + NC * (c + 1)],
                         preferred_element_type=jnp.float32)
        val = jnp.dot(xn, w_in_ref[:, 2 * D_MODEL + NC * c:2 * D_MODEL + NC * (c + 1)],
                      preferred_element_type=jnp.float32)
        u = gate_c * val
        tail = carry_scr[:, cols]
        c1 = tail[7:8, :]
        c2 = tail[6:7, :]
        um1 = jnp.where(row == 0, c1, pltpu.roll(u, 1, 0))
        um2 = jnp.where(row == 0, c2, jnp.where(row == 1, c1, pltpu.roll(u, 2, 0)))
        conv = um2 * cw_ref[0:1, cols] + um1 * cw_ref[1:2, cols] + u * cw_ref[2:3, cols]

        is_meta = s == 0
        meta_tail = u[N_META - 8:N_META, :]
        carry_scr[:, cols] = jnp.where(is_meta, meta_tail, u[TM - 8:TM, :])
        meta_carry_scr[:, cols] = jnp.where(is_meta, meta_tail, meta_carry_scr[:, cols])

        gate_b = jnp.dot(xn, w_in_ref[:, cols], preferred_element_type=jnp.float32)
        gated = (gate_b * conv).astype(jnp.bfloat16)
        part = jnp.dot(gated, w_out_ref[cols, :], preferred_element_type=jnp.float32)
        if c == 0:
            acc_scr[...] = h0 + part
        else:
            acc_scr[...] = acc_scr[...] + part

    h1 = acc_scr[...]
    h1_ref[...] = h1
    tile = jnp.where(s == 0, N_XT, s - 1)
    _ffn_prologue(h1, g_ffn_ref, wr_ref, br_ref, run_scr, tile * TM, T_VALID0,
                  xp_ref, code_ref, wts_ref, cnt_ref)


def _tile_first_meta(s):
    return jnp.where(s == 0, N_XT, s - 1)


_RESIDENT = pl.BlockSpec(memory_space=pltpu.VMEM)


def _mixer0(x2d, meta, g_conv, w_in, cw, w_out, g_ffn, wr, br):
    out_shape = [
        jax.ShapeDtypeStruct((T_PAD0, D_MODEL), jnp.float32),
        jax.ShapeDtypeStruct((T_PAD0 * SLAB, LANES), jnp.uint32),
        jax.ShapeDtypeStruct((2, T_PAD0), jnp.int32),
        jax.ShapeDtypeStruct((T_PAD0, W_COLS), jnp.float32),
        jax.ShapeDtypeStruct((N_KEYS, LANES), jnp.int32),
    ]
    return pl.pallas_call(
        _mixer0_kernel,
        grid=(N_XT + 1,),
        in_specs=[pl.BlockSpec((TM, D_MODEL), lambda s: (jnp.maximum(s - 1, 0), 0))] + [_RESIDENT] * 8,
        out_specs=[
            pl.BlockSpec((TM, D_MODEL), lambda s: (_tile_first_meta(s), 0)),
            pl.BlockSpec((TM * SLAB, LANES), lambda s: (_tile_first_meta(s), 0)),
            pl.BlockSpec((2, TM), lambda s: (0, _tile_first_meta(s))),
            pl.BlockSpec((TM, W_COLS), lambda s: (_tile_first_meta(s), 0)),
            _RESIDENT,
        ],
        out_shape=out_shape,
        scratch_shapes=[
            pltpu.VMEM((TM, D_MODEL), jnp.float32),
            pltpu.VMEM((TM, D_MODEL), jnp.float32),
            pltpu.VMEM((8, D_MODEL), jnp.float32),
            pltpu.VMEM((8, D_MODEL), jnp.float32),
            pltpu.VMEM((N_KEYS, 1), jnp.float32),
        ],
        compiler_params=pltpu.CompilerParams(
            dimension_semantics=("arbitrary",), vmem_limit_bytes=VMEM_LIMIT),
        name="mixer0_route",
    )(x2d, meta, g_conv, w_in, cw, w_out, g_ffn, wr, br)


def _n_tiles(t_valid):
    return _cdiv(2 * t_valid, TMG)


def _n_visits(t_valid):
    return _n_tiles(t_valid) + N_KEYS


def _half0_rows(t_valid):
    return 2 * (HALF_SPLIT + t_valid - T_X)


def _half1_base(half0_rows):
    return half0_rows + TMG


def _sorted_rows(t_valid):
    return (_n_tiles(t_valid) + 2) * TMG


def _positions_kernel(t_pad, t_valid, cnt_ref, code_ref, pos_ref, off_ref, gstart_ref, vrow_ref, vgroup_ref,
                      vslot_ref, vnext_ref, nvis_ref, nexte_scr):
    n_vis = _n_visits(t_valid)

    def offs(g, acc):
        acc = jnp.where(g == N_EXPERTS, _half1_base(_half0_rows(t_valid)), acc)
        gstart_ref[g] = acc
        return acc + cnt_ref[g]

    total = lax.fori_loop(0, N_KEYS, offs, jnp.int32(0))
    gstart_ref[N_KEYS] = total

    def rows_of(e):
        return cnt_ref[e] + cnt_ref[e + N_EXPERTS]

    def next_nonempty(i, nxt):
        e = N_EXPERTS - 1 - i
        nexte_scr[e] = nxt
        return jnp.where(rows_of(e) > 0, e, nxt)

    lax.fori_loop(0, N_EXPERTS, next_nonempty, jnp.int32(-1))

    def per_expert(e, carry):
        v, last_g, rank = carry
        for half in range(2):
            g = e + half * N_EXPERTS
            c = cnt_ref[g]
            start = gstart_ref[g]

            def per_window(k, vv, g=g, start=start):
                vrow_ref[vv] = start + k * TMG
                vgroup_ref[vv] = g
                vslot_ref[vv] = rank % 2
                vnext_ref[vv] = nexte_scr[e]
                return vv + 1

            v = lax.fori_loop(0, (c + TMG - 1) // TMG, per_window, v)
            last_g = jnp.where(c > 0, g, last_g)
        return v, last_g, jnp.where(rows_of(e) > 0, rank + 1, rank)

    nvis, last_g, _ = lax.fori_loop(0, N_EXPERTS, per_expert, (jnp.int32(0), jnp.int32(0), jnp.int32(0)))
    nvis_ref[0] = nvis

    def pad(vv, c):
        vrow_ref[vv] = 0
        vgroup_ref[vv] = last_g
        vslot_ref[vv] = 0
        vnext_ref[vv] = -1
        return c

    lax.fori_loop(nvis, n_vis, pad, 0)

    code = code_ref[...]
    key = code >> 16
    pos = code & 0xFFFF
    for g in range(N_KEYS):
        pos = pos + jnp.where(key == g, gstart_ref[g], 0)
    pos_ref[...] = pos
    half_base = jnp.where(key >= N_EXPERTS, _half1_base(_half0_rows(t_valid)), 0)
    off_ref[...] = (pos - half_base) * SLAB


def _positions(cnt, code, t_pad, t_valid):
    n_vis = _n_visits(t_valid)
    smem = pl.BlockSpec(memory_space=pltpu.SMEM)
    vmem = pl.BlockSpec(memory_space=pltpu.VMEM)
    return pl.pallas_call(
        functools.partial(_positions_kernel, t_pad, t_valid),
        in_specs=[smem, vmem],
        out_specs=[vmem, vmem, smem, smem, smem, smem, smem, smem],
        out_shape=[
            jax.ShapeDtypeStruct((2, t_pad), jnp.int32),
            jax.ShapeDtypeStruct((2, t_pad), jnp.int32),
            jax.ShapeDtypeStruct((N_KEYS + 1,), jnp.int32),
            jax.ShapeDtypeStruct((n_vis,), jnp.int32),
            jax.ShapeDtypeStruct((n_vis,), jnp.int32),
            jax.ShapeDtypeStruct((n_vis,), jnp.int32),
            jax.ShapeDtypeStruct((n_vis,), jnp.int32),
            jax.ShapeDtypeStruct((1,), jnp.int32),
        ],
        scratch_shapes=[pltpu.SMEM((N_EXPERTS,), jnp.int32)],
        name="sort_positions",
    )(cnt, code)


SRC_UNROLL = 16


def _invert_positions(t_valid, pos1_ref, pos2_ref, src_ref):
    def slack(r, c):
        src_ref[r] = 0
        return c

    half0_rows = _half0_rows(t_valid)
    lax.fori_loop(half0_rows, _half1_base(half0_rows), slack, 0)
    lax.fori_loop(2 * t_valid + TMG, _sorted_rows(t_valid), slack, 0)

    def body(i, c):
        for j in range(SRC_UNROLL):
            t = i * SRC_UNROLL + j
            src_ref[pos1_ref[t]] = t
            src_ref[pos2_ref[t]] = t
        return c

    lax.fori_loop(0, t_valid // SRC_UNROLL, body, 0)


GATHER_UNROLL = 8


def _ffn_kernel(layer, t_valid, vrow_ref, vgroup_ref, vslot_ref, vnext_ref, nvis_ref, pos1_ref, pos2_ref,
                xp_hbm, wg_hbm, wu_hbm, wd_hbm, ys_hbm,
                xp_scr, wg_scr, wu_scr, wd_scr, xnext_scr, out_scr, src_ref, xsem, wsem, osem):
    nvis = nvis_ref[0]

    def expert_of(visit):
        return vgroup_ref[visit] % N_EXPERTS

    def weight_copies(e, sl):
        return (pltpu.make_async_copy(wg_hbm.at[layer, e], wg_scr.at[sl], wsem.at[sl, 0]),
                pltpu.make_async_copy(wu_hbm.at[layer, e], wu_scr.at[sl], wsem.at[sl, 1]),
                pltpu.make_async_copy(wd_hbm.at[layer, e], wd_scr.at[sl], wsem.at[sl, 2]))

    def out_copy(visit):
        sl = visit % 2
        return pltpu.make_async_copy(
            out_scr.at[sl], ys_hbm.at[pl.ds(pl.multiple_of(vrow_ref[visit] * SLAB, SLAB), TMG * SLAB), :],
            osem.at[sl])

    def gather_rows(visit, unrolled):
        base = vrow_ref[visit]

        def one(r):
            tok = src_ref[base + r]
            xnext_scr[pl.ds(pl.multiple_of(r * SLAB, SLAB), SLAB), :] = (
                xp_scr[pl.ds(pl.multiple_of(tok * SLAB, SLAB), SLAB), :])

        if unrolled:
            for r in range(TMG):
                one(r)
        else:
            def chunk(i, c):
                for j in range(GATHER_UNROLL):
                    one(i * GATHER_UNROLL + j)
                return c

            lax.fori_loop(0, TMG // GATHER_UNROLL, chunk, 0)

    cp = pltpu.make_async_copy(xp_hbm, xp_scr, xsem)
    cp.start()
    for c in weight_copies(expert_of(0), vslot_ref[0]):
        c.start()
    out_scr[...] = jnp.zeros_like(out_scr)
    cap = ys_hbm.shape[0] // SLAB
    for first_row in (_half0_rows(t_valid), cap - 2 * TMG, cap - TMG):
        fill = pltpu.make_async_copy(
            out_scr.at[0], ys_hbm.at[pl.ds(first_row * SLAB, TMG * SLAB), :], osem.at[0])
        fill.start()
        fill.wait()
    _invert_positions(t_valid, pos1_ref, pos2_ref, src_ref)
    cp.wait()
    gather_rows(0, False)

    def visit(v, carry):
        e = expert_of(v)
        slot = vslot_ref[v]

        @pl.when(jnp.logical_or(v == 0, e != expert_of(jnp.maximum(v - 1, 0))))
        def _():
            for c in weight_copies(e, slot):
                c.wait()
            nxt = vnext_ref[v]

            @pl.when(nxt >= 0)
            def _():
                for c in weight_copies(nxt, 1 - slot):
                    c.start()

        lo, hi = _unpack_words(_load_slabs(xnext_scr, TMG))
        xs = jnp.concatenate([lo, hi], axis=1).astype(jnp.bfloat16)
        gather_rows(jnp.minimum(v + 1, nvis - 1), True)
        hg = jnp.dot(xs, wg_scr[slot].astype(jnp.bfloat16), preferred_element_type=jnp.float32)
        hu = jnp.dot(xs, wu_scr[slot].astype(jnp.bfloat16), preferred_element_type=jnp.float32)
        hdn = (hg * jax.nn.sigmoid(hg) * hu).astype(jnp.bfloat16)
        y = jnp.dot(hdn, wd_scr[slot].astype(jnp.bfloat16), preferred_element_type=jnp.float32)
        _store_slabs(out_scr.at[v % 2], _pack_rows(y))

        @pl.when(v > 0)
        def _():
            out_copy(v - 1).wait()

        out_copy(v).start()
        return carry

    lax.fori_loop(0, nvis, visit, 0)
    out_copy(nvis - 1).wait()


def _ffn(vrow, vgroup, vslot, vnext, nvis, pos1, pos2, xp, wg, wu, wd, layer, t_pad, t_valid):
    any_spec = pl.BlockSpec(memory_space=pl.ANY)
    return pl.pallas_call(
        functools.partial(_ffn_kernel, layer, t_valid),
        grid_spec=pltpu.PrefetchScalarGridSpec(
            num_scalar_prefetch=7,
            grid=(1,),
            in_specs=[any_spec, any_spec, any_spec, any_spec],
            out_specs=any_spec,
            scratch_shapes=[
                pltpu.VMEM((t_pad * SLAB, LANES), jnp.uint32),
                pltpu.VMEM((2, D_MODEL, D_EXPERT), jnp.float32),
                pltpu.VMEM((2, D_MODEL, D_EXPERT), jnp.float32),
                pltpu.VMEM((2, D_EXPERT, D_MODEL), jnp.float32),
                pltpu.VMEM((TMG * SLAB, LANES), jnp.uint32),
                pltpu.VMEM((2, TMG * SLAB, LANES), jnp.uint32),
                pltpu.SMEM((_sorted_rows(t_valid),), jnp.int32),
                pltpu.SemaphoreType.DMA,
                pltpu.SemaphoreType.DMA((2, 3)),
                pltpu.SemaphoreType.DMA((2,)),
            ],
        ),
        out_shape=jax.ShapeDtypeStruct((_sorted_rows(t_valid) * SLAB, LANES), jnp.uint32),
        compiler_params=pltpu.CompilerParams(
            dimension_semantics=("arbitrary",), vmem_limit_bytes=VMEM_LIMIT),
        name="expert_ffn",
    )(vrow, vgroup, vslot, vnext, nvis, pos1, pos2, xp, wg, wu, wd)


TILES_PER_HALF = HALF_SPLIT // TM


def _load_half(ys_hbm, ys_scr, sem, half, half0_rows, half1_rows):
    start, rows = (0, half0_rows) if half == 0 else (_half1_base(half0_rows), half1_rows)
    cp = pltpu.make_async_copy(ys_hbm.at[pl.ds(start * SLAB, rows * SLAB), :],
                               ys_scr.at[pl.ds(0, rows * SLAB), :], sem)
    cp.start()
    cp.wait()


def _gather_pairs(off1_ref, off2_ref, ys_scr, y1_scr, y2_scr, tok0, n, unrolled):
    def one(r):
        dst = pl.ds(pl.multiple_of(r * SLAB, SLAB), SLAB)
        y1_scr[dst, :] = ys_scr[pl.ds(pl.multiple_of(off1_ref[tok0 + r], SLAB), SLAB), :]
        y2_scr[dst, :] = ys_scr[pl.ds(pl.multiple_of(off2_ref[tok0 + r], SLAB), SLAB), :]

    if unrolled:
        for r in range(n):
            one(r)
    else:
        def chunk(i, c):
            for j in range(GATHER_UNROLL):
                one(i * GATHER_UNROLL + j)
            return c

        lax.fori_loop(0, n // GATHER_UNROLL, chunk, 0)


def _weighted_sum(y1_scr, y2_scr, w_ref, m):
    lo1, hi1 = _unpack_words(_load_slabs(y1_scr, m))
    lo2, hi2 = _unpack_words(_load_slabs(y2_scr, m))
    w1 = w_ref[0:m, 0:1]
    w2 = w_ref[0:m, 1:2]
    return jnp.concatenate([w1 * lo1 + w2 * lo2, w1 * hi1 + w2 * hi2], axis=1)


def _next_x_tile(tile):
    nxt = tile + 1
    return jnp.where(jnp.logical_or(nxt == TILES_PER_HALF, nxt == N_XT), tile, nxt)


def _final_kernel(half_rows, off1_ref, off2_ref, h_ref, w_ref, g_ref, ys_hbm,
                  o_ref, ys_scr, y1_scr, y2_scr, sem):
    tile = pl.program_id(0)
    gather = functools.partial(_gather_pairs, off1_ref, off2_ref, ys_scr, y1_scr, y2_scr)

    @pl.when(tile == 0)
    def _():
        _load_half(ys_hbm, ys_scr, sem, 0, half_rows, half_rows)
        gather(0, TM, False)

    @pl.when(tile == TILES_PER_HALF)
    def _():
        _load_half(ys_hbm, ys_scr, sem, 1, half_rows, half_rows)
        gather(TILES_PER_HALF * TM, TM, False)

    h = h_ref[...] + _weighted_sum(y1_scr, y2_scr, w_ref, TM)
    gather(_next_x_tile(tile) * TM, TM, True)
    o_ref[...] = _rms_hat(h) * g_ref[...]


def _final(pos1, pos2, h, wcols, g, ys):
    half_rows = 2 * HALF_SPLIT
    return pl.pallas_call(
        functools.partial(_final_kernel, half_rows),
        grid_spec=pltpu.PrefetchScalarGridSpec(
            num_scalar_prefetch=2,
            grid=(N_XT,),
            in_specs=[
                pl.BlockSpec((TM, D_MODEL), lambda i, p1, p2: (i, 0)),
                pl.BlockSpec((TM, W_COLS), lambda i, p1, p2: (i, 0)),
                _RESIDENT,
                pl.BlockSpec(memory_space=pl.ANY),
            ],
            out_specs=pl.BlockSpec((TM, D_MODEL), lambda i, p1, p2: (i, 0)),
            scratch_shapes=[
                pltpu.VMEM((half_rows * SLAB, LANES), jnp.uint32),
                pltpu.VMEM((TM * SLAB, LANES), jnp.uint32),
                pltpu.VMEM((TM * SLAB, LANES), jnp.uint32),
                pltpu.SemaphoreType.DMA,
            ],
        ),
        out_shape=jax.ShapeDtypeStruct((T_X, D_MODEL), jnp.float32),
        compiler_params=pltpu.CompilerParams(
            dimension_semantics=("arbitrary",), vmem_limit_bytes=VMEM_LIMIT),
        name="moe_combine_final",
    )(pos1, pos2, h, wcols, g, ys)


QB = WINDOW
KV_W = N_KV_HEADS * HEAD_DIM
N_QB = TM // QB
META_ROW0 = QB - N_META


def _rope(x, cos, sin_signed):
    q = lax.broadcasted_iota(jnp.int32, x.shape, 1) // (HEAD_DIM // 2)
    swapped = jnp.where(q % 2 == 0, pltpu.roll(x, LANES - HEAD_DIM // 2, 1),
                        pltpu.roll(x, HEAD_DIM // 2, 1))
    return x * cos + swapped * sin_signed


def _dup_heads(blk):
    lane = lax.broadcasted_iota(jnp.int32, blk.shape, 1)
    rolled = pltpu.roll(blk, HEAD_DIM, 1)
    return jnp.where(lane < HEAD_DIM, blk, rolled), jnp.where(lane < HEAD_DIM, rolled, blk)


def _kv_rows(xhat, g_kv_ref, w_kv_ref, cos, sin_signed):
    xk = (xhat * g_kv_ref[...]).astype(jnp.bfloat16)
    kv = jnp.dot(xk, w_kv_ref[...], preferred_element_type=jnp.float32)
    ks, vs = [], []
    for b in range(KV_W // LANES):
        kb = _rope(kv[:, LANES * b:LANES * (b + 1)], cos, sin_signed)
        vb = kv[:, KV_W + LANES * b:KV_W + LANES * (b + 1)]
        ks.extend(_dup_heads(kb))
        vs.extend(_dup_heads(vb))
    return [k.astype(jnp.bfloat16) for k in ks], [v.astype(jnp.bfloat16) for v in vs]


HALF0_ROWS0 = 2 * (HALF_SPLIT + N_META)
HALF1_ROWS0 = 2 * HALF_SPLIT


def _attn_kernel(off1_ref, off2_ref,
                 h_ref, w_ref, ys_hbm, cos_ref, sin_ref, cosm_ref, sinm_ref, g_attn_ref, g_kv_ref, w_q_ref,
                 w_kv_ref, w_o_ref, sink_ref, g_ffn_ref, wr_ref, br_ref,
                 h3_ref, xp_ref, code_ref, wts_ref, cnt_ref,
                 k_scr, v_scr, mk_scr, mv_scr, q_scr, o_scr, bias_scr, h_scr, run_scr,
                 ys_scr, y1_scr, y2_scr, ysem):
    s = pl.program_id(0)
    tiles_per_batch = N_XT // BATCH
    gather = functools.partial(_gather_pairs, off1_ref, off2_ref, ys_scr, y1_scr, y2_scr)

    @pl.when(s == 0)
    def _():
        run_scr[...] = jnp.zeros_like(run_scr)
        _load_half(ys_hbm, ys_scr, ysem, 0, HALF0_ROWS0, HALF1_ROWS0)
        gather(N_XT * TM, N_META, True)
        h_meta = h_ref[0:N_META, :] + _weighted_sum(y1_scr, y2_scr, w_ref, N_META)
        gather(0, TM, False)
        xhat = _rms_hat(h_meta)
        ks, vs = _kv_rows(xhat, g_kv_ref, w_kv_ref, cosm_ref[...], sinm_ref[...])
        mk_scr[...] = jnp.zeros_like(mk_scr)
        mv_scr[...] = jnp.zeros_like(mv_scr)
        for kvh in range(N_KV_HEADS):
            mk_scr[kvh, META_ROW0:QB, :] = ks[kvh]
            mv_scr[kvh, META_ROW0:QB, :] = vs[kvh]
        qi = lax.broadcasted_iota(jnp.int32, (QB, 2 * QB), 0)
        kj = lax.broadcasted_iota(jnp.int32, (QB, 2 * QB), 1)
        band = jnp.logical_and(kj > qi, kj <= qi + QB)
        bias_scr[0] = jnp.where(band, 0.0, NEG_INF)
        bias_scr[1] = jnp.where(jnp.logical_and(band, kj >= META_ROW0), 0.0, NEG_INF)

    @pl.when(s > 0)
    def _():
        batch_first = jnp.logical_or(s == 1, s == 1 + tiles_per_batch)

        @pl.when(batch_first)
        def _():
            k_scr[:, 0:QB, :] = mk_scr[...]
            v_scr[:, 0:QB, :] = mv_scr[...]

        tile = s - 1

        @pl.when(tile == TILES_PER_HALF)
        def _():
            _load_half(ys_hbm, ys_scr, ysem, 1, HALF0_ROWS0, HALF1_ROWS0)
            gather(TILES_PER_HALF * TM, TM, False)

        h = h_ref[...] + _weighted_sum(y1_scr, y2_scr, w_ref, TM)
        h_scr[...] = h
        gather(_next_x_tile(tile) * TM, TM, True)
        xhat = _rms_hat(h)
        ks, vs = _kv_rows(xhat, g_kv_ref, w_kv_ref, cos_ref[...], sin_ref[...])
        for kvh in range(N_KV_HEADS):
            k_scr[kvh, QB:QB + TM, :] = ks[kvh]
            v_scr[kvh, QB:QB + TM, :] = vs[kvh]

        xq = (xhat * g_attn_ref[...]).astype(jnp.bfloat16)
        q = jnp.dot(xq, w_q_ref[...], preferred_element_type=jnp.float32)
        for hb in range(N_HEADS // 2):
            q_scr[hb] = q[:, LANES * hb:LANES * (hb + 1)]
        first_bias = jnp.where(batch_first, 1, 0)

        def head_pair(hb, carry):
            kvh = hb // (N_HEADS // N_KV_HEADS // 2)
            lane = lax.broadcasted_iota(jnp.int32, (QB, LANES), 1)
            q_all = _rope(q_scr[hb], cos_ref[...], sin_ref[...]) * (HEAD_DIM ** -0.5)
            for b in range(N_QB):
                qb = q_all[QB * b:QB * (b + 1), :]
                qs = jnp.concatenate([jnp.where(lane < HEAD_DIM, qb, 0.0),
                                      jnp.where(lane < HEAD_DIM, 0.0, qb)], axis=0).astype(jnp.bfloat16)
                kk = k_scr[kvh, QB * b:QB * (b + 2), :]
                vv = v_scr[kvh, QB * b:QB * (b + 2), :]
                sc = lax.dot_general(qs, kk, (((1,), (1,)), ((), ())),
                                     preferred_element_type=jnp.float32)
                bias = bias_scr[first_bias] if b == 0 else bias_scr[0]
                outs = []
                for j in range(2):
                    sj = sc[QB * j:QB * (j + 1), :] + bias
                    sink = sink_ref[2 * hb + j]
                    m = jnp.maximum(jnp.max(sj, axis=-1, keepdims=True), sink)
                    p = jnp.exp(sj - m)
                    den = jnp.sum(p, axis=-1, keepdims=True) + jnp.exp(sink - m)
                    pv = jnp.dot(p.astype(jnp.bfloat16), vv, preferred_element_type=jnp.float32)
                    outs.append(pv * (1.0 / den))
                o_scr[hb, QB * b:QB * (b + 1), :] = jnp.where(
                    lane < HEAD_DIM, outs[0], outs[1]).astype(jnp.bfloat16)
            return carry

        lax.fori_loop(0, N_HEADS // 2, head_pair, 0)

        k_scr[:, 0:QB, :] = k_scr[:, TM:TM + QB, :]
        v_scr[:, 0:QB, :] = v_scr[:, TM:TM + QB, :]

        o = jnp.concatenate([o_scr[hb] for hb in range(N_HEADS // 2)], axis=1)
        h3 = h_scr[...] + jnp.dot(o, w_o_ref[...], preferred_element_type=jnp.float32)
        h3_ref[...] = h3
        _ffn_prologue(h3, g_ffn_ref, wr_ref, br_ref, run_scr, (s - 1) * TM, T_X,
                      xp_ref, code_ref, wts_ref, cnt_ref)


def _attn(pos1, pos2, h1, wcols, ys, rope, g_attn, g_kv, w_q, w_kv, w_o, sinks, g_ffn, wr, br):
    cos_x, sin_x, cos_m, sin_m = rope

    def tile_x(s, p1, p2):
        return jnp.maximum(s - 1, 0)

    def tile_in(s, p1, p2):
        return (_tile_first_meta(s), 0)

    def rope_tile(s, p1, p2):
        return (jnp.maximum(s - 1, 0) % (N_XT // BATCH), 0)

    out_shape = [
        jax.ShapeDtypeStruct((T_X, D_MODEL), jnp.float32),
        jax.ShapeDtypeStruct((T_X * SLAB, LANES), jnp.uint32),
        jax.ShapeDtypeStruct((2, T_X), jnp.int32),
        jax.ShapeDtypeStruct((T_X, W_COLS), jnp.float32),
        jax.ShapeDtypeStruct((N_KEYS, LANES), jnp.int32),
    ]
    return pl.pallas_call(
        _attn_kernel,
        grid_spec=pltpu.PrefetchScalarGridSpec(
            num_scalar_prefetch=2,
            grid=(N_XT + 1,),
            in_specs=[
                pl.BlockSpec((TM, D_MODEL), tile_in),
                pl.BlockSpec((TM, W_COLS), tile_in),
                pl.BlockSpec(memory_space=pl.ANY),
                pl.BlockSpec((TM, LANES), rope_tile),
                pl.BlockSpec((TM, LANES), rope_tile),
                _RESIDENT, _RESIDENT,
                _RESIDENT, _RESIDENT, _RESIDENT, _RESIDENT, _RESIDENT,
                pl.BlockSpec(memory_space=pltpu.SMEM),
                _RESIDENT, _RESIDENT, _RESIDENT,
            ],
            out_specs=[
                pl.BlockSpec((TM, D_MODEL), lambda s, p1, p2: (tile_x(s, p1, p2), 0)),
                pl.BlockSpec((TM * SLAB, LANES), lambda s, p1, p2: (tile_x(s, p1, p2), 0)),
                pl.BlockSpec((2, TM), lambda s, p1, p2: (0, tile_x(s, p1, p2))),
                pl.BlockSpec((TM, W_COLS), lambda s, p1, p2: (tile_x(s, p1, p2), 0)),
                _RESIDENT,
            ],
            scratch_shapes=[
                pltpu.VMEM((N_KV_HEADS, QB + TM, LANES), jnp.bfloat16),
                pltpu.VMEM((N_KV_HEADS, QB + TM, LANES), jnp.bfloat16),
                pltpu.VMEM((N_KV_HEADS, QB, LANES), jnp.bfloat16),
                pltpu.VMEM((N_KV_HEADS, QB, LANES), jnp.bfloat16),
                pltpu.VMEM((N_HEADS // 2, TM, LANES), jnp.float32),
                pltpu.VMEM((N_HEADS // 2, TM, LANES), jnp.bfloat16),
                pltpu.VMEM((2, QB, 2 * QB), jnp.float32),
                pltpu.VMEM((TM, D_MODEL), jnp.float32),
                pltpu.VMEM((N_KEYS, 1), jnp.float32),
                pltpu.VMEM((HALF0_ROWS0 * SLAB, LANES), jnp.uint32),
                pltpu.VMEM((TM * SLAB, LANES), jnp.uint32),
                pltpu.VMEM((TM * SLAB, LANES), jnp.uint32),
                pltpu.SemaphoreType.DMA,
            ],
        ),
        out_shape=out_shape,
        compiler_params=pltpu.CompilerParams(
            dimension_semantics=("arbitrary",), vmem_limit_bytes=VMEM_LIMIT),
        name="attn_route",
    )(pos1, pos2, h1, wcols, ys, cos_x, sin_x, cos_m, sin_m, g_attn, g_kv, w_q, w_kv, w_o, sinks, g_ffn, wr, br)


def _router_rows(rg_w, rg_b, re_w, re_b):
    wr = jnp.zeros((N_KEYS, D_MODEL), jnp.float32)
    wr = wr.at[0:N_GROUPS].set(rg_w.T).at[8:8 + N_EXPERTS].set(re_w.T)
    br = jnp.zeros((N_KEYS, 1), jnp.float32)
    br = br.at[0:N_GROUPS, 0].set(rg_b).at[8:8 + N_EXPERTS, 0].set(re_b)
    return wr, br


def _rope_tables():
    half = HEAD_DIM // 2
    inv_freq = jnp.tile(ROPE_THETA ** (-jnp.arange(half, dtype=jnp.float32) / half), 4)[None, :]
    sign = jnp.tile(jnp.concatenate([-jnp.ones(half), jnp.ones(half)]), 2).astype(jnp.float32)[None, :]
    ang_hi = (N_META + QB * jnp.arange(SEQ // QB)).astype(jnp.float32)[:, None] * inv_freq
    ang_lo = jnp.arange(QB).astype(jnp.float32)[:, None] * inv_freq
    c_hi, s_hi = jnp.cos(ang_hi)[:, None, :], jnp.sin(ang_hi)[:, None, :]
    c_lo, s_lo = jnp.cos(ang_lo)[None, :, :], jnp.sin(ang_lo)[None, :, :]
    cos_x = (c_hi * c_lo - s_hi * s_lo).reshape(SEQ, LANES)
    sin_x = ((s_hi * c_lo + c_hi * s_lo) * sign[None]).reshape(SEQ, LANES)
    ang_m = jnp.arange(N_META).astype(jnp.float32)[:, None] * inv_freq
    return cos_x, sin_x, jnp.cos(ang_m), jnp.sin(ang_m) * sign


def _moe(cnt, code, xp, wg, wu, wd, layer, t_pad, t_valid):
    pos, off, _, vrow, vgroup, vslot, vnext, nvis = _positions(cnt[:, 0], code, t_pad, t_valid)
    ys = _ffn(vrow, vgroup, vslot, vnext, nvis, pos[0], pos[1], xp, wg, wu, wd, layer, t_pad, t_valid)
    return off, ys


def kernel(x, meta_tokens, conv_norm_g, conv_w_in, conv_w, conv_w_out, kv_norm_g, w_kv, attn_norm_g,
           w_q, w_o, sinks, ffn_norm_g, router_group_w, router_group_b, router_expert_w,
           router_expert_b, w_gate, w_up, w_down, final_norm_g):
    bf = jnp.bfloat16
    x2d = x.reshape(T_X, D_MODEL)
    wr0, br0 = _router_rows(router_group_w[0], router_group_b[0], router_expert_w[0], router_expert_b[0])
    wr1, br1 = _router_rows(router_group_w[1], router_group_b[1], router_expert_w[1], router_expert_b[1])

    h1, xp0, code0, wts0, cnt0 = _mixer0(
        x2d, meta_tokens, conv_norm_g[0].reshape(1, D_MODEL), conv_w_in[0].astype(bf), conv_w[0],
        conv_w_out[0].astype(bf), ffn_norm_g[0].reshape(1, D_MODEL), wr0, br0)
    pos0, ys0 = _moe(cnt0, code0, xp0, w_gate, w_up, w_down, 0, T_PAD0, T_VALID0)

    h3, xp1, code1, wts1, cnt1 = _attn(
        pos0[0], pos0[1], h1, wts0, ys0, _rope_tables(), attn_norm_g[0].reshape(1, D_MODEL),
        kv_norm_g.reshape(1, D_MODEL), w_q[0].astype(bf), w_kv.astype(bf), w_o[0].astype(bf), sinks[0],
        ffn_norm_g[1].reshape(1, D_MODEL), wr1, br1)
    pos1, ys1 = _moe(cnt1, code1, xp1, w_gate, w_up, w_down, 1, T_X, T_X)
    out = _final(pos1[0], pos1[1], h3, wts1, final_norm_g.reshape(1, D_MODEL), ys1)
    return out.reshape(BATCH, SEQ, D_MODEL)
```

```python
import functools

import jax
import jax.numpy as jnp
from jax import lax
from jax.experimental import pallas as pl
from jax.experimental.pallas import tpu as pltpu

D_MODEL = 1024
BATCH = 2
SEQ = 8192
N_META = 16
N_HEADS = 16
HEAD_DIM = 64
N_KV_HEADS = 4
WINDOW = 128
ROPE_THETA = 10000.0
N_GROUPS = 4
EXPERTS_PER_GROUP = 8
N_EXPERTS = N_GROUPS * EXPERTS_PER_GROUP
D_EXPERT = 256
NORM_EPS = 1e-5
NEG_INF = -1e30

TM = 512
N_XT = BATCH * SEQ // TM
T_X = BATCH * SEQ
T_PAD0 = (N_XT + 1) * TM
T_VALID0 = T_X + N_META
HALF_SPLIT = SEQ
N_KEYS = 2 * N_EXPERTS
TMG = 576
LANES = 128
SLAB = D_MODEL // 2 // LANES
VMEM_V7X = 64 * 1024 * 1024
VMEM_LIMIT = VMEM_V7X - 1024 * 1024


def _cdiv(a, b):
    return (a + b - 1) // b


def _rms_hat(x):
    return x * lax.rsqrt(jnp.mean(x * x, axis=-1, keepdims=True) + NORM_EPS)


def _pack_rows(xn):
    half = D_MODEL // 2
    return pltpu.pack_elementwise([xn[:, :half], xn[:, half:]], packed_dtype=jnp.bfloat16)


def _store_slabs(ref, words):
    m = words.shape[0]
    for k in range(SLAB):
        ref[pl.ds(k, m, stride=SLAB), :] = words[:, LANES * k:LANES * (k + 1)]


def _load_slabs(ref, m):
    return jnp.concatenate([ref[pl.ds(k, m, stride=SLAB), :] for k in range(SLAB)], axis=1)


def _unpack_words(words):
    lo = pltpu.unpack_elementwise(words, index=0, packed_dtype=jnp.bfloat16, unpacked_dtype=jnp.float32)
    hi = pltpu.unpack_elementwise(words, index=1, packed_dtype=jnp.bfloat16, unpacked_dtype=jnp.float32)
    return lo, hi


def _route(xn, wr_ref, br_ref, run_scr, tok_base, valid_limit):
    def split(a):
        hi = a.astype(jnp.bfloat16)
        return hi, (a - hi.astype(jnp.float32)).astype(jnp.bfloat16)

    def nt_dot(a, b):
        return lax.dot_general(a, b, (((1,), (1,)), ((), ())), preferred_element_type=jnp.float32)

    w_hi, w_lo = split(wr_ref[...])
    x_hi, x_lo = split(xn)
    logits = nt_dot(w_hi, x_hi) + nt_dot(w_hi, x_lo) + nt_dot(w_lo, x_hi) + br_ref[...]
    g = logits[0:N_GROUPS]
    gmax = jnp.max(g, axis=0, keepdims=True)
    rid_g = lax.broadcasted_iota(jnp.int32, g.shape, 0).astype(jnp.float32)
    g_idx = jnp.min(jnp.where(g == gmax, rid_g, float(N_GROUPS)), axis=0, keepdims=True).astype(jnp.int32)
    g_w = 1.0 / jnp.sum(jnp.exp(g - gmax), axis=0, keepdims=True)
    e_sel = logits[8:8 + EXPERTS_PER_GROUP]
    for gi in range(1, N_GROUPS):
        lo = 8 + EXPERTS_PER_GROUP * gi
        e_sel = jnp.where(g_idx == gi, logits[lo:lo + EXPERTS_PER_GROUP], e_sel)
    rid_e = lax.broadcasted_iota(jnp.int32, e_sel.shape, 0).astype(jnp.float32)
    none = float(EXPERTS_PER_GROUP)
    m1 = jnp.max(e_sel, axis=0, keepdims=True)
    i1f = jnp.min(jnp.where(e_sel == m1, rid_e, none), axis=0, keepdims=True)
    e_rest = jnp.where(rid_e == i1f, -jnp.inf, e_sel)
    m2 = jnp.max(e_rest, axis=0, keepdims=True)
    i2 = jnp.min(jnp.where(e_rest == m2, rid_e, none), axis=0, keepdims=True).astype(jnp.int32)
    i1 = i1f.astype(jnp.int32)
    ex = jnp.exp(m2 - m1)
    den = 1.0 / (1.0 + ex)
    w1 = den * g_w
    w2 = ex * den * g_w

    tok = tok_base + lax.broadcasted_iota(jnp.int32, (1, TM), 1)
    half = jnp.where(jnp.logical_and(tok >= HALF_SPLIT, tok < T_X), N_EXPERTS, 0)
    key1 = half + g_idx * EXPERTS_PER_GROUP + i1
    key2 = half + g_idx * EXPERTS_PER_GROUP + i2
    kid = lax.broadcasted_iota(jnp.int32, (N_KEYS, TM), 0)
    validf = jnp.where(tok < valid_limit, 1.0, 0.0)
    oh1 = jnp.where(kid == key1, validf, 0.0)
    oh2 = jnp.where(kid == key2, validf, 0.0)
    cnt = oh1 + oh2
    tri = jnp.where(lax.broadcasted_iota(jnp.int32, (TM, TM), 0)
                    <= lax.broadcasted_iota(jnp.int32, (TM, TM), 1), 1.0, 0.0).astype(jnp.bfloat16)
    cum = jnp.dot(cnt.astype(jnp.bfloat16), tri, preferred_element_type=jnp.float32)
    before = run_scr[...] + (cum - cnt)
    rank1 = jnp.sum(oh1 * before, axis=0, keepdims=True).astype(jnp.int32)
    rank2 = jnp.sum(oh2 * before, axis=0, keepdims=True).astype(jnp.int32)
    run_scr[...] = run_scr[...] + cum[:, TM - 1:TM]
    code = jnp.concatenate([key1 * 65536 + rank1, key2 * 65536 + rank2], axis=0)
    wts = jnp.concatenate([w1, w2], axis=0)
    return code, wts


W_COLS = 8


def _ffn_prologue(h_new, g_ffn_ref, wr_ref, br_ref, run_scr, tok_base, valid_limit,
                  xp_ref, code_ref, wts_ref, cnt_ref):
    xn2 = _rms_hat(h_new) * g_ffn_ref[...]
    _store_slabs(xp_ref, _pack_rows(xn2))
    code, wts = _route(xn2, wr_ref, br_ref, run_scr, tok_base, valid_limit)
    code_ref[...] = code
    wts_ref[...] = jnp.concatenate([wts, jnp.zeros((W_COLS - 2, TM), jnp.float32)], axis=0).T
    cnt_ref[...] = jnp.broadcast_to(run_scr[...], cnt_ref.shape).astype(jnp.int32)


NC = 1024


def _mixer0_kernel(x_ref, meta_ref, g_conv_ref, w_in_ref, cw_ref, w_out_ref,
                   g_ffn_ref, wr_ref, br_ref,
                   h1_ref, xp_ref, code_ref, wts_ref, cnt_ref,
                   h0_scr, acc_scr, carry_scr, meta_carry_scr, run_scr):
    s = pl.program_id(0)

    @pl.when(s == 0)
    def _():
        h0_scr[...] = jnp.zeros_like(h0_scr)
        h0_scr[0:N_META, :] = meta_ref[...]
        carry_scr[...] = jnp.zeros_like(carry_scr)
        meta_carry_scr[...] = jnp.zeros_like(meta_carry_scr)
        run_scr[...] = jnp.zeros_like(run_scr)

    @pl.when(s > 0)
    def _():
        h0_scr[...] = x_ref[...]

    @pl.when(s == 1 + N_XT // BATCH)
    def _():
        carry_scr[...] = meta_carry_scr[...]

    h0 = h0_scr[...]
    xn = (_rms_hat(h0) * g_conv_ref[...]).astype(jnp.bfloat16)
    row = lax.broadcasted_iota(jnp.int32, (TM, NC), 0)
    for c in range(D_MODEL // NC):
        cols = slice(NC * c, NC * (c + 1))
        gate_c = jnp.dot(xn, w_in_ref[:, D_MODEL + NC * c:D_MODEL + NC * (c + 1)],
                         preferred_element_type=jnp.float32)
        val = jnp.dot(xn, w_in_ref[:, 2 * D_MODEL + NC * c:2 * D_MODEL + NC * (c + 1)],
                      preferred_element_type=jnp.float32)
        u = gate_c * val
        tail = carry_scr[:, cols]
        c1 = tail[7:8, :]
        c2 = tail[6:7, :]
        um1 = jnp.where(row == 0, c1, pltpu.roll(u, 1, 0))
        um2 = jnp.where(row == 0, c2, jnp.where(row == 1, c1, pltpu.roll(u, 2, 0)))
        conv = um2 * cw_ref[0:1, cols] + um1 * cw_ref[1:2, cols] + u * cw_ref[2:3, cols]

        is_meta = s == 0
        meta_tail = u[N_META - 8:N_META, :]
        carry_scr[:, cols] = jnp.where(is_meta, meta_tail, u[TM - 8:TM, :])
        meta_carry_scr[:, cols] = jnp.where(is_meta, meta_tail, meta_carry_scr[:, cols])

        gate_b = jnp.dot(xn, w_in_ref[:, cols], preferred_element_type=jnp.float32)
        gated = (gate_b * conv).astype(jnp.bfloat16)
        part = jnp.dot(gated, w_out_ref[cols, :], preferred_element_type=jnp.float32)
        if c == 0:
            acc_scr[...] = h0 + part
        else:
            acc_scr[...] = acc_scr[...] + part

    h1 = acc_scr[...]
    h1_ref[...] = h1
    tile = jnp.where(s == 0, N_XT, s - 1)
    _ffn_prologue(h1, g_ffn_ref, wr_ref, br_ref, run_scr, tile * TM, T_VALID0,
                  xp_ref, code_ref, wts_ref, cnt_ref)


def _tile_first_meta(s):
    return jnp.where(s == 0, N_XT, s - 1)


_RESIDENT = pl.BlockSpec(memory_space=pltpu.VMEM)


def _mixer0(x2d, meta, g_conv, w_in, cw, w_out, g_ffn, wr, br):
    out_shape = [
        jax.ShapeDtypeStruct((T_PAD0, D_MODEL), jnp.float32),
        jax.ShapeDtypeStruct((T_PAD0 * SLAB, LANES), jnp.uint32),
        jax.ShapeDtypeStruct((2, T_PAD0), jnp.int32),
        jax.ShapeDtypeStruct((T_PAD0, W_COLS), jnp.float32),
        jax.ShapeDtypeStruct((N_KEYS, LANES), jnp.int32),
    ]
    return pl.pallas_call(
        _mixer0_kernel,
        grid=(N_XT + 1,),
        in_specs=[pl.BlockSpec((TM, D_MODEL), lambda s: (jnp.maximum(s - 1, 0), 0))] + [_RESIDENT] * 8,
        out_specs=[
            pl.BlockSpec((TM, D_MODEL), lambda s: (_tile_first_meta(s), 0)),
            pl.BlockSpec((TM * SLAB, LANES), lambda s: (_tile_first_meta(s), 0)),
            pl.BlockSpec((2, TM), lambda s: (0, _tile_first_meta(s))),
            pl.BlockSpec((TM, W_COLS), lambda s: (_tile_first_meta(s), 0)),
            _RESIDENT,
        ],
        out_shape=out_shape,
        scratch_shapes=[
            pltpu.VMEM((TM, D_MODEL), jnp.float32),
            pltpu.VMEM((TM, D_MODEL), jnp.float32),
            pltpu.VMEM((8, D_MODEL), jnp.float32),
            pltpu.VMEM((8, D_MODEL), jnp.float32),
            pltpu.VMEM((N_KEYS, 1), jnp.float32),
        ],
        compiler_params=pltpu.CompilerParams(
            dimension_semantics=("arbitrary",), vmem_limit_bytes=VMEM_LIMIT),
        name="mixer0_route",
    )(x2d, meta, g_conv, w_in, cw, w_out, g_ffn, wr, br)


def _n_tiles(t_valid):
    return _cdiv(2 * t_valid, TMG)


def _n_visits(t_valid):
    return _n_tiles(t_valid) + N_KEYS


def _half0_rows(t_valid):
    return 2 * (HALF_SPLIT + t_valid - T_X)


def _half1_base(half0_rows):
    return half0_rows + TMG


def _sorted_rows(t_valid):
    return (_n_tiles(t_valid) + 2) * TMG


def _sort_bookkeeping(t_valid, cnt_ref, code_ref, pos_ref, off_ref, gstart_ref, vrow_ref, vgroup_ref,
                      vslot_ref, vnext_ref, nvis_ref, nexte_scr):
    n_vis = _n_visits(t_valid)

    def offs(g, acc):
        acc = jnp.where(g == N_EXPERTS, _half1_base(_half0_rows(t_valid)), acc)
        gstart_ref[g] = acc
        return acc + cnt_ref[g]

    total = lax.fori_loop(0, N_KEYS, offs, jnp.int32(0))
    gstart_ref[N_KEYS] = total

    def rows_of(e):
        return cnt_ref[e] + cnt_ref[e + N_EXPERTS]

    def next_nonempty(i, nxt):
        e = N_EXPERTS - 1 - i
        nexte_scr[e] = nxt
        return jnp.where(rows_of(e) > 0, e, nxt)

    lax.fori_loop(0, N_EXPERTS, next_nonempty, jnp.int32(-1))

    def per_expert(e, carry):
        v, last_g, rank = carry
        for half in range(2):
            g = e + half * N_EXPERTS
            c = cnt_ref[g]
            start = gstart_ref[g]

            def per_window(k, vv, g=g, start=start):
                vrow_ref[vv] = start + k * TMG
                vgroup_ref[vv] = g
                vslot_ref[vv] = rank % 2
                vnext_ref[vv] = nexte_scr[e]
                return vv + 1

            v = lax.fori_loop(0, (c + TMG - 1) // TMG, per_window, v)
            last_g = jnp.where(c > 0, g, last_g)
        return v, last_g, jnp.where(rows_of(e) > 0, rank + 1, rank)

    nvis, last_g, _ = lax.fori_loop(0, N_EXPERTS, per_expert, (jnp.int32(0), jnp.int32(0), jnp.int32(0)))
    nvis_ref[0] = nvis

    def pad(vv, c):
        vrow_ref[vv] = 0
        vgroup_ref[vv] = last_g
        vslot_ref[vv] = 0
        vnext_ref[vv] = -1
        return c

    lax.fori_loop(nvis, n_vis, pad, 0)

    code = code_ref[...]
    key = code >> 16
    pos = code & 0xFFFF
    for g in range(N_KEYS):
        pos = pos + jnp.where(key == g, gstart_ref[g], 0)
    pos_ref[...] = pos
    half_base = jnp.where(key >= N_EXPERTS, _half1_base(_half0_rows(t_valid)), 0)
    off_ref[...] = (pos - half_base) * SLAB


SRC_UNROLL = 16


def _invert_positions(t_valid, pos_ref, src_ref):
    def slack(r, c):
        src_ref[r] = 0
        return c

    half0_rows = _half0_rows(t_valid)
    lax.fori_loop(half0_rows, _half1_base(half0_rows), slack, 0)
    lax.fori_loop(2 * t_valid + TMG, _sorted_rows(t_valid), slack, 0)

    def body(i, c):
        for j in range(SRC_UNROLL):
            t = i * SRC_UNROLL + j
            src_ref[pos_ref[0, t]] = t
            src_ref[pos_ref[1, t]] = t
        return c

    lax.fori_loop(0, t_valid // SRC_UNROLL, body, 0)


GATHER_UNROLL = 8


def _ffn_kernel(layer, t_valid, cnt_ref,
                code_ref, xp_hbm, wg_hbm, wu_hbm, wd_hbm, ys_hbm, off_ref,
                xp_scr, wg_scr, wu_scr, wd_scr, xnext_scr, out_scr, pos_scr,
                pos_ref, src_ref, gstart_ref, vrow_ref, vgroup_ref, vslot_ref, vnext_ref, nvis_ref, nexte_scr,
                xsem, wsem, osem, psem):
    cp = pltpu.make_async_copy(xp_hbm, xp_scr, xsem)
    cp.start()
    _sort_bookkeeping(t_valid, cnt_ref, code_ref, pos_scr, off_ref, gstart_ref, vrow_ref, vgroup_ref,
                      vslot_ref, vnext_ref, nvis_ref, nexte_scr)
    pos_copies = [pltpu.make_async_copy(pos_scr.at[k], pos_ref.at[k], psem) for k in range(2)]
    for c in pos_copies:
        c.start()
    nvis = nvis_ref[0]

    def expert_of(visit):
        return vgroup_ref[visit] % N_EXPERTS

    def weight_copies(e, sl):
        return (pltpu.make_async_copy(wg_hbm.at[layer, e], wg_scr.at[sl], wsem.at[sl, 0]),
                pltpu.make_async_copy(wu_hbm.at[layer, e], wu_scr.at[sl], wsem.at[sl, 1]),
                pltpu.make_async_copy(wd_hbm.at[layer, e], wd_scr.at[sl], wsem.at[sl, 2]))

    def out_copy(visit):
        sl = visit % 2
        return pltpu.make_async_copy(
            out_scr.at[sl], ys_hbm.at[pl.ds(pl.multiple_of(vrow_ref[visit] * SLAB, SLAB), TMG * SLAB), :],
            osem.at[sl])

    def gather_rows(visit, unrolled):
        base = vrow_ref[visit]

        def one(r):
            tok = src_ref[base + r]
            xnext_scr[pl.ds(pl.multiple_of(r * SLAB, SLAB), SLAB), :] = (
                xp_scr[pl.ds(pl.multiple_of(tok * SLAB, SLAB), SLAB), :])

        if unrolled:
            for r in range(TMG):
                one(r)
        else:
            def chunk(i, c):
                for j in range(GATHER_UNROLL):
                    one(i * GATHER_UNROLL + j)
                return c

            lax.fori_loop(0, TMG // GATHER_UNROLL, chunk, 0)

    for c in weight_copies(expert_of(0), vslot_ref[0]):
        c.start()
    out_scr[...] = jnp.zeros_like(out_scr)
    cap = ys_hbm.shape[0] // SLAB
    for first_row in (_half0_rows(t_valid), cap - 2 * TMG, cap - TMG):
        fill = pltpu.make_async_copy(
            out_scr.at[0], ys_hbm.at[pl.ds(first_row * SLAB, TMG * SLAB), :], osem.at[0])
        fill.start()
        fill.wait()
    for c in pos_copies:
        c.wait()
    _invert_positions(t_valid, pos_ref, src_ref)
    cp.wait()
    gather_rows(0, False)

    def visit(v, carry):
        e = expert_of(v)
        slot = vslot_ref[v]

        @pl.when(jnp.logical_or(v == 0, e != expert_of(jnp.maximum(v - 1, 0))))
        def _():
            for c in weight_copies(e, slot):
                c.wait()
            nxt = vnext_ref[v]

            @pl.when(nxt >= 0)
            def _():
                for c in weight_copies(nxt, 1 - slot):
                    c.start()

        lo, hi = _unpack_words(_load_slabs(xnext_scr, TMG))
        xs = jnp.concatenate([lo, hi], axis=1).astype(jnp.bfloat16)
        gather_rows(jnp.minimum(v + 1, nvis - 1), True)
        hg = jnp.dot(xs, wg_scr[slot].astype(jnp.bfloat16), preferred_element_type=jnp.float32)
        hu = jnp.dot(xs, wu_scr[slot].astype(jnp.bfloat16), preferred_element_type=jnp.float32)
        hdn = (hg * jax.nn.sigmoid(hg) * hu).astype(jnp.bfloat16)
        y = jnp.dot(hdn, wd_scr[slot].astype(jnp.bfloat16), preferred_element_type=jnp.float32)
        _store_slabs(out_scr.at[v % 2], _pack_rows(y))

        @pl.when(v > 0)
        def _():
            out_copy(v - 1).wait()

        out_copy(v).start()
        return carry

    lax.fori_loop(0, nvis, visit, 0)
    out_copy(nvis - 1).wait()


def _ffn(cnt, code, xp, wg, wu, wd, layer, t_pad, t_valid):
    any_spec = pl.BlockSpec(memory_space=pl.ANY)
    n_vis = _n_visits(t_valid)
    smem_i32 = lambda n: pltpu.SMEM((n,), jnp.int32)
    return pl.pallas_call(
        functools.partial(_ffn_kernel, layer, t_valid),
        grid_spec=pltpu.PrefetchScalarGridSpec(
            num_scalar_prefetch=1,
            grid=(1,),
            in_specs=[_RESIDENT, any_spec, any_spec, any_spec, any_spec],
            out_specs=[any_spec, _RESIDENT],
            scratch_shapes=[
                pltpu.VMEM((t_pad * SLAB, LANES), jnp.uint32),
                pltpu.VMEM((2, D_MODEL, D_EXPERT), jnp.float32),
                pltpu.VMEM((2, D_MODEL, D_EXPERT), jnp.float32),
                pltpu.VMEM((2, D_EXPERT, D_MODEL), jnp.float32),
                pltpu.VMEM((TMG * SLAB, LANES), jnp.uint32),
                pltpu.VMEM((2, TMG * SLAB, LANES), jnp.uint32),
                pltpu.VMEM((2, t_pad), jnp.int32),
                pltpu.SMEM((2, t_pad), jnp.int32),
                smem_i32(_sorted_rows(t_valid)),
                smem_i32(N_KEYS + 1), smem_i32(n_vis), smem_i32(n_vis), smem_i32(n_vis), smem_i32(n_vis),
                smem_i32(1), smem_i32(N_EXPERTS),
                pltpu.SemaphoreType.DMA,
                pltpu.SemaphoreType.DMA((2, 3)),
                pltpu.SemaphoreType.DMA((2,)),
                pltpu.SemaphoreType.DMA,
            ],
        ),
        out_shape=[jax.ShapeDtypeStruct((_sorted_rows(t_valid) * SLAB, LANES), jnp.uint32),
                   jax.ShapeDtypeStruct((2, t_pad), jnp.int32)],
        compiler_params=pltpu.CompilerParams(
            dimension_semantics=("arbitrary",), vmem_limit_bytes=VMEM_LIMIT),
        name="expert_ffn",
    )(cnt, code, xp, wg, wu, wd)


TILES_PER_HALF = HALF_SPLIT // TM


def _load_half(ys_hbm, ys_scr, sem, half, half0_rows, half1_rows):
    start, rows = (0, half0_rows) if half == 0 else (_half1_base(half0_rows), half1_rows)
    cp = pltpu.make_async_copy(ys_hbm.at[pl.ds(start * SLAB, rows * SLAB), :],
                               ys_scr.at[pl.ds(0, rows * SLAB), :], sem)
    cp.start()
    cp.wait()


def _gather_pairs(off1_ref, off2_ref, ys_scr, y1_scr, y2_scr, tok0, n, unrolled):
    def one(r):
        dst = pl.ds(pl.multiple_of(r * SLAB, SLAB), SLAB)
        y1_scr[dst, :] = ys_scr[pl.ds(pl.multiple_of(off1_ref[tok0 + r], SLAB), SLAB), :]
        y2_scr[dst, :] = ys_scr[pl.ds(pl.multiple_of(off2_ref[tok0 + r], SLAB), SLAB), :]

    if unrolled:
        for r in range(n):
            one(r)
    else:
        def chunk(i, c):
            for j in range(GATHER_UNROLL):
                one(i * GATHER_UNROLL + j)
            return c

        lax.fori_loop(0, n // GATHER_UNROLL, chunk, 0)


def _weighted_sum(y1_scr, y2_scr, w_ref, m):
    lo1, hi1 = _unpack_words(_load_slabs(y1_scr, m))
    lo2, hi2 = _unpack_words(_load_slabs(y2_scr, m))
    w1 = w_ref[0:m, 0:1]
    w2 = w_ref[0:m, 1:2]
    return jnp.concatenate([w1 * lo1 + w2 * lo2, w1 * hi1 + w2 * hi2], axis=1)


def _next_x_tile(tile):
    nxt = tile + 1
    return jnp.where(jnp.logical_or(nxt == TILES_PER_HALF, nxt == N_XT), tile, nxt)


def _final_kernel(half_rows, off1_ref, off2_ref, h_ref, w_ref, g_ref, ys_hbm,
                  o_ref, ys_scr, y1_scr, y2_scr, sem):
    tile = pl.program_id(0)
    gather = functools.partial(_gather_pairs, off1_ref, off2_ref, ys_scr, y1_scr, y2_scr)

    @pl.when(tile == 0)
    def _():
        _load_half(ys_hbm, ys_scr, sem, 0, half_rows, half_rows)
        gather(0, TM, False)

    @pl.when(tile == TILES_PER_HALF)
    def _():
        _load_half(ys_hbm, ys_scr, sem, 1, half_rows, half_rows)
        gather(TILES_PER_HALF * TM, TM, False)

    h = h_ref[...] + _weighted_sum(y1_scr, y2_scr, w_ref, TM)
    gather(_next_x_tile(tile) * TM, TM, True)
    o_ref[...] = _rms_hat(h) * g_ref[...]


def _final(pos1, pos2, h, wcols, g, ys):
    half_rows = 2 * HALF_SPLIT
    return pl.pallas_call(
        functools.partial(_final_kernel, half_rows),
        grid_spec=pltpu.PrefetchScalarGridSpec(
            num_scalar_prefetch=2,
            grid=(N_XT,),
            in_specs=[
                pl.BlockSpec((TM, D_MODEL), lambda i, p1, p2: (i, 0)),
                pl.BlockSpec((TM, W_COLS), lambda i, p1, p2: (i, 0)),
                _RESIDENT,
                pl.BlockSpec(memory_space=pl.ANY),
            ],
            out_specs=pl.BlockSpec((TM, D_MODEL), lambda i, p1, p2: (i, 0)),
            scratch_shapes=[
                pltpu.VMEM((half_rows * SLAB, LANES), jnp.uint32),
                pltpu.VMEM((TM * SLAB, LANES), jnp.uint32),
                pltpu.VMEM((TM * SLAB, LANES), jnp.uint32),
                pltpu.SemaphoreType.DMA,
            ],
        ),
        out_shape=jax.ShapeDtypeStruct((T_X, D_MODEL), jnp.float32),
        compiler_params=pltpu.CompilerParams(
            dimension_semantics=("arbitrary",), vmem_limit_bytes=VMEM_LIMIT),
        name="moe_combine_final",
    )(pos1, pos2, h, wcols, g, ys)


QB = WINDOW
KV_W = N_KV_HEADS * HEAD_DIM
N_QB = TM // QB
META_ROW0 = QB - N_META


def _rope(x, cos, sin_signed):
    q = lax.broadcasted_iota(jnp.int32, x.shape, 1) // (HEAD_DIM // 2)
    swapped = jnp.where(q % 2 == 0, pltpu.roll(x, LANES - HEAD_DIM // 2, 1),
                        pltpu.roll(x, HEAD_DIM // 2, 1))
    return x * cos + swapped * sin_signed


def _dup_heads(blk):
    lane = lax.broadcasted_iota(jnp.int32, blk.shape, 1)
    rolled = pltpu.roll(blk, HEAD_DIM, 1)
    return jnp.where(lane < HEAD_DIM, blk, rolled), jnp.where(lane < HEAD_DIM, rolled, blk)


def _kv_rows(xhat, g_kv_ref, w_kv_ref, cos, sin_signed):
    xk = (xhat * g_kv_ref[...]).astype(jnp.bfloat16)
    kv = jnp.dot(xk, w_kv_ref[...], preferred_element_type=jnp.float32)
    ks, vs = [], []
    for b in range(KV_W // LANES):
        kb = _rope(kv[:, LANES * b:LANES * (b + 1)], cos, sin_signed)
        vb = kv[:, KV_W + LANES * b:KV_W + LANES * (b + 1)]
        ks.extend(_dup_heads(kb))
        vs.extend(_dup_heads(vb))
    return [k.astype(jnp.bfloat16) for k in ks], [v.astype(jnp.bfloat16) for v in vs]


HALF0_ROWS0 = 2 * (HALF_SPLIT + N_META)
HALF1_ROWS0 = 2 * HALF_SPLIT


def _attn_kernel(off1_ref, off2_ref,
                 h_ref, w_ref, ys_hbm, cos_ref, sin_ref, cosm_ref, sinm_ref, g_attn_ref, g_kv_ref, w_q_ref,
                 w_kv_ref, w_o_ref, sink_ref, g_ffn_ref, wr_ref, br_ref,
                 h3_ref, xp_ref, code_ref, wts_ref, cnt_ref,
                 k_scr, v_scr, mk_scr, mv_scr, q_scr, o_scr, bias_scr, h_scr, run_scr,
                 ys_scr, y1_scr, y2_scr, ysem):
    s = pl.program_id(0)
    tiles_per_batch = N_XT // BATCH
    gather = functools.partial(_gather_pairs, off1_ref, off2_ref, ys_scr, y1_scr, y2_scr)

    @pl.when(s == 0)
    def _():
        run_scr[...] = jnp.zeros_like(run_scr)
        _load_half(ys_hbm, ys_scr, ysem, 0, HALF0_ROWS0, HALF1_ROWS0)
        gather(N_XT * TM, N_META, True)
        h_meta = h_ref[0:N_META, :] + _weighted_sum(y1_scr, y2_scr, w_ref, N_META)
        gather(0, TM, False)
        xhat = _rms_hat(h_meta)
        ks, vs = _kv_rows(xhat, g_kv_ref, w_kv_ref, cosm_ref[...], sinm_ref[...])
        mk_scr[...] = jnp.zeros_like(mk_scr)
        mv_scr[...] = jnp.zeros_like(mv_scr)
        for kvh in range(N_KV_HEADS):
            mk_scr[kvh, META_ROW0:QB, :] = ks[kvh]
            mv_scr[kvh, META_ROW0:QB, :] = vs[kvh]
        qi = lax.broadcasted_iota(jnp.int32, (QB, 2 * QB), 0)
        kj = lax.broadcasted_iota(jnp.int32, (QB, 2 * QB), 1)
        band = jnp.logical_and(kj > qi, kj <= qi + QB)
        bias_scr[0] = jnp.where(band, 0.0, NEG_INF)
        bias_scr[1] = jnp.where(jnp.logical_and(band, kj >= META_ROW0), 0.0, NEG_INF)

    @pl.when(s > 0)
    def _():
        batch_first = jnp.logical_or(s == 1, s == 1 + tiles_per_batch)

        @pl.when(batch_first)
        def _():
            k_scr[:, 0:QB, :] = mk_scr[...]
            v_scr[:, 0:QB, :] = mv_scr[...]

        tile = s - 1

        @pl.when(tile == TILES_PER_HALF)
        def _():
            _load_half(ys_hbm, ys_scr, ysem, 1, HALF0_ROWS0, HALF1_ROWS0)
            gather(TILES_PER_HALF * TM, TM, False)

        h = h_ref[...] + _weighted_sum(y1_scr, y2_scr, w_ref, TM)
        h_scr[...] = h
        gather(_next_x_tile(tile) * TM, TM, True)
        xhat = _rms_hat(h)
        ks, vs = _kv_rows(xhat, g_kv_ref, w_kv_ref, cos_ref[...], sin_ref[...])
        for kvh in range(N_KV_HEADS):
            k_scr[kvh, QB:QB + TM, :] = ks[kvh]
            v_scr[kvh, QB:QB + TM, :] = vs[kvh]

        xq = (xhat * g_attn_ref[...]).astype(jnp.bfloat16)
        q = jnp.dot(xq, w_q_ref[...], preferred_element_type=jnp.float32)
        for hb in range(N_HEADS // 2):
            q_scr[hb] = q[:, LANES * hb:LANES * (hb + 1)]
        first_bias = jnp.where(batch_first, 1, 0)

        def head_pair(hb, carry):
            kvh = hb // (N_HEADS // N_KV_HEADS // 2)
            lane = lax.broadcasted_iota(jnp.int32, (QB, LANES), 1)
            q_all = _rope(q_scr[hb], cos_ref[...], sin_ref[...]) * (HEAD_DIM ** -0.5)
            for b in range(N_QB):
                qb = q_all[QB * b:QB * (b + 1), :]
                qs = jnp.concatenate([jnp.where(lane < HEAD_DIM, qb, 0.0),
                                      jnp.where(lane < HEAD_DIM, 0.0, qb)], axis=0).astype(jnp.bfloat16)
                kk = k_scr[kvh, QB * b:QB * (b + 2), :]
                vv = v_scr[kvh, QB * b:QB * (b + 2), :]
                sc = lax.dot_general(qs, kk, (((1,), (1,)), ((), ())),
                                     preferred_element_type=jnp.float32)
                bias = bias_scr[first_bias] if b == 0 else bias_scr[0]
                outs = []
                for j in range(2):
                    sj = sc[QB * j:QB * (j + 1), :] + bias
                    sink = sink_ref[2 * hb + j]
                    m = jnp.maximum(jnp.max(sj, axis=-1, keepdims=True), sink)
                    p = jnp.exp(sj - m)
                    den = jnp.sum(p, axis=-1, keepdims=True) + jnp.exp(sink - m)
                    pv = jnp.dot(p.astype(jnp.bfloat16), vv, preferred_element_type=jnp.float32)
                    outs.append(pv * (1.0 / den))
                o_scr[hb, QB * b:QB * (b + 1), :] = jnp.where(
                    lane < HEAD_DIM, outs[0], outs[1]).astype(jnp.bfloat16)
            return carry

        lax.fori_loop(0, N_HEADS // 2, head_pair, 0)

        k_scr[:, 0:QB, :] = k_scr[:, TM:TM + QB, :]
        v_scr[:, 0:QB, :] = v_scr[:, TM:TM + QB, :]

        o = jnp.concatenate([o_scr[hb] for hb in range(N_HEADS // 2)], axis=1)
        h3 = h_scr[...] + jnp.dot(o, w_o_ref[...], preferred_element_type=jnp.float32)
        h3_ref[...] = h3
        _ffn_prologue(h3, g_ffn_ref, wr_ref, br_ref, run_scr, (s - 1) * TM, T_X,
                      xp_ref, code_ref, wts_ref, cnt_ref)


def _attn(pos1, pos2, h1, wcols, ys, rope, g_attn, g_kv, w_q, w_kv, w_o, sinks, g_ffn, wr, br):
    cos_x, sin_x, cos_m, sin_m = rope

    def tile_x(s, p1, p2):
        return jnp.maximum(s - 1, 0)

    def tile_in(s, p1, p2):
        return (_tile_first_meta(s), 0)

    def rope_tile(s, p1, p2):
        return (jnp.maximum(s - 1, 0) % (N_XT // BATCH), 0)

    out_shape = [
        jax.ShapeDtypeStruct((T_X, D_MODEL), jnp.float32),
        jax.ShapeDtypeStruct((T_X * SLAB, LANES), jnp.uint32),
        jax.ShapeDtypeStruct((2, T_X), jnp.int32),
        jax.ShapeDtypeStruct((T_X, W_COLS), jnp.float32),
        jax.ShapeDtypeStruct((N_KEYS, LANES), jnp.int32),
    ]
    return pl.pallas_call(
        _attn_kernel,
        grid_spec=pltpu.PrefetchScalarGridSpec(
            num_scalar_prefetch=2,
            grid=(N_XT + 1,),
            in_specs=[
                pl.BlockSpec((TM, D_MODEL), tile_in),
                pl.BlockSpec((TM, W_COLS), tile_in),
                pl.BlockSpec(memory_space=pl.ANY),
                pl.BlockSpec((TM, LANES), rope_tile),
                pl.BlockSpec((TM, LANES), rope_tile),
                _RESIDENT, _RESIDENT,
                _RESIDENT, _RESIDENT, _RESIDENT, _RESIDENT, _RESIDENT,
                pl.BlockSpec(memory_space=pltpu.SMEM),
                _RESIDENT, _RESIDENT, _RESIDENT,
            ],
            out_specs=[
                pl.BlockSpec((TM, D_MODEL), lambda s, p1, p2: (tile_x(s, p1, p2), 0)),
                pl.BlockSpec((TM * SLAB, LANES), lambda s, p1, p2: (tile_x(s, p1, p2), 0)),
                pl.BlockSpec((2, TM), lambda s, p1, p2: (0, tile_x(s, p1, p2))),
                pl.BlockSpec((TM, W_COLS), lambda s, p1, p2: (tile_x(s, p1, p2), 0)),
                _RESIDENT,
            ],
            scratch_shapes=[
                pltpu.VMEM((N_KV_HEADS, QB + TM, LANES), jnp.bfloat16),
                pltpu.VMEM((N_KV_HEADS, QB + TM, LANES), jnp.bfloat16),
                pltpu.VMEM((N_KV_HEADS, QB, LANES), jnp.bfloat16),
                pltpu.VMEM((N_KV_HEADS, QB, LANES), jnp.bfloat16),
                pltpu.VMEM((N_HEADS // 2, TM, LANES), jnp.float32),
                pltpu.VMEM((N_HEADS // 2, TM, LANES), jnp.bfloat16),
                pltpu.VMEM((2, QB, 2 * QB), jnp.float32),
                pltpu.VMEM((TM, D_MODEL), jnp.float32),
                pltpu.VMEM((N_KEYS, 1), jnp.float32),
                pltpu.VMEM((HALF0_ROWS0 * SLAB, LANES), jnp.uint32),
                pltpu.VMEM((TM * SLAB, LANES), jnp.uint32),
                pltpu.VMEM((TM * SLAB, LANES), jnp.uint32),
                pltpu.SemaphoreType.DMA,
            ],
        ),
        out_shape=out_shape,
        compiler_params=pltpu.CompilerParams(
            dimension_semantics=("arbitrary",), vmem_limit_bytes=VMEM_LIMIT),
        name="attn_route",
    )(pos1, pos2, h1, wcols, ys, cos_x, sin_x, cos_m, sin_m, g_attn, g_kv, w_q, w_kv, w_o, sinks, g_ffn, wr, br)


def _router_rows(rg_w, rg_b, re_w, re_b):
    wr = jnp.zeros((N_KEYS, D_MODEL), jnp.float32)
    wr = wr.at[0:N_GROUPS].set(rg_w.T).at[8:8 + N_EXPERTS].set(re_w.T)
    br = jnp.zeros((N_KEYS, 1), jnp.float32)
    br = br.at[0:N_GROUPS, 0].set(rg_b).at[8:8 + N_EXPERTS, 0].set(re_b)
    return wr, br


def _rope_tables():
    half = HEAD_DIM // 2
    inv_freq = jnp.tile(ROPE_THETA ** (-jnp.arange(half, dtype=jnp.float32) / half), 4)[None, :]
    sign = jnp.tile(jnp.concatenate([-jnp.ones(half), jnp.ones(half)]), 2).astype(jnp.float32)[None, :]
    ang_hi = (N_META + QB * jnp.arange(SEQ // QB)).astype(jnp.float32)[:, None] * inv_freq
    ang_lo = jnp.arange(QB).astype(jnp.float32)[:, None] * inv_freq
    c_hi, s_hi = jnp.cos(ang_hi)[:, None, :], jnp.sin(ang_hi)[:, None, :]
    c_lo, s_lo = jnp.cos(ang_lo)[None, :, :], jnp.sin(ang_lo)[None, :, :]
    cos_x = (c_hi * c_lo - s_hi * s_lo).reshape(SEQ, LANES)
    sin_x = ((s_hi * c_lo + c_hi * s_lo) * sign[None]).reshape(SEQ, LANES)
    ang_m = jnp.arange(N_META).astype(jnp.float32)[:, None] * inv_freq
    return cos_x, sin_x, jnp.cos(ang_m), jnp.sin(ang_m) * sign


def _moe(cnt, code, xp, wg, wu, wd, layer, t_pad, t_valid):
    ys, off = _ffn(cnt[:, 0], code, xp, wg, wu, wd, layer, t_pad, t_valid)
    return off, ys


def kernel(x, meta_tokens, conv_norm_g, conv_w_in, conv_w, conv_w_out, kv_norm_g, w_kv, attn_norm_g,
           w_q, w_o, sinks, ffn_norm_g, router_group_w, router_group_b, router_expert_w,
           router_expert_b, w_gate, w_up, w_down, final_norm_g):
    bf = jnp.bfloat16
    x2d = x.reshape(T_X, D_MODEL)
    wr0, br0 = _router_rows(router_group_w[0], router_group_b[0], router_expert_w[0], router_expert_b[0])
    wr1, br1 = _router_rows(router_group_w[1], router_group_b[1], router_expert_w[1], router_expert_b[1])

    h1, xp0, code0, wts0, cnt0 = _mixer0(
        x2d, meta_tokens, conv_norm_g[0].reshape(1, D_MODEL), conv_w_in[0].astype(bf), conv_w[0],
        conv_w_out[0].astype(bf), ffn_norm_g[0].reshape(1, D_MODEL), wr0, br0)
    pos0, ys0 = _moe(cnt0, code0, xp0, w_gate, w_up, w_down, 0, T_PAD0, T_VALID0)

    h3, xp1, code1, wts1, cnt1 = _attn(
        pos0[0], pos0[1], h1, wts0, ys0, _rope_tables(), attn_norm_g[0].reshape(1, D_MODEL),
        kv_norm_g.reshape(1, D_MODEL), w_q[0].astype(bf), w_kv.astype(bf), w_o[0].astype(bf), sinks[0],
        ffn_norm_g[1].reshape(1, D_MODEL), wr1, br1)
    pos1, ys1 = _moe(cnt1, code1, xp1, w_gate, w_up, w_down, 1, T_X, T_X)
    out = _final(pos1[0], pos1[1], h3, wts1, final_norm_g.reshape(1, D_MODEL), ys1)
    return out.reshape(BATCH, SEQ, D_MODEL)
```

```python
import functools

import jax
import jax.numpy as jnp
from jax import lax
from jax.experimental import pallas as pl
from jax.experimental.pallas import tpu as pltpu

D_MODEL = 1024
BATCH = 2
SEQ = 8192
N_META = 16
N_HEADS = 16
HEAD_DIM = 64
N_KV_HEADS = 4
WINDOW = 128
ROPE_THETA = 10000.0
N_GROUPS = 4
EXPERTS_PER_GROUP = 8
N_EXPERTS = N_GROUPS * EXPERTS_PER_GROUP
D_EXPERT = 256
NORM_EPS = 1e-5
NEG_INF = -1e30

TM = 512
N_XT = BATCH * SEQ // TM
T_X = BATCH * SEQ
T_PAD0 = (N_XT + 1) * TM
T_VALID0 = T_X + N_META
HALF_SPLIT = SEQ
N_KEYS = 2 * N_EXPERTS
TMG = 576
LANES = 128
SLAB = D_MODEL // 2 // LANES
VMEM_V7X = 64 * 1024 * 1024
VMEM_LIMIT = VMEM_V7X - 1024 * 1024


def _cdiv(a, b):
    return (a + b - 1) // b


def _rms_hat(x):
    return x * lax.rsqrt(jnp.mean(x * x, axis=-1, keepdims=True) + NORM_EPS)


def _pack_rows(xn):
    half = D_MODEL // 2
    return pltpu.pack_elementwise([xn[:, :half], xn[:, half:]], packed_dtype=jnp.bfloat16)


def _store_slabs(ref, words):
    m = words.shape[0]
    for k in range(SLAB):
        ref[pl.ds(k, m, stride=SLAB), :] = words[:, LANES * k:LANES * (k + 1)]


def _load_slabs(ref, m):
    return jnp.concatenate([ref[pl.ds(k, m, stride=SLAB), :] for k in range(SLAB)], axis=1)


def _unpack_words(words):
    lo = pltpu.unpack_elementwise(words, index=0, packed_dtype=jnp.bfloat16, unpacked_dtype=jnp.float32)
    hi = pltpu.unpack_elementwise(words, index=1, packed_dtype=jnp.bfloat16, unpacked_dtype=jnp.float32)
    return lo, hi


def _route(xn, wr_ref, br_ref, run_scr, tok_base, valid_limit):
    def split(a):
        hi = a.astype(jnp.bfloat16)
        return hi, (a - hi.astype(jnp.float32)).astype(jnp.bfloat16)

    def nt_dot(a, b):
        return lax.dot_general(a, b, (((1,), (1,)), ((), ())), preferred_element_type=jnp.float32)

    w_hi, w_lo = split(wr_ref[...])
    x_hi, x_lo = split(xn)
    logits = nt_dot(w_hi, x_hi) + nt_dot(w_hi, x_lo) + nt_dot(w_lo, x_hi) + br_ref[...]
    g = logits[0:N_GROUPS]
    gmax = jnp.max(g, axis=0, keepdims=True)
    rid_g = lax.broadcasted_iota(jnp.int32, g.shape, 0).astype(jnp.float32)
    g_idx = jnp.min(jnp.where(g == gmax, rid_g, float(N_GROUPS)), axis=0, keepdims=True).astype(jnp.int32)
    g_w = 1.0 / jnp.sum(jnp.exp(g - gmax), axis=0, keepdims=True)
    e_sel = logits[8:8 + EXPERTS_PER_GROUP]
    for gi in range(1, N_GROUPS):
        lo = 8 + EXPERTS_PER_GROUP * gi
        e_sel = jnp.where(g_idx == gi, logits[lo:lo + EXPERTS_PER_GROUP], e_sel)
    rid_e = lax.broadcasted_iota(jnp.int32, e_sel.shape, 0).astype(jnp.float32)
    none = float(EXPERTS_PER_GROUP)
    m1 = jnp.max(e_sel, axis=0, keepdims=True)
    i1f = jnp.min(jnp.where(e_sel == m1, rid_e, none), axis=0, keepdims=True)
    e_rest = jnp.where(rid_e == i1f, -jnp.inf, e_sel)
    m2 = jnp.max(e_rest, axis=0, keepdims=True)
    i2 = jnp.min(jnp.where(e_rest == m2, rid_e, none), axis=0, keepdims=True).astype(jnp.int32)
    i1 = i1f.astype(jnp.int32)
    ex = jnp.exp(m2 - m1)
    den = 1.0 / (1.0 + ex)
    w1 = den * g_w
    w2 = ex * den * g_w

    tok = tok_base + lax.broadcasted_iota(jnp.int32, (1, TM), 1)
    half = jnp.where(jnp.logical_and(tok >= HALF_SPLIT, tok < T_X), N_EXPERTS, 0)
    key1 = half + g_idx * EXPERTS_PER_GROUP + i1
    key2 = half + g_idx * EXPERTS_PER_GROUP + i2
    kid = lax.broadcasted_iota(jnp.int32, (N_KEYS, TM), 0)
    validf = jnp.where(tok < valid_limit, 1.0, 0.0)
    oh1 = jnp.where(kid == key1, validf, 0.0)
    oh2 = jnp.where(kid == key2, validf, 0.0)
    cnt = oh1 + oh2
    tri = jnp.where(lax.broadcasted_iota(jnp.int32, (TM, TM), 0)
                    <= lax.broadcasted_iota(jnp.int32, (TM, TM), 1), 1.0, 0.0).astype(jnp.bfloat16)
    cum = jnp.dot(cnt.astype(jnp.bfloat16), tri, preferred_element_type=jnp.float32)
    before = run_scr[...] + (cum - cnt)
    rank1 = jnp.sum(oh1 * before, axis=0, keepdims=True).astype(jnp.int32)
    rank2 = jnp.sum(oh2 * before, axis=0, keepdims=True).astype(jnp.int32)
    run_scr[...] = run_scr[...] + cum[:, TM - 1:TM]
    code = jnp.concatenate([key1 * 65536 + rank1, key2 * 65536 + rank2], axis=0)
    wts = jnp.concatenate([w1, w2], axis=0)
    return code, wts


W_COLS = 8


def _ffn_prologue(h_new, g_ffn_ref, wr_ref, br_ref, run_scr, tok_base, valid_limit,
                  xp_ref, code_ref, wts_ref, cnt_ref):
    xn2 = _rms_hat(h_new) * g_ffn_ref[...]
    _store_slabs(xp_ref, _pack_rows(xn2))
    code, wts = _route(xn2, wr_ref, br_ref, run_scr, tok_base, valid_limit)
    code_ref[...] = code
    wts_ref[...] = jnp.concatenate([wts, jnp.zeros((W_COLS - 2, TM), jnp.float32)], axis=0).T
    cnt_ref[...] = jnp.broadcast_to(run_scr[...], cnt_ref.shape).astype(jnp.int32)


NC = 1024


def _mixer0_kernel(x_ref, meta_ref, g_conv_ref, w_in_ref, cw_ref, w_out_ref,
                   g_ffn_ref, wr_ref, br_ref,
                   h1_ref, xp_ref, code_ref, wts_ref, cnt_ref,
                   acc_scr, carry_scr, meta_carry_scr, run_scr):
    s = pl.program_id(0)

    @pl.when(s == 0)
    def _():
        carry_scr[...] = jnp.zeros_like(carry_scr)
        meta_carry_scr[...] = jnp.zeros_like(meta_carry_scr)
        run_scr[...] = jnp.zeros_like(run_scr)

    @pl.when(s == 1 + N_XT // BATCH)
    def _():
        carry_scr[...] = meta_carry_scr[...]

    meta_tile = jnp.concatenate([meta_ref[...], jnp.zeros((TM - N_META, D_MODEL), jnp.float32)], axis=0)
    h0 = jnp.where(s == 0, meta_tile, x_ref[...])
    xn = (_rms_hat(h0) * g_conv_ref[...]).astype(jnp.bfloat16)
    row = lax.broadcasted_iota(jnp.int32, (TM, NC), 0)
    for c in range(D_MODEL // NC):
        cols = slice(NC * c, NC * (c + 1))
        gate_c = jnp.dot(xn, w_in_ref[:, D_MODEL + NC * c:D_MODEL + NC * (c + 1)],
                         preferred_element_type=jnp.float32)
        val = jnp.dot(xn, w_in_ref[:, 2 * D_MODEL + NC * c:2 * D_MODEL + NC * (c + 1)],
                      preferred_element_type=jnp.float32)
        u = gate_c * val
        tail = carry_scr[:, cols]
        c1 = tail[7:8, :]
        c2 = tail[6:7, :]
        um1 = jnp.where(row == 0, c1, pltpu.roll(u, 1, 0))
        um2 = jnp.where(row == 0, c2, jnp.where(row == 1, c1, pltpu.roll(u, 2, 0)))
        conv = um2 * cw_ref[0:1, cols] + um1 * cw_ref[1:2, cols] + u * cw_ref[2:3, cols]

        is_meta = s == 0
        meta_tail = u[N_META - 8:N_META, :]
        carry_scr[:, cols] = jnp.where(is_meta, meta_tail, u[TM - 8:TM, :])
        meta_carry_scr[:, cols] = jnp.where(is_meta, meta_tail, meta_carry_scr[:, cols])

        gate_b = jnp.dot(xn, w_in_ref[:, cols], preferred_element_type=jnp.float32)
        gated = (gate_b * conv).astype(jnp.bfloat16)
        part = jnp.dot(gated, w_out_ref[cols, :], preferred_element_type=jnp.float32)
        if c == 0:
            acc_scr[...] = h0 + part
        else:
            acc_scr[...] = acc_scr[...] + part

    h1 = acc_scr[...]
    h1_ref[...] = h1
    tile = jnp.where(s == 0, N_XT, s - 1)
    _ffn_prologue(h1, g_ffn_ref, wr_ref, br_ref, run_scr, tile * TM, T_VALID0,
                  xp_ref, code_ref, wts_ref, cnt_ref)


def _tile_first_meta(s):
    return jnp.where(s == 0, N_XT, s - 1)


_RESIDENT = pl.BlockSpec(memory_space=pltpu.VMEM)


def _mixer0(x2d, meta, g_conv, w_in, cw, w_out, g_ffn, wr, br):
    out_shape = [
        jax.ShapeDtypeStruct((T_PAD0, D_MODEL), jnp.float32),
        jax.ShapeDtypeStruct((T_PAD0 * SLAB, LANES), jnp.uint32),
        jax.ShapeDtypeStruct((2, T_PAD0), jnp.int32),
        jax.ShapeDtypeStruct((T_PAD0, W_COLS), jnp.float32),
        jax.ShapeDtypeStruct((N_KEYS, LANES), jnp.int32),
    ]
    return pl.pallas_call(
        _mixer0_kernel,
        grid=(N_XT + 1,),
        in_specs=[pl.BlockSpec((TM, D_MODEL), lambda s: (jnp.maximum(s - 1, 0), 0))] + [_RESIDENT] * 8,
        out_specs=[
            pl.BlockSpec((TM, D_MODEL), lambda s: (_tile_first_meta(s), 0)),
            pl.BlockSpec((TM * SLAB, LANES), lambda s: (_tile_first_meta(s), 0)),
            pl.BlockSpec((2, TM), lambda s: (0, _tile_first_meta(s))),
            pl.BlockSpec((TM, W_COLS), lambda s: (_tile_first_meta(s), 0)),
            _RESIDENT,
        ],
        out_shape=out_shape,
        scratch_shapes=[
            pltpu.VMEM((TM, D_MODEL), jnp.float32),
            pltpu.VMEM((8, D_MODEL), jnp.float32),
            pltpu.VMEM((8, D_MODEL), jnp.float32),
            pltpu.VMEM((N_KEYS, 1), jnp.float32),
        ],
        compiler_params=pltpu.CompilerParams(
            dimension_semantics=("arbitrary",), vmem_limit_bytes=VMEM_LIMIT),
        name="mixer0_route",
    )(x2d, meta, g_conv, w_in, cw, w_out, g_ffn, wr, br)


def _n_tiles(t_valid):
    return _cdiv(2 * t_valid, TMG)


def _n_visits(t_valid):
    return _n_tiles(t_valid) + N_KEYS


def _half0_rows(t_valid):
    return 2 * (HALF_SPLIT + t_valid - T_X)


def _half1_base(half0_rows):
    return half0_rows + TMG


def _sorted_rows(t_valid):
    return (_n_tiles(t_valid) + 2) * TMG


def _positions_kernel(t_pad, t_valid, cnt_ref, code_ref, pos_ref, off_ref, gstart_ref, vrow_ref, vgroup_ref,
                      vslot_ref, vnext_ref, nvis_ref, nexte_scr):
    n_vis = _n_visits(t_valid)

    def offs(g, acc):
        acc = jnp.where(g == N_EXPERTS, _half1_base(_half0_rows(t_valid)), acc)
        gstart_ref[g] = acc
        return acc + cnt_ref[g]

    total = lax.fori_loop(0, N_KEYS, offs, jnp.int32(0))
    gstart_ref[N_KEYS] = total

    def rows_of(e):
        return cnt_ref[e] + cnt_ref[e + N_EXPERTS]

    def next_nonempty(i, nxt):
        e = N_EXPERTS - 1 - i
        nexte_scr[e] = nxt
        return jnp.where(rows_of(e) > 0, e, nxt)

    lax.fori_loop(0, N_EXPERTS, next_nonempty, jnp.int32(-1))

    def per_expert(e, carry):
        v, last_g, rank = carry
        for half in range(2):
            g = e + half * N_EXPERTS
            c = cnt_ref[g]
            start = gstart_ref[g]

            def per_window(k, vv, g=g, start=start):
                vrow_ref[vv] = start + k * TMG
                vgroup_ref[vv] = g
                vslot_ref[vv] = rank % 2
                vnext_ref[vv] = nexte_scr[e]
                return vv + 1

            v = lax.fori_loop(0, (c + TMG - 1) // TMG, per_window, v)
            last_g = jnp.where(c > 0, g, last_g)
        return v, last_g, jnp.where(rows_of(e) > 0, rank + 1, rank)

    nvis, last_g, _ = lax.fori_loop(0, N_EXPERTS, per_expert, (jnp.int32(0), jnp.int32(0), jnp.int32(0)))
    nvis_ref[0] = nvis

    def pad(vv, c):
        vrow_ref[vv] = 0
        vgroup_ref[vv] = last_g
        vslot_ref[vv] = 0
        vnext_ref[vv] = -1
        return c

    lax.fori_loop(nvis, n_vis, pad, 0)

    code = code_ref[...]
    key = code >> 16
    pos = code & 0xFFFF
    for g in range(N_KEYS):
        pos = pos + jnp.where(key == g, gstart_ref[g], 0)
    pos_ref[...] = pos
    half_base = jnp.where(key >= N_EXPERTS, _half1_base(_half0_rows(t_valid)), 0)
    off_ref[...] = (pos - half_base) * SLAB


def _positions(cnt, code, t_pad, t_valid):
    n_vis = _n_visits(t_valid)
    smem = pl.BlockSpec(memory_space=pltpu.SMEM)
    vmem = pl.BlockSpec(memory_space=pltpu.VMEM)
    return pl.pallas_call(
        functools.partial(_positions_kernel, t_pad, t_valid),
        in_specs=[smem, vmem],
        out_specs=[vmem, vmem, smem, smem, smem, smem, smem, smem],
        out_shape=[
            jax.ShapeDtypeStruct((2, t_pad), jnp.int32),
            jax.ShapeDtypeStruct((2, t_pad), jnp.int32),
            jax.ShapeDtypeStruct((N_KEYS + 1,), jnp.int32),
            jax.ShapeDtypeStruct((n_vis,), jnp.int32),
            jax.ShapeDtypeStruct((n_vis,), jnp.int32),
            jax.ShapeDtypeStruct((n_vis,), jnp.int32),
            jax.ShapeDtypeStruct((n_vis,), jnp.int32),
            jax.ShapeDtypeStruct((1,), jnp.int32),
        ],
        scratch_shapes=[pltpu.SMEM((N_EXPERTS,), jnp.int32)],
        name="sort_positions",
    )(cnt, code)


SRC_UNROLL = 16


def _invert_positions(t_valid, pos1_ref, pos2_ref, src_ref):
    def slack(lo, hi):
        n_full = (hi - lo) // SRC_UNROLL

        def body(i, c):
            for j in range(SRC_UNROLL):
                src_ref[lo + i * SRC_UNROLL + j] = 0
            return c

        lax.fori_loop(0, n_full, body, 0)
        for r in range(lo + n_full * SRC_UNROLL, hi):
            src_ref[r] = 0

    half0_rows = _half0_rows(t_valid)
    slack(half0_rows, _half1_base(half0_rows))
    slack(2 * t_valid + TMG, _sorted_rows(t_valid))

    def body(i, c):
        for j in range(SRC_UNROLL):
            t = i * SRC_UNROLL + j
            src_ref[pos1_ref[t]] = t
            src_ref[pos2_ref[t]] = t
        return c

    lax.fori_loop(0, t_valid // SRC_UNROLL, body, 0)


GATHER_UNROLL = 8


def _ffn_kernel(layer, t_valid, vrow_ref, vgroup_ref, vslot_ref, vnext_ref, nvis_ref, pos1_ref, pos2_ref,
                xp_hbm, wg_hbm, wu_hbm, wd_hbm, ys_hbm,
                xp_scr, wg_scr, wu_scr, wd_scr, xnext_scr, out_scr, src_ref, xsem, wsem, osem):
    nvis = nvis_ref[0]

    def expert_of(visit):
        return vgroup_ref[visit] % N_EXPERTS

    def weight_copies(e, sl):
        return (pltpu.make_async_copy(wg_hbm.at[layer, e], wg_scr.at[sl], wsem.at[sl, 0]),
                pltpu.make_async_copy(wu_hbm.at[layer, e], wu_scr.at[sl], wsem.at[sl, 1]),
                pltpu.make_async_copy(wd_hbm.at[layer, e], wd_scr.at[sl], wsem.at[sl, 2]))

    def out_copy(visit):
        sl = visit % 2
        return pltpu.make_async_copy(
            out_scr.at[sl], ys_hbm.at[pl.ds(pl.multiple_of(vrow_ref[visit] * SLAB, SLAB), TMG * SLAB), :],
            osem.at[sl])

    def gather_rows(visit, unrolled):
        base = vrow_ref[visit]

        def one(r):
            tok = src_ref[base + r]
            xnext_scr[pl.ds(pl.multiple_of(r * SLAB, SLAB), SLAB), :] = (
                xp_scr[pl.ds(pl.multiple_of(tok * SLAB, SLAB), SLAB), :])

        if unrolled:
            for r in range(TMG):
                one(r)
        else:
            def chunk(i, c):
                for j in range(GATHER_UNROLL):
                    one(i * GATHER_UNROLL + j)
                return c

            lax.fori_loop(0, TMG // GATHER_UNROLL, chunk, 0)

    cp = pltpu.make_async_copy(xp_hbm, xp_scr, xsem)
    cp.start()
    for c in weight_copies(expert_of(0), vslot_ref[0]):
        c.start()
    out_scr[...] = jnp.zeros_like(out_scr)
    cap = ys_hbm.shape[0] // SLAB
    for first_row in (_half0_rows(t_valid), cap - 2 * TMG, cap - TMG):
        fill = pltpu.make_async_copy(
            out_scr.at[0], ys_hbm.at[pl.ds(first_row * SLAB, TMG * SLAB), :], osem.at[0])
        fill.start()
        fill.wait()
    _invert_positions(t_valid, pos1_ref, pos2_ref, src_ref)
    cp.wait()
    gather_rows(0, False)

    def visit(v, carry):
        e = expert_of(v)
        slot = vslot_ref[v]

        @pl.when(jnp.logical_or(v == 0, e != expert_of(jnp.maximum(v - 1, 0))))
        def _():
            for c in weight_copies(e, slot):
                c.wait()
            nxt = vnext_ref[v]

            @pl.when(nxt >= 0)
            def _():
                for c in weight_copies(nxt, 1 - slot):
                    c.start()

        lo, hi = _unpack_words(_load_slabs(xnext_scr, TMG))
        xs = jnp.concatenate([lo, hi], axis=1).astype(jnp.bfloat16)
        gather_rows(jnp.minimum(v + 1, nvis - 1), True)
        hg = jnp.dot(xs, wg_scr[slot].astype(jnp.bfloat16), preferred_element_type=jnp.float32)
        hu = jnp.dot(xs, wu_scr[slot].astype(jnp.bfloat16), preferred_element_type=jnp.float32)
        hdn = (hg * jax.nn.sigmoid(hg) * hu).astype(jnp.bfloat16)
        y = jnp.dot(hdn, wd_scr[slot].astype(jnp.bfloat16), preferred_element_type=jnp.float32)
        _store_slabs(out_scr.at[v % 2], _pack_rows(y))

        @pl.when(v > 0)
        def _():
            out_copy(v - 1).wait()

        out_copy(v).start()
        return carry

    lax.fori_loop(0, nvis, visit, 0)
    out_copy(nvis - 1).wait()


def _ffn(vrow, vgroup, vslot, vnext, nvis, pos1, pos2, xp, wg, wu, wd, layer, t_pad, t_valid):
    any_spec = pl.BlockSpec(memory_space=pl.ANY)
    return pl.pallas_call(
        functools.partial(_ffn_kernel, layer, t_valid),
        grid_spec=pltpu.PrefetchScalarGridSpec(
            num_scalar_prefetch=7,
            grid=(1,),
            in_specs=[any_spec, any_spec, any_spec, any_spec],
            out_specs=any_spec,
            scratch_shapes=[
                pltpu.VMEM((t_pad * SLAB, LANES), jnp.uint32),
                pltpu.VMEM((2, D_MODEL, D_EXPERT), jnp.float32),
                pltpu.VMEM((2, D_MODEL, D_EXPERT), jnp.float32),
                pltpu.VMEM((2, D_EXPERT, D_MODEL), jnp.float32),
                pltpu.VMEM((TMG * SLAB, LANES), jnp.uint32),
                pltpu.VMEM((2, TMG * SLAB, LANES), jnp.uint32),
                pltpu.SMEM((_sorted_rows(t_valid),), jnp.int32),
                pltpu.SemaphoreType.DMA,
                pltpu.SemaphoreType.DMA((2, 3)),
                pltpu.SemaphoreType.DMA((2,)),
            ],
        ),
        out_shape=jax.ShapeDtypeStruct((_sorted_rows(t_valid) * SLAB, LANES), jnp.uint32),
        compiler_params=pltpu.CompilerParams(
            dimension_semantics=("arbitrary",), vmem_limit_bytes=VMEM_LIMIT),
        name="expert_ffn",
    )(vrow, vgroup, vslot, vnext, nvis, pos1, pos2, xp, wg, wu, wd)


TILES_PER_HALF = HALF_SPLIT // TM


def _load_half(ys_hbm, ys_scr, sem, half, half0_rows, half1_rows):
    start, rows = (0, half0_rows) if half == 0 else (_half1_base(half0_rows), half1_rows)
    cp = pltpu.make_async_copy(ys_hbm.at[pl.ds(start * SLAB, rows * SLAB), :],
                               ys_scr.at[pl.ds(0, rows * SLAB), :], sem)
    cp.start()
    cp.wait()


def _gather_pairs(off1_ref, off2_ref, ys_scr, y1_scr, y2_scr, tok0, n, unrolled):
    def one(r):
        dst = pl.ds(pl.multiple_of(r * SLAB, SLAB), SLAB)
        y1_scr[dst, :] = ys_scr[pl.ds(pl.multiple_of(off1_ref[tok0 + r], SLAB), SLAB), :]
        y2_scr[dst, :] = ys_scr[pl.ds(pl.multiple_of(off2_ref[tok0 + r], SLAB), SLAB), :]

    if unrolled:
        for r in range(n):
            one(r)
    else:
        def chunk(i, c):
            for j in range(GATHER_UNROLL):
                one(i * GATHER_UNROLL + j)
            return c

        lax.fori_loop(0, n // GATHER_UNROLL, chunk, 0)


def _weighted_sum(y1_scr, y2_scr, w_ref, m):
    lo1, hi1 = _unpack_words(_load_slabs(y1_scr, m))
    lo2, hi2 = _unpack_words(_load_slabs(y2_scr, m))
    w1 = w_ref[0:m, 0:1]
    w2 = w_ref[0:m, 1:2]
    return jnp.concatenate([w1 * lo1 + w2 * lo2, w1 * hi1 + w2 * hi2], axis=1)


def _next_x_tile(tile):
    nxt = tile + 1
    return jnp.where(jnp.logical_or(nxt == TILES_PER_HALF, nxt == N_XT), tile, nxt)


def _final_kernel(half_rows, off1_ref, off2_ref, h_ref, w_ref, g_ref, ys_hbm,
                  o_ref, ys_scr, y1_scr, y2_scr, sem):
    tile = pl.program_id(0)
    gather = functools.partial(_gather_pairs, off1_ref, off2_ref, ys_scr, y1_scr, y2_scr)

    @pl.when(tile == 0)
    def _():
        _load_half(ys_hbm, ys_scr, sem, 0, half_rows, half_rows)
        gather(0, TM, False)

    @pl.when(tile == TILES_PER_HALF)
    def _():
        _load_half(ys_hbm, ys_scr, sem, 1, half_rows, half_rows)
        gather(TILES_PER_HALF * TM, TM, False)

    h = h_ref[...] + _weighted_sum(y1_scr, y2_scr, w_ref, TM)
    gather(_next_x_tile(tile) * TM, TM, True)
    o_ref[...] = _rms_hat(h) * g_ref[...]


def _final(pos1, pos2, h, wcols, g, ys):
    half_rows = 2 * HALF_SPLIT
    return pl.pallas_call(
        functools.partial(_final_kernel, half_rows),
        grid_spec=pltpu.PrefetchScalarGridSpec(
            num_scalar_prefetch=2,
            grid=(N_XT,),
            in_specs=[
                pl.BlockSpec((TM, D_MODEL), lambda i, p1, p2: (i, 0)),
                pl.BlockSpec((TM, W_COLS), lambda i, p1, p2: (i, 0)),
                _RESIDENT,
                pl.BlockSpec(memory_space=pl.ANY),
            ],
            out_specs=pl.BlockSpec((TM, D_MODEL), lambda i, p1, p2: (i, 0)),
            scratch_shapes=[
                pltpu.VMEM((half_rows * SLAB, LANES), jnp.uint32),
                pltpu.VMEM((TM * SLAB, LANES), jnp.uint32),
                pltpu.VMEM((TM * SLAB, LANES), jnp.uint32),
                pltpu.SemaphoreType.DMA,
            ],
        ),
        out_shape=jax.ShapeDtypeStruct((T_X, D_MODEL), jnp.float32),
        compiler_params=pltpu.CompilerParams(
            dimension_semantics=("arbitrary",), vmem_limit_bytes=VMEM_LIMIT),
        name="moe_combine_final",
    )(pos1, pos2, h, wcols, g, ys)


QB = WINDOW
KV_W = N_KV_HEADS * HEAD_DIM
N_QB = TM // QB
META_ROW0 = QB - N_META


def _rope(x, cos, sin_signed):
    q = lax.broadcasted_iota(jnp.int32, x.shape, 1) // (HEAD_DIM // 2)
    swapped = jnp.where(q % 2 == 0, pltpu.roll(x, LANES - HEAD_DIM // 2, 1),
                        pltpu.roll(x, HEAD_DIM // 2, 1))
    return x * cos + swapped * sin_signed


def _dup_heads(blk):
    lane = lax.broadcasted_iota(jnp.int32, blk.shape, 1)
    rolled = pltpu.roll(blk, HEAD_DIM, 1)
    return jnp.where(lane < HEAD_DIM, blk, rolled), jnp.where(lane < HEAD_DIM, rolled, blk)


def _kv_rows(xhat, g_kv_ref, w_kv_ref, cos, sin_signed):
    xk = (xhat * g_kv_ref[...]).astype(jnp.bfloat16)
    kv = jnp.dot(xk, w_kv_ref[...], preferred_element_type=jnp.float32)
    ks, vs = [], []
    for b in range(KV_W // LANES):
        kb = _rope(kv[:, LANES * b:LANES * (b + 1)], cos, sin_signed)
        vb = kv[:, KV_W + LANES * b:KV_W + LANES * (b + 1)]
        ks.extend(_dup_heads(kb))
        vs.extend(_dup_heads(vb))
    return [k.astype(jnp.bfloat16) for k in ks], [v.astype(jnp.bfloat16) for v in vs]


HALF0_ROWS0 = 2 * (HALF_SPLIT + N_META)
HALF1_ROWS0 = 2 * HALF_SPLIT


def _attn_kernel(off1_ref, off2_ref,
                 h_ref, w_ref, ys_hbm, cos_ref, sin_ref, cosm_ref, sinm_ref, g_attn_ref, g_kv_ref, w_q_ref,
                 w_kv_ref, w_o_ref, sink_ref, g_ffn_ref, wr_ref, br_ref,
                 h3_ref, xp_ref, code_ref, wts_ref, cnt_ref,
                 k_scr, v_scr, mk_scr, mv_scr, q_scr, o_scr, bias_scr, h_scr, run_scr,
                 ys_scr, y1_scr, y2_scr, ysem):
    s = pl.program_id(0)
    tiles_per_batch = N_XT // BATCH
    gather = functools.partial(_gather_pairs, off1_ref, off2_ref, ys_scr, y1_scr, y2_scr)

    @pl.when(s == 0)
    def _():
        run_scr[...] = jnp.zeros_like(run_scr)
        _load_half(ys_hbm, ys_scr, ysem, 0, HALF0_ROWS0, HALF1_ROWS0)
        gather(N_XT * TM, N_META, True)
        h_meta = h_ref[0:N_META, :] + _weighted_sum(y1_scr, y2_scr, w_ref, N_META)
        gather(0, TM, False)
        xhat = _rms_hat(h_meta)
        ks, vs = _kv_rows(xhat, g_kv_ref, w_kv_ref, cosm_ref[...], sinm_ref[...])
        mk_scr[...] = jnp.zeros_like(mk_scr)
        mv_scr[...] = jnp.zeros_like(mv_scr)
        for kvh in range(N_KV_HEADS):
            mk_scr[kvh, META_ROW0:QB, :] = ks[kvh]
            mv_scr[kvh, META_ROW0:QB, :] = vs[kvh]
        qi = lax.broadcasted_iota(jnp.int32, (QB, 2 * QB), 0)
        kj = lax.broadcasted_iota(jnp.int32, (QB, 2 * QB), 1)
        band = jnp.logical_and(kj > qi, kj <= qi + QB)
        bias_scr[0] = jnp.where(band, 0.0, NEG_INF)
        bias_scr[1] = jnp.where(jnp.logical_and(band, kj >= META_ROW0), 0.0, NEG_INF)

    @pl.when(s > 0)
    def _():
        batch_first = jnp.logical_or(s == 1, s == 1 + tiles_per_batch)

        @pl.when(batch_first)
        def _():
            k_scr[:, 0:QB, :] = mk_scr[...]
            v_scr[:, 0:QB, :] = mv_scr[...]

        tile = s - 1

        @pl.when(tile == TILES_PER_HALF)
        def _():
            _load_half(ys_hbm, ys_scr, ysem, 1, HALF0_ROWS0, HALF1_ROWS0)
            gather(TILES_PER_HALF * TM, TM, False)

        h = h_ref[...] + _weighted_sum(y1_scr, y2_scr, w_ref, TM)
        h_scr[...] = h
        gather(_next_x_tile(tile) * TM, TM, True)
        xhat = _rms_hat(h)
        ks, vs = _kv_rows(xhat, g_kv_ref, w_kv_ref, cos_ref[...], sin_ref[...])
        for kvh in range(N_KV_HEADS):
            k_scr[kvh, QB:QB + TM, :] = ks[kvh]
            v_scr[kvh, QB:QB + TM, :] = vs[kvh]

        xq = (xhat * g_attn_ref[...]).astype(jnp.bfloat16)
        q = jnp.dot(xq, w_q_ref[...], preferred_element_type=jnp.float32)
        for hb in range(N_HEADS // 2):
            q_scr[hb] = q[:, LANES * hb:LANES * (hb + 1)]
        first_bias = jnp.where(batch_first, 1, 0)

        def head_pair(hb, carry):
            kvh = hb // (N_HEADS // N_KV_HEADS // 2)
            lane = lax.broadcasted_iota(jnp.int32, (QB, LANES), 1)
            q_all = _rope(q_scr[hb], cos_ref[...], sin_ref[...]) * (HEAD_DIM ** -0.5)
            for b in range(N_QB):
                qb = q_all[QB * b:QB * (b + 1), :]
                qs = jnp.concatenate([jnp.where(lane < HEAD_DIM, qb, 0.0),
                                      jnp.where(lane < HEAD_DIM, 0.0, qb)], axis=0).astype(jnp.bfloat16)
                kk = k_scr[kvh, QB * b:QB * (b + 2), :]
                vv = v_scr[kvh, QB * b:QB * (b + 2), :]
                sc = lax.dot_general(qs, kk, (((1,), (1,)), ((), ())),
                                     preferred_element_type=jnp.float32)
                bias = bias_scr[first_bias] if b == 0 else bias_scr[0]
                outs = []
                for j in range(2):
                    sj = sc[QB * j:QB * (j + 1), :] + bias
                    sink = sink_ref[2 * hb + j]
                    m = jnp.maximum(jnp.max(sj, axis=-1, keepdims=True), sink)
                    p = jnp.exp(sj - m)
                    den = jnp.sum(p, axis=-1, keepdims=True) + jnp.exp(sink - m)
                    pv = jnp.dot(p.astype(jnp.bfloat16), vv, preferred_element_type=jnp.float32)
                    outs.append(pv * (1.0 / den))
                o_scr[hb, QB * b:QB * (b + 1), :] = jnp.where(
                    lane < HEAD_DIM, outs[0], outs[1]).astype(jnp.bfloat16)
            return carry

        lax.fori_loop(0, N_HEADS // 2, head_pair, 0)

        k_scr[:, 0:QB, :] = k_scr[:, TM:TM + QB, :]
        v_scr[:, 0:QB, :] = v_scr[:, TM:TM + QB, :]

        o = jnp.concatenate([o_scr[hb] for hb in range(N_HEADS // 2)], axis=1)
        h3 = h_scr[...] + jnp.dot(o, w_o_ref[...], preferred_element_type=jnp.float32)
        h3_ref[...] = h3
        _ffn_prologue(h3, g_ffn_ref, wr_ref, br_ref, run_scr, (s - 1) * TM, T_X,
                      xp_ref, code_ref, wts_ref, cnt_ref)


def _attn(pos1, pos2, h1, wcols, ys, rope, g_attn, g_kv, w_q, w_kv, w_o, sinks, g_ffn, wr, br):
    cos_x, sin_x, cos_m, sin_m = rope

    def tile_x(s, p1, p2):
        return jnp.maximum(s - 1, 0)

    def tile_in(s, p1, p2):
        return (_tile_first_meta(s), 0)

    def rope_tile(s, p1, p2):
        return (jnp.maximum(s - 1, 0) % (N_XT // BATCH), 0)

    out_shape = [
        jax.ShapeDtypeStruct((T_X, D_MODEL), jnp.float32),
        jax.ShapeDtypeStruct((T_X * SLAB, LANES), jnp.uint32),
        jax.ShapeDtypeStruct((2, T_X), jnp.int32),
        jax.ShapeDtypeStruct((T_X, W_COLS), jnp.float32),
        jax.ShapeDtypeStruct((N_KEYS, LANES), jnp.int32),
    ]
    return pl.pallas_call(
        _attn_kernel,
        grid_spec=pltpu.PrefetchScalarGridSpec(
            num_scalar_prefetch=2,
            grid=(N_XT + 1,),
            in_specs=[
                pl.BlockSpec((TM, D_MODEL), tile_in),
                pl.BlockSpec((TM, W_COLS), tile_in),
                pl.BlockSpec(memory_space=pl.ANY),
                pl.BlockSpec((TM, LANES), rope_tile),
                pl.BlockSpec((TM, LANES), rope_tile),
                _RESIDENT, _RESIDENT,
                _RESIDENT, _RESIDENT, _RESIDENT, _RESIDENT, _RESIDENT,
                pl.BlockSpec(memory_space=pltpu.SMEM),
                _RESIDENT, _RESIDENT, _RESIDENT,
            ],
            out_specs=[
                pl.BlockSpec((TM, D_MODEL), lambda s, p1, p2: (tile_x(s, p1, p2), 0)),
                pl.BlockSpec((TM * SLAB, LANES), lambda s, p1, p2: (tile_x(s, p1, p2), 0)),
                pl.BlockSpec((2, TM), lambda s, p1, p2: (0, tile_x(s, p1, p2))),
                pl.BlockSpec((TM, W_COLS), lambda s, p1, p2: (tile_x(s, p1, p2), 0)),
                _RESIDENT,
            ],
            scratch_shapes=[
                pltpu.VMEM((N_KV_HEADS, QB + TM, LANES), jnp.bfloat16),
                pltpu.VMEM((N_KV_HEADS, QB + TM, LANES), jnp.bfloat16),
                pltpu.VMEM((N_KV_HEADS, QB, LANES), jnp.bfloat16),
                pltpu.VMEM((N_KV_HEADS, QB, LANES), jnp.bfloat16),
                pltpu.VMEM((N_HEADS // 2, TM, LANES), jnp.float32),
                pltpu.VMEM((N_HEADS // 2, TM, LANES), jnp.bfloat16),
                pltpu.VMEM((2, QB, 2 * QB), jnp.float32),
                pltpu.VMEM((TM, D_MODEL), jnp.float32),
                pltpu.VMEM((N_KEYS, 1), jnp.float32),
                pltpu.VMEM((HALF0_ROWS0 * SLAB, LANES), jnp.uint32),
                pltpu.VMEM((TM * SLAB, LANES), jnp.uint32),
                pltpu.VMEM((TM * SLAB, LANES), jnp.uint32),
                pltpu.SemaphoreType.DMA,
            ],
        ),
        out_shape=out_shape,
        compiler_params=pltpu.CompilerParams(
            dimension_semantics=("arbitrary",), vmem_limit_bytes=VMEM_LIMIT),
        name="attn_route",
    )(pos1, pos2, h1, wcols, ys, cos_x, sin_x, cos_m, sin_m, g_attn, g_kv, w_q, w_kv, w_o, sinks, g_ffn, wr, br)


def _router_rows(rg_w, rg_b, re_w, re_b):
    wr = jnp.zeros((N_KEYS, D_MODEL), jnp.float32)
    wr = wr.at[0:N_GROUPS].set(rg_w.T).at[8:8 + N_EXPERTS].set(re_w.T)
    br = jnp.zeros((N_KEYS, 1), jnp.float32)
    br = br.at[0:N_GROUPS, 0].set(rg_b).at[8:8 + N_EXPERTS, 0].set(re_b)
    return wr, br


def _rope_tables():
    half = HEAD_DIM // 2
    inv_freq = jnp.tile(ROPE_THETA ** (-jnp.arange(half, dtype=jnp.float32) / half), 4)[None, :]
    sign = jnp.tile(jnp.concatenate([-jnp.ones(half), jnp.ones(half)]), 2).astype(jnp.float32)[None, :]
    ang_hi = (N_META + QB * jnp.arange(SEQ // QB)).astype(jnp.float32)[:, None] * inv_freq
    ang_lo = jnp.arange(QB).astype(jnp.float32)[:, None] * inv_freq
    c_hi, s_hi = jnp.cos(ang_hi)[:, None, :], jnp.sin(ang_hi)[:, None, :]
    c_lo, s_lo = jnp.cos(ang_lo)[None, :, :], jnp.sin(ang_lo)[None, :, :]
    cos_x = (c_hi * c_lo - s_hi * s_lo).reshape(SEQ, LANES)
    sin_x = ((s_hi * c_lo + c_hi * s_lo) * sign[None]).reshape(SEQ, LANES)
    ang_m = jnp.arange(N_META).astype(jnp.float32)[:, None] * inv_freq
    return cos_x, sin_x, jnp.cos(ang_m), jnp.sin(ang_m) * sign


def _moe(cnt, code, xp, wg, wu, wd, layer, t_pad, t_valid):
    pos, off, _, vrow, vgroup, vslot, vnext, nvis = _positions(cnt[:, 0], code, t_pad, t_valid)
    ys = _ffn(vrow, vgroup, vslot, vnext, nvis, pos[0], pos[1], xp, wg, wu, wd, layer, t_pad, t_valid)
    return off, ys


def kernel(x, meta_tokens, conv_norm_g, conv_w_in, conv_w, conv_w_out, kv_norm_g, w_kv, attn_norm_g,
           w_q, w_o, sinks, ffn_norm_g, router_group_w, router_group_b, router_expert_w,
           router_expert_b, w_gate, w_up, w_down, final_norm_g):
    bf = jnp.bfloat16
    x2d = x.reshape(T_X, D_MODEL)
    wr0, br0 = _router_rows(router_group_w[0], router_group_b[0], router_expert_w[0], router_expert_b[0])
    wr1, br1 = _router_rows(router_group_w[1], router_group_b[1], router_expert_w[1], router_expert_b[1])

    h1, xp0, code0, wts0, cnt0 = _mixer0(
        x2d, meta_tokens, conv_norm_g[0].reshape(1, D_MODEL), conv_w_in[0].astype(bf), conv_w[0],
        conv_w_out[0].astype(bf), ffn_norm_g[0].reshape(1, D_MODEL), wr0, br0)
    pos0, ys0 = _moe(cnt0, code0, xp0, w_gate, w_up, w_down, 0, T_PAD0, T_VALID0)

    h3, xp1, code1, wts1, cnt1 = _attn(
        pos0[0], pos0[1], h1, wts0, ys0, _rope_tables(), attn_norm_g[0].reshape(1, D_MODEL),
        kv_norm_g.reshape(1, D_MODEL), w_q[0].astype(bf), w_kv.astype(bf), w_o[0].astype(bf), sinks[0],
        ffn_norm_g[1].reshape(1, D_MODEL), wr1, br1)
    pos1, ys1 = _moe(cnt1, code1, xp1, w_gate, w_up, w_down, 1, T_X, T_X)
    out = _final(pos1[0], pos1[1], h3, wts1, final_norm_g.reshape(1, D_MODEL), ys1)
    return out.reshape(BATCH, SEQ, D_MODEL)
```

```python
import functools

import jax
import jax.numpy as jnp
from jax import lax
from jax.experimental import pallas as pl
from jax.experimental.pallas import tpu as pltpu

D_MODEL = 1024
BATCH = 2
SEQ = 8192
N_META = 16
N_HEADS = 16
HEAD_DIM = 64
N_KV_HEADS = 4
WINDOW = 128
ROPE_THETA = 10000.0
N_GROUPS = 4
EXPERTS_PER_GROUP = 8
N_EXPERTS = N_GROUPS * EXPERTS_PER_GROUP
D_EXPERT = 256
NORM_EPS = 1e-5
NEG_INF = -1e30

TM = 512
N_XT = BATCH * SEQ // TM
T_X = BATCH * SEQ
T_PAD0 = (N_XT + 1) * TM
T_VALID0 = T_X + N_META
HALF_SPLIT = SEQ
N_KEYS = 2 * N_EXPERTS
TMG = 576
LANES = 128
SLAB = D_MODEL // 2 // LANES
VMEM_V7X = 64 * 1024 * 1024
VMEM_LIMIT = VMEM_V7X - 1024 * 1024


def _cdiv(a, b):
    return (a + b - 1) // b


def _rms_hat(x):
    return x * lax.rsqrt(jnp.mean(x * x, axis=-1, keepdims=True) + NORM_EPS)


def _pack_rows(xn):
    half = D_MODEL // 2
    return pltpu.pack_elementwise([xn[:, :half], xn[:, half:]], packed_dtype=jnp.bfloat16)


def _store_slabs(ref, words):
    m = words.shape[0]
    for k in range(SLAB):
        ref[pl.ds(k, m, stride=SLAB), :] = words[:, LANES * k:LANES * (k + 1)]


def _load_slabs(ref, m):
    return jnp.concatenate([ref[pl.ds(k, m, stride=SLAB), :] for k in range(SLAB)], axis=1)


def _unpack_words(words):
    lo = pltpu.unpack_elementwise(words, index=0, packed_dtype=jnp.bfloat16, unpacked_dtype=jnp.float32)
    hi = pltpu.unpack_elementwise(words, index=1, packed_dtype=jnp.bfloat16, unpacked_dtype=jnp.float32)
    return lo, hi


def _route(xn, wr_ref, br_ref, run_scr, tok_base, valid_limit):
    def split(a):
        hi = a.astype(jnp.bfloat16)
        return hi, (a - hi.astype(jnp.float32)).astype(jnp.bfloat16)

    def nt_dot(a, b):
        return lax.dot_general(a, b, (((1,), (1,)), ((), ())), preferred_element_type=jnp.float32)

    w_hi, w_lo = split(wr_ref[...])
    x_hi, x_lo = split(xn)
    logits = nt_dot(w_hi, x_hi) + nt_dot(w_hi, x_lo) + nt_dot(w_lo, x_hi) + br_ref[...]
    g = logits[0:N_GROUPS]
    gmax = jnp.max(g, axis=0, keepdims=True)
    rid_g = lax.broadcasted_iota(jnp.int32, g.shape, 0).astype(jnp.float32)
    g_idx = jnp.min(jnp.where(g == gmax, rid_g, float(N_GROUPS)), axis=0, keepdims=True).astype(jnp.int32)
    g_w = 1.0 / jnp.sum(jnp.exp(g - gmax), axis=0, keepdims=True)
    e_sel = logits[8:8 + EXPERTS_PER_GROUP]
    for gi in range(1, N_GROUPS):
        lo = 8 + EXPERTS_PER_GROUP * gi
        e_sel = jnp.where(g_idx == gi, logits[lo:lo + EXPERTS_PER_GROUP], e_sel)
    rid_e = lax.broadcasted_iota(jnp.int32, e_sel.shape, 0).astype(jnp.float32)
    none = float(EXPERTS_PER_GROUP)
    m1 = jnp.max(e_sel, axis=0, keepdims=True)
    i1f = jnp.min(jnp.where(e_sel == m1, rid_e, none), axis=0, keepdims=True)
    e_rest = jnp.where(rid_e == i1f, -jnp.inf, e_sel)
    m2 = jnp.max(e_rest, axis=0, keepdims=True)
    i2 = jnp.min(jnp.where(e_rest == m2, rid_e, none), axis=0, keepdims=True).astype(jnp.int32)
    i1 = i1f.astype(jnp.int32)
    ex = jnp.exp(m2 - m1)
    den = 1.0 / (1.0 + ex)
    w1 = den * g_w
    w2 = ex * den * g_w

    tok = tok_base + lax.broadcasted_iota(jnp.int32, (1, TM), 1)
    half = jnp.where(jnp.logical_and(tok >= HALF_SPLIT, tok < T_X), N_EXPERTS, 0)
    key1 = half + g_idx * EXPERTS_PER_GROUP + i1
    key2 = half + g_idx * EXPERTS_PER_GROUP + i2
    kid = lax.broadcasted_iota(jnp.int32, (N_KEYS, TM), 0)
    validf = jnp.where(tok < valid_limit, 1.0, 0.0)
    oh1 = jnp.where(kid == key1, validf, 0.0)
    oh2 = jnp.where(kid == key2, validf, 0.0)
    cnt = oh1 + oh2
    tri = jnp.where(lax.broadcasted_iota(jnp.int32, (TM, TM), 0)
                    <= lax.broadcasted_iota(jnp.int32, (TM, TM), 1), 1.0, 0.0).astype(jnp.bfloat16)
    cum = jnp.dot(cnt.astype(jnp.bfloat16), tri, preferred_element_type=jnp.float32)
    before = run_scr[...] + (cum - cnt)
    rank1 = jnp.sum(oh1 * before, axis=0, keepdims=True).astype(jnp.int32)
    rank2 = jnp.sum(oh2 * before, axis=0, keepdims=True).astype(jnp.int32)
    run_scr[...] = run_scr[...] + cum[:, TM - 1:TM]
    code = jnp.concatenate([key1 * 65536 + rank1, key2 * 65536 + rank2], axis=0)
    wts = jnp.concatenate([w1, w2], axis=0)
    return code, wts


W_COLS = 8


def _ffn_prologue(h_new, g_ffn_ref, wr_ref, br_ref, run_scr, tok_base, valid_limit,
                  xp_ref, code_ref, wts_ref, cnt_ref):
    xn2 = _rms_hat(h_new) * g_ffn_ref[...]
    _store_slabs(xp_ref, _pack_rows(xn2))
    code, wts = _route(xn2, wr_ref, br_ref, run_scr, tok_base, valid_limit)
    code_ref[...] = code
    wts_ref[...] = jnp.concatenate([wts, jnp.zeros((W_COLS - 2, TM), jnp.float32)], axis=0).T
    cnt_ref[...] = jnp.broadcast_to(run_scr[...], cnt_ref.shape).astype(jnp.int32)


NC = 1024


def _mixer0_kernel(x_ref, meta_ref, g_conv_ref, w_in_ref, cw_ref, w_out_ref,
                   g_ffn_ref, wr_ref, br_ref,
                   h1_ref, xp_ref, code_ref, wts_ref, cnt_ref,
                   acc_scr, carry_scr, meta_carry_scr, run_scr):
    s = pl.program_id(0)

    @pl.when(s == 0)
    def _():
        carry_scr[...] = jnp.zeros_like(carry_scr)
        meta_carry_scr[...] = jnp.zeros_like(meta_carry_scr)
        run_scr[...] = jnp.zeros_like(run_scr)

    @pl.when(s == 1 + N_XT // BATCH)
    def _():
        carry_scr[...] = meta_carry_scr[...]

    meta_tile = jnp.concatenate([meta_ref[...], jnp.zeros((TM - N_META, D_MODEL), jnp.float32)], axis=0)
    h0 = jnp.where(s == 0, meta_tile, x_ref[...])
    xn = (_rms_hat(h0) * g_conv_ref[...]).astype(jnp.bfloat16)
    row = lax.broadcasted_iota(jnp.int32, (TM, NC), 0)
    for c in range(D_MODEL // NC):
        cols = slice(NC * c, NC * (c + 1))
        gate_c = jnp.dot(xn, w_in_ref[:, D_MODEL + NC * c:D_MODEL + NC * (c + 1)],
                         preferred_element_type=jnp.float32)
        val = jnp.dot(xn, w_in_ref[:, 2 * D_MODEL + NC * c:2 * D_MODEL + NC * (c + 1)],
                      preferred_element_type=jnp.float32)
        u = gate_c * val
        tail = carry_scr[:, cols]
        c1 = tail[7:8, :]
        c2 = tail[6:7, :]
        um1 = jnp.where(row == 0, c1, pltpu.roll(u, 1, 0))
        um2 = jnp.where(row == 0, c2, jnp.where(row == 1, c1, pltpu.roll(u, 2, 0)))
        conv = um2 * cw_ref[0:1, cols] + um1 * cw_ref[1:2, cols] + u * cw_ref[2:3, cols]

        is_meta = s == 0
        meta_tail = u[N_META - 8:N_META, :]
        carry_scr[:, cols] = jnp.where(is_meta, meta_tail, u[TM - 8:TM, :])
        meta_carry_scr[:, cols] = jnp.where(is_meta, meta_tail, meta_carry_scr[:, cols])

        gate_b = jnp.dot(xn, w_in_ref[:, cols], preferred_element_type=jnp.float32)
        gated = (gate_b * conv).astype(jnp.bfloat16)
        part = jnp.dot(gated, w_out_ref[cols, :], preferred_element_type=jnp.float32)
        if c == 0:
            acc_scr[...] = h0 + part
        else:
            acc_scr[...] = acc_scr[...] + part

    h1 = acc_scr[...]
    h1_ref[...] = h1
    tile = jnp.where(s == 0, N_XT, s - 1)
    _ffn_prologue(h1, g_ffn_ref, wr_ref, br_ref, run_scr, tile * TM, T_VALID0,
                  xp_ref, code_ref, wts_ref, cnt_ref)


def _tile_first_meta(s):
    return jnp.where(s == 0, N_XT, s - 1)


_RESIDENT = pl.BlockSpec(memory_space=pltpu.VMEM)


def _mixer0(x2d, meta, g_conv, w_in, cw, w_out, g_ffn, wr, br):
    out_shape = [
        jax.ShapeDtypeStruct((T_PAD0, D_MODEL), jnp.float32),
        jax.ShapeDtypeStruct((T_PAD0 * SLAB, LANES), jnp.uint32),
        jax.ShapeDtypeStruct((2, T_PAD0), jnp.int32),
        jax.ShapeDtypeStruct((T_PAD0, W_COLS), jnp.float32),
        jax.ShapeDtypeStruct((N_KEYS, LANES), jnp.int32),
    ]
    return pl.pallas_call(
        _mixer0_kernel,
        grid=(N_XT + 1,),
        in_specs=[pl.BlockSpec((TM, D_MODEL), lambda s: (jnp.maximum(s - 1, 0), 0))] + [_RESIDENT] * 8,
        out_specs=[
            pl.BlockSpec((TM, D_MODEL), lambda s: (_tile_first_meta(s), 0)),
            pl.BlockSpec((TM * SLAB, LANES), lambda s: (_tile_first_meta(s), 0)),
            pl.BlockSpec((2, TM), lambda s: (0, _tile_first_meta(s))),
            pl.BlockSpec((TM, W_COLS), lambda s: (_tile_first_meta(s), 0)),
            _RESIDENT,
        ],
        out_shape=out_shape,
        scratch_shapes=[
            pltpu.VMEM((TM, D_MODEL), jnp.float32),
            pltpu.VMEM((8, D_MODEL), jnp.float32),
            pltpu.VMEM((8, D_MODEL), jnp.float32),
            pltpu.VMEM((N_KEYS, 1), jnp.float32),
        ],
        compiler_params=pltpu.CompilerParams(
            dimension_semantics=("arbitrary",), vmem_limit_bytes=VMEM_LIMIT),
        name="mixer0_route",
    )(x2d, meta, g_conv, w_in, cw, w_out, g_ffn, wr, br)


def _n_tiles(t_valid):
    return _cdiv(2 * t_valid, TMG)


def _n_visits(t_valid):
    return _n_tiles(t_valid) + N_KEYS


def _half0_rows(t_valid):
    return 2 * (HALF_SPLIT + t_valid - T_X)


def _half1_base(half0_rows):
    return half0_rows + TMG


def _sorted_rows(t_valid):
    return (_n_tiles(t_valid) + 2) * TMG


def _positions_kernel(t_pad, t_valid, cnt_ref, code_ref, pos_ref, off_ref, gstart_ref, vrow_ref, vgroup_ref,
                      vslot_ref, vnext_ref, nvis_ref, nexte_scr):
    n_vis = _n_visits(t_valid)

    def offs(g, acc):
        acc = jnp.where(g == N_EXPERTS, _half1_base(_half0_rows(t_valid)), acc)
        gstart_ref[g] = acc
        return acc + cnt_ref[g]

    total = lax.fori_loop(0, N_KEYS, offs, jnp.int32(0))
    gstart_ref[N_KEYS] = total

    def rows_of(e):
        return cnt_ref[e] + cnt_ref[e + N_EXPERTS]

    def next_nonempty(i, nxt):
        e = N_EXPERTS - 1 - i
        nexte_scr[e] = nxt
        return jnp.where(rows_of(e) > 0, e, nxt)

    lax.fori_loop(0, N_EXPERTS, next_nonempty, jnp.int32(-1))

    def per_expert(e, carry):
        v, last_g, rank = carry
        for half in range(2):
            g = e + half * N_EXPERTS
            c = cnt_ref[g]
            start = gstart_ref[g]

            def per_window(k, vv, g=g, start=start):
                vrow_ref[vv] = start + k * TMG
                vgroup_ref[vv] = g
                vslot_ref[vv] = rank % 2
                vnext_ref[vv] = nexte_scr[e]
                return vv + 1

            v = lax.fori_loop(0, (c + TMG - 1) // TMG, per_window, v)
            last_g = jnp.where(c > 0, g, last_g)
        return v, last_g, jnp.where(rows_of(e) > 0, rank + 1, rank)

    nvis, last_g, _ = lax.fori_loop(0, N_EXPERTS, per_expert, (jnp.int32(0), jnp.int32(0), jnp.int32(0)))
    nvis_ref[0] = nvis

    def pad(vv, c):
        vrow_ref[vv] = 0
        vgroup_ref[vv] = last_g
        vslot_ref[vv] = 0
        vnext_ref[vv] = -1
        return c

    lax.fori_loop(nvis, n_vis, pad, 0)

    code = code_ref[...]
    key = code >> 16
    pos = code & 0xFFFF
    for g in range(N_KEYS):
        pos = pos + jnp.where(key == g, gstart_ref[g], 0)
    pos_ref[...] = pos
    half_base = jnp.where(key >= N_EXPERTS, _half1_base(_half0_rows(t_valid)), 0)
    off_ref[...] = (pos - half_base) * SLAB


def _positions(cnt, code, t_pad, t_valid):
    n_vis = _n_visits(t_valid)
    smem = pl.BlockSpec(memory_space=pltpu.SMEM)
    vmem = pl.BlockSpec(memory_space=pltpu.VMEM)
    return pl.pallas_call(
        functools.partial(_positions_kernel, t_pad, t_valid),
        in_specs=[smem, vmem],
        out_specs=[vmem, vmem, smem, smem, smem, smem, smem, smem],
        out_shape=[
            jax.ShapeDtypeStruct((2, t_pad), jnp.int32),
            jax.ShapeDtypeStruct((2, t_pad), jnp.int32),
            jax.ShapeDtypeStruct((N_KEYS + 1,), jnp.int32),
            jax.ShapeDtypeStruct((n_vis,), jnp.int32),
            jax.ShapeDtypeStruct((n_vis,), jnp.int32),
            jax.ShapeDtypeStruct((n_vis,), jnp.int32),
            jax.ShapeDtypeStruct((n_vis,), jnp.int32),
            jax.ShapeDtypeStruct((1,), jnp.int32),
        ],
        scratch_shapes=[pltpu.SMEM((N_EXPERTS,), jnp.int32)],
        name="sort_positions",
    )(cnt, code)


SRC_UNROLL = 64


def _invert_positions(t_valid, pos1_ref, pos2_ref, src_ref):
    def slack(lo, hi):
        n_full = (hi - lo) // SRC_UNROLL

        def body(i, c):
            for j in range(SRC_UNROLL):
                src_ref[lo + i * SRC_UNROLL + j] = 0
            return c

        lax.fori_loop(0, n_full, body, 0)
        for r in range(lo + n_full * SRC_UNROLL, hi):
            src_ref[r] = 0

    half0_rows = _half0_rows(t_valid)
    slack(half0_rows, _half1_base(half0_rows))
    slack(2 * t_valid + TMG, _sorted_rows(t_valid))

    def place(t):
        src_ref[pos1_ref[t]] = t
        src_ref[pos2_ref[t]] = t

    def body(i, c):
        for j in range(SRC_UNROLL):
            place(i * SRC_UNROLL + j)
        return c

    n_full = t_valid // SRC_UNROLL
    lax.fori_loop(0, n_full, body, 0)
    for t in range(n_full * SRC_UNROLL, t_valid):
        place(t)


GATHER_UNROLL = 8


def _ffn_kernel(layer, t_valid, vrow_ref, vgroup_ref, vslot_ref, vnext_ref, nvis_ref, pos1_ref, pos2_ref,
                xp_hbm, wg_hbm, wu_hbm, wd_hbm, ys_hbm,
                xp_scr, wg_scr, wu_scr, wd_scr, xnext_scr, out_scr, src_ref, xsem, wsem, osem):
    nvis = nvis_ref[0]

    def expert_of(visit):
        return vgroup_ref[visit] % N_EXPERTS

    def weight_copies(e, sl):
        return (pltpu.make_async_copy(wg_hbm.at[layer, e], wg_scr.at[sl], wsem.at[sl, 0]),
                pltpu.make_async_copy(wu_hbm.at[layer, e], wu_scr.at[sl], wsem.at[sl, 1]),
                pltpu.make_async_copy(wd_hbm.at[layer, e], wd_scr.at[sl], wsem.at[sl, 2]))

    def out_copy(visit):
        sl = visit % 2
        return pltpu.make_async_copy(
            out_scr.at[sl], ys_hbm.at[pl.ds(pl.multiple_of(vrow_ref[visit] * SLAB, SLAB), TMG * SLAB), :],
            osem.at[sl])

    def gather_rows(visit, unrolled):
        base = vrow_ref[visit]

        def one(r):
            tok = src_ref[base + r]
            xnext_scr[pl.ds(pl.multiple_of(r * SLAB, SLAB), SLAB), :] = (
                xp_scr[pl.ds(pl.multiple_of(tok * SLAB, SLAB), SLAB), :])

        if unrolled:
            for r in range(TMG):
                one(r)
        else:
            def chunk(i, c):
                for j in range(GATHER_UNROLL):
                    one(i * GATHER_UNROLL + j)
                return c

            lax.fori_loop(0, TMG // GATHER_UNROLL, chunk, 0)

    cp = pltpu.make_async_copy(xp_hbm, xp_scr, xsem)
    cp.start()
    for c in weight_copies(expert_of(0), vslot_ref[0]):
        c.start()
    out_scr[...] = jnp.zeros_like(out_scr)
    cap = ys_hbm.shape[0] // SLAB
    for first_row in (_half0_rows(t_valid), cap - 2 * TMG, cap - TMG):
        fill = pltpu.make_async_copy(
            out_scr.at[0], ys_hbm.at[pl.ds(first_row * SLAB, TMG * SLAB), :], osem.at[0])
        fill.start()
        fill.wait()
    _invert_positions(t_valid, pos1_ref, pos2_ref, src_ref)
    cp.wait()
    gather_rows(0, False)

    def visit(v, carry):
        e = expert_of(v)
        slot = vslot_ref[v]

        @pl.when(jnp.logical_or(v == 0, e != expert_of(jnp.maximum(v - 1, 0))))
        def _():
            for c in weight_copies(e, slot):
                c.wait()
            nxt = vnext_ref[v]

            @pl.when(nxt >= 0)
            def _():
                for c in weight_copies(nxt, 1 - slot):
                    c.start()

        lo, hi = _unpack_words(_load_slabs(xnext_scr, TMG))
        xs = jnp.concatenate([lo, hi], axis=1).astype(jnp.bfloat16)
        gather_rows(jnp.minimum(v + 1, nvis - 1), True)
        hg = jnp.dot(xs, wg_scr[slot].astype(jnp.bfloat16), preferred_element_type=jnp.float32)
        hu = jnp.dot(xs, wu_scr[slot].astype(jnp.bfloat16), preferred_element_type=jnp.float32)
        hdn = (hg * jax.nn.sigmoid(hg) * hu).astype(jnp.bfloat16)
        y = jnp.dot(hdn, wd_scr[slot].astype(jnp.bfloat16), preferred_element_type=jnp.float32)
        _store_slabs(out_scr.at[v % 2], _pack_rows(y))

        @pl.when(v > 0)
        def _():
            out_copy(v - 1).wait()

        out_copy(v).start()
        return carry

    lax.fori_loop(0, nvis, visit, 0)
    out_copy(nvis - 1).wait()


def _ffn(vrow, vgroup, vslot, vnext, nvis, pos1, pos2, xp, wg, wu, wd, layer, t_pad, t_valid):
    any_spec = pl.BlockSpec(memory_space=pl.ANY)
    return pl.pallas_call(
        functools.partial(_ffn_kernel, layer, t_valid),
        grid_spec=pltpu.PrefetchScalarGridSpec(
            num_scalar_prefetch=7,
            grid=(1,),
            in_specs=[any_spec, any_spec, any_spec, any_spec],
            out_specs=any_spec,
            scratch_shapes=[
                pltpu.VMEM((t_pad * SLAB, LANES), jnp.uint32),
                pltpu.VMEM((2, D_MODEL, D_EXPERT), jnp.float32),
                pltpu.VMEM((2, D_MODEL, D_EXPERT), jnp.float32),
                pltpu.VMEM((2, D_EXPERT, D_MODEL), jnp.float32),
                pltpu.VMEM((TMG * SLAB, LANES), jnp.uint32),
                pltpu.VMEM((2, TMG * SLAB, LANES), jnp.uint32),
                pltpu.SMEM((_sorted_rows(t_valid),), jnp.int32),
                pltpu.SemaphoreType.DMA,
                pltpu.SemaphoreType.DMA((2, 3)),
                pltpu.SemaphoreType.DMA((2,)),
            ],
        ),
        out_shape=jax.ShapeDtypeStruct((_sorted_rows(t_valid) * SLAB, LANES), jnp.uint32),
        compiler_params=pltpu.CompilerParams(
            dimension_semantics=("arbitrary",), vmem_limit_bytes=VMEM_LIMIT),
        name="expert_ffn",
    )(vrow, vgroup, vslot, vnext, nvis, pos1, pos2, xp, wg, wu, wd)


TILES_PER_HALF = HALF_SPLIT // TM


def _load_half(ys_hbm, ys_scr, sem, half, half0_rows, half1_rows):
    start, rows = (0, half0_rows) if half == 0 else (_half1_base(half0_rows), half1_rows)
    cp = pltpu.make_async_copy(ys_hbm.at[pl.ds(start * SLAB, rows * SLAB), :],
                               ys_scr.at[pl.ds(0, rows * SLAB), :], sem)
    cp.start()
    cp.wait()


def _gather_pairs(off1_ref, off2_ref, ys_scr, y1_scr, y2_scr, tok0, n, unrolled):
    def one(r):
        dst = pl.ds(pl.multiple_of(r * SLAB, SLAB), SLAB)
        y1_scr[dst, :] = ys_scr[pl.ds(pl.multiple_of(off1_ref[tok0 + r], SLAB), SLAB), :]
        y2_scr[dst, :] = ys_scr[pl.ds(pl.multiple_of(off2_ref[tok0 + r], SLAB), SLAB), :]

    if unrolled:
        for r in range(n):
            one(r)
    else:
        def chunk(i, c):
            for j in range(GATHER_UNROLL):
                one(i * GATHER_UNROLL + j)
            return c

        lax.fori_loop(0, n // GATHER_UNROLL, chunk, 0)


def _weighted_sum(y1_scr, y2_scr, w_ref, m):
    lo1, hi1 = _unpack_words(_load_slabs(y1_scr, m))
    lo2, hi2 = _unpack_words(_load_slabs(y2_scr, m))
    w1 = w_ref[0:m, 0:1]
    w2 = w_ref[0:m, 1:2]
    return jnp.concatenate([w1 * lo1 + w2 * lo2, w1 * hi1 + w2 * hi2], axis=1)


def _next_x_tile(tile):
    nxt = tile + 1
    return jnp.where(jnp.logical_or(nxt == TILES_PER_HALF, nxt == N_XT), tile, nxt)


def _final_kernel(half_rows, off1_ref, off2_ref, h_ref, w_ref, g_ref, ys_hbm,
                  o_ref, ys_scr, y1_scr, y2_scr, sem):
    tile = pl.program_id(0)
    gather = functools.partial(_gather_pairs, off1_ref, off2_ref, ys_scr, y1_scr, y2_scr)

    @pl.when(tile == 0)
    def _():
        _load_half(ys_hbm, ys_scr, sem, 0, half_rows, half_rows)
        gather(0, TM, False)

    @pl.when(tile == TILES_PER_HALF)
    def _():
        _load_half(ys_hbm, ys_scr, sem, 1, half_rows, half_rows)
        gather(TILES_PER_HALF * TM, TM, False)

    h = h_ref[...] + _weighted_sum(y1_scr, y2_scr, w_ref, TM)
    gather(_next_x_tile(tile) * TM, TM, True)
    o_ref[...] = _rms_hat(h) * g_ref[...]


def _final(pos1, pos2, h, wcols, g, ys):
    half_rows = 2 * HALF_SPLIT
    return pl.pallas_call(
        functools.partial(_final_kernel, half_rows),
        grid_spec=pltpu.PrefetchScalarGridSpec(
            num_scalar_prefetch=2,
            grid=(N_XT,),
            in_specs=[
                pl.BlockSpec((TM, D_MODEL), lambda i, p1, p2: (i, 0)),
                pl.BlockSpec((TM, W_COLS), lambda i, p1, p2: (i, 0)),
                _RESIDENT,
                pl.BlockSpec(memory_space=pl.ANY),
            ],
            out_specs=pl.BlockSpec((TM, D_MODEL), lambda i, p1, p2: (i, 0)),
            scratch_shapes=[
                pltpu.VMEM((half_rows * SLAB, LANES), jnp.uint32),
                pltpu.VMEM((TM * SLAB, LANES), jnp.uint32),
                pltpu.VMEM((TM * SLAB, LANES), jnp.uint32),
                pltpu.SemaphoreType.DMA,
            ],
        ),
        out_shape=jax.ShapeDtypeStruct((T_X, D_MODEL), jnp.float32),
        compiler_params=pltpu.CompilerParams(
            dimension_semantics=("arbitrary",), vmem_limit_bytes=VMEM_LIMIT),
        name="moe_combine_final",
    )(pos1, pos2, h, wcols, g, ys)


QB = WINDOW
KV_W = N_KV_HEADS * HEAD_DIM
N_QB = TM // QB
META_ROW0 = QB - N_META


def _rope(x, cos, sin_signed):
    q = lax.broadcasted_iota(jnp.int32, x.shape, 1) // (HEAD_DIM // 2)
    swapped = jnp.where(q % 2 == 0, pltpu.roll(x, LANES - HEAD_DIM // 2, 1),
                        pltpu.roll(x, HEAD_DIM // 2, 1))
    return x * cos + swapped * sin_signed


def _dup_heads(blk):
    lane = lax.broadcasted_iota(jnp.int32, blk.shape, 1)
    rolled = pltpu.roll(blk, HEAD_DIM, 1)
    return jnp.where(lane < HEAD_DIM, blk, rolled), jnp.where(lane < HEAD_DIM, rolled, blk)


def _kv_rows(xhat, g_kv_ref, w_kv_ref, cos, sin_signed):
    xk = (xhat * g_kv_ref[...]).astype(jnp.bfloat16)
    kv = jnp.dot(xk, w_kv_ref[...], preferred_element_type=jnp.float32)
    ks, vs = [], []
    for b in range(KV_W // LANES):
        kb = _rope(kv[:, LANES * b:LANES * (b + 1)], cos, sin_signed)
        vb = kv[:, KV_W + LANES * b:KV_W + LANES * (b + 1)]
        ks.extend(_dup_heads(kb))
        vs.extend(_dup_heads(vb))
    return [k.astype(jnp.bfloat16) for k in ks], [v.astype(jnp.bfloat16) for v in vs]


HALF0_ROWS0 = 2 * (HALF_SPLIT + N_META)
HALF1_ROWS0 = 2 * HALF_SPLIT


def _attn_kernel(off1_ref, off2_ref,
                 h_ref, w_ref, ys_hbm, cos_ref, sin_ref, cosm_ref, sinm_ref, g_attn_ref, g_kv_ref, w_q_ref,
                 w_kv_ref, w_o_ref, sink_ref, g_ffn_ref, wr_ref, br_ref,
                 h3_ref, xp_ref, code_ref, wts_ref, cnt_ref,
                 k_scr, v_scr, mk_scr, mv_scr, q_scr, o_scr, bias_scr, h_scr, run_scr,
                 ys_scr, y1_scr, y2_scr, ysem):
    s = pl.program_id(0)
    tiles_per_batch = N_XT // BATCH
    gather = functools.partial(_gather_pairs, off1_ref, off2_ref, ys_scr, y1_scr, y2_scr)

    @pl.when(s == 0)
    def _():
        run_scr[...] = jnp.zeros_like(run_scr)
        _load_half(ys_hbm, ys_scr, ysem, 0, HALF0_ROWS0, HALF1_ROWS0)
        gather(N_XT * TM, N_META, True)
        h_meta = h_ref[0:N_META, :] + _weighted_sum(y1_scr, y2_scr, w_ref, N_META)
        gather(0, TM, False)
        xhat = _rms_hat(h_meta)
        ks, vs = _kv_rows(xhat, g_kv_ref, w_kv_ref, cosm_ref[...], sinm_ref[...])
        mk_scr[...] = jnp.zeros_like(mk_scr)
        mv_scr[...] = jnp.zeros_like(mv_scr)
        for kvh in range(N_KV_HEADS):
            mk_scr[kvh, META_ROW0:QB, :] = ks[kvh]
            mv_scr[kvh, META_ROW0:QB, :] = vs[kvh]
        qi = lax.broadcasted_iota(jnp.int32, (QB, 2 * QB), 0)
        kj = lax.broadcasted_iota(jnp.int32, (QB, 2 * QB), 1)
        band = jnp.logical_and(kj > qi, kj <= qi + QB)
        bias_scr[0] = jnp.where(band, 0.0, NEG_INF)
        bias_scr[1] = jnp.where(jnp.logical_and(band, kj >= META_ROW0), 0.0, NEG_INF)

    @pl.when(s > 0)
    def _():
        batch_first = jnp.logical_or(s == 1, s == 1 + tiles_per_batch)

        @pl.when(batch_first)
        def _():
            k_scr[:, 0:QB, :] = mk_scr[...]
            v_scr[:, 0:QB, :] = mv_scr[...]

        tile = s - 1

        @pl.when(tile == TILES_PER_HALF)
        def _():
            _load_half(ys_hbm, ys_scr, ysem, 1, HALF0_ROWS0, HALF1_ROWS0)
            gather(TILES_PER_HALF * TM, TM, False)

        h = h_ref[...] + _weighted_sum(y1_scr, y2_scr, w_ref, TM)
        h_scr[...] = h
        gather(_next_x_tile(tile) * TM, TM, True)
        xhat = _rms_hat(h)
        ks, vs = _kv_rows(xhat, g_kv_ref, w_kv_ref, cos_ref[...], sin_ref[...])
        for kvh in range(N_KV_HEADS):
            k_scr[kvh, QB:QB + TM, :] = ks[kvh]
            v_scr[kvh, QB:QB + TM, :] = vs[kvh]

        xq = (xhat * g_attn_ref[...]).astype(jnp.bfloat16)
        q = jnp.dot(xq, w_q_ref[...], preferred_element_type=jnp.float32)
        for hb in range(N_HEADS // 2):
            q_scr[hb] = q[:, LANES * hb:LANES * (hb + 1)]
        first_bias = jnp.where(batch_first, 1, 0)

        def head_pair(hb, carry):
            kvh = hb // (N_HEADS // N_KV_HEADS // 2)
            lane = lax.broadcasted_iota(jnp.int32, (QB, LANES), 1)
            q_all = _rope(q_scr[hb], cos_ref[...], sin_ref[...]) * (HEAD_DIM ** -0.5)
            for b in range(N_QB):
                qb = q_all[QB * b:QB * (b + 1), :]
                qs = jnp.concatenate([jnp.where(lane < HEAD_DIM, qb, 0.0),
                                      jnp.where(lane < HEAD_DIM, 0.0, qb)], axis=0).astype(jnp.bfloat16)
                kk = k_scr[kvh, QB * b:QB * (b + 2), :]
                vv = v_scr[kvh, QB * b:QB * (b + 2), :]
                sc = lax.dot_general(qs, kk, (((1,), (1,)), ((), ())),
                                     preferred_element_type=jnp.float32)
                bias = bias_scr[first_bias] if b == 0 else bias_scr[0]
                outs = []
                for j in range(2):
                    sj = sc[QB * j:QB * (j + 1), :] + bias
                    sink = sink_ref[2 * hb + j]
                    m = jnp.maximum(jnp.max(sj, axis=-1, keepdims=True), sink)
                    p = jnp.exp(sj - m)
                    den = jnp.sum(p, axis=-1, keepdims=True) + jnp.exp(sink - m)
                    pv = jnp.dot(p.astype(jnp.bfloat16), vv, preferred_element_type=jnp.float32)
                    outs.append(pv * (1.0 / den))
                o_scr[hb, QB * b:QB * (b + 1), :] = jnp.where(
                    lane < HEAD_DIM, outs[0], outs[1]).astype(jnp.bfloat16)
            return carry

        lax.fori_loop(0, N_HEADS // 2, head_pair, 0)

        k_scr[:, 0:QB, :] = k_scr[:, TM:TM + QB, :]
        v_scr[:, 0:QB, :] = v_scr[:, TM:TM + QB, :]

        o = jnp.concatenate([o_scr[hb] for hb in range(N_HEADS // 2)], axis=1)
        h3 = h_scr[...] + jnp.dot(o, w_o_ref[...], preferred_element_type=jnp.float32)
        h3_ref[...] = h3
        _ffn_prologue(h3, g_ffn_ref, wr_ref, br_ref, run_scr, (s - 1) * TM, T_X,
                      xp_ref, code_ref, wts_ref, cnt_ref)


def _attn(pos1, pos2, h1, wcols, ys, rope, g_attn, g_kv, w_q, w_kv, w_o, sinks, g_ffn, wr, br):
    cos_x, sin_x, cos_m, sin_m = rope

    def tile_x(s, p1, p2):
        return jnp.maximum(s - 1, 0)

    def tile_in(s, p1, p2):
        return (_tile_first_meta(s), 0)

    def rope_tile(s, p1, p2):
        return (jnp.maximum(s - 1, 0) % (N_XT // BATCH), 0)

    out_shape = [
        jax.ShapeDtypeStruct((T_X, D_MODEL), jnp.float32),
        jax.ShapeDtypeStruct((T_X * SLAB, LANES), jnp.uint32),
        jax.ShapeDtypeStruct((2, T_X), jnp.int32),
        jax.ShapeDtypeStruct((T_X, W_COLS), jnp.float32),
        jax.ShapeDtypeStruct((N_KEYS, LANES), jnp.int32),
    ]
    return pl.pallas_call(
        _attn_kernel,
        grid_spec=pltpu.PrefetchScalarGridSpec(
            num_scalar_prefetch=2,
            grid=(N_XT + 1,),
            in_specs=[
                pl.BlockSpec((TM, D_MODEL), tile_in),
                pl.BlockSpec((TM, W_COLS), tile_in),
                pl.BlockSpec(memory_space=pl.ANY),
                pl.BlockSpec((TM, LANES), rope_tile),
                pl.BlockSpec((TM, LANES), rope_tile),
                _RESIDENT, _RESIDENT,
                _RESIDENT, _RESIDENT, _RESIDENT, _RESIDENT, _RESIDENT,
                pl.BlockSpec(memory_space=pltpu.SMEM),
                _RESIDENT, _RESIDENT, _RESIDENT,
            ],
            out_specs=[
                pl.BlockSpec((TM, D_MODEL), lambda s, p1, p2: (tile_x(s, p1, p2), 0)),
                pl.BlockSpec((TM * SLAB, LANES), lambda s, p1, p2: (tile_x(s, p1, p2), 0)),
                pl.BlockSpec((2, TM), lambda s, p1, p2: (0, tile_x(s, p1, p2))),
                pl.BlockSpec((TM, W_COLS), lambda s, p1, p2: (tile_x(s, p1, p2), 0)),
                _RESIDENT,
            ],
            scratch_shapes=[
                pltpu.VMEM((N_KV_HEADS, QB + TM, LANES), jnp.bfloat16),
                pltpu.VMEM((N_KV_HEADS, QB + TM, LANES), jnp.bfloat16),
                pltpu.VMEM((N_KV_HEADS, QB, LANES), jnp.bfloat16),
                pltpu.VMEM((N_KV_HEADS, QB, LANES), jnp.bfloat16),
                pltpu.VMEM((N_HEADS // 2, TM, LANES), jnp.float32),
                pltpu.VMEM((N_HEADS // 2, TM, LANES), jnp.bfloat16),
                pltpu.VMEM((2, QB, 2 * QB), jnp.float32),
                pltpu.VMEM((TM, D_MODEL), jnp.float32),
                pltpu.VMEM((N_KEYS, 1), jnp.float32),
                pltpu.VMEM((HALF0_ROWS0 * SLAB, LANES), jnp.uint32),
                pltpu.VMEM((TM * SLAB, LANES), jnp.uint32),
                pltpu.VMEM((TM * SLAB, LANES), jnp.uint32),
                pltpu.SemaphoreType.DMA,
            ],
        ),
        out_shape=out_shape,
        compiler_params=pltpu.CompilerParams(
            dimension_semantics=("arbitrary",), vmem_limit_bytes=VMEM_LIMIT),
        name="attn_route",
    )(pos1, pos2, h1, wcols, ys, cos_x, sin_x, cos_m, sin_m, g_attn, g_kv, w_q, w_kv, w_o, sinks, g_ffn, wr, br)


def _router_rows(rg_w, rg_b, re_w, re_b):
    wr = jnp.zeros((N_KEYS, D_MODEL), jnp.float32)
    wr = wr.at[0:N_GROUPS].set(rg_w.T).at[8:8 + N_EXPERTS].set(re_w.T)
    br = jnp.zeros((N_KEYS, 1), jnp.float32)
    br = br.at[0:N_GROUPS, 0].set(rg_b).at[8:8 + N_EXPERTS, 0].set(re_b)
    return wr, br


def _rope_tables():
    half = HEAD_DIM // 2
    inv_freq = jnp.tile(ROPE_THETA ** (-jnp.arange(half, dtype=jnp.float32) / half), 4)[None, :]
    sign = jnp.tile(jnp.concatenate([-jnp.ones(half), jnp.ones(half)]), 2).astype(jnp.float32)[None, :]
    ang_hi = (N_META + QB * jnp.arange(SEQ // QB)).astype(jnp.float32)[:, None] * inv_freq
    ang_lo = jnp.arange(QB).astype(jnp.float32)[:, None] * inv_freq
    c_hi, s_hi = jnp.cos(ang_hi)[:, None, :], jnp.sin(ang_hi)[:, None, :]
    c_lo, s_lo = jnp.cos(ang_lo)[None, :, :], jnp.sin(ang_lo)[None, :, :]
    cos_x = (c_hi * c_lo - s_hi * s_lo).reshape(SEQ, LANES)
    sin_x = ((s_hi * c_lo + c_hi * s_lo) * sign[None]).reshape(SEQ, LANES)
    ang_m = jnp.arange(N_META).astype(jnp.float32)[:, None] * inv_freq
    return cos_x, sin_x, jnp.cos(ang_m), jnp.sin(ang_m) * sign


def _moe(cnt, code, xp, wg, wu, wd, layer, t_pad, t_valid):
    pos, off, _, vrow, vgroup, vslot, vnext, nvis = _positions(cnt[:, 0], code, t_pad, t_valid)
    ys = _ffn(vrow, vgroup, vslot, vnext, nvis, pos[0], pos[1], xp, wg, wu, wd, layer, t_pad, t_valid)
    return off, ys


def kernel(x, meta_tokens, conv_norm_g, conv_w_in, conv_w, conv_w_out, kv_norm_g, w_kv, attn_norm_g,
           w_q, w_o, sinks, ffn_norm_g, router_group_w, router_group_b, router_expert_w,
           router_expert_b, w_gate, w_up, w_down, final_norm_g):
    bf = jnp.bfloat16
    x2d = x.reshape(T_X, D_MODEL)
    wr0, br0 = _router_rows(router_group_w[0], router_group_b[0], router_expert_w[0], router_expert_b[0])
    wr1, br1 = _router_rows(router_group_w[1], router_group_b[1], router_expert_w[1], router_expert_b[1])

    h1, xp0, code0, wts0, cnt0 = _mixer0(
        x2d, meta_tokens, conv_norm_g[0].reshape(1, D_MODEL), conv_w_in[0].astype(bf), conv_w[0],
        conv_w_out[0].astype(bf), ffn_norm_g[0].reshape(1, D_MODEL), wr0, br0)
    pos0, ys0 = _moe(cnt0, code0, xp0, w_gate, w_up, w_down, 0, T_PAD0, T_VALID0)

    h3, xp1, code1, wts1, cnt1 = _attn(
        pos0[0], pos0[1], h1, wts0, ys0, _rope_tables(), attn_norm_g[0].reshape(1, D_MODEL),
        kv_norm_g.reshape(1, D_MODEL), w_q[0].astype(bf), w_kv.astype(bf), w_o[0].astype(bf), sinks[0],
        ffn_norm_g[1].reshape(1, D_MODEL), wr1, br1)
    pos1, ys1 = _moe(cnt1, code1, xp1, w_gate, w_up, w_down, 1, T_X, T_X)
    out = _final(pos1[0], pos1[1], h3, wts1, final_norm_g.reshape(1, D_MODEL), ys1)
    return out.reshape(BATCH, SEQ, D_MODEL)
```

```python
import functools

import jax
import jax.numpy as jnp
from jax import lax
from jax.experimental import pallas as pl
from jax.experimental.pallas import tpu as pltpu

D_MODEL = 1024
BATCH = 2
SEQ = 8192
N_META = 16
N_HEADS = 16
HEAD_DIM = 64
N_KV_HEADS = 4
WINDOW = 128
ROPE_THETA = 10000.0
N_GROUPS = 4
EXPERTS_PER_GROUP = 8
N_EXPERTS = N_GROUPS * EXPERTS_PER_GROUP
D_EXPERT = 256
NORM_EPS = 1e-5
NEG_INF = -1e30

TM = 512
N_XT = BATCH * SEQ // TM
T_X = BATCH * SEQ
T_PAD0 = (N_XT + 1) * TM
T_VALID0 = T_X + N_META
HALF_SPLIT = SEQ
N_KEYS = 2 * N_EXPERTS
TMG = 576
LANES = 128
SLAB = D_MODEL // 2 // LANES
VMEM_V7X = 64 * 1024 * 1024
VMEM_LIMIT = VMEM_V7X - 1024 * 1024


def _cdiv(a, b):
    return (a + b - 1) // b


def _rms_hat(x):
    return x * lax.rsqrt(jnp.mean(x * x, axis=-1, keepdims=True) + NORM_EPS)


def _pack_rows(xn):
    half = D_MODEL // 2
    return pltpu.pack_elementwise([xn[:, :half], xn[:, half:]], packed_dtype=jnp.bfloat16)


def _store_slabs(ref, words):
    m = words.shape[0]
    for k in range(SLAB):
        ref[pl.ds(k, m, stride=SLAB), :] = words[:, LANES * k:LANES * (k + 1)]


def _load_slabs(ref, m):
    return jnp.concatenate([ref[pl.ds(k, m, stride=SLAB), :] for k in range(SLAB)], axis=1)


def _unpack_words(words):
    lo = pltpu.unpack_elementwise(words, index=0, packed_dtype=jnp.bfloat16, unpacked_dtype=jnp.float32)
    hi = pltpu.unpack_elementwise(words, index=1, packed_dtype=jnp.bfloat16, unpacked_dtype=jnp.float32)
    return lo, hi


def _route(xn, wr_ref, br_ref, run_scr, tok_base, valid_limit):
    def split(a):
        hi = a.astype(jnp.bfloat16)
        return hi, (a - hi.astype(jnp.float32)).astype(jnp.bfloat16)

    def nt_dot(a, b):
        return lax.dot_general(a, b, (((1,), (1,)), ((), ())), preferred_element_type=jnp.float32)

    w_hi, w_lo = split(wr_ref[...])
    x_hi, x_lo = split(xn)
    logits = nt_dot(w_hi, x_hi) + nt_dot(w_hi, x_lo) + nt_dot(w_lo, x_hi) + br_ref[...]
    g = logits[0:N_GROUPS]
    gmax = jnp.max(g, axis=0, keepdims=True)
    rid_g = lax.broadcasted_iota(jnp.int32, g.shape, 0).astype(jnp.float32)
    g_idx = jnp.min(jnp.where(g == gmax, rid_g, float(N_GROUPS)), axis=0, keepdims=True).astype(jnp.int32)
    g_w = 1.0 / jnp.sum(jnp.exp(g - gmax), axis=0, keepdims=True)
    e_sel = logits[8:8 + EXPERTS_PER_GROUP]
    for gi in range(1, N_GROUPS):
        lo = 8 + EXPERTS_PER_GROUP * gi
        e_sel = jnp.where(g_idx == gi, logits[lo:lo + EXPERTS_PER_GROUP], e_sel)
    rid_e = lax.broadcasted_iota(jnp.int32, e_sel.shape, 0).astype(jnp.float32)
    none = float(EXPERTS_PER_GROUP)
    m1 = jnp.max(e_sel, axis=0, keepdims=True)
    i1f = jnp.min(jnp.where(e_sel == m1, rid_e, none), axis=0, keepdims=True)
    e_rest = jnp.where(rid_e == i1f, -jnp.inf, e_sel)
    m2 = jnp.max(e_rest, axis=0, keepdims=True)
    i2 = jnp.min(jnp.where(e_rest == m2, rid_e, none), axis=0, keepdims=True).astype(jnp.int32)
    i1 = i1f.astype(jnp.int32)
    ex = jnp.exp(m2 - m1)
    den = 1.0 / (1.0 + ex)
    w1 = den * g_w
    w2 = ex * den * g_w

    tok = tok_base + lax.broadcasted_iota(jnp.int32, (1, TM), 1)
    half = jnp.where(jnp.logical_and(tok >= HALF_SPLIT, tok < T_X), N_EXPERTS, 0)
    key1 = half + g_idx * EXPERTS_PER_GROUP + i1
    key2 = half + g_idx * EXPERTS_PER_GROUP + i2
    kid = lax.broadcasted_iota(jnp.int32, (N_KEYS, TM), 0)
    validf = jnp.where(tok < valid_limit, 1.0, 0.0)
    oh1 = jnp.where(kid == key1, validf, 0.0)
    oh2 = jnp.where(kid == key2, validf, 0.0)
    cnt = oh1 + oh2
    tri = jnp.where(lax.broadcasted_iota(jnp.int32, (TM, TM), 0)
                    <= lax.broadcasted_iota(jnp.int32, (TM, TM), 1), 1.0, 0.0).astype(jnp.bfloat16)
    cum = jnp.dot(cnt.astype(jnp.bfloat16), tri, preferred_element_type=jnp.float32)
    before = run_scr[...] + (cum - cnt)
    rank1 = jnp.sum(oh1 * before, axis=0, keepdims=True).astype(jnp.int32)
    rank2 = jnp.sum(oh2 * before, axis=0, keepdims=True).astype(jnp.int32)
    run_scr[...] = run_scr[...] + cum[:, TM - 1:TM]
    code = jnp.concatenate([key1 * 65536 + rank1, key2 * 65536 + rank2], axis=0)
    wts = jnp.concatenate([w1, w2], axis=0)
    return code, wts


W_COLS = 8


def _ffn_prologue(h_new, g_ffn_ref, wr_ref, br_ref, run_scr, tok_base, valid_limit,
                  xp_ref, code_ref, wts_ref, cnt_ref):
    xn2 = _rms_hat(h_new) * g_ffn_ref[...]
    _store_slabs(xp_ref, _pack_rows(xn2))
    code, wts = _route(xn2, wr_ref, br_ref, run_scr, tok_base, valid_limit)
    code_ref[...] = code
    wts_ref[...] = jnp.concatenate([wts, jnp.zeros((W_COLS - 2, TM), jnp.float32)], axis=0).T
    cnt_ref[...] = jnp.broadcast_to(run_scr[...], cnt_ref.shape).astype(jnp.int32)


NC = 1024


def _mixer0_kernel(x_ref, meta_ref, g_conv_ref, w_in_ref, cw_ref, w_out_ref,
                   g_ffn_ref, wr_ref, br_ref,
                   h1_ref, xp_ref, code_ref, wts_ref, cnt_ref,
                   acc_scr, carry_scr, meta_carry_scr, run_scr):
    s = pl.program_id(0)

    @pl.when(s == 0)
    def _():
        carry_scr[...] = jnp.zeros_like(carry_scr)
        meta_carry_scr[...] = jnp.zeros_like(meta_carry_scr)
        run_scr[...] = jnp.zeros_like(run_scr)

    @pl.when(s == 1 + N_XT // BATCH)
    def _():
        carry_scr[...] = meta_carry_scr[...]

    meta_tile = jnp.concatenate([meta_ref[...], jnp.zeros((TM - N_META, D_MODEL), jnp.float32)], axis=0)
    h0 = jnp.where(s == 0, meta_tile, x_ref[...])
    xn = (_rms_hat(h0) * g_conv_ref[...]).astype(jnp.bfloat16)
    row = lax.broadcasted_iota(jnp.int32, (TM, NC), 0)
    for c in range(D_MODEL // NC):
        cols = slice(NC * c, NC * (c + 1))
        gate_c = jnp.dot(xn, w_in_ref[:, D_MODEL + NC * c:D_MODEL + NC * (c + 1)],
                         preferred_element_type=jnp.float32)
        val = jnp.dot(xn, w_in_ref[:, 2 * D_MODEL + NC * c:2 * D_MODEL + NC * (c + 1)],
                      preferred_element_type=jnp.float32)
        u = gate_c * val
        tail = carry_scr[:, cols]
        c1 = tail[7:8, :]
        c2 = tail[6:7, :]
        um1 = jnp.where(row == 0, c1, pltpu.roll(u, 1, 0))
        um2 = jnp.where(row == 0, c2, jnp.where(row == 1, c1, pltpu.roll(u, 2, 0)))
        conv = um2 * cw_ref[0:1, cols] + um1 * cw_ref[1:2, cols] + u * cw_ref[2:3, cols]

        is_meta = s == 0
        meta_tail = u[N_META - 8:N_META, :]
        carry_scr[:, cols] = jnp.where(is_meta, meta_tail, u[TM - 8:TM, :])
        meta_carry_scr[:, cols] = jnp.where(is_meta, meta_tail, meta_carry_scr[:, cols])

        gate_b = jnp.dot(xn, w_in_ref[:, cols], preferred_element_type=jnp.float32)
        gated = (gate_b * conv).astype(jnp.bfloat16)
        part = jnp.dot(gated, w_out_ref[cols, :], preferred_element_type=jnp.float32)
        if c == 0:
            acc_scr[...] = h0 + part
        else:
            acc_scr[...] = acc_scr[...] + part

    h1 = acc_scr[...]
    h1_ref[...] = h1
    tile = jnp.where(s == 0, N_XT, s - 1)
    _ffn_prologue(h1, g_ffn_ref, wr_ref, br_ref, run_scr, tile * TM, T_VALID0,
                  xp_ref, code_ref, wts_ref, cnt_ref)


def _tile_first_meta(s):
    return jnp.where(s == 0, N_XT, s - 1)


_RESIDENT = pl.BlockSpec(memory_space=pltpu.VMEM)


def _mixer0(x2d, meta, g_conv, w_in, cw, w_out, g_ffn, wr, br):
    out_shape = [
        jax.ShapeDtypeStruct((T_PAD0, D_MODEL), jnp.float32),
        jax.ShapeDtypeStruct((T_PAD0 * SLAB, LANES), jnp.uint32),
        jax.ShapeDtypeStruct((2, T_PAD0), jnp.int32),
        jax.ShapeDtypeStruct((T_PAD0, W_COLS), jnp.float32),
        jax.ShapeDtypeStruct((N_KEYS, LANES), jnp.int32),
    ]
    return pl.pallas_call(
        _mixer0_kernel,
        grid=(N_XT + 1,),
        in_specs=[pl.BlockSpec((TM, D_MODEL), lambda s: (jnp.maximum(s - 1, 0), 0))] + [_RESIDENT] * 8,
        out_specs=[
            pl.BlockSpec((TM, D_MODEL), lambda s: (_tile_first_meta(s), 0)),
            pl.BlockSpec((TM * SLAB, LANES), lambda s: (_tile_first_meta(s), 0)),
            pl.BlockSpec((2, TM), lambda s: (0, _tile_first_meta(s))),
            pl.BlockSpec((TM, W_COLS), lambda s: (_tile_first_meta(s), 0)),
            _RESIDENT,
        ],
        out_shape=out_shape,
        scratch_shapes=[
            pltpu.VMEM((TM, D_MODEL), jnp.float32),
            pltpu.VMEM((8, D_MODEL), jnp.float32),
            pltpu.VMEM((8, D_MODEL), jnp.float32),
            pltpu.VMEM((N_KEYS, 1), jnp.float32),
        ],
        compiler_params=pltpu.CompilerParams(
            dimension_semantics=("arbitrary",), vmem_limit_bytes=VMEM_LIMIT),
        name="mixer0_route",
    )(x2d, meta, g_conv, w_in, cw, w_out, g_ffn, wr, br)


def _n_tiles(t_valid):
    return _cdiv(2 * t_valid, TMG)


def _n_visits(t_valid):
    return _n_tiles(t_valid) + N_KEYS


def _half0_rows(t_valid):
    return 2 * (HALF_SPLIT + t_valid - T_X)


def _half1_base(half0_rows):
    return half0_rows + TMG


def _sorted_rows(t_valid):
    return (_n_tiles(t_valid) + 2) * TMG


def _positions_kernel(t_pad, t_valid, cnt_ref, code_ref, pos_ref, off_ref, gstart_ref, vrow_ref, vgroup_ref,
                      vslot_ref, vnext_ref, nvis_ref, nexte_scr):
    n_vis = _n_visits(t_valid)

    total = jnp.int32(0)
    for g in range(N_KEYS):
        if g == N_EXPERTS:
            total = jnp.int32(_half1_base(_half0_rows(t_valid)))
        gstart_ref[g] = total
        total = total + cnt_ref[g]
    gstart_ref[N_KEYS] = total

    def rows_of(e):
        return cnt_ref[e] + cnt_ref[e + N_EXPERTS]

    nxt = jnp.int32(-1)
    for e in reversed(range(N_EXPERTS)):
        nexte_scr[e] = nxt
        nxt = jnp.where(rows_of(e) > 0, e, nxt)

    def per_expert(e, carry):
        v, last_g, rank = carry
        for half in range(2):
            g = e + half * N_EXPERTS
            c = cnt_ref[g]
            start = gstart_ref[g]

            def per_window(k, vv, g=g, start=start):
                vrow_ref[vv] = start + k * TMG
                vgroup_ref[vv] = g
                vslot_ref[vv] = rank % 2
                vnext_ref[vv] = nexte_scr[e]
                return vv + 1

            v = lax.fori_loop(0, (c + TMG - 1) // TMG, per_window, v)
            last_g = jnp.where(c > 0, g, last_g)
        return v, last_g, jnp.where(rows_of(e) > 0, rank + 1, rank)

    nvis, last_g, _ = lax.fori_loop(0, N_EXPERTS, per_expert, (jnp.int32(0), jnp.int32(0), jnp.int32(0)))
    nvis_ref[0] = nvis

    def pad(vv, c):
        vrow_ref[vv] = 0
        vgroup_ref[vv] = last_g
        vslot_ref[vv] = 0
        vnext_ref[vv] = -1
        return c

    lax.fori_loop(nvis, n_vis, pad, 0)

    code = code_ref[...]
    key = code >> 16
    pos = code & 0xFFFF
    for g in range(N_KEYS):
        pos = pos + jnp.where(key == g, gstart_ref[g], 0)
    pos_ref[...] = pos
    half_base = jnp.where(key >= N_EXPERTS, _half1_base(_half0_rows(t_valid)), 0)
    off_ref[...] = (pos - half_base) * SLAB


def _positions(cnt, code, t_pad, t_valid):
    n_vis = _n_visits(t_valid)
    smem = pl.BlockSpec(memory_space=pltpu.SMEM)
    vmem = pl.BlockSpec(memory_space=pltpu.VMEM)
    return pl.pallas_call(
        functools.partial(_positions_kernel, t_pad, t_valid),
        in_specs=[smem, vmem],
        out_specs=[vmem, vmem, smem, smem, smem, smem, smem, smem],
        out_shape=[
            jax.ShapeDtypeStruct((2, t_pad), jnp.int32),
            jax.ShapeDtypeStruct((2, t_pad), jnp.int32),
            jax.ShapeDtypeStruct((N_KEYS + 1,), jnp.int32),
            jax.ShapeDtypeStruct((n_vis,), jnp.int32),
            jax.ShapeDtypeStruct((n_vis,), jnp.int32),
            jax.ShapeDtypeStruct((n_vis,), jnp.int32),
            jax.ShapeDtypeStruct((n_vis,), jnp.int32),
            jax.ShapeDtypeStruct((1,), jnp.int32),
        ],
        scratch_shapes=[pltpu.SMEM((N_EXPERTS,), jnp.int32)],
        name="sort_positions",
    )(cnt, code)


SRC_UNROLL = 64


def _invert_positions(t_valid, pos1_ref, pos2_ref, src_ref):
    def slack(lo, hi):
        n_full = (hi - lo) // SRC_UNROLL

        def body(i, c):
            for j in range(SRC_UNROLL):
                src_ref[lo + i * SRC_UNROLL + j] = 0
            return c

        lax.fori_loop(0, n_full, body, 0)
        for r in range(lo + n_full * SRC_UNROLL, hi):
            src_ref[r] = 0

    half0_rows = _half0_rows(t_valid)
    slack(half0_rows, _half1_base(half0_rows))
    slack(2 * t_valid + TMG, _sorted_rows(t_valid))

    def place(t):
        src_ref[pos1_ref[t]] = t
        src_ref[pos2_ref[t]] = t

    def body(i, c):
        for j in range(SRC_UNROLL):
            place(i * SRC_UNROLL + j)
        return c

    n_full = t_valid // SRC_UNROLL
    lax.fori_loop(0, n_full, body, 0)
    for t in range(n_full * SRC_UNROLL, t_valid):
        place(t)


GATHER_UNROLL = 8


def _ffn_kernel(layer, t_valid, vrow_ref, vgroup_ref, vslot_ref, vnext_ref, nvis_ref, pos1_ref, pos2_ref,
                xp_hbm, wg_hbm, wu_hbm, wd_hbm, ys_hbm,
                xp_scr, wg_scr, wu_scr, wd_scr, xnext_scr, out_scr, src_ref, xsem, wsem, osem):
    nvis = nvis_ref[0]

    def expert_of(visit):
        return vgroup_ref[visit] % N_EXPERTS

    def weight_copies(e, sl):
        return (pltpu.make_async_copy(wg_hbm.at[layer, e], wg_scr.at[sl], wsem.at[sl, 0]),
                pltpu.make_async_copy(wu_hbm.at[layer, e], wu_scr.at[sl], wsem.at[sl, 1]),
                pltpu.make_async_copy(wd_hbm.at[layer, e], wd_scr.at[sl], wsem.at[sl, 2]))

    def out_copy(visit):
        sl = visit % 2
        return pltpu.make_async_copy(
            out_scr.at[sl], ys_hbm.at[pl.ds(pl.multiple_of(vrow_ref[visit] * SLAB, SLAB), TMG * SLAB), :],
            osem.at[sl])

    def gather_rows(visit, unrolled):
        base = vrow_ref[visit]

        def one(r):
            tok = src_ref[base + r]
            xnext_scr[pl.ds(pl.multiple_of(r * SLAB, SLAB), SLAB), :] = (
                xp_scr[pl.ds(pl.multiple_of(tok * SLAB, SLAB), SLAB), :])

        if unrolled:
            for r in range(TMG):
                one(r)
        else:
            def chunk(i, c):
                for j in range(GATHER_UNROLL):
                    one(i * GATHER_UNROLL + j)
                return c

            lax.fori_loop(0, TMG // GATHER_UNROLL, chunk, 0)

    cp = pltpu.make_async_copy(xp_hbm, xp_scr, xsem)
    cp.start()
    for c in weight_copies(expert_of(0), vslot_ref[0]):
        c.start()
    out_scr[...] = jnp.zeros_like(out_scr)
    cap = ys_hbm.shape[0] // SLAB
    for first_row in (_half0_rows(t_valid), cap - 2 * TMG, cap - TMG):
        fill = pltpu.make_async_copy(
            out_scr.at[0], ys_hbm.at[pl.ds(first_row * SLAB, TMG * SLAB), :], osem.at[0])
        fill.start()
        fill.wait()
    _invert_positions(t_valid, pos1_ref, pos2_ref, src_ref)
    cp.wait()
    gather_rows(0, False)

    def visit(v, carry):
        e = expert_of(v)
        slot = vslot_ref[v]

        @pl.when(jnp.logical_or(v == 0, e != expert_of(jnp.maximum(v - 1, 0))))
        def _():
            for c in weight_copies(e, slot):
                c.wait()
            nxt = vnext_ref[v]

            @pl.when(nxt >= 0)
            def _():
                for c in weight_copies(nxt, 1 - slot):
                    c.start()

        lo, hi = _unpack_words(_load_slabs(xnext_scr, TMG))
        xs = jnp.concatenate([lo, hi], axis=1).astype(jnp.bfloat16)
        gather_rows(jnp.minimum(v + 1, nvis - 1), True)
        hg = jnp.dot(xs, wg_scr[slot].astype(jnp.bfloat16), preferred_element_type=jnp.float32)
        hu = jnp.dot(xs, wu_scr[slot].astype(jnp.bfloat16), preferred_element_type=jnp.float32)
        hdn = (hg * jax.nn.sigmoid(hg) * hu).astype(jnp.bfloat16)
        y = jnp.dot(hdn, wd_scr[slot].astype(jnp.bfloat16), preferred_element_type=jnp.float32)
        _store_slabs(out_scr.at[v % 2], _pack_rows(y))

        @pl.when(v > 0)
        def _():
            out_copy(v - 1).wait()

        out_copy(v).start()
        return carry

    lax.fori_loop(0, nvis, visit, 0)
    out_copy(nvis - 1).wait()


def _ffn(vrow, vgroup, vslot, vnext, nvis, pos1, pos2, xp, wg, wu, wd, layer, t_pad, t_valid):
    any_spec = pl.BlockSpec(memory_space=pl.ANY)
    return pl.pallas_call(
        functools.partial(_ffn_kernel, layer, t_valid),
        grid_spec=pltpu.PrefetchScalarGridSpec(
            num_scalar_prefetch=7,
            grid=(1,),
            in_specs=[any_spec, any_spec, any_spec, any_spec],
            out_specs=any_spec,
            scratch_shapes=[
                pltpu.VMEM((t_pad * SLAB, LANES), jnp.uint32),
                pltpu.VMEM((2, D_MODEL, D_EXPERT), jnp.float32),
                pltpu.VMEM((2, D_MODEL, D_EXPERT), jnp.float32),
                pltpu.VMEM((2, D_EXPERT, D_MODEL), jnp.float32),
                pltpu.VMEM((TMG * SLAB, LANES), jnp.uint32),
                pltpu.VMEM((2, TMG * SLAB, LANES), jnp.uint32),
                pltpu.SMEM((_sorted_rows(t_valid),), jnp.int32),
                pltpu.SemaphoreType.DMA,
                pltpu.SemaphoreType.DMA((2, 3)),
                pltpu.SemaphoreType.DMA((2,)),
            ],
        ),
        out_shape=jax.ShapeDtypeStruct((_sorted_rows(t_valid) * SLAB, LANES), jnp.uint32),
        compiler_params=pltpu.CompilerParams(
            dimension_semantics=("arbitrary",), vmem_limit_bytes=VMEM_LIMIT),
        name="expert_ffn",
    )(vrow, vgroup, vslot, vnext, nvis, pos1, pos2, xp, wg, wu, wd)


TILES_PER_HALF = HALF_SPLIT // TM


def _load_half(ys_hbm, ys_scr, sem, half, half0_rows, half1_rows):
    start, rows = (0, half0_rows) if half == 0 else (_half1_base(half0_rows), half1_rows)
    cp = pltpu.make_async_copy(ys_hbm.at[pl.ds(start * SLAB, rows * SLAB), :],
                               ys_scr.at[pl.ds(0, rows * SLAB), :], sem)
    cp.start()
    cp.wait()


def _gather_pairs(off1_ref, off2_ref, ys_scr, y1_scr, y2_scr, tok0, n, unrolled):
    def one(r):
        dst = pl.ds(pl.multiple_of(r * SLAB, SLAB), SLAB)
        y1_scr[dst, :] = ys_scr[pl.ds(pl.multiple_of(off1_ref[tok0 + r], SLAB), SLAB), :]
        y2_scr[dst, :] = ys_scr[pl.ds(pl.multiple_of(off2_ref[tok0 + r], SLAB), SLAB), :]

    if unrolled:
        for r in range(n):
            one(r)
    else:
        def chunk(i, c):
            for j in range(GATHER_UNROLL):
                one(i * GATHER_UNROLL + j)
            return c

        lax.fori_loop(0, n // GATHER_UNROLL, chunk, 0)


def _weighted_sum(y1_scr, y2_scr, w_ref, m):
    lo1, hi1 = _unpack_words(_load_slabs(y1_scr, m))
    lo2, hi2 = _unpack_words(_load_slabs(y2_scr, m))
    w1 = w_ref[0:m, 0:1]
    w2 = w_ref[0:m, 1:2]
    return jnp.concatenate([w1 * lo1 + w2 * lo2, w1 * hi1 + w2 * hi2], axis=1)


def _next_x_tile(tile):
    nxt = tile + 1
    return jnp.where(jnp.logical_or(nxt == TILES_PER_HALF, nxt == N_XT), tile, nxt)


def _final_kernel(half_rows, off1_ref, off2_ref, h_ref, w_ref, g_ref, ys_hbm,
                  o_ref, ys_scr, y1_scr, y2_scr, sem):
    tile = pl.program_id(0)
    gather = functools.partial(_gather_pairs, off1_ref, off2_ref, ys_scr, y1_scr, y2_scr)

    @pl.when(tile == 0)
    def _():
        _load_half(ys_hbm, ys_scr, sem, 0, half_rows, half_rows)
        gather(0, TM, False)

    @pl.when(tile == TILES_PER_HALF)
    def _():
        _load_half(ys_hbm, ys_scr, sem, 1, half_rows, half_rows)
        gather(TILES_PER_HALF * TM, TM, False)

    h = h_ref[...] + _weighted_sum(y1_scr, y2_scr, w_ref, TM)
    gather(_next_x_tile(tile) * TM, TM, True)
    o_ref[...] = _rms_hat(h) * g_ref[...]


def _final(pos1, pos2, h, wcols, g, ys):
    half_rows = 2 * HALF_SPLIT
    return pl.pallas_call(
        functools.partial(_final_kernel, half_rows),
        grid_spec=pltpu.PrefetchScalarGridSpec(
            num_scalar_prefetch=2,
            grid=(N_XT,),
            in_specs=[
                pl.BlockSpec((TM, D_MODEL), lambda i, p1, p2: (i, 0)),
                pl.BlockSpec((TM, W_COLS), lambda i, p1, p2: (i, 0)),
                _RESIDENT,
                pl.BlockSpec(memory_space=pl.ANY),
            ],
            out_specs=pl.BlockSpec((TM, D_MODEL), lambda i, p1, p2: (i, 0)),
            scratch_shapes=[
                pltpu.VMEM((half_rows * SLAB, LANES), jnp.uint32),
                pltpu.VMEM((TM * SLAB, LANES), jnp.uint32),
                pltpu.VMEM((TM * SLAB, LANES), jnp.uint32),
                pltpu.SemaphoreType.DMA,
            ],
        ),
        out_shape=jax.ShapeDtypeStruct((T_X, D_MODEL), jnp.float32),
        compiler_params=pltpu.CompilerParams(
            dimension_semantics=("arbitrary",), vmem_limit_bytes=VMEM_LIMIT),
        name="moe_combine_final",
    )(pos1, pos2, h, wcols, g, ys)


QB = WINDOW
KV_W = N_KV_HEADS * HEAD_DIM
N_QB = TM // QB
META_ROW0 = QB - N_META


def _rope(x, cos, sin_signed):
    q = lax.broadcasted_iota(jnp.int32, x.shape, 1) // (HEAD_DIM // 2)
    swapped = jnp.where(q % 2 == 0, pltpu.roll(x, LANES - HEAD_DIM // 2, 1),
                        pltpu.roll(x, HEAD_DIM // 2, 1))
    return x * cos + swapped * sin_signed


def _dup_heads(blk):
    lane = lax.broadcasted_iota(jnp.int32, blk.shape, 1)
    rolled = pltpu.roll(blk, HEAD_DIM, 1)
    return jnp.where(lane < HEAD_DIM, blk, rolled), jnp.where(lane < HEAD_DIM, rolled, blk)


def _kv_rows(xhat, g_kv_ref, w_kv_ref, cos, sin_signed):
    xk = (xhat * g_kv_ref[...]).astype(jnp.bfloat16)
    kv = jnp.dot(xk, w_kv_ref[...], preferred_element_type=jnp.float32)
    ks, vs = [], []
    for b in range(KV_W // LANES):
        kb = _rope(kv[:, LANES * b:LANES * (b + 1)], cos, sin_signed)
        vb = kv[:, KV_W + LANES * b:KV_W + LANES * (b + 1)]
        ks.extend(_dup_heads(kb))
        vs.extend(_dup_heads(vb))
    return [k.astype(jnp.bfloat16) for k in ks], [v.astype(jnp.bfloat16) for v in vs]


HALF0_ROWS0 = 2 * (HALF_SPLIT + N_META)
HALF1_ROWS0 = 2 * HALF_SPLIT


def _attn_kernel(off1_ref, off2_ref,
                 h_ref, w_ref, ys_hbm, cos_ref, sin_ref, cosm_ref, sinm_ref, g_attn_ref, g_kv_ref, w_q_ref,
                 w_kv_ref, w_o_ref, sink_ref, g_ffn_ref, wr_ref, br_ref,
                 h3_ref, xp_ref, code_ref, wts_ref, cnt_ref,
                 k_scr, v_scr, mk_scr, mv_scr, q_scr, o_scr, bias_scr, h_scr, run_scr,
                 ys_scr, y1_scr, y2_scr, ysem):
    s = pl.program_id(0)
    tiles_per_batch = N_XT // BATCH
    gather = functools.partial(_gather_pairs, off1_ref, off2_ref, ys_scr, y1_scr, y2_scr)

    @pl.when(s == 0)
    def _():
        run_scr[...] = jnp.zeros_like(run_scr)
        _load_half(ys_hbm, ys_scr, ysem, 0, HALF0_ROWS0, HALF1_ROWS0)
        gather(N_XT * TM, N_META, True)
        h_meta = h_ref[0:N_META, :] + _weighted_sum(y1_scr, y2_scr, w_ref, N_META)
        gather(0, TM, False)
        xhat = _rms_hat(h_meta)
        ks, vs = _kv_rows(xhat, g_kv_ref, w_kv_ref, cosm_ref[...], sinm_ref[...])
        mk_scr[...] = jnp.zeros_like(mk_scr)
        mv_scr[...] = jnp.zeros_like(mv_scr)
        for kvh in range(N_KV_HEADS):
            mk_scr[kvh, META_ROW0:QB, :] = ks[kvh]
            mv_scr[kvh, META_ROW0:QB, :] = vs[kvh]
        qi = lax.broadcasted_iota(jnp.int32, (QB, 2 * QB), 0)
        kj = lax.broadcasted_iota(jnp.int32, (QB, 2 * QB), 1)
        band = jnp.logical_and(kj > qi, kj <= qi + QB)
        bias_scr[0] = jnp.where(band, 0.0, NEG_INF)
        bias_scr[1] = jnp.where(jnp.logical_and(band, kj >= META_ROW0), 0.0, NEG_INF)

    @pl.when(s > 0)
    def _():
        batch_first = jnp.logical_or(s == 1, s == 1 + tiles_per_batch)

        @pl.when(batch_first)
        def _():
            k_scr[:, 0:QB, :] = mk_scr[...]
            v_scr[:, 0:QB, :] = mv_scr[...]

        tile = s - 1

        @pl.when(tile == TILES_PER_HALF)
        def _():
            _load_half(ys_hbm, ys_scr, ysem, 1, HALF0_ROWS0, HALF1_ROWS0)
            gather(TILES_PER_HALF * TM, TM, False)

        h = h_ref[...] + _weighted_sum(y1_scr, y2_scr, w_ref, TM)
        h_scr[...] = h
        gather(_next_x_tile(tile) * TM, TM, True)
        xhat = _rms_hat(h)
        ks, vs = _kv_rows(xhat, g_kv_ref, w_kv_ref, cos_ref[...], sin_ref[...])
        for kvh in range(N_KV_HEADS):
            k_scr[kvh, QB:QB + TM, :] = ks[kvh]
            v_scr[kvh, QB:QB + TM, :] = vs[kvh]

        xq = (xhat * g_attn_ref[...]).astype(jnp.bfloat16)
        q = jnp.dot(xq, w_q_ref[...], preferred_element_type=jnp.float32)
        for hb in range(N_HEADS // 2):
            q_scr[hb] = q[:, LANES * hb:LANES * (hb + 1)]
        first_bias = jnp.where(batch_first, 1, 0)

        def head_pair(hb, carry):
            kvh = hb // (N_HEADS // N_KV_HEADS // 2)
            lane = lax.broadcasted_iota(jnp.int32, (QB, LANES), 1)
            q_all = _rope(q_scr[hb], cos_ref[...], sin_ref[...]) * (HEAD_DIM ** -0.5)
            for b in range(N_QB):
                qb = q_all[QB * b:QB * (b + 1), :]
                qs = jnp.concatenate([jnp.where(lane < HEAD_DIM, qb, 0.0),
                                      jnp.where(lane < HEAD_DIM, 0.0, qb)], axis=0).astype(jnp.bfloat16)
                kk = k_scr[kvh, QB * b:QB * (b + 2), :]
                vv = v_scr[kvh, QB * b:QB * (b + 2), :]
                sc = lax.dot_general(qs, kk, (((1,), (1,)), ((), ())),
                                     preferred_element_type=jnp.float32)
                bias = bias_scr[first_bias] if b == 0 else bias_scr[0]
                outs = []
                for j in range(2):
                    sj = sc[QB * j:QB * (j + 1), :] + bias
                    sink = sink_ref[2 * hb + j]
                    m = jnp.maximum(jnp.max(sj, axis=-1, keepdims=True), sink)
                    p = jnp.exp(sj - m)
                    den = jnp.sum(p, axis=-1, keepdims=True) + jnp.exp(sink - m)
                    pv = jnp.dot(p.astype(jnp.bfloat16), vv, preferred_element_type=jnp.float32)
                    outs.append(pv * (1.0 / den))
                o_scr[hb, QB * b:QB * (b + 1), :] = jnp.where(
                    lane < HEAD_DIM, outs[0], outs[1]).astype(jnp.bfloat16)
            return carry

        lax.fori_loop(0, N_HEADS // 2, head_pair, 0)

        k_scr[:, 0:QB, :] = k_scr[:, TM:TM + QB, :]
        v_scr[:, 0:QB, :] = v_scr[:, TM:TM + QB, :]

        o = jnp.concatenate([o_scr[hb] for hb in range(N_HEADS // 2)], axis=1)
        h3 = h_scr[...] + jnp.dot(o, w_o_ref[...], preferred_element_type=jnp.float32)
        h3_ref[...] = h3
        _ffn_prologue(h3, g_ffn_ref, wr_ref, br_ref, run_scr, (s - 1) * TM, T_X,
                      xp_ref, code_ref, wts_ref, cnt_ref)


def _attn(pos1, pos2, h1, wcols, ys, rope, g_attn, g_kv, w_q, w_kv, w_o, sinks, g_ffn, wr, br):
    cos_x, sin_x, cos_m, sin_m = rope

    def tile_x(s, p1, p2):
        return jnp.maximum(s - 1, 0)

    def tile_in(s, p1, p2):
        return (_tile_first_meta(s), 0)

    def rope_tile(s, p1, p2):
        return (jnp.maximum(s - 1, 0) % (N_XT // BATCH), 0)

    out_shape = [
        jax.ShapeDtypeStruct((T_X, D_MODEL), jnp.float32),
        jax.ShapeDtypeStruct((T_X * SLAB, LANES), jnp.uint32),
        jax.ShapeDtypeStruct((2, T_X), jnp.int32),
        jax.ShapeDtypeStruct((T_X, W_COLS), jnp.float32),
        jax.ShapeDtypeStruct((N_KEYS, LANES), jnp.int32),
    ]
    return pl.pallas_call(
        _attn_kernel,
        grid_spec=pltpu.PrefetchScalarGridSpec(
            num_scalar_prefetch=2,
            grid=(N_XT + 1,),
            in_specs=[
                pl.BlockSpec((TM, D_MODEL), tile_in),
                pl.BlockSpec((TM, W_COLS), tile_in),
                pl.BlockSpec(memory_space=pl.ANY),
                pl.BlockSpec((TM, LANES), rope_tile),
                pl.BlockSpec((TM, LANES), rope_tile),
                _RESIDENT, _RESIDENT,
                _RESIDENT, _RESIDENT, _RESIDENT, _RESIDENT, _RESIDENT,
                pl.BlockSpec(memory_space=pltpu.SMEM),
                _RESIDENT, _RESIDENT, _RESIDENT,
            ],
            out_specs=[
                pl.BlockSpec((TM, D_MODEL), lambda s, p1, p2: (tile_x(s, p1, p2), 0)),
                pl.BlockSpec((TM * SLAB, LANES), lambda s, p1, p2: (tile_x(s, p1, p2), 0)),
                pl.BlockSpec((2, TM), lambda s, p1, p2: (0, tile_x(s, p1, p2))),
                pl.BlockSpec((TM, W_COLS), lambda s, p1, p2: (tile_x(s, p1, p2), 0)),
                _RESIDENT,
            ],
            scratch_shapes=[
                pltpu.VMEM((N_KV_HEADS, QB + TM, LANES), jnp.bfloat16),
                pltpu.VMEM((N_KV_HEADS, QB + TM, LANES), jnp.bfloat16),
                pltpu.VMEM((N_KV_HEADS, QB, LANES), jnp.bfloat16),
                pltpu.VMEM((N_KV_HEADS, QB, LANES), jnp.bfloat16),
                pltpu.VMEM((N_HEADS // 2, TM, LANES), jnp.float32),
                pltpu.VMEM((N_HEADS // 2, TM, LANES), jnp.bfloat16),
                pltpu.VMEM((2, QB, 2 * QB), jnp.float32),
                pltpu.VMEM((TM, D_MODEL), jnp.float32),
                pltpu.VMEM((N_KEYS, 1), jnp.float32),
                pltpu.VMEM((HALF0_ROWS0 * SLAB, LANES), jnp.uint32),
                pltpu.VMEM((TM * SLAB, LANES), jnp.uint32),
                pltpu.VMEM((TM * SLAB, LANES), jnp.uint32),
                pltpu.SemaphoreType.DMA,
            ],
        ),
        out_shape=out_shape,
        compiler_params=pltpu.CompilerParams(
            dimension_semantics=("arbitrary",), vmem_limit_bytes=VMEM_LIMIT),
        name="attn_route",
    )(pos1, pos2, h1, wcols, ys, cos_x, sin_x, cos_m, sin_m, g_attn, g_kv, w_q, w_kv, w_o, sinks, g_ffn, wr, br)


def _router_rows(rg_w, rg_b, re_w, re_b):
    wr = jnp.zeros((N_KEYS, D_MODEL), jnp.float32)
    wr = wr.at[0:N_GROUPS].set(rg_w.T).at[8:8 + N_EXPERTS].set(re_w.T)
    br = jnp.zeros((N_KEYS, 1), jnp.float32)
    br = br.at[0:N_GROUPS, 0].set(rg_b).at[8:8 + N_EXPERTS, 0].set(re_b)
    return wr, br


def _rope_tables():
    half = HEAD_DIM // 2
    inv_freq = jnp.tile(ROPE_THETA ** (-jnp.arange(half, dtype=jnp.float32) / half), 4)[None, :]
    sign = jnp.tile(jnp.concatenate([-jnp.ones(half), jnp.ones(half)]), 2).astype(jnp.float32)[None, :]
    ang_hi = (N_META + QB * jnp.arange(SEQ // QB)).astype(jnp.float32)[:, None] * inv_freq
    ang_lo = jnp.arange(QB).astype(jnp.float32)[:, None] * inv_freq
    c_hi, s_hi = jnp.cos(ang_hi)[:, None, :], jnp.sin(ang_hi)[:, None, :]
    c_lo, s_lo = jnp.cos(ang_lo)[None, :, :], jnp.sin(ang_lo)[None, :, :]
    cos_x = (c_hi * c_lo - s_hi * s_lo).reshape(SEQ, LANES)
    sin_x = ((s_hi * c_lo + c_hi * s_lo) * sign[None]).reshape(SEQ, LANES)
    ang_m = jnp.arange(N_META).astype(jnp.float32)[:, None] * inv_freq
    return cos_x, sin_x, jnp.cos(ang_m), jnp.sin(ang_m) * sign


def _moe(cnt, code, xp, wg, wu, wd, layer, t_pad, t_valid):
    pos, off, _, vrow, vgroup, vslot, vnext, nvis = _positions(cnt[:, 0], code, t_pad, t_valid)
    ys = _ffn(vrow, vgroup, vslot, vnext, nvis, pos[0], pos[1], xp, wg, wu, wd, layer, t_pad, t_valid)
    return off, ys


def kernel(x, meta_tokens, conv_norm_g, conv_w_in, conv_w, conv_w_out, kv_norm_g, w_kv, attn_norm_g,
           w_q, w_o, sinks, ffn_norm_g, router_group_w, router_group_b, router_expert_w,
           router_expert_b, w_gate, w_up, w_down, final_norm_g):
    bf = jnp.bfloat16
    x2d = x.reshape(T_X, D_MODEL)
    wr0, br0 = _router_rows(router_group_w[0], router_group_b[0], router_expert_w[0], router_expert_b[0])
    wr1, br1 = _router_rows(router_group_w[1], router_group_b[1], router_expert_w[1], router_expert_b[1])

    h1, xp0, code0, wts0, cnt0 = _mixer0(
        x2d, meta_tokens, conv_norm_g[0].reshape(1, D_MODEL), conv_w_in[0].astype(bf), conv_w[0],
        conv_w_out[0].astype(bf), ffn_norm_g[0].reshape(1, D_MODEL), wr0, br0)
    pos0, ys0 = _moe(cnt0, code0, xp0, w_gate, w_up, w_down, 0, T_PAD0, T_VALID0)

    h3, xp1, code1, wts1, cnt1 = _attn(
        pos0[0], pos0[1], h1, wts0, ys0, _rope_tables(), attn_norm_g[0].reshape(1, D_MODEL),
        kv_norm_g.reshape(1, D_MODEL), w_q[0].astype(bf), w_kv.astype(bf), w_o[0].astype(bf), sinks[0],
        ffn_norm_g[1].reshape(1, D_MODEL), wr1, br1)
    pos1, ys1 = _moe(cnt1, code1, xp1, w_gate, w_up, w_down, 1, T_X, T_X)
    out = _final(pos1[0], pos1[1], h3, wts1, final_norm_g.reshape(1, D_MODEL), ys1)
    return out.reshape(BATCH, SEQ, D_MODEL)
```
